```python
import math
import jax, jax.numpy as jnp
from jax import lax
import numpy as np

D_MODEL = 2048
BATCH = 8
SEQ = 2048
DEPTH = 2
DEC_BATCH = 8
DEC_SEQ = 32
PAST_LEN = 1024

CHUNK = 64
QB = 128
N_EVEN = (DEPTH + 1) // 2
N_ODD = DEPTH // 2
H_A = 8
HKV_A = 2
GROUP_A = H_A // HKV_A
DH_A = 128
H_IDX = 16
D_IDX = 64
TOPK_MAX = 256
T5_BUCKETS = 32
T5_MAX_DIST = 128
H_B = 8
Q_LORA = 512
KV_LORA = 256
NOPE = 128
ROPE_DIM = 64
V_DIM = 128
ROPE_BASE = 10000.0
H_C = 8
DH_C = 128
C_BACK = 8
C_PAST = C_BACK * CHUNK
REL_CLIP = 128
REL_SIZE = CHUNK + REL_CLIP
W_D = 1024
G_D = 8
DG_D = W_D // G_D
D_CHUNK = 128
PLE_DIM = 256

EVEN_SPLITS = (H_A * DH_A, HKV_A * DH_A, HKV_A * DH_A, H_A * DH_A,
               H_IDX * D_IDX, D_IDX, H_IDX,
               Q_LORA, KV_LORA, ROPE_DIM, H_B * V_DIM)
ODD_SPLITS = (H_C * DH_C, H_C * DH_C, H_C * DH_C, H_C * DH_C, W_D, W_D, W_D)
EVEN_IN = sum(EVEN_SPLITS)
ODD_IN = sum(ODD_SPLITS)
W_OUT_EVEN = H_A * DH_A + H_B * V_DIM
W_OUT_ODD = H_C * DH_C + W_D

kernel_name = 'hybrid_streaming_encoder_step'


def _split(z, sizes):
    return jnp.split(z, np.cumsum(sizes)[:-1].tolist(), axis=-1)


def _rms(x, eps=1e-6):
    xf = x.astype(jnp.float32)
    return (xf * lax.rsqrt(jnp.mean(xf * xf, -1, keepdims=True) + eps)).astype(x.dtype)


def _rmsnorm(x, g):
    return _rms(x) * g


def _layernorm(x, g, b, eps=1e-5):
    xf = x.astype(jnp.float32)
    xc = xf - jnp.mean(xf, -1, keepdims=True)
    var = jnp.mean(xc * xc, -1, keepdims=True)
    return (xc * lax.rsqrt(var + eps)).astype(x.dtype) * g + b


def _rope(x, pos):
    half = x.shape[-1] // 2
    freq = ROPE_BASE ** (-jnp.arange(half, dtype=jnp.float32) / half)
    ang = pos.astype(jnp.float32)[:, None] * freq[None, :]
    cos = jnp.cos(ang)[:, None, :].astype(x.dtype)
    sin = jnp.sin(ang)[:, None, :].astype(x.dtype)
    x1, x2 = x[..., :half], x[..., half:]
    return jnp.concatenate([x1 * cos - x2 * sin, x1 * sin + x2 * cos], -1)


def _chunk_causal(q_pos, k_pos):
    return (q_pos[:, None] // CHUNK) >= (k_pos[None, :] // CHUNK)


def _t5_bucket(rel):
    nb = T5_BUCKETS // 2
    max_exact = nb // 2
    n = jnp.abs(rel)
    nf = jnp.maximum(n, 1).astype(jnp.float32)
    large = max_exact + (jnp.log(nf / max_exact) / math.log(T5_MAX_DIST / max_exact)
                         * (nb - max_exact)).astype(jnp.int32)
    large = jnp.minimum(large, nb - 1)
    return jnp.where(rel > 0, nb, 0) + jnp.where(n < max_exact, n, large)


def _rel_index(rel):
    return jnp.clip(rel, -(CHUNK - 1), REL_CLIP) + (CHUNK - 1)


def _sweep_queries(fn, q_arrays, q_pos):
    sq = q_pos.shape[0]
    if sq <= QB or sq % QB:
        return fn(*q_arrays, q_pos)
    nb = sq // QB
    blk = lambda t: jnp.swapaxes(t.reshape(t.shape[0], nb, QB, *t.shape[2:]), 0, 1)
    args = tuple(blk(t) for t in q_arrays) + (q_pos.reshape(nb, QB),)
    out = lax.map(lambda a: fn(*a[:-1], a[-1]), args)
    out = jnp.swapaxes(out, 0, 1)
    return out.reshape(out.shape[0], sq, *out.shape[3:])


def _dsa_block(q, qi, wi, q_pos, k, v, ki, k_pos, t5_bias, ksel):
    bn, qn = q.shape[:2]
    idx_logits = jax.nn.relu(jnp.einsum('bqhd,bkd->bqhk', qi, ki).astype(jnp.float32))
    score = jnp.einsum('bqh,bqhk->bqk', wi.astype(jnp.float32), idx_logits)
    score = jnp.where(_chunk_causal(q_pos, k_pos), score, -jnp.inf)
    top_val, top_idx = lax.top_k(score, ksel)
    kg = jax.vmap(lambda a, i: a[i])(k, top_idx)
    vg = jax.vmap(lambda a, i: a[i])(v, top_idx)
    qg = q.reshape(bn, qn, HKV_A, GROUP_A, DH_A)
    logits = jnp.einsum('bqngd,bqjnd->bqngj', qg, kg).astype(jnp.float32) * DH_A ** -0.5
    rel = k_pos[top_idx] - q_pos[None, :, None]
    bias = t5_bias[_t5_bucket(rel)].astype(jnp.float32)
    bias = bias.reshape(bn, qn, ksel, HKV_A, GROUP_A).transpose(0, 1, 3, 4, 2)
    valid = jnp.isfinite(top_val)[:, :, None, None, :]
    logits = jnp.where(valid, logits + bias, -jnp.inf)
    p = jax.nn.softmax(logits, axis=-1).astype(v.dtype)
    o = jnp.einsum('bqngj,bqjnd->bqngd', p, vg)
    return o.reshape(bn, qn, H_A, DH_A)


def _mla_block(q, q_pos, k, v, k_pos):
    s = jnp.einsum('bqhd,bkhd->bhqk', q, k).astype(jnp.float32) * (NOPE + ROPE_DIM) ** -0.5
    s = jnp.where(_chunk_causal(q_pos, k_pos), s, -jnp.inf)
    p = jax.nn.softmax(s, axis=-1).astype(v.dtype)
    return jnp.einsum('bhqk,bkhd->bqhd', p, v)


def _band_attn_prompt(q, k, v, rel_tab):
    bn, s, h, d = q.shape
    nc = s // CHUNK
    band = (C_BACK + 1) * CHUNK
    padw = C_BACK * CHUNK
    kp = jnp.pad(k, ((0, 0), (padw, 0), (0, 0), (0, 0)))
    vp = jnp.pad(v, ((0, 0), (padw, 0), (0, 0), (0, 0)))
    off = jnp.arange(band) - padw
    bias = rel_tab[_rel_index(jnp.arange(CHUNK)[:, None] - off[None, :])]
    bias = bias.transpose(2, 0, 1).astype(jnp.float32)
    qc = jnp.swapaxes(q.reshape(bn, nc, CHUNK, h, d), 0, 1)

    def one_chunk(args):
        qb, c = args
        kb = lax.dynamic_slice_in_dim(kp, c * CHUNK, band, axis=1)
        vb = lax.dynamic_slice_in_dim(vp, c * CHUNK, band, axis=1)
        lg = jnp.einsum('bqhd,bkhd->bhqk', qb, kb).astype(jnp.float32) * DH_C ** -0.5 + bias
        lg = jnp.where((c * CHUNK + off) >= 0, lg, -jnp.inf)
        pr = jax.nn.softmax(lg, axis=-1).astype(vb.dtype)
        return jnp.einsum('bhqk,bkhd->bqhd', pr, vb)

    out = lax.map(one_chunk, (qc, jnp.arange(nc)))
    return jnp.swapaxes(out, 0, 1).reshape(bn, s, h, d)


def _band_attn_step(q, k, v, q_pos, k_pos, rel_tab):
    s = jnp.einsum('bqhd,bkhd->bhqk', q, k).astype(jnp.float32) * DH_C ** -0.5
    s = s + rel_tab[_rel_index(q_pos[:, None] - k_pos[None, :])].transpose(2, 0, 1).astype(jnp.float32)
    dc = q_pos[:, None] // CHUNK - k_pos[None, :] // CHUNK
    s = jnp.where((dc >= 0) & (dc <= C_BACK), s, -jnp.inf)
    p = jax.nn.softmax(s, axis=-1).astype(v.dtype)
    return jnp.einsum('bhqk,bkhd->bqhd', p, v)


def _spatial_gate(vn, w_s, b_s):
    bn, s, _ = vn.shape
    n = min(s, D_CHUNK)
    w = w_s[:, :n, :n] * jnp.tril(jnp.ones((n, n), w_s.dtype))
    vc = vn.reshape(bn, s // n, n, G_D, DG_D)
    sg = jnp.einsum('gts,bcsgd->bctgd', w, vc) + b_s[:, :n].T[:, :, None]
    return sg.reshape(bn, s, W_D)


def _even_mixer(xn, pos, past, w_in, a_qn, a_kn, t5_bias, b_qln, b_kvln, b_wuq, b_wukv, b_qn, b_kn, w_out):
    bn, s, _ = xn.shape
    aq, ak, av, ag, iq, ik, iw, bcq, bckv, bkpe, bg = _split(xn @ w_in, EVEN_SPLITS)
    aq = _rmsnorm(aq.reshape(bn, s, H_A, DH_A), a_qn)
    ak = _rmsnorm(ak.reshape(bn, s, HKV_A, DH_A), a_kn)
    av = av.reshape(bn, s, HKV_A, DH_A)
    iq = iq.reshape(bn, s, H_IDX, D_IDX)
    iw = iw * (H_IDX ** -0.5 * D_IDX ** -0.5)
    qb = (_rmsnorm(bcq, b_qln) @ b_wuq).reshape(bn, s, H_B, NOPE + ROPE_DIM)
    qb = jnp.concatenate([qb[..., :NOPE], _rope(qb[..., NOPE:], pos)], -1)
    qb = _rmsnorm(qb, b_qn)
    ckv = _rmsnorm(bckv, b_kvln)
    kpe = _rope(bkpe[:, :, None, :], pos)[:, :, 0]
    new = (ak, av, ik, ckv, kpe)
    if past is None:
        k_a, v_a, ik_all, ckv_all, kpe_all = new
        k_pos = pos
    else:
        k_a, v_a, ik_all, ckv_all, kpe_all = [jnp.concatenate([c, n_], 1) for c, n_ in zip(past, new)]
        k_pos = jnp.concatenate([jnp.arange(past[0].shape[1], dtype=jnp.int32), pos])
    L = k_pos.shape[0]
    ksel = min(TOPK_MAX, L // 4)
    o_a = _sweep_queries(
        lambda q_, qi_, wi_, qp_: _dsa_block(q_, qi_, wi_, qp_, k_a, v_a, ik_all, k_pos, t5_bias, ksel),
        (aq, iq, iw), pos)
    kv = (ckv_all @ b_wukv).reshape(bn, L, H_B, NOPE + V_DIM)
    kb = jnp.concatenate([kv[..., :NOPE],
                          jnp.broadcast_to(kpe_all[:, :, None, :], (bn, L, H_B, ROPE_DIM))], -1)
    kb = _rmsnorm(kb, b_kn)
    vb = kv[..., NOPE:]
    o_b = _sweep_queries(lambda q_, qp_: _mla_block(q_, qp_, kb, vb, k_pos), (qb,), pos)
    o = jnp.concatenate([o_a.reshape(bn, s, -1) * jax.nn.silu(ag),
                         o_b.reshape(bn, s, -1) * jax.nn.silu(bg)], -1)
    return o @ w_out, new


def _odd_mixer(xn, pos, past, w_in, c_qn, c_kn, c_rel, d_g, d_b, d_ws, d_bs, w_out):
    bn, s, _ = xn.shape
    cq, ck, cv, cg, du, dv, dg = _split(xn @ w_in, ODD_SPLITS)
    cq = _rmsnorm(cq.reshape(bn, s, H_C, DH_C), c_qn)
    ck = _rmsnorm(ck.reshape(bn, s, H_C, DH_C), c_kn)
    cv = cv.reshape(bn, s, H_C, DH_C)
    if past is None:
        o_c = _band_attn_prompt(cq, ck, cv, c_rel)
        keep = min(C_PAST, s)
        c_new = (ck[:, s - keep:], cv[:, s - keep:])
    else:
        nc = past[0].shape[1]
        ck_all = jnp.concatenate([past[0], ck], 1)
        cv_all = jnp.concatenate([past[1], cv], 1)
        k_pos = jnp.concatenate([pos[0] - nc + jnp.arange(nc, dtype=jnp.int32), pos])
        o_c = _band_attn_step(cq, ck_all, cv_all, pos, k_pos, c_rel)
        c_new = (ck, cv)
    dvn = _layernorm(dv, d_g, d_b)
    o_d = du * _spatial_gate(dvn, d_ws, d_bs)
    o = jnp.concatenate([o_c.reshape(bn, s, -1) * jax.nn.silu(cg), o_d * jax.nn.silu(dg)], -1)
    return o @ w_out, c_new, dvn


def _ple(h, p, w_proj, w_gate):
    return h + jax.nn.sigmoid(_rms(h) @ w_gate) * (p @ w_proj)


def setup_inputs(seed: int = 0) -> dict:
    key = jax.random.key(seed)
    ks = iter(jax.random.split(key, 40))

    def nrm(shape, scale=1.0):
        return scale * jax.random.normal(next(ks), shape, jnp.float32)

    def gain(shape):
        return 1.0 + 0.02 * nrm(shape)

    c_cache = min(C_PAST, PAST_LEN)
    return {
        'x_prompt': nrm((BATCH, SEQ, D_MODEL)),
        'x_sample': nrm((DEC_BATCH, DEC_SEQ, D_MODEL)),
        'cache_a_k': nrm((N_EVEN, DEC_BATCH, PAST_LEN, HKV_A, DH_A)),
        'cache_a_v': nrm((N_EVEN, DEC_BATCH, PAST_LEN, HKV_A, DH_A)),
        'cache_a_idx_k': nrm((N_EVEN, DEC_BATCH, PAST_LEN, D_IDX)),
        'cache_b_ckv': nrm((N_EVEN, DEC_BATCH, PAST_LEN, KV_LORA)),
        'cache_b_kpe': nrm((N_EVEN, DEC_BATCH, PAST_LEN, ROPE_DIM)),
        'cache_c_k': nrm((N_ODD, DEC_BATCH, c_cache, H_C, DH_C)),
        'cache_c_v': nrm((N_ODD, DEC_BATCH, c_cache, H_C, DH_C)),
        'p_prompt': nrm((DEPTH, BATCH, SEQ, PLE_DIM)),
        'p_sample': nrm((DEPTH, DEC_BATCH, DEC_SEQ, PLE_DIM)),
        'ln_g': gain((DEPTH, D_MODEL)),
        'w_in_even': nrm((N_EVEN, D_MODEL, EVEN_IN), D_MODEL ** -0.5),
        'a_q_norm': gain((N_EVEN, DH_A)),
        'a_k_norm': gain((N_EVEN, DH_A)),
        't5_bias': nrm((T5_BUCKETS, H_A), 0.5),
        'b_q_lora_norm': gain((N_EVEN, Q_LORA)),
        'b_kv_lora_norm': gain((N_EVEN, KV_LORA)),
        'b_w_uq': nrm((N_EVEN, Q_LORA, H_B * (NOPE + ROPE_DIM)), Q_LORA ** -0.5),
        'b_w_ukv': nrm((N_EVEN, KV_LORA, H_B * (NOPE + V_DIM)), KV_LORA ** -0.5),
        'b_q_norm': gain((N_EVEN, NOPE + ROPE_DIM)),
        'b_k_norm': gain((N_EVEN, NOPE + ROPE_DIM)),
        'w_out_even': nrm((N_EVEN, W_OUT_EVEN, D_MODEL), W_OUT_EVEN ** -0.5),
        'w_in_odd': nrm((N_ODD, D_MODEL, ODD_IN), D_MODEL ** -0.5),
        'c_q_norm': gain((N_ODD, DH_C)),
        'c_k_norm': gain((N_ODD, DH_C)),
        'c_rel_bias': nrm((N_ODD, REL_SIZE, H_C), 0.5),
        'd_ln_g': gain((N_ODD, W_D)),
        'd_ln_b': nrm((N_ODD, W_D), 0.02),
        'd_w_s': nrm((N_ODD, G_D, D_CHUNK, D_CHUNK), D_CHUNK ** -0.5),
        'd_b_s': 1.0 + nrm((N_ODD, G_D, D_CHUNK), 0.1),
        'w_out_odd': nrm((N_ODD, W_OUT_ODD, D_MODEL), W_OUT_ODD ** -0.5),
        'ple_proj': nrm((DEPTH, PLE_DIM, D_MODEL), PLE_DIM ** -0.5),
        'ple_gate': nrm((DEPTH, D_MODEL, D_MODEL), D_MODEL ** -0.5),
    }


def reference(x_prompt, x_sample, cache_a_k, cache_a_v, cache_a_idx_k, cache_b_ckv, cache_b_kpe,
              cache_c_k, cache_c_v, p_prompt, p_sample, ln_g, w_in_even, a_q_norm, a_k_norm, t5_bias,
              b_q_lora_norm, b_kv_lora_norm, b_w_uq, b_w_ukv, b_q_norm, b_k_norm, w_out_even,
              w_in_odd, c_q_norm, c_k_norm, c_rel_bias, d_ln_g, d_ln_b, d_w_s, d_b_s, w_out_odd,
              ple_proj, ple_gate):
    pos_p = jnp.arange(x_prompt.shape[1], dtype=jnp.int32)
    pos_s = PAST_LEN + jnp.arange(x_sample.shape[1], dtype=jnp.int32)
    hp, hs = x_prompt, x_sample
    ev_p, ev_s, od_p, od_s, dv_s = [], [], [], [], []
    for i in range(DEPTH):
        j = i // 2
        if i % 2 == 0:
            w = (w_in_even[j], a_q_norm[j], a_k_norm[j], t5_bias, b_q_lora_norm[j], b_kv_lora_norm[j],
                 b_w_uq[j], b_w_ukv[j], b_q_norm[j], b_k_norm[j], w_out_even[j])
            mp, sp = _even_mixer(_rmsnorm(hp, ln_g[i]), pos_p, None, *w)
            past = (cache_a_k[j], cache_a_v[j], cache_a_idx_k[j], cache_b_ckv[j], cache_b_kpe[j])
            ms, ss = _even_mixer(_rmsnorm(hs, ln_g[i]), pos_s, past, *w)
            ev_p.append(sp)
            ev_s.append(ss)
        else:
            w = (w_in_odd[j], c_q_norm[j], c_k_norm[j], c_rel_bias[j], d_ln_g[j], d_ln_b[j],
                 d_w_s[j], d_b_s[j], w_out_odd[j])
            mp, sp, _ = _odd_mixer(_rmsnorm(hp, ln_g[i]), pos_p, None, *w)
            ms, ss, dvs = _odd_mixer(_rmsnorm(hs, ln_g[i]), pos_s, (cache_c_k[j], cache_c_v[j]), *w)
            od_p.append(sp)
            od_s.append(ss)
            dv_s.append(dvs)
        hp = _ple(hp + mp, p_prompt[i], ple_proj[i], ple_gate[i])
        hs = _ple(hs + ms, p_sample[i], ple_proj[i], ple_gate[i])
    st = lambda lst, n: jnp.stack([e[n] for e in lst], 0)
    new_a_k_p, new_a_v_p, new_a_idx_k_p = st(ev_p, 0), st(ev_p, 1), st(ev_p, 2)
    new_b_ckv_p, new_b_kpe_p = st(ev_p, 3), st(ev_p, 4)
    new_c_k_p, new_c_v_p = st(od_p, 0), st(od_p, 1)
    new_a_k_s, new_a_v_s, new_a_idx_k_s = st(ev_s, 0), st(ev_s, 1), st(ev_s, 2)
    new_b_ckv_s, new_b_kpe_s = st(ev_s, 3), st(ev_s, 4)
    new_c_k_s, new_c_v_s = st(od_s, 0), st(od_s, 1)
    new_d_v_s = jnp.stack(dv_s, 0)
    return (hp, hs, new_a_k_p, new_a_v_p, new_a_idx_k_p, new_b_ckv_p, new_b_kpe_p, new_c_k_p, new_c_v_p,
            new_a_k_s, new_a_v_s, new_a_idx_k_s, new_b_ckv_s, new_b_kpe_s, new_c_k_s, new_c_v_s, new_d_v_s)
```

```python
import functools
import math

import numpy as np
import jax
import jax.numpy as jnp
from jax import lax
from jax.experimental import pallas as pl
from jax.experimental.pallas import tpu as pltpu

F32 = jnp.float32
BF16 = jnp.bfloat16
INT_MIN = -2 ** 31

D_MODEL = 2048
CHUNK = 64
CHUNK_SHIFT = 6
LANE = 128
H_A, HKV_A, GROUP_A, DH_A = 8, 2, 4, 128
H_IDX, D_IDX = 16, 64
TOPK_MAX = 256
T5_BUCKETS, T5_MAX_DIST = 32, 128
H_B, Q_LORA, KV_LORA, NOPE, ROPE_DIM, V_DIM = 8, 512, 256, 128, 64, 128
ROPE_BASE = 10000.0
QK_B = NOPE + ROPE_DIM
H_C, DH_C, C_BACK, REL_CLIP = 8, 128, 8, 128
W_D, G_D, DG_D, D_CHUNK = 1024, 8, 128, 128
PLE_DIM = 256

EVEN_SPLITS = (H_A * DH_A, HKV_A * DH_A, HKV_A * DH_A, H_A * DH_A, H_IDX * D_IDX, D_IDX, H_IDX,
               Q_LORA, KV_LORA, ROPE_DIM, H_B * V_DIM)
E_AQ, E_AG, E_IQ, E_BG, E_BCQ, E_AK, E_AV, E_CKV, E_IDX, E_KPE, E_END = (
    0, 1024, 2048, 3072, 4096, 4608, 4864, 5120, 5376, 5504, 5632)
O_CQ, O_CK, O_CV, O_CG, O_DU, O_DV, O_DG, O_END = 0, 1024, 2048, 3072, 4096, 5120, 6144, 7168

VMEM_LIMIT_BYTES = 56 * 1024 * 1024
NT_DIMS = (((1,), (1,)), ((), ()))


def _params(*sem):
    return pltpu.CompilerParams(dimension_semantics=sem, vmem_limit_bytes=VMEM_LIMIT_BYTES)


def _const_spec(shape):
    zeros = (0,) * len(shape)
    return pl.BlockSpec(shape, lambda *_: zeros, pipeline_mode=pl.Buffered(1))


def _rs(x, n=None, eps=1e-6):
    n = x.shape[-1] if n is None else n
    return lax.rsqrt(jnp.sum(x * x, axis=-1, keepdims=True) / n + eps)


def _silu(x):
    return x * (1.0 / (1.0 + jnp.exp(-x)))


def _softmax_pv(lg, v):
    m = jnp.max(lg, axis=-1, keepdims=True)
    p = jnp.exp(lg - m)
    s = jnp.sum(p, axis=-1, keepdims=True)
    return jnp.dot(p.astype(BF16), v, preferred_element_type=F32) / s


def _norm_mm_kernel(x_ref, g_ref, w_ref, o_ref, xn_ref):
    @pl.when(pl.program_id(1) == 0)
    def _():
        x = x_ref[...]
        xn_ref[...] = (x * _rs(x) * g_ref[...]).astype(BF16)

    o_ref[...] = jnp.dot(xn_ref[...], w_ref[...], preferred_element_type=F32)


def _norm_mm(x, g, w, bm, bn):
    m, d = x.shape
    n = w.shape[1]
    return pl.pallas_call(
        _norm_mm_kernel,
        grid=(m // bm, n // bn),
        in_specs=[pl.BlockSpec((bm, d), lambda i, j: (i, 0)),
                  pl.BlockSpec((1, d), lambda i, j: (0, 0)),
                  pl.BlockSpec((d, bn), lambda i, j: (0, j))],
        out_specs=pl.BlockSpec((bm, bn), lambda i, j: (i, j)),
        out_shape=jax.ShapeDtypeStruct((m, n), F32),
        scratch_shapes=[pltpu.VMEM((bm, d), BF16)],
        compiler_params=_params("parallel", "arbitrary"),
        name="norm_mm",
    )(x, g, w)


def _rope(x, cos, sin):
    return x * cos + pltpu.roll(x, 64, 1) * sin


def _even_prep_kernel(aq_ref, bcq_ref, ak_ref, ckv_ref, kpe_ref, cos_ref, sin_ref,
                      aqn_ref, akn_ref, qln_ref, kvln_ref, wuq_ref, bqn_ref,
                      qa_o, ka_o, qb_o, ckv_o, kpe_o):
    for h in range(H_A):
        x = aq_ref[:, h * DH_A:(h + 1) * DH_A]
        qa_o[:, h * DH_A:(h + 1) * DH_A] = (x * _rs(x) * aqn_ref[...]).astype(BF16)
    for n in range(HKV_A):
        x = ak_ref[:, n * DH_A:(n + 1) * DH_A]
        ka_o[:, n * DH_A:(n + 1) * DH_A] = x * _rs(x) * akn_ref[...]
    c = ckv_ref[...]
    ckv_o[...] = c * _rs(c) * kvln_ref[...]
    cos = cos_ref[...]
    sin = sin_ref[...]
    kpe_o[...] = _rope(kpe_ref[...], cos, sin)
    cq = bcq_ref[...]
    cqn = (cq * _rs(cq) * qln_ref[...]).astype(BF16)
    qb = jnp.dot(cqn, wuq_ref[...], preferred_element_type=F32)
    g = bqn_ref[...]
    for h in range(H_B):
        nope = qb[:, h * 256:h * 256 + 128]
        rot = _rope(qb[:, h * 256 + 128:(h + 1) * 256], cos, sin)
        ss = jnp.sum(nope * nope, -1, keepdims=True) + jnp.sum(rot * rot, -1, keepdims=True)
        r = lax.rsqrt(ss / QK_B + 1e-6)
        qb_o[:, h * 256:h * 256 + 128] = (nope * r * g[:, :128]).astype(BF16)
        qb_o[:, h * 256 + 128:(h + 1) * 256] = (rot * r * g[:, 128:]).astype(BF16)


def _even_prep(z, cos, sin, aqn, akn, qln, kvln, wuq, bqn, bs):
    b, s, _ = z.shape

    def zspec(width, off):
        return pl.BlockSpec((None, bs, width), lambda bi, si: (bi, si, off // width))

    def ospec(width):
        return pl.BlockSpec((None, bs, width), lambda bi, si: (bi, si, 0))

    pos_spec = pl.BlockSpec((bs, LANE), lambda bi, si: (si, 0))
    return pl.pallas_call(
        _even_prep_kernel,
        grid=(b, s // bs),
        in_specs=[zspec(1024, E_AQ), zspec(512, E_BCQ), zspec(256, E_AK), zspec(256, E_CKV),
                  zspec(128, E_KPE), pos_spec, pos_spec,
                  _const_spec((1, DH_A)), _const_spec((1, DH_A)), _const_spec((1, Q_LORA)),
                  _const_spec((1, KV_LORA)), _const_spec((Q_LORA, H_B * 256)), _const_spec((1, 256))],
        out_specs=[ospec(1024), ospec(256), ospec(2048), ospec(256), ospec(128)],
        out_shape=[jax.ShapeDtypeStruct((b, s, 1024), BF16), jax.ShapeDtypeStruct((b, s, 256), F32),
                   jax.ShapeDtypeStruct((b, s, 2048), BF16), jax.ShapeDtypeStruct((b, s, 256), F32),
                   jax.ShapeDtypeStruct((b, s, 128), F32)],
        compiler_params=_params("parallel", "arbitrary"),
        name="even_prep",
    )(z, z, z, z, z, cos, sin, aqn, akn, qln, kvln, wuq, bqn)


def _mla_kv_kernel(ckv_ref, kpe_ref, w_ref, g_ref, kb_o, vb_o):
    kv = jnp.dot(ckv_ref[...].astype(BF16), w_ref[...], preferred_element_type=F32)
    kp = kpe_ref[...]
    skp = jnp.sum(kp * kp, -1, keepdims=True)
    g = g_ref[...]
    for h in range(H_B):
        nope = kv[:, h * NOPE:(h + 1) * NOPE]
        r = lax.rsqrt((jnp.sum(nope * nope, -1, keepdims=True) + skp) / QK_B + 1e-6)
        kb_o[:, h * 256:h * 256 + 128] = (nope * r * g[:, :128]).astype(BF16)
        kb_o[:, h * 256 + 128:(h + 1) * 256] = (kp * r * g[:, 128:]).astype(BF16)
    vb_o[...] = kv[:, H_B * NOPE:].astype(BF16)


def _mla_kv(ckv_all, kpe_all, wukv, bkn, bl):
    b, lp, _ = ckv_all.shape
    return pl.pallas_call(
        _mla_kv_kernel,
        grid=(b, lp // bl),
        in_specs=[pl.BlockSpec((None, bl, KV_LORA), lambda bi, li: (bi, li, 0)),
                  pl.BlockSpec((None, bl, LANE), lambda bi, li: (bi, li, 0)),
                  _const_spec((KV_LORA, 2048)), _const_spec((1, 256))],
        out_specs=[pl.BlockSpec((None, bl, 2048), lambda bi, li: (bi, li, 0)),
                   pl.BlockSpec((None, bl, 1024), lambda bi, li: (bi, li, 0))],
        out_shape=[jax.ShapeDtypeStruct((b, lp, 2048), BF16), jax.ShapeDtypeStruct((b, lp, 1024), BF16)],
        compiler_params=_params("parallel", "arbitrary"),
        name="mla_kv",
    )(ckv_all, kpe_all, wukv, bkn)


def _key_chunk(lp):
    for c in (512, 384, 256, 128):
        if lp % c == 0:
            return c
    raise ValueError(lp)


def _dsa_kernel(qa_ref, iq_ref, sa_ref, ag_ref, ka_ref, va_ref, kidx_ref, bias_ref, o_ref, key_ref,
                *, bq, lp, l_true, q0, ksel, nd):
    i = pl.program_id(1)
    q_start = q0 + i * bq
    qpos = q_start + lax.broadcasted_iota(jnp.int32, (bq, 1), 0)
    ck = _key_chunk(lp)

    iq = iq_ref[...].astype(BF16)
    a = jnp.concatenate([iq[:, p * LANE:(p + 1) * LANE] for p in range(H_IDX // 2)], axis=0)
    wi = sa_ref[...] * (H_IDX ** -0.5 * D_IDX ** -0.5)
    for c0 in range(0, lp, ck):
        kk = kidx_ref[c0:c0 + ck, :]
        lane = lax.broadcasted_iota(jnp.int32, kk.shape, 1)
        k_lo = jnp.where(lane < D_IDX, kk, 0.0).astype(BF16)
        k_hi = jnp.where(lane >= D_IDX, pltpu.roll(kk, D_IDX, 1), 0.0).astype(BF16)
        s_lo = lax.dot_general(a, k_lo, NT_DIMS, preferred_element_type=F32)
        s_hi = lax.dot_general(a, k_hi, NT_DIMS, preferred_element_type=F32)
        sc = jnp.zeros((bq, ck), F32)
        for p in range(H_IDX // 2):
            w0 = wi[:, D_IDX + 2 * p:D_IDX + 2 * p + 1]
            w1 = wi[:, D_IDX + 2 * p + 1:D_IDX + 2 * p + 2]
            sc = sc + w0 * jnp.maximum(s_lo[p * bq:(p + 1) * bq], 0.0)
            sc = sc + w1 * jnp.maximum(s_hi[p * bq:(p + 1) * bq], 0.0)
        kpos = c0 + lax.broadcasted_iota(jnp.int32, (1, ck), 1)
        valid = ((kpos >> CHUNK_SHIFT) <= (qpos >> CHUNK_SHIFT)) & (kpos < l_true)
        bits = lax.bitcast_convert_type(sc, jnp.int32)
        key = jnp.where(bits < 0, bits ^ 0x7FFFFFFF, bits)
        key = jnp.where(bits == INT_MIN, 0, key)
        key_ref[:, c0:c0 + ck] = jnp.where(valid, key, INT_MIN)

    def bit_pass(it, t_u):
        cand = t_u | lax.shift_left(jnp.int32(1), 31 - it)
        cnt = jnp.sum((key_ref[...] >= (cand ^ INT_MIN)).astype(F32), axis=-1, keepdims=True)
        return jnp.where(cnt >= ksel, cand, t_u)

    t_s = lax.fori_loop(0, 32, bit_pass, jnp.zeros((bq, 1), jnp.int32)) ^ INT_MIN
    key = key_ref[...]
    sel = (key >= t_s) & (key > INT_MIN)

    d_base = (nd - 1) - (q0 // LANE + (i if bq == LANE else 0))
    nk = lp // LANE
    for n in range(HKV_A):
        q4 = jnp.concatenate([qa_ref[:, (n * GROUP_A + g) * DH_A:(n * GROUP_A + g + 1) * DH_A]
                              for g in range(GROUP_A)], axis=0)
        k_n = ka_ref[:, n * DH_A:(n + 1) * DH_A].astype(BF16)
        v_n = va_ref[:, n * DH_A:(n + 1) * DH_A].astype(BF16)
        lg4 = lax.dot_general(q4, k_n, NT_DIMS, preferred_element_type=F32) * DH_A ** -0.5
        ps, ss = [], []
        for g in range(GROUP_A):
            bias = jnp.concatenate([bias_ref[n * GROUP_A + g, jnp.minimum(d_base + j, nd - 1)]
                                    for j in range(nk)], axis=1)
            lg = jnp.where(sel, lg4[g * bq:(g + 1) * bq] + bias, -jnp.inf)
            p = jnp.exp(lg - jnp.max(lg, axis=-1, keepdims=True))
            ss.append(jnp.sum(p, axis=-1, keepdims=True))
            ps.append(p.astype(BF16))
        o4 = jnp.dot(jnp.concatenate(ps, axis=0), v_n, preferred_element_type=F32)
        for g in range(GROUP_A):
            h = n * GROUP_A + g
            gate = ag_ref[:, h * DH_A:(h + 1) * DH_A]
            o_ref[:, h * DH_A:(h + 1) * DH_A] = (o4[g * bq:(g + 1) * bq] / ss[g] * _silu(gate)).astype(BF16)


def _dsa(qa, z, ka_all, va_all, kidx_all, bias_tiles, *, bq, l_true, q0, ksel):
    b, s, _ = qa.shape
    lp = ka_all.shape[1]
    nq = s // bq
    nd = bias_tiles.shape[1]
    assert q0 % LANE == 0 and (bq == LANE or nq == 1)
    assert nd == (q0 + (nq - 1) * bq) // LANE + 1
    kern = functools.partial(_dsa_kernel, bq=bq, lp=lp, l_true=l_true, q0=q0, ksel=ksel, nd=nd)

    def zspec(width, off):
        return pl.BlockSpec((None, bq, width), lambda bi, qi: (bi, qi, off // width))

    def kspec(width):
        return pl.BlockSpec((None, lp, width), lambda bi, qi: (bi, 0, 0))

    return pl.pallas_call(
        kern,
        grid=(b, nq),
        in_specs=[pl.BlockSpec((None, bq, 1024), lambda bi, qi: (bi, qi, 0)),
                  zspec(1024, E_IQ), zspec(128, E_IDX), zspec(1024, E_AG),
                  kspec(256), kspec(256), kspec(128),
                  _const_spec(bias_tiles.shape)],
        out_specs=pl.BlockSpec((None, bq, 1024), lambda bi, qi: (bi, qi, 0)),
        out_shape=jax.ShapeDtypeStruct((b, s, 1024), BF16),
        scratch_shapes=[pltpu.VMEM((bq, lp), jnp.int32)],
        compiler_params=_params("parallel", "arbitrary"),
        name="dsa",
    )(qa, z, z, z, ka_all, va_all, kidx_all, bias_tiles)


def _mla_kernel(qb_ref, bg_ref, kb_ref, vb_ref, o_ref, *, bq, lp, l_true, q0):
    q_start = q0 + pl.program_id(1) * bq
    qpos = q_start + lax.broadcasted_iota(jnp.int32, (bq, 1), 0)
    kpos = lax.broadcasted_iota(jnp.int32, (1, lp), 1)
    valid = ((kpos >> CHUNK_SHIFT) <= (qpos >> CHUNK_SHIFT)) & (kpos < l_true)
    for h in range(H_B):
        lg = lax.dot_general(qb_ref[:, h * 256:(h + 1) * 256], kb_ref[:, h * 256:(h + 1) * 256], NT_DIMS,
                             preferred_element_type=F32) * QK_B ** -0.5
        lg = jnp.where(valid, lg, -jnp.inf)
        o = _softmax_pv(lg, vb_ref[:, h * V_DIM:(h + 1) * V_DIM])
        gate = bg_ref[:, h * V_DIM:(h + 1) * V_DIM]
        o_ref[:, h * V_DIM:(h + 1) * V_DIM] = (o * _silu(gate)).astype(BF16)


def _mla(qb, z, kb, vb, *, bq, l_true, q0):
    b, s, _ = qb.shape
    lp = kb.shape[1]
    kern = functools.partial(_mla_kernel, bq=bq, lp=lp, l_true=l_true, q0=q0)
    return pl.pallas_call(
        kern,
        grid=(b, s // bq),
        in_specs=[pl.BlockSpec((None, bq, 2048), lambda bi, qi: (bi, qi, 0)),
                  pl.BlockSpec((None, bq, 1024), lambda bi, qi: (bi, qi, E_BG // 1024)),
                  pl.BlockSpec((None, lp, 2048), lambda bi, qi: (bi, 0, 0)),
                  pl.BlockSpec((None, lp, 1024), lambda bi, qi: (bi, 0, 0))],
        out_specs=pl.BlockSpec((None, bq, 1024), lambda bi, qi: (bi, qi, 0)),
        out_shape=jax.ShapeDtypeStruct((b, s, 1024), BF16),
        compiler_params=_params("parallel", "arbitrary"),
        name="mla",
    )(qb, z, kb, vb)


def _out_ple_kernel(h_ref, oa_ref, ob_ref, p_ref, wo_ref, wg_ref, wp_ref, o_ref):
    half = oa_ref.shape[-1]
    h1 = (h_ref[...]
          + jnp.dot(oa_ref[...], wo_ref[:half, :], preferred_element_type=F32)
          + jnp.dot(ob_ref[...], wo_ref[half:, :], preferred_element_type=F32))
    r = (h1 * _rs(h1)).astype(BF16)
    gate = 1.0 / (1.0 + jnp.exp(-jnp.dot(r, wg_ref[...], preferred_element_type=F32)))
    o_ref[...] = h1 + gate * jnp.dot(p_ref[...].astype(BF16), wp_ref[...], preferred_element_type=F32)


def _out_ple(h, oa, ob, p, wo, wg, wp, bm):
    m, d = h.shape
    half = oa.shape[1]

    def rows(width):
        return pl.BlockSpec((bm, width), lambda i: (i, 0))

    return pl.pallas_call(
        _out_ple_kernel,
        grid=(m // bm,),
        in_specs=[rows(d), rows(half), rows(half), rows(PLE_DIM),
                  _const_spec((2 * half, d)), _const_spec((d, d)), _const_spec((PLE_DIM, d))],
        out_specs=rows(d),
        out_shape=jax.ShapeDtypeStruct((m, d), F32),
        compiler_params=_params("parallel"),
        name="out_ple",
    )(h, oa, ob, p, wo, wg, wp)


def _odd_prep_kernel(cq_ref, ck_ref, cv_ref, dv_ref, cqn_ref, ckn_ref, dg_ref, db_ref,
                     qc_o, kc_o, kcb_o, vcb_o, dvn_o):
    for h in range(H_C):
        sl = slice(h * DH_C, (h + 1) * DH_C)
        x = cq_ref[:, sl]
        qc_o[:, sl] = (x * _rs(x) * cqn_ref[...]).astype(BF16)
        x = ck_ref[:, sl]
        kn = x * _rs(x) * ckn_ref[...]
        kc_o[:, sl] = kn
        kcb_o[:, sl] = kn.astype(BF16)
    vcb_o[...] = cv_ref[...].astype(BF16)
    dv = dv_ref[...]
    xc = dv - jnp.mean(dv, -1, keepdims=True)
    var = jnp.mean(xc * xc, -1, keepdims=True)
    dvn_o[...] = xc * lax.rsqrt(var + 1e-5) * dg_ref[...] + db_ref[...]


def _odd_prep(z, cqn, ckn, dg, db, bs):
    b, s, _ = z.shape

    def zspec(off):
        return pl.BlockSpec((None, bs, 1024), lambda bi, si: (bi, si, off // 1024))

    ospec = pl.BlockSpec((None, bs, 1024), lambda bi, si: (bi, si, 0))
    return pl.pallas_call(
        _odd_prep_kernel,
        grid=(b, s // bs),
        in_specs=[zspec(O_CQ), zspec(O_CK), zspec(O_CV), zspec(O_DV),
                  _const_spec((1, DH_C)), _const_spec((1, DH_C)), _const_spec((1, W_D)), _const_spec((1, W_D))],
        out_specs=[ospec] * 5,
        out_shape=[jax.ShapeDtypeStruct((b, s, 1024), BF16), jax.ShapeDtypeStruct((b, s, 1024), F32),
                   jax.ShapeDtypeStruct((b, s, 1024), BF16), jax.ShapeDtypeStruct((b, s, 1024), BF16),
                   jax.ShapeDtypeStruct((b, s, 1024), F32)],
        compiler_params=_params("parallel", "arbitrary"),
        name="odd_prep",
    )(z, z, z, z, cqn, ckn, dg, db)


BAND_TILES = C_BACK * CHUNK // LANE + 1
BAND_W = BAND_TILES * LANE


def _band_kernel(q_ref, cg_ref, k_ref, v_ref, bias_ref, o_ref, *, bq, sliding, q0, k0, k_end):
    i = pl.program_id(1)
    q_start = q0 + i * bq
    if sliding:
        tiles = [i - (BAND_TILES - 1) + t for t in range(BAND_TILES)]
        win_start = (i - (BAND_TILES - 1)) * LANE
        rows = [pl.ds(pl.multiple_of(jnp.maximum(j, 0) * LANE, LANE), LANE) for j in tiles]
    else:
        win_start = k0
        rows = [pl.ds(t * LANE, LANE) for t in range(BAND_TILES)]
    kw = jnp.concatenate([k_ref[r, :] for r in rows], axis=0)
    vw = jnp.concatenate([v_ref[r, :] for r in rows], axis=0)
    qpos = q_start + lax.broadcasted_iota(jnp.int32, (bq, 1), 0)
    kpos = win_start + lax.broadcasted_iota(jnp.int32, (1, BAND_W), 1)
    dc = (qpos >> CHUNK_SHIFT) - (kpos >> CHUNK_SHIFT)
    valid = (dc >= 0) & (dc <= C_BACK) & (kpos >= 0) & (kpos < k_end)
    for h in range(H_C):
        sl = slice(h * DH_C, (h + 1) * DH_C)
        lg = lax.dot_general(q_ref[:, sl], kw[:, sl], NT_DIMS, preferred_element_type=F32) * DH_C ** -0.5
        lg = jnp.where(valid, lg + bias_ref[h], -jnp.inf)
        o = _softmax_pv(lg, vw[:, sl])
        o_ref[:, sl] = (o * _silu(cg_ref[:, sl])).astype(BF16)


def _band(qc, z, kcb, vcb, bias, *, bq, sliding, q0, k0, k_end):
    b, s, _ = qc.shape
    lk = kcb.shape[1]
    if sliding:
        assert bq == LANE and q0 == 0 and k0 == 0
    else:
        assert lk == BAND_W and s == bq
    kern = functools.partial(_band_kernel, bq=bq, sliding=sliding, q0=q0, k0=k0, k_end=k_end)
    return pl.pallas_call(
        kern,
        grid=(b, s // bq),
        in_specs=[pl.BlockSpec((None, bq, 1024), lambda bi, qi: (bi, qi, 0)),
                  pl.BlockSpec((None, bq, 1024), lambda bi, qi: (bi, qi, O_CG // 1024)),
                  pl.BlockSpec((None, lk, 1024), lambda bi, qi: (bi, 0, 0)),
                  pl.BlockSpec((None, lk, 1024), lambda bi, qi: (bi, 0, 0)),
                  _const_spec(bias.shape)],
        out_specs=pl.BlockSpec((None, bq, 1024), lambda bi, qi: (bi, qi, 0)),
        out_shape=jax.ShapeDtypeStruct((b, s, 1024), BF16),
        compiler_params=_params("parallel", "arbitrary"),
        name="band",
    )(qc, z, kcb, vcb, bias)


def _sgate_kernel(dvn_ref, du_ref, dg_ref, ws_ref, bs_ref, o_ref, *, n):
    row = lax.broadcasted_iota(jnp.int32, (n, n), 0)
    col = lax.broadcasted_iota(jnp.int32, (n, n), 1)
    tril = col <= row
    for g in range(G_D):
        sl = slice(g * DG_D, (g + 1) * DG_D)
        w = jnp.where(tril, ws_ref[g], 0.0).astype(BF16)
        sg = jnp.dot(w, dvn_ref[:, sl].astype(BF16), preferred_element_type=F32) + bs_ref[:, g:g + 1]
        o_ref[:, sl] = (du_ref[:, sl] * sg * _silu(dg_ref[:, sl])).astype(BF16)


def _sgate(dvn, z, ws, bs_t, n):
    b, s, _ = dvn.shape

    def zspec(off):
        return pl.BlockSpec((None, n, 1024), lambda bi, ci: (bi, ci, off // 1024))

    return pl.pallas_call(
        functools.partial(_sgate_kernel, n=n),
        grid=(b, s // n),
        in_specs=[pl.BlockSpec((None, n, 1024), lambda bi, ci: (bi, ci, 0)), zspec(O_DU), zspec(O_DG),
                  _const_spec((G_D, n, n)), _const_spec((n, G_D))],
        out_specs=pl.BlockSpec((None, n, 1024), lambda bi, ci: (bi, ci, 0)),
        out_shape=jax.ShapeDtypeStruct((b, s, 1024), BF16),
        compiler_params=_params("parallel", "arbitrary"),
        name="sgate",
    )(dvn, z, z, ws, bs_t)


def _rope_tables(pos):
    half = ROPE_DIM // 2
    freq = ROPE_BASE ** (-jnp.arange(half, dtype=F32) / half)
    ang = pos.astype(F32)[:, None] * freq[None, :]
    cos, sin = jnp.cos(ang), jnp.sin(ang)
    z = jnp.zeros_like(cos)
    return jnp.concatenate([cos, z, cos, z], 1), jnp.concatenate([-sin, z, sin, z], 1)


def _rope_lanes(x):
    half = ROPE_DIM // 2
    z = jnp.zeros(x.shape[:-1] + (half,), x.dtype)
    return jnp.concatenate([x[..., :half], z, x[..., half:], z], -1)


def _rope_unlanes(x):
    half = ROPE_DIM // 2
    return jnp.concatenate([x[..., :half], x[..., 2 * half:3 * half]], -1)


def _t5_bucket_np(rel):
    nb = T5_BUCKETS // 2
    max_exact = nb // 2
    n = np.abs(rel)
    nf = np.maximum(n, 1).astype(np.float64)
    large = max_exact + (np.log(nf / max_exact) / math.log(T5_MAX_DIST / max_exact) * (nb - max_exact)).astype(np.int64)
    large = np.minimum(large, nb - 1)
    return np.where(rel > 0, nb, 0) + np.where(n < max_exact, n, large)


def _t5_tiles(t5_bias, bq, nd):
    d = np.arange(nd) - (nd - 1)
    rel = d[:, None, None] * LANE + np.arange(LANE)[None, None, :] - np.arange(bq)[None, :, None]
    tiles = t5_bias[_t5_bucket_np(rel)]
    return jnp.transpose(tiles, (3, 0, 1, 2))


def _band_bias(rel_tab, bq, qk_off):
    rel = qk_off + np.arange(bq)[:, None] - np.arange(BAND_W)[None, :]
    idx = np.clip(rel, -(CHUNK - 1), REL_CLIP) + (CHUNK - 1)
    return jnp.transpose(rel_tab[idx], (2, 0, 1))


def _even_weights(w_in, b_wuq, b_wukv, b_qn, b_kn):
    d = w_in.shape[0]
    offs = np.cumsum((0,) + EVEN_SPLITS)
    aq, ak, av, ag, iq, ik, iw, bcq, bckv, bkpe, bg = [w_in[:, offs[t]:offs[t + 1]] for t in range(11)]
    slab_idx = jnp.concatenate([ik, iw, jnp.zeros((d, LANE - D_IDX - H_IDX), w_in.dtype)], 1)
    w = jnp.concatenate([aq, ag, iq, bg, bcq, ak, av, bckv, slab_idx, _rope_lanes(bkpe)], 1).astype(BF16)
    uq = b_wuq.reshape(Q_LORA, H_B, QK_B)
    uq = jnp.concatenate([uq[..., :NOPE], _rope_lanes(uq[..., NOPE:])], -1).reshape(Q_LORA, H_B * 256).astype(BF16)
    ukv = b_wukv.reshape(KV_LORA, H_B, NOPE + V_DIM)
    ukv = jnp.concatenate([ukv[..., :NOPE].reshape(KV_LORA, H_B * NOPE),
                           ukv[..., NOPE:].reshape(KV_LORA, H_B * V_DIM)], 1).astype(BF16)
    pad_gain = lambda g: jnp.concatenate([g[:NOPE], _rope_lanes(g[NOPE:])])[None, :]
    return w, uq, ukv, pad_gain(b_qn), pad_gain(b_kn)


def _pad_rows(x, lp):
    return jnp.pad(x, ((0, 0), (0, lp - x.shape[1]), (0, 0)))


def _even_layer(h, p, past, q0, ln_g, w_in, uq, ukv, a_qn, a_kn, t5_bias, b_qln, b_kvln, bqn, bkn, wo, wg, wp):
    b, s, d = h.shape
    m = b * s
    z = _norm_mm(h.reshape(m, d), ln_g[None, :], w_in, min(m, 1024), 512).reshape(b, s, E_END)
    cos, sin = _rope_tables(q0 + jnp.arange(s, dtype=jnp.int32))
    bs = min(s, 256)
    qa, ka, qb, ckv, kpe_l = _even_prep(z, cos, sin, a_qn[None, :], a_kn[None, :], b_qln[None, :], b_kvln[None, :],
                                        uq, bqn, bs)
    av = z[..., E_AV:E_AV + 256]
    kidx = z[..., E_IDX:E_IDX + LANE]
    new = (ka.reshape(b, s, HKV_A, DH_A), av.reshape(b, s, HKV_A, DH_A), kidx[..., :D_IDX], ckv, _rope_unlanes(kpe_l))
    if past is None:
        l_true = s
        ka_all, va_all, kidx_all, ckv_all, kpe_all = ka, av, kidx, ckv, kpe_l
    else:
        c_k, c_v, c_ik, c_ckv, c_kpe = past
        pl_ = c_k.shape[1]
        l_true = pl_ + s
        lp = -(-l_true // LANE) * LANE
        cat = lambda c, n_: _pad_rows(jnp.concatenate([c, n_], 1), lp)
        ka_all = cat(c_k.reshape(b, pl_, 256), ka)
        va_all = cat(c_v.reshape(b, pl_, 256), av)
        kidx_all = cat(jnp.pad(c_ik, ((0, 0), (0, 0), (0, LANE - D_IDX))), kidx)
        ckv_all = cat(c_ckv, ckv)
        kpe_all = cat(_rope_lanes(c_kpe), kpe_l)
    lp = ka_all.shape[1]
    ksel = min(TOPK_MAX, l_true // 4)
    bq = min(s, LANE)
    nd = (q0 + s - bq) // LANE + 1
    o_a = _dsa(qa, z, ka_all, va_all, kidx_all, _t5_tiles(t5_bias, bq, nd), bq=bq, l_true=l_true, q0=q0, ksel=ksel)
    kb, vb = _mla_kv(ckv_all, kpe_all, ukv, bkn, _key_chunk(lp))
    o_b = _mla(qb, z, kb, vb, bq=min(s, 512), l_true=l_true, q0=q0)
    y = _out_ple(h.reshape(m, d), o_a.reshape(m, -1), o_b.reshape(m, -1), p.reshape(m, -1), wo, wg, wp, min(m, 256))
    return y.reshape(b, s, d), new


def _odd_layer(h, p, past, q0, ln_g, w_in, c_qn, c_kn, c_rel, d_g, d_b, d_ws, d_bs, wo, wg, wp):
    b, s, d = h.shape
    m = b * s
    z = _norm_mm(h.reshape(m, d), ln_g[None, :], w_in, min(m, 1024), 512).reshape(b, s, O_END)
    qc, kc, kcb, vcb, dvn = _odd_prep(z, c_qn[None, :], c_kn[None, :], d_g[None, :], d_b[None, :], min(s, 256))
    cv = z[..., O_CV:O_CV + 1024]
    if past is None:
        keep = min(C_BACK * CHUNK, s)
        c_new = (kc[:, s - keep:].reshape(b, keep, H_C, DH_C), cv[:, s - keep:].reshape(b, keep, H_C, DH_C))
        bias = _band_bias(c_rel, LANE, C_BACK * CHUNK)
        o_c = _band(qc, z, kcb, vcb, bias, bq=LANE, sliding=True, q0=0, k0=0, k_end=s)
    else:
        nc = past[0].shape[1]
        c_new = (kc.reshape(b, s, H_C, DH_C), cv.reshape(b, s, H_C, DH_C))
        cat = lambda c, n_: _pad_rows(jnp.concatenate([c.reshape(b, nc, 1024).astype(BF16), n_], 1), BAND_W)
        bias = _band_bias(c_rel, s, nc)
        o_c = _band(qc, z, cat(past[0], kcb), cat(past[1], vcb), bias,
                    bq=s, sliding=False, q0=q0, k0=q0 - nc, k_end=q0 + s)
    n = min(s, D_CHUNK)
    o_d = _sgate(dvn, z, d_ws[:, :n, :n], d_bs[:, :n].T, n)
    y = _out_ple(h.reshape(m, d), o_c.reshape(m, -1), o_d.reshape(m, -1), p.reshape(m, -1), wo, wg, wp, min(m, 256))
    return y.reshape(b, s, d), c_new, dvn


def kernel(x_prompt, x_sample, cache_a_k, cache_a_v, cache_a_idx_k, cache_b_ckv, cache_b_kpe, cache_c_k, cache_c_v, p_prompt, p_sample, ln_g, w_in_even, a_q_norm, a_k_norm, t5_bias, b_q_lora_norm, b_kv_lora_norm, b_w_uq, b_w_ukv, b_q_norm, b_k_norm, w_out_even, w_in_odd, c_q_norm, c_k_norm, c_rel_bias, d_ln_g, d_ln_b, d_w_s, d_b_s, w_out_odd, ple_proj, ple_gate):
    depth = ln_g.shape[0]
    past_len = cache_a_k.shape[2]
    hp, hs = x_prompt, x_sample
    ev_p, ev_s, od_p, od_s, dv_s = [], [], [], [], []
    for i in range(depth):
        j = i // 2
        wg = ple_gate[i].astype(BF16)
        wp = ple_proj[i].astype(BF16)
        if i % 2 == 0:
            w_in, uq, ukv, bqn, bkn = _even_weights(w_in_even[j], b_w_uq[j], b_w_ukv[j], b_q_norm[j], b_k_norm[j])
            w = (ln_g[i], w_in, uq, ukv, a_q_norm[j], a_k_norm[j], t5_bias, b_q_lora_norm[j], b_kv_lora_norm[j],
                 bqn, bkn, w_out_even[j].astype(BF16), wg, wp)
            hp, sp = _even_layer(hp, p_prompt[i], None, 0, *w)
            past = (cache_a_k[j], cache_a_v[j], cache_a_idx_k[j], cache_b_ckv[j], cache_b_kpe[j])
            hs, ss = _even_layer(hs, p_sample[i], past, past_len, *w)
            ev_p.append(sp)
            ev_s.append(ss)
        else:
            w = (ln_g[i], w_in_odd[j].astype(BF16), c_q_norm[j], c_k_norm[j], c_rel_bias[j], d_ln_g[j], d_ln_b[j],
                 d_w_s[j], d_b_s[j], w_out_odd[j].astype(BF16), wg, wp)
            hp, sp, _ = _odd_layer(hp, p_prompt[i], None, 0, *w)
            hs, ss, dvs = _odd_layer(hs, p_sample[i], (cache_c_k[j], cache_c_v[j]), past_len, *w)
            od_p.append(sp)
            od_s.append(ss)
            dv_s.append(dvs)
    st = lambda lst, n_: jnp.stack([e[n_] for e in lst], 0)
    return (hp, hs, st(ev_p, 0), st(ev_p, 1), st(ev_p, 2), st(ev_p, 3), st(ev_p, 4), st(od_p, 0), st(od_p, 1),
            st(ev_s, 0), st(ev_s, 1), st(ev_s, 2), st(ev_s, 3), st(ev_s, 4), st(od_s, 0), st(od_s, 1),
            jnp.stack(dv_s, 0))
```

```python
import functools
import math

import numpy as np
import jax
import jax.numpy as jnp
from jax import lax
from jax.experimental import pallas as pl
from jax.experimental.pallas import tpu as pltpu

F32 = jnp.float32
BF16 = jnp.bfloat16
INT_MIN = -2 ** 31
LOG2E = math.log2(math.e)
CLASS_ROWS = 512

D_MODEL = 2048
CHUNK = 64
CHUNK_SHIFT = 6
LANE = 128
H_A, HKV_A, GROUP_A, DH_A = 8, 2, 4, 128
H_IDX, D_IDX = 16, 64
TOPK_MAX = 256
T5_BUCKETS, T5_MAX_DIST = 32, 128
H_B, Q_LORA, KV_LORA, NOPE, ROPE_DIM, V_DIM = 8, 512, 256, 128, 64, 128
ROPE_BASE = 10000.0
QK_B = NOPE + ROPE_DIM
H_C, DH_C, C_BACK, REL_CLIP = 8, 128, 8, 128
W_D, G_D, DG_D, D_CHUNK = 1024, 8, 128, 128
PLE_DIM = 256

EVEN_SPLITS = (H_A * DH_A, HKV_A * DH_A, HKV_A * DH_A, H_A * DH_A, H_IDX * D_IDX, D_IDX, H_IDX,
               Q_LORA, KV_LORA, ROPE_DIM, H_B * V_DIM)
E_AQ, E_AG, E_IQ, E_BG, E_BCQ, E_AK, E_AV, E_CKV, E_IDX, E_KPE, E_END = (
    0, 1024, 2048, 3072, 4096, 4608, 4864, 5120, 5376, 5504, 5632)
O_CQ, O_CK, O_CV, O_CG, O_DU, O_DV, O_DG, O_END = 0, 1024, 2048, 3072, 4096, 5120, 6144, 7168

VMEM_LIMIT_BYTES = 56 * 1024 * 1024
NT_DIMS = (((1,), (1,)), ((), ()))


def _params(*sem):
    return pltpu.CompilerParams(dimension_semantics=sem, vmem_limit_bytes=VMEM_LIMIT_BYTES)


def _const_spec(shape):
    zeros = (0,) * len(shape)
    return pl.BlockSpec(shape, lambda *_: zeros, pipeline_mode=pl.Buffered(1))


def _rs(x, n=None, eps=1e-6):
    n = x.shape[-1] if n is None else n
    return lax.rsqrt(jnp.sum(x * x, axis=-1, keepdims=True) / n + eps)


def _silu(x):
    return x * (1.0 / (1.0 + jnp.exp(-x)))


def _softmax_pv(parts, scale=1.0):
    m = functools.reduce(jnp.maximum, [jnp.max(lg, axis=-1, keepdims=True) for lg, _ in parts])
    o = s = None
    for lg, v in parts:
        p = jnp.exp2((lg - m) * (scale * LOG2E))
        ps = jnp.sum(p, axis=-1, keepdims=True)
        po = jnp.dot(p.astype(BF16), v, preferred_element_type=F32)
        o, s = (po, ps) if o is None else (o + po, s + ps)
    return o / s


def _norm_mm_kernel(x_ref, g_ref, w_ref, o_ref, xn_ref):
    @pl.when(pl.program_id(1) == 0)
    def _():
        x = x_ref[...]
        xn_ref[...] = (x * _rs(x) * g_ref[...]).astype(BF16)

    o_ref[...] = jnp.dot(xn_ref[...], w_ref[...], preferred_element_type=F32)


def _norm_mm(x, g, w, bm, bn):
    m, d = x.shape
    n = w.shape[1]
    return pl.pallas_call(
        _norm_mm_kernel,
        grid=(m // bm, n // bn),
        in_specs=[pl.BlockSpec((bm, d), lambda i, j: (i, 0)),
                  pl.BlockSpec((1, d), lambda i, j: (0, 0)),
                  pl.BlockSpec((d, bn), lambda i, j: (0, j))],
        out_specs=pl.BlockSpec((bm, bn), lambda i, j: (i, j)),
        out_shape=jax.ShapeDtypeStruct((m, n), F32),
        scratch_shapes=[pltpu.VMEM((bm, d), BF16)],
        compiler_params=_params("parallel", "arbitrary"),
        name="norm_mm",
    )(x, g, w)


def _rope(x, cos, sin):
    return x * cos + pltpu.roll(x, 64, 1) * sin


def _even_prep_kernel(aq_ref, bcq_ref, ak_ref, ckv_ref, kpe_ref, cos_ref, sin_ref,
                      aqn_ref, akn_ref, qln_ref, kvln_ref, wuq_ref, bqn_ref,
                      qa_o, ka_o, qb_o, ckv_o, kpe_o):
    for h in range(H_A):
        x = aq_ref[:, h * DH_A:(h + 1) * DH_A]
        qa_o[:, h * DH_A:(h + 1) * DH_A] = (x * _rs(x) * aqn_ref[...]).astype(BF16)
    for n in range(HKV_A):
        x = ak_ref[:, n * DH_A:(n + 1) * DH_A]
        ka_o[:, n * DH_A:(n + 1) * DH_A] = x * _rs(x) * akn_ref[...]
    c = ckv_ref[...]
    ckv_o[...] = c * _rs(c) * kvln_ref[...]
    cos = cos_ref[...]
    sin = sin_ref[...]
    kpe_o[...] = _rope(kpe_ref[...], cos, sin)
    cq = bcq_ref[...]
    cqn = (cq * _rs(cq) * qln_ref[...]).astype(BF16)
    qb = jnp.dot(cqn, wuq_ref[...], preferred_element_type=F32)
    g = bqn_ref[...]
    for h in range(H_B):
        nope = qb[:, h * 256:h * 256 + 128]
        rot = _rope(qb[:, h * 256 + 128:(h + 1) * 256], cos, sin)
        ss = jnp.sum(nope * nope, -1, keepdims=True) + jnp.sum(rot * rot, -1, keepdims=True)
        r = lax.rsqrt(ss / QK_B + 1e-6)
        qb_o[:, h * 256:h * 256 + 128] = (nope * r * g[:, :128]).astype(BF16)
        qb_o[:, h * 256 + 128:(h + 1) * 256] = (rot * r * g[:, 128:]).astype(BF16)


def _even_prep(z, cos, sin, aqn, akn, qln, kvln, wuq, bqn, bs):
    b, s, _ = z.shape

    def zspec(width, off):
        return pl.BlockSpec((None, bs, width), lambda bi, si: (bi, si, off // width))

    def ospec(width):
        return pl.BlockSpec((None, bs, width), lambda bi, si: (bi, si, 0))

    pos_spec = pl.BlockSpec((bs, LANE), lambda bi, si: (si, 0))
    return pl.pallas_call(
        _even_prep_kernel,
        grid=(b, s // bs),
        in_specs=[zspec(1024, E_AQ), zspec(512, E_BCQ), zspec(256, E_AK), zspec(256, E_CKV),
                  zspec(128, E_KPE), pos_spec, pos_spec,
                  _const_spec((1, DH_A)), _const_spec((1, DH_A)), _const_spec((1, Q_LORA)),
                  _const_spec((1, KV_LORA)), _const_spec((Q_LORA, H_B * 256)), _const_spec((1, 256))],
        out_specs=[ospec(1024), ospec(256), ospec(2048), ospec(256), ospec(128)],
        out_shape=[jax.ShapeDtypeStruct((b, s, 1024), BF16), jax.ShapeDtypeStruct((b, s, 256), F32),
                   jax.ShapeDtypeStruct((b, s, 2048), BF16), jax.ShapeDtypeStruct((b, s, 256), F32),
                   jax.ShapeDtypeStruct((b, s, 128), F32)],
        compiler_params=_params("parallel", "arbitrary"),
        name="even_prep",
    )(z, z, z, z, z, cos, sin, aqn, akn, qln, kvln, wuq, bqn)


def _mla_kv_kernel(ckv_ref, kpe_ref, w_ref, g_ref, kb_o, vb_o):
    kv = jnp.dot(ckv_ref[...].astype(BF16), w_ref[...], preferred_element_type=F32)
    kp = kpe_ref[...]
    skp = jnp.sum(kp * kp, -1, keepdims=True)
    g = g_ref[...]
    for h in range(H_B):
        nope = kv[:, h * NOPE:(h + 1) * NOPE]
        r = lax.rsqrt((jnp.sum(nope * nope, -1, keepdims=True) + skp) / QK_B + 1e-6)
        kb_o[:, h * 256:h * 256 + 128] = (nope * r * g[:, :128]).astype(BF16)
        kb_o[:, h * 256 + 128:(h + 1) * 256] = (kp * r * g[:, 128:]).astype(BF16)
    vb_o[...] = kv[:, H_B * NOPE:].astype(BF16)


def _mla_kv(ckv_all, kpe_all, wukv, bkn, bl):
    b, lp, _ = ckv_all.shape
    return pl.pallas_call(
        _mla_kv_kernel,
        grid=(b, lp // bl),
        in_specs=[pl.BlockSpec((None, bl, KV_LORA), lambda bi, li: (bi, li, 0)),
                  pl.BlockSpec((None, bl, LANE), lambda bi, li: (bi, li, 0)),
                  _const_spec((KV_LORA, 2048)), _const_spec((1, 256))],
        out_specs=[pl.BlockSpec((None, bl, 2048), lambda bi, li: (bi, li, 0)),
                   pl.BlockSpec((None, bl, 1024), lambda bi, li: (bi, li, 0))],
        out_shape=[jax.ShapeDtypeStruct((b, lp, 2048), BF16), jax.ShapeDtypeStruct((b, lp, 1024), BF16)],
        compiler_params=_params("parallel", "arbitrary"),
        name="mla_kv",
    )(ckv_all, kpe_all, wukv, bkn)


def _key_chunk(lp):
    for c in (512, 384, 256, 128):
        if lp % c == 0:
            return c
    raise ValueError(lp)


def _dsa_kernel(qa_ref, iq_ref, sa_ref, ag_ref, ka_ref, va_ref, kidx_ref, bias_ref, o_ref, key_ref,
                *, bq, lp, l_true, q0, ksel, nd, blk0):
    i = pl.program_id(1) + blk0
    q_start = q0 + i * bq
    qpos = q_start + lax.broadcasted_iota(jnp.int32, (bq, 1), 0)
    ck = _key_chunk(lp)

    iq = iq_ref[...].astype(BF16)
    a = jnp.concatenate([iq[:, p * LANE:(p + 1) * LANE] for p in range(H_IDX // 2)], axis=0)
    wi = sa_ref[...] * (H_IDX ** -0.5 * D_IDX ** -0.5)
    for c0 in range(0, lp, ck):
        kk = kidx_ref[c0:c0 + ck, :]
        lane = lax.broadcasted_iota(jnp.int32, kk.shape, 1)
        k_lo = jnp.where(lane < D_IDX, kk, 0.0).astype(BF16)
        k_hi = jnp.where(lane >= D_IDX, pltpu.roll(kk, D_IDX, 1), 0.0).astype(BF16)
        s_lo = lax.dot_general(a, k_lo, NT_DIMS, preferred_element_type=F32)
        s_hi = lax.dot_general(a, k_hi, NT_DIMS, preferred_element_type=F32)
        sc = jnp.zeros((bq, ck), F32)
        for p in range(H_IDX // 2):
            w0 = wi[:, D_IDX + 2 * p:D_IDX + 2 * p + 1]
            w1 = wi[:, D_IDX + 2 * p + 1:D_IDX + 2 * p + 2]
            sc = sc + w0 * jnp.maximum(s_lo[p * bq:(p + 1) * bq], 0.0)
            sc = sc + w1 * jnp.maximum(s_hi[p * bq:(p + 1) * bq], 0.0)
        kpos = c0 + lax.broadcasted_iota(jnp.int32, (1, ck), 1)
        valid = ((kpos >> CHUNK_SHIFT) <= (qpos >> CHUNK_SHIFT)) & (kpos < l_true)
        bits = lax.bitcast_convert_type(sc, jnp.int32)
        key = jnp.where(bits < 0, bits ^ 0x7FFFFFFF, bits)
        key = jnp.where(bits == INT_MIN, 0, key)
        key_ref[:, c0:c0 + ck] = jnp.where(valid, key, INT_MIN)

    hb = bq // 2

    def bit_pass(it, t_u):
        bit = lax.shift_left(jnp.int32(1), 31 - it)
        out = []
        for r, t in enumerate(t_u):
            cand = t | bit
            cnt = jnp.sum((key_ref[r * hb:(r + 1) * hb, :] >= (cand ^ INT_MIN)).astype(F32), axis=-1, keepdims=True)
            out.append(jnp.where(cnt >= ksel, cand, t))
        return tuple(out)

    t_u = lax.fori_loop(0, 32, bit_pass, (jnp.zeros((hb, 1), jnp.int32),) * 2)
    t_s = jnp.concatenate(t_u, axis=0) ^ INT_MIN
    key = key_ref[...]
    sel = (key >= t_s) & (key > INT_MIN)

    d_base = (nd - 1) - (q0 // LANE + (i if bq == LANE else 0))
    nk = lp // LANE
    for n in range(HKV_A):
        q4 = jnp.concatenate([qa_ref[:, (n * GROUP_A + g) * DH_A:(n * GROUP_A + g + 1) * DH_A]
                              for g in range(GROUP_A)], axis=0)
        k_n = ka_ref[:, n * DH_A:(n + 1) * DH_A].astype(BF16)
        v_n = va_ref[:, n * DH_A:(n + 1) * DH_A].astype(BF16)
        lg4 = lax.dot_general(q4, k_n, NT_DIMS, preferred_element_type=F32) * DH_A ** -0.5
        ps, ss = [], []
        for g in range(GROUP_A):
            bias = jnp.concatenate([bias_ref[n * GROUP_A + g, jnp.minimum(d_base + j, nd - 1)]
                                    for j in range(nk)], axis=1)
            lg = jnp.where(sel, lg4[g * bq:(g + 1) * bq] + bias, -jnp.inf)
            p = jnp.exp(lg - jnp.max(lg, axis=-1, keepdims=True))
            ss.append(jnp.sum(p, axis=-1, keepdims=True))
            ps.append(p.astype(BF16))
        o4 = jnp.dot(jnp.concatenate(ps, axis=0), v_n, preferred_element_type=F32)
        for g in range(GROUP_A):
            h = n * GROUP_A + g
            gate = ag_ref[:, h * DH_A:(h + 1) * DH_A]
            o_ref[:, h * DH_A:(h + 1) * DH_A] = (o4[g * bq:(g + 1) * bq] / ss[g] * _silu(gate)).astype(BF16)


def _dsa(qa, z, ka_all, va_all, kidx_all, bias_tiles, *, bq, l_true, q0, ksel, row0, rows, lp):
    b, s, _ = qa.shape
    nd = bias_tiles.shape[1]
    blk0 = row0 // bq
    assert q0 % LANE == 0 and (bq == LANE or s == bq)
    assert nd == (q0 + s - bq) // LANE + 1
    kern = functools.partial(_dsa_kernel, bq=bq, lp=lp, l_true=l_true, q0=q0, ksel=ksel, nd=nd, blk0=blk0)

    def zspec(width, off):
        return pl.BlockSpec((None, bq, width), lambda bi, qi: (bi, qi + blk0, off // width))

    def kspec(width):
        return pl.BlockSpec((None, lp, width), lambda bi, qi: (bi, 0, 0))

    return pl.pallas_call(
        kern,
        grid=(b, rows // bq),
        in_specs=[pl.BlockSpec((None, bq, 1024), lambda bi, qi: (bi, qi + blk0, 0)),
                  zspec(1024, E_IQ), zspec(128, E_IDX), zspec(1024, E_AG),
                  kspec(256), kspec(256), kspec(128),
                  _const_spec(bias_tiles.shape)],
        out_specs=pl.BlockSpec((None, bq, 1024), lambda bi, qi: (bi, qi, 0)),
        out_shape=jax.ShapeDtypeStruct((b, rows, 1024), BF16),
        scratch_shapes=[pltpu.VMEM((bq, lp), jnp.int32)],
        compiler_params=_params("parallel", "arbitrary"),
        name="dsa",
    )(qa, z, z, z, ka_all, va_all, kidx_all, bias_tiles)


def _causal_classes(s, lp, q0):
    if q0 == 0 and s % CLASS_ROWS == 0 and lp == s:
        return [(c * CLASS_ROWS, CLASS_ROWS, (c + 1) * CLASS_ROWS) for c in range(s // CLASS_ROWS)]
    return [(0, s, lp)]


def _mla_kernel(qb_ref, bg_ref, kb_ref, vb_ref, o_ref, *, bq, lp, l_true, q0, blk0, n_full):
    q_start = q0 + (pl.program_id(1) + blk0) * bq
    qpos = q_start + lax.broadcasted_iota(jnp.int32, (bq, 1), 0)
    kpos = n_full + lax.broadcasted_iota(jnp.int32, (1, lp - n_full), 1)
    valid = ((kpos >> CHUNK_SHIFT) <= (qpos >> CHUNK_SHIFT)) & (kpos < l_true)
    for h in range(H_B):
        q = qb_ref[:, h * 256:(h + 1) * 256]
        parts = []
        if n_full:
            parts.append((lax.dot_general(q, kb_ref[:n_full, h * 256:(h + 1) * 256], NT_DIMS,
                                          preferred_element_type=F32), vb_ref[:n_full, h * V_DIM:(h + 1) * V_DIM]))
        lg = lax.dot_general(q, kb_ref[n_full:, h * 256:(h + 1) * 256], NT_DIMS, preferred_element_type=F32)
        parts.append((jnp.where(valid, lg, -jnp.inf), vb_ref[n_full:, h * V_DIM:(h + 1) * V_DIM]))
        o = _softmax_pv(parts, QK_B ** -0.5)
        gate = bg_ref[:, h * V_DIM:(h + 1) * V_DIM]
        o_ref[:, h * V_DIM:(h + 1) * V_DIM] = (o * _silu(gate)).astype(BF16)


def _mla(qb, z, kb, vb, *, bq, l_true, q0, row0, rows, lp):
    b = qb.shape[0]
    blk0 = row0 // bq
    n_full = min(min(q0 + row0 + CHUNK, l_true) // LANE * LANE, lp - LANE)
    kern = functools.partial(_mla_kernel, bq=bq, lp=lp, l_true=l_true, q0=q0, blk0=blk0, n_full=n_full)
    return pl.pallas_call(
        kern,
        grid=(b, rows // bq),
        in_specs=[pl.BlockSpec((None, bq, 2048), lambda bi, qi: (bi, qi + blk0, 0)),
                  pl.BlockSpec((None, bq, 1024), lambda bi, qi: (bi, qi + blk0, E_BG // 1024)),
                  pl.BlockSpec((None, lp, 2048), lambda bi, qi: (bi, 0, 0)),
                  pl.BlockSpec((None, lp, 1024), lambda bi, qi: (bi, 0, 0))],
        out_specs=pl.BlockSpec((None, bq, 1024), lambda bi, qi: (bi, qi, 0)),
        out_shape=jax.ShapeDtypeStruct((b, rows, 1024), BF16),
        compiler_params=_params("parallel", "arbitrary"),
        name="mla",
    )(qb, z, kb, vb)


def _out_ple_kernel(h_ref, oa_ref, ob_ref, p_ref, wo_ref, wg_ref, wp_ref, o_ref):
    half = oa_ref.shape[-1]
    h1 = (h_ref[...]
          + jnp.dot(oa_ref[...], wo_ref[:half, :], preferred_element_type=F32)
          + jnp.dot(ob_ref[...], wo_ref[half:, :], preferred_element_type=F32))
    r = (h1 * _rs(h1)).astype(BF16)
    gate = 1.0 / (1.0 + jnp.exp(-jnp.dot(r, wg_ref[...], preferred_element_type=F32)))
    o_ref[...] = h1 + gate * jnp.dot(p_ref[...].astype(BF16), wp_ref[...], preferred_element_type=F32)


def _out_ple(h, oa, ob, p, wo, wg, wp, bm):
    m, d = h.shape
    half = oa.shape[1]

    def rows(width):
        return pl.BlockSpec((bm, width), lambda i: (i, 0))

    return pl.pallas_call(
        _out_ple_kernel,
        grid=(m // bm,),
        in_specs=[rows(d), rows(half), rows(half), rows(PLE_DIM),
                  _const_spec((2 * half, d)), _const_spec((d, d)), _const_spec((PLE_DIM, d))],
        out_specs=rows(d),
        out_shape=jax.ShapeDtypeStruct((m, d), F32),
        compiler_params=_params("parallel"),
        name="out_ple",
    )(h, oa, ob, p, wo, wg, wp)


def _odd_prep_kernel(cq_ref, ck_ref, cv_ref, dv_ref, cqn_ref, ckn_ref, dg_ref, db_ref,
                     qc_o, kc_o, kcb_o, vcb_o, dvn_o):
    for h in range(H_C):
        sl = slice(h * DH_C, (h + 1) * DH_C)
        x = cq_ref[:, sl]
        qc_o[:, sl] = (x * _rs(x) * cqn_ref[...]).astype(BF16)
        x = ck_ref[:, sl]
        kn = x * _rs(x) * ckn_ref[...]
        kc_o[:, sl] = kn
        kcb_o[:, sl] = kn.astype(BF16)
    vcb_o[...] = cv_ref[...].astype(BF16)
    dv = dv_ref[...]
    xc = dv - jnp.mean(dv, -1, keepdims=True)
    var = jnp.mean(xc * xc, -1, keepdims=True)
    dvn_o[...] = xc * lax.rsqrt(var + 1e-5) * dg_ref[...] + db_ref[...]


def _odd_prep(z, cqn, ckn, dg, db, bs):
    b, s, _ = z.shape

    def zspec(off):
        return pl.BlockSpec((None, bs, 1024), lambda bi, si: (bi, si, off // 1024))

    ospec = pl.BlockSpec((None, bs, 1024), lambda bi, si: (bi, si, 0))
    return pl.pallas_call(
        _odd_prep_kernel,
        grid=(b, s // bs),
        in_specs=[zspec(O_CQ), zspec(O_CK), zspec(O_CV), zspec(O_DV),
                  _const_spec((1, DH_C)), _const_spec((1, DH_C)), _const_spec((1, W_D)), _const_spec((1, W_D))],
        out_specs=[ospec] * 5,
        out_shape=[jax.ShapeDtypeStruct((b, s, 1024), BF16), jax.ShapeDtypeStruct((b, s, 1024), F32),
                   jax.ShapeDtypeStruct((b, s, 1024), BF16), jax.ShapeDtypeStruct((b, s, 1024), BF16),
                   jax.ShapeDtypeStruct((b, s, 1024), F32)],
        compiler_params=_params("parallel", "arbitrary"),
        name="odd_prep",
    )(z, z, z, z, cqn, ckn, dg, db)


BAND_TILES = C_BACK * CHUNK // LANE + 1
BAND_W = BAND_TILES * LANE


def _band_kernel(q_ref, cg_ref, k_ref, v_ref, bias_ref, o_ref, *, bq, sliding, q0, k0, k_end):
    i = pl.program_id(1)
    q_start = q0 + i * bq
    if sliding:
        tiles = [i - (BAND_TILES - 1) + t for t in range(BAND_TILES)]
        win_start = (i - (BAND_TILES - 1)) * LANE
        rows = [pl.ds(pl.multiple_of(jnp.maximum(j, 0) * LANE, LANE), LANE) for j in tiles]
    else:
        win_start = k0
        rows = [pl.ds(t * LANE, LANE) for t in range(BAND_TILES)]
    kw = jnp.concatenate([k_ref[r, :] for r in rows], axis=0)
    vw = jnp.concatenate([v_ref[r, :] for r in rows], axis=0)
    qpos = q_start + lax.broadcasted_iota(jnp.int32, (bq, 1), 0)
    kpos = win_start + lax.broadcasted_iota(jnp.int32, (1, BAND_W), 1)
    dc = (qpos >> CHUNK_SHIFT) - (kpos >> CHUNK_SHIFT)
    valid = (dc >= 0) & (dc <= C_BACK) & (kpos >= 0) & (kpos < k_end)
    for h in range(H_C):
        sl = slice(h * DH_C, (h + 1) * DH_C)
        lg = lax.dot_general(q_ref[:, sl], kw[:, sl], NT_DIMS, preferred_element_type=F32) * DH_C ** -0.5
        lg = jnp.where(valid, lg + bias_ref[h], -jnp.inf)
        o = _softmax_pv([(lg, vw[:, sl])])
        o_ref[:, sl] = (o * _silu(cg_ref[:, sl])).astype(BF16)


def _band(qc, z, kcb, vcb, bias, *, bq, sliding, q0, k0, k_end):
    b, s, _ = qc.shape
    lk = kcb.shape[1]
    if sliding:
        assert bq == LANE and q0 == 0 and k0 == 0
    else:
        assert lk == BAND_W and s == bq
    kern = functools.partial(_band_kernel, bq=bq, sliding=sliding, q0=q0, k0=k0, k_end=k_end)
    return pl.pallas_call(
        kern,
        grid=(b, s // bq),
        in_specs=[pl.BlockSpec((None, bq, 1024), lambda bi, qi: (bi, qi, 0)),
                  pl.BlockSpec((None, bq, 1024), lambda bi, qi: (bi, qi, O_CG // 1024)),
                  pl.BlockSpec((None, lk, 1024), lambda bi, qi: (bi, 0, 0)),
                  pl.BlockSpec((None, lk, 1024), lambda bi, qi: (bi, 0, 0)),
                  _const_spec(bias.shape)],
        out_specs=pl.BlockSpec((None, bq, 1024), lambda bi, qi: (bi, qi, 0)),
        out_shape=jax.ShapeDtypeStruct((b, s, 1024), BF16),
        compiler_params=_params("parallel", "arbitrary"),
        name="band",
    )(qc, z, kcb, vcb, bias)


def _sgate_kernel(dvn_ref, du_ref, dg_ref, ws_ref, bs_ref, o_ref, *, n):
    row = lax.broadcasted_iota(jnp.int32, (n, n), 0)
    col = lax.broadcasted_iota(jnp.int32, (n, n), 1)
    tril = col <= row
    for g in range(G_D):
        sl = slice(g * DG_D, (g + 1) * DG_D)
        w = jnp.where(tril, ws_ref[g], 0.0).astype(BF16)
        for c in range(dvn_ref.shape[0] // n):
            rs = slice(c * n, (c + 1) * n)
            sg = jnp.dot(w, dvn_ref[rs, sl].astype(BF16), preferred_element_type=F32) + bs_ref[:, g:g + 1]
            o_ref[rs, sl] = (du_ref[rs, sl] * sg * _silu(dg_ref[rs, sl])).astype(BF16)


def _sgate(dvn, z, ws, bs_t, n):
    b, s, _ = dvn.shape
    br = n * math.gcd(s // n, 4)

    def zspec(off):
        return pl.BlockSpec((None, br, 1024), lambda bi, ci: (bi, ci, off // 1024))

    return pl.pallas_call(
        functools.partial(_sgate_kernel, n=n),
        grid=(b, s // br),
        in_specs=[pl.BlockSpec((None, br, 1024), lambda bi, ci: (bi, ci, 0)), zspec(O_DU), zspec(O_DG),
                  _const_spec((G_D, n, n)), _const_spec((n, G_D))],
        out_specs=pl.BlockSpec((None, br, 1024), lambda bi, ci: (bi, ci, 0)),
        out_shape=jax.ShapeDtypeStruct((b, s, 1024), BF16),
        compiler_params=_params("parallel", "arbitrary"),
        name="sgate",
    )(dvn, z, z, ws, bs_t)


def _rope_tables(pos):
    half = ROPE_DIM // 2
    freq = ROPE_BASE ** (-jnp.arange(half, dtype=F32) / half)
    ang = pos.astype(F32)[:, None] * freq[None, :]
    cos, sin = jnp.cos(ang), jnp.sin(ang)
    z = jnp.zeros_like(cos)
    return jnp.concatenate([cos, z, cos, z], 1), jnp.concatenate([-sin, z, sin, z], 1)


def _rope_lanes(x):
    half = ROPE_DIM // 2
    z = jnp.zeros(x.shape[:-1] + (half,), x.dtype)
    return jnp.concatenate([x[..., :half], z, x[..., half:], z], -1)


def _rope_unlanes(x):
    half = ROPE_DIM // 2
    return jnp.concatenate([x[..., :half], x[..., 2 * half:3 * half]], -1)


def _t5_bucket_np(rel):
    nb = T5_BUCKETS // 2
    max_exact = nb // 2
    n = np.abs(rel)
    nf = np.maximum(n, 1).astype(np.float64)
    large = max_exact + (np.log(nf / max_exact) / math.log(T5_MAX_DIST / max_exact) * (nb - max_exact)).astype(np.int64)
    large = np.minimum(large, nb - 1)
    return np.where(rel > 0, nb, 0) + np.where(n < max_exact, n, large)


def _toeplitz(w, rows, width, cols):
    flat = jnp.tile(w, (1,) * (w.ndim - 1) + (rows,))[..., :rows * width]
    return flat.reshape(w.shape[:-1] + (rows, width))[..., :cols]


def _t5_tiles(t5_bias, bq, nd):
    width = 2 * LANE
    k = np.arange(width + 1)
    delta = np.where(k < LANE, k, k - (width + 1))
    rel = (np.arange(nd) - (nd - 1))[:, None] * LANE + delta[None, :]
    w = jnp.transpose(t5_bias[_t5_bucket_np(rel)], (2, 0, 1))
    return _toeplitz(w, bq, width, LANE)


def _band_bias(rel_tab, bq, qk_off):
    width = BAND_W + LANE
    k = np.arange(width + 1)
    delta = np.where(k < BAND_W, k, k - (width + 1))
    idx = np.clip(qk_off - delta, -(CHUNK - 1), REL_CLIP) + (CHUNK - 1)
    return _toeplitz(jnp.transpose(rel_tab[idx], (1, 0)), bq, width, BAND_W)


def _even_weights(w_in, b_wuq, b_wukv, b_qn, b_kn):
    d = w_in.shape[0]
    offs = np.cumsum((0,) + EVEN_SPLITS)
    aq, ak, av, ag, iq, ik, iw, bcq, bckv, bkpe, bg = [w_in[:, offs[t]:offs[t + 1]] for t in range(11)]
    slab_idx = jnp.concatenate([ik, iw, jnp.zeros((d, LANE - D_IDX - H_IDX), w_in.dtype)], 1)
    w = jnp.concatenate([aq, ag, iq, bg, bcq, ak, av, bckv, slab_idx, _rope_lanes(bkpe)], 1).astype(BF16)
    uq = b_wuq.reshape(Q_LORA, H_B, QK_B)
    uq = jnp.concatenate([uq[..., :NOPE], _rope_lanes(uq[..., NOPE:])], -1).reshape(Q_LORA, H_B * 256).astype(BF16)
    ukv = b_wukv.reshape(KV_LORA, H_B, NOPE + V_DIM)
    ukv = jnp.concatenate([ukv[..., :NOPE].reshape(KV_LORA, H_B * NOPE),
                           ukv[..., NOPE:].reshape(KV_LORA, H_B * V_DIM)], 1).astype(BF16)
    pad_gain = lambda g: jnp.concatenate([g[:NOPE], _rope_lanes(g[NOPE:])])[None, :]
    return w, uq, ukv, pad_gain(b_qn), pad_gain(b_kn)


def _pad_rows(x, lp):
    return jnp.pad(x, ((0, 0), (0, lp - x.shape[1]), (0, 0)))


def _even_layer(h, p, past, q0, ln_g, w_in, uq, ukv, a_qn, a_kn, t5_bias, b_qln, b_kvln, bqn, bkn, wo, wg, wp):
    b, s, d = h.shape
    m = b * s
    z = _norm_mm(h.reshape(m, d), ln_g[None, :], w_in, min(m, 1024), 512).reshape(b, s, E_END)
    cos, sin = _rope_tables(q0 + jnp.arange(s, dtype=jnp.int32))
    bs = min(s, 256)
    qa, ka, qb, ckv, kpe_l = _even_prep(z, cos, sin, a_qn[None, :], a_kn[None, :], b_qln[None, :], b_kvln[None, :],
                                        uq, bqn, bs)
    av = z[..., E_AV:E_AV + 256]
    kidx = z[..., E_IDX:E_IDX + LANE]
    new = (ka.reshape(b, s, HKV_A, DH_A), av.reshape(b, s, HKV_A, DH_A), kidx[..., :D_IDX], ckv, _rope_unlanes(kpe_l))
    if past is None:
        l_true = s
        ka_all, va_all, kidx_all, ckv_all, kpe_all = ka, av, kidx, ckv, kpe_l
    else:
        c_k, c_v, c_ik, c_ckv, c_kpe = past
        pl_ = c_k.shape[1]
        l_true = pl_ + s
        lp = -(-l_true // LANE) * LANE
        cat = lambda c, n_: _pad_rows(jnp.concatenate([c, n_], 1), lp)
        ka_all = cat(c_k.reshape(b, pl_, 256), ka)
        va_all = cat(c_v.reshape(b, pl_, 256), av)
        kidx_all = cat(jnp.pad(c_ik, ((0, 0), (0, 0), (0, LANE - D_IDX))), kidx)
        ckv_all = cat(c_ckv, ckv)
        kpe_all = cat(_rope_lanes(c_kpe), kpe_l)
    lp = ka_all.shape[1]
    ksel = min(TOPK_MAX, l_true // 4)
    bq = min(s, LANE)
    nd = (q0 + s - bq) // LANE + 1
    tiles = _t5_tiles(t5_bias, bq, nd)
    classes = _causal_classes(s, lp, q0)
    o_a = jnp.concatenate([_dsa(qa, z, ka_all, va_all, kidx_all, tiles, bq=bq, l_true=l_true, q0=q0, ksel=ksel,
                                row0=r0, rows=nr, lp=lc) for r0, nr, lc in classes], axis=1)
    kb, vb = _mla_kv(ckv_all, kpe_all, ukv, bkn, _key_chunk(lp))
    o_b = jnp.concatenate([_mla(qb, z, kb, vb, bq=min(s, 256), l_true=l_true, q0=q0,
                                row0=r0, rows=nr, lp=lc) for r0, nr, lc in classes], axis=1)
    y = _out_ple(h.reshape(m, d), o_a.reshape(m, -1), o_b.reshape(m, -1), p.reshape(m, -1), wo, wg, wp, min(m, 256))
    return y.reshape(b, s, d), new


def _odd_layer(h, p, past, q0, ln_g, w_in, c_qn, c_kn, c_rel, d_g, d_b, d_ws, d_bs, wo, wg, wp):
    b, s, d = h.shape
    m = b * s
    z = _norm_mm(h.reshape(m, d), ln_g[None, :], w_in, min(m, 1024), 512).reshape(b, s, O_END)
    qc, kc, kcb, vcb, dvn = _odd_prep(z, c_qn[None, :], c_kn[None, :], d_g[None, :], d_b[None, :], min(s, 256))
    cv = z[..., O_CV:O_CV + 1024]
    if past is None:
        keep = min(C_BACK * CHUNK, s)
        c_new = (kc[:, s - keep:].reshape(b, keep, H_C, DH_C), cv[:, s - keep:].reshape(b, keep, H_C, DH_C))
        bias = _band_bias(c_rel, LANE, C_BACK * CHUNK)
        o_c = _band(qc, z, kcb, vcb, bias, bq=LANE, sliding=True, q0=0, k0=0, k_end=s)
    else:
        nc = past[0].shape[1]
        c_new = (kc.reshape(b, s, H_C, DH_C), cv.reshape(b, s, H_C, DH_C))
        cat = lambda c, n_: _pad_rows(jnp.concatenate([c.reshape(b, nc, 1024).astype(BF16), n_], 1), BAND_W)
        bias = _band_bias(c_rel, s, nc)
        o_c = _band(qc, z, cat(past[0], kcb), cat(past[1], vcb), bias,
                    bq=s, sliding=False, q0=q0, k0=q0 - nc, k_end=q0 + s)
    n = min(s, D_CHUNK)
    o_d = _sgate(dvn, z, d_ws[:, :n, :n], d_bs[:, :n].T, n)
    y = _out_ple(h.reshape(m, d), o_c.reshape(m, -1), o_d.reshape(m, -1), p.reshape(m, -1), wo, wg, wp, min(m, 256))
    return y.reshape(b, s, d), c_new, dvn


def kernel(x_prompt, x_sample, cache_a_k, cache_a_v, cache_a_idx_k, cache_b_ckv, cache_b_kpe, cache_c_k, cache_c_v, p_prompt, p_sample, ln_g, w_in_even, a_q_norm, a_k_norm, t5_bias, b_q_lora_norm, b_kv_lora_norm, b_w_uq, b_w_ukv, b_q_norm, b_k_norm, w_out_even, w_in_odd, c_q_norm, c_k_norm, c_rel_bias, d_ln_g, d_ln_b, d_w_s, d_b_s, w_out_odd, ple_proj, ple_gate):
    depth = ln_g.shape[0]
    past_len = cache_a_k.shape[2]
    hp, hs = x_prompt, x_sample
    ev_p, ev_s, od_p, od_s, dv_s = [], [], [], [], []
    for i in range(depth):
        j = i // 2
        wg = ple_gate[i].astype(BF16)
        wp = ple_proj[i].astype(BF16)
        if i % 2 == 0:
            w_in, uq, ukv, bqn, bkn = _even_weights(w_in_even[j], b_w_uq[j], b_w_ukv[j], b_q_norm[j], b_k_norm[j])
            w = (ln_g[i], w_in, uq, ukv, a_q_norm[j], a_k_norm[j], t5_bias, b_q_lora_norm[j], b_kv_lora_norm[j],
                 bqn, bkn, w_out_even[j].astype(BF16), wg, wp)
            hp, sp = _even_layer(hp, p_prompt[i], None, 0, *w)
            past = (cache_a_k[j], cache_a_v[j], cache_a_idx_k[j], cache_b_ckv[j], cache_b_kpe[j])
            hs, ss = _even_layer(hs, p_sample[i], past, past_len, *w)
            ev_p.append(sp)
            ev_s.append(ss)
        else:
            w = (ln_g[i], w_in_odd[j].astype(BF16), c_q_norm[j], c_k_norm[j], c_rel_bias[j], d_ln_g[j], d_ln_b[j],
                 d_w_s[j], d_b_s[j], w_out_odd[j].astype(BF16), wg, wp)
            hp, sp, _ = _odd_layer(hp, p_prompt[i], None, 0, *w)
            hs, ss, dvs = _odd_layer(hs, p_sample[i], (cache_c_k[j], cache_c_v[j]), past_len, *w)
            od_p.append(sp)
            od_s.append(ss)
            dv_s.append(dvs)
    st = lambda lst, n_: jnp.stack([e[n_] for e in lst], 0)
    return (hp, hs, st(ev_p, 0), st(ev_p, 1), st(ev_p, 2), st(ev_p, 3), st(ev_p, 4), st(od_p, 0), st(od_p, 1),
            st(ev_s, 0), st(ev_s, 1), st(ev_s, 2), st(ev_s, 3), st(ev_s, 4), st(od_s, 0), st(od_s, 1),
            jnp.stack(dv_s, 0))
```

```python
import functools
import math

import numpy as np
import jax
import jax.numpy as jnp
from jax import lax
from jax.experimental import pallas as pl
from jax.experimental.pallas import tpu as pltpu

F32 = jnp.float32
BF16 = jnp.bfloat16
INT_MIN = -2 ** 31
LOG2E = math.log2(math.e)
CLASS_ROWS = 512
SEARCH_GROUPS = 4

D_MODEL = 2048
CHUNK = 64
CHUNK_SHIFT = 6
LANE = 128
H_A, HKV_A, GROUP_A, DH_A = 8, 2, 4, 128
H_IDX, D_IDX = 16, 64
TOPK_MAX = 256
T5_BUCKETS, T5_MAX_DIST = 32, 128
H_B, Q_LORA, KV_LORA, NOPE, ROPE_DIM, V_DIM = 8, 512, 256, 128, 64, 128
ROPE_BASE = 10000.0
QK_B = NOPE + ROPE_DIM
H_C, DH_C, C_BACK, REL_CLIP = 8, 128, 8, 128
W_D, G_D, DG_D, D_CHUNK = 1024, 8, 128, 128
PLE_DIM = 256

EVEN_SPLITS = (H_A * DH_A, HKV_A * DH_A, HKV_A * DH_A, H_A * DH_A, H_IDX * D_IDX, D_IDX, H_IDX,
               Q_LORA, KV_LORA, ROPE_DIM, H_B * V_DIM)
E_AQ, E_AG, E_IQ, E_BG, E_BCQ, E_AK, E_AV, E_CKV, E_IDX, E_KPE, E_END = (
    0, 1024, 2048, 3072, 4096, 4608, 4864, 5120, 5376, 5504, 5632)
O_CQ, O_CK, O_CV, O_CG, O_DU, O_DV, O_DG, O_END = 0, 1024, 2048, 3072, 4096, 5120, 6144, 7168

VMEM_LIMIT_BYTES = 56 * 1024 * 1024
NT_DIMS = (((1,), (1,)), ((), ()))


def _params(*sem):
    return pltpu.CompilerParams(dimension_semantics=sem, vmem_limit_bytes=VMEM_LIMIT_BYTES)


def _const_spec(shape):
    zeros = (0,) * len(shape)
    return pl.BlockSpec(shape, lambda *_: zeros, pipeline_mode=pl.Buffered(1))


def _rs(x, n=None, eps=1e-6):
    n = x.shape[-1] if n is None else n
    return lax.rsqrt(jnp.sum(x * x, axis=-1, keepdims=True) / n + eps)


def _silu(x):
    return x * (1.0 / (1.0 + jnp.exp(-x)))


def _softmax_pv(parts, scale=1.0):
    m = functools.reduce(jnp.maximum, [jnp.max(lg, axis=-1, keepdims=True) for lg, _ in parts])
    o = s = None
    for lg, v in parts:
        p = jnp.exp2((lg - m) * (scale * LOG2E))
        ps = jnp.sum(p, axis=-1, keepdims=True)
        po = jnp.dot(p.astype(BF16), v, preferred_element_type=F32)
        o, s = (po, ps) if o is None else (o + po, s + ps)
    return o / s


def _norm_mm_kernel(x_ref, g_ref, w_ref, o_ref, xn_ref):
    @pl.when(pl.program_id(1) == 0)
    def _():
        x = x_ref[...]
        xn_ref[...] = (x * _rs(x) * g_ref[...]).astype(BF16)

    o_ref[...] = jnp.dot(xn_ref[...], w_ref[...], preferred_element_type=F32)


def _norm_mm(x, g, w, bm, bn):
    m, d = x.shape
    n = w.shape[1]
    return pl.pallas_call(
        _norm_mm_kernel,
        grid=(m // bm, n // bn),
        in_specs=[pl.BlockSpec((bm, d), lambda i, j: (i, 0)),
                  pl.BlockSpec((1, d), lambda i, j: (0, 0)),
                  pl.BlockSpec((d, bn), lambda i, j: (0, j))],
        out_specs=pl.BlockSpec((bm, bn), lambda i, j: (i, j)),
        out_shape=jax.ShapeDtypeStruct((m, n), F32),
        scratch_shapes=[pltpu.VMEM((bm, d), BF16)],
        compiler_params=_params("parallel", "arbitrary"),
        name="norm_mm",
    )(x, g, w)


def _rope(x, cos, sin):
    return x * cos + pltpu.roll(x, 64, 1) * sin


def _even_prep_kernel(aq_ref, bcq_ref, ak_ref, ckv_ref, kpe_ref, cos_ref, sin_ref,
                      aqn_ref, akn_ref, qln_ref, kvln_ref, wuq_ref, bqn_ref,
                      qa_o, ka_o, qb_o, ckv_o, kpe_o):
    for h in range(H_A):
        x = aq_ref[:, h * DH_A:(h + 1) * DH_A]
        qa_o[:, h * DH_A:(h + 1) * DH_A] = (x * _rs(x) * aqn_ref[...]).astype(BF16)
    for n in range(HKV_A):
        x = ak_ref[:, n * DH_A:(n + 1) * DH_A]
        ka_o[:, n * DH_A:(n + 1) * DH_A] = x * _rs(x) * akn_ref[...]
    c = ckv_ref[...]
    ckv_o[...] = c * _rs(c) * kvln_ref[...]
    cos = cos_ref[...]
    sin = sin_ref[...]
    kpe_o[...] = _rope(kpe_ref[...], cos, sin)
    cq = bcq_ref[...]
    cqn = (cq * _rs(cq) * qln_ref[...]).astype(BF16)
    qb = jnp.dot(cqn, wuq_ref[...], preferred_element_type=F32)
    g = bqn_ref[...]
    for h in range(H_B):
        nope = qb[:, h * 256:h * 256 + 128]
        rot = _rope(qb[:, h * 256 + 128:(h + 1) * 256], cos, sin)
        ss = jnp.sum(nope * nope, -1, keepdims=True) + jnp.sum(rot * rot, -1, keepdims=True)
        r = lax.rsqrt(ss / QK_B + 1e-6)
        qb_o[:, h * 256:h * 256 + 128] = (nope * r * g[:, :128]).astype(BF16)
        qb_o[:, h * 256 + 128:(h + 1) * 256] = (rot * r * g[:, 128:]).astype(BF16)


def _even_prep(z, cos, sin, aqn, akn, qln, kvln, wuq, bqn, bs):
    b, s, _ = z.shape

    def zspec(width, off):
        return pl.BlockSpec((None, bs, width), lambda bi, si: (bi, si, off // width))

    def ospec(width):
        return pl.BlockSpec((None, bs, width), lambda bi, si: (bi, si, 0))

    pos_spec = pl.BlockSpec((bs, LANE), lambda bi, si: (si, 0))
    return pl.pallas_call(
        _even_prep_kernel,
        grid=(b, s // bs),
        in_specs=[zspec(1024, E_AQ), zspec(512, E_BCQ), zspec(256, E_AK), zspec(256, E_CKV),
                  zspec(128, E_KPE), pos_spec, pos_spec,
                  _const_spec((1, DH_A)), _const_spec((1, DH_A)), _const_spec((1, Q_LORA)),
                  _const_spec((1, KV_LORA)), _const_spec((Q_LORA, H_B * 256)), _const_spec((1, 256))],
        out_specs=[ospec(1024), ospec(256), ospec(2048), ospec(256), ospec(128)],
        out_shape=[jax.ShapeDtypeStruct((b, s, 1024), BF16), jax.ShapeDtypeStruct((b, s, 256), F32),
                   jax.ShapeDtypeStruct((b, s, 2048), BF16), jax.ShapeDtypeStruct((b, s, 256), F32),
                   jax.ShapeDtypeStruct((b, s, 128), F32)],
        compiler_params=_params("parallel", "arbitrary"),
        name="even_prep",
    )(z, z, z, z, z, cos, sin, aqn, akn, qln, kvln, wuq, bqn)


def _mla_kv_kernel(ckv_ref, kpe_ref, w_ref, g_ref, kb_o, vb_o):
    kv = jnp.dot(ckv_ref[...].astype(BF16), w_ref[...], preferred_element_type=F32)
    kp = kpe_ref[...]
    skp = jnp.sum(kp * kp, -1, keepdims=True)
    g = g_ref[...]
    for h in range(H_B):
        nope = kv[:, h * NOPE:(h + 1) * NOPE]
        r = lax.rsqrt((jnp.sum(nope * nope, -1, keepdims=True) + skp) / QK_B + 1e-6)
        kb_o[:, h * 256:h * 256 + 128] = (nope * r * g[:, :128]).astype(BF16)
        kb_o[:, h * 256 + 128:(h + 1) * 256] = (kp * r * g[:, 128:]).astype(BF16)
    vb_o[...] = kv[:, H_B * NOPE:].astype(BF16)


def _mla_kv(ckv_all, kpe_all, wukv, bkn, bl):
    b, lp, _ = ckv_all.shape
    return pl.pallas_call(
        _mla_kv_kernel,
        grid=(b, lp // bl),
        in_specs=[pl.BlockSpec((None, bl, KV_LORA), lambda bi, li: (bi, li, 0)),
                  pl.BlockSpec((None, bl, LANE), lambda bi, li: (bi, li, 0)),
                  _const_spec((KV_LORA, 2048)), _const_spec((1, 256))],
        out_specs=[pl.BlockSpec((None, bl, 2048), lambda bi, li: (bi, li, 0)),
                   pl.BlockSpec((None, bl, 1024), lambda bi, li: (bi, li, 0))],
        out_shape=[jax.ShapeDtypeStruct((b, lp, 2048), BF16), jax.ShapeDtypeStruct((b, lp, 1024), BF16)],
        compiler_params=_params("parallel", "arbitrary"),
        name="mla_kv",
    )(ckv_all, kpe_all, wukv, bkn)


def _key_chunk(lp):
    for c in (512, 384, 256, 128):
        if lp % c == 0:
            return c
    raise ValueError(lp)


def _dsa_kernel(qa_ref, iq_ref, sa_ref, ag_ref, ka_ref, va_ref, kidx_ref, bias_ref, o_ref, key_ref, hi_ref, lo_ref,
                *, bq, lp, l_true, q0, ksel, nd, blk0):
    i = pl.program_id(1) + blk0
    q_start = q0 + i * bq
    qpos = q_start + lax.broadcasted_iota(jnp.int32, (bq, 1), 0)
    ck = _key_chunk(lp)

    iq = iq_ref[...].astype(BF16)
    a = jnp.concatenate([iq[:, p * LANE:(p + 1) * LANE] for p in range(H_IDX // 2)], axis=0)
    wi = sa_ref[...] * (H_IDX ** -0.5 * D_IDX ** -0.5)
    for c0 in range(0, lp, ck):
        kk = kidx_ref[c0:c0 + ck, :]
        lane = lax.broadcasted_iota(jnp.int32, kk.shape, 1)
        k_lo = jnp.where(lane < D_IDX, kk, 0.0).astype(BF16)
        k_hi = jnp.where(lane >= D_IDX, pltpu.roll(kk, D_IDX, 1), 0.0).astype(BF16)
        s_lo = lax.dot_general(a, k_lo, NT_DIMS, preferred_element_type=F32)
        s_hi = lax.dot_general(a, k_hi, NT_DIMS, preferred_element_type=F32)
        sc = jnp.zeros((bq, ck), F32)
        for p in range(H_IDX // 2):
            w0 = wi[:, D_IDX + 2 * p:D_IDX + 2 * p + 1]
            w1 = wi[:, D_IDX + 2 * p + 1:D_IDX + 2 * p + 2]
            sc = sc + w0 * jnp.maximum(s_lo[p * bq:(p + 1) * bq], 0.0)
            sc = sc + w1 * jnp.maximum(s_hi[p * bq:(p + 1) * bq], 0.0)
        kpos = c0 + lax.broadcasted_iota(jnp.int32, (1, ck), 1)
        valid = ((kpos >> CHUNK_SHIFT) <= (qpos >> CHUNK_SHIFT)) & (kpos < l_true)
        bits = lax.bitcast_convert_type(sc, jnp.int32)
        key = jnp.where(bits < 0, bits ^ 0x7FFFFFFF, bits)
        key = jnp.where(bits == INT_MIN, 0, key)
        key = jnp.where(valid, key, INT_MIN)
        key_ref[:, c0:c0 + ck] = key
        hi_ref[:, c0:c0 + ck] = (key >> 16) + 2 ** 15
        lo_ref[:, c0:c0 + ck] = key & 0xFFFF

    ng = SEARCH_GROUPS
    gr = bq // ng
    groups = [slice(r * gr, (r + 1) * gr) for r in range(ng)]

    def count_ge(ref, rows, cand):
        acc = None
        for j in range(lp // LANE):
            d = (ref[rows, j * LANE:(j + 1) * LANE] - cand) >> 31
            acc = d if acc is None else acc + d
        return lp + jnp.sum(acc.astype(F32), axis=-1, keepdims=True)

    def search16(ref, need):
        def bit_pass(it, ts):
            bit = lax.shift_left(jnp.int32(1), 15 - it)
            out = []
            for rows, t, nd in zip(groups, ts, need):
                cand = t | bit
                out.append(jnp.where(count_ge(ref, rows, cand) >= nd, cand, t))
            return tuple(out)

        return lax.fori_loop(0, 16, bit_pass, (jnp.zeros((gr, 1), jnp.int32),) * ng)

    t_hi = search16(hi_ref, (float(ksel),) * ng)
    need_lo = []
    for rows, t in zip(groups, t_hi):
        need_lo.append(ksel - count_ge(hi_ref, rows, t + 1))
        lo_ref[rows, :] = jnp.where(hi_ref[rows, :] == t, lo_ref[rows, :], -1)
    t_lo = search16(lo_ref, need_lo)
    t_s = jnp.concatenate([lax.shift_left(h - 2 ** 15, 16) | l for h, l in zip(t_hi, t_lo)], axis=0)
    key = key_ref[...]
    sel = (key >= t_s) & (key > INT_MIN)

    d_base = (nd - 1) - (q0 // LANE + (i if bq == LANE else 0))
    nk = lp // LANE
    for n in range(HKV_A):
        q4 = jnp.concatenate([qa_ref[:, (n * GROUP_A + g) * DH_A:(n * GROUP_A + g + 1) * DH_A]
                              for g in range(GROUP_A)], axis=0)
        k_n = ka_ref[:, n * DH_A:(n + 1) * DH_A].astype(BF16)
        v_n = va_ref[:, n * DH_A:(n + 1) * DH_A].astype(BF16)
        lg4 = lax.dot_general(q4, k_n, NT_DIMS, preferred_element_type=F32) * DH_A ** -0.5
        ps, ss = [], []
        for g in range(GROUP_A):
            bias = jnp.concatenate([bias_ref[n * GROUP_A + g, jnp.minimum(d_base + j, nd - 1)]
                                    for j in range(nk)], axis=1)
            lg = jnp.where(sel, lg4[g * bq:(g + 1) * bq] + bias, -jnp.inf)
            p = jnp.exp(lg - jnp.max(lg, axis=-1, keepdims=True))
            ss.append(jnp.sum(p, axis=-1, keepdims=True))
            ps.append(p.astype(BF16))
        o4 = jnp.dot(jnp.concatenate(ps, axis=0), v_n, preferred_element_type=F32)
        for g in range(GROUP_A):
            h = n * GROUP_A + g
            gate = ag_ref[:, h * DH_A:(h + 1) * DH_A]
            o_ref[:, h * DH_A:(h + 1) * DH_A] = (o4[g * bq:(g + 1) * bq] / ss[g] * _silu(gate)).astype(BF16)


def _dsa(qa, z, ka_all, va_all, kidx_all, bias_tiles, *, bq, l_true, q0, ksel, row0, rows, lp):
    b, s, _ = qa.shape
    nd = bias_tiles.shape[1]
    blk0 = row0 // bq
    assert q0 % LANE == 0 and (bq == LANE or s == bq)
    assert nd == (q0 + s - bq) // LANE + 1
    kern = functools.partial(_dsa_kernel, bq=bq, lp=lp, l_true=l_true, q0=q0, ksel=ksel, nd=nd, blk0=blk0)

    def zspec(width, off):
        return pl.BlockSpec((None, bq, width), lambda bi, qi: (bi, qi + blk0, off // width))

    def kspec(width):
        return pl.BlockSpec((None, lp, width), lambda bi, qi: (bi, 0, 0))

    return pl.pallas_call(
        kern,
        grid=(b, rows // bq),
        in_specs=[pl.BlockSpec((None, bq, 1024), lambda bi, qi: (bi, qi + blk0, 0)),
                  zspec(1024, E_IQ), zspec(128, E_IDX), zspec(1024, E_AG),
                  kspec(256), kspec(256), kspec(128),
                  _const_spec(bias_tiles.shape)],
        out_specs=pl.BlockSpec((None, bq, 1024), lambda bi, qi: (bi, qi, 0)),
        out_shape=jax.ShapeDtypeStruct((b, rows, 1024), BF16),
        scratch_shapes=[pltpu.VMEM((bq, lp), jnp.int32)] * 3,
        compiler_params=_params("parallel", "arbitrary"),
        name="dsa",
    )(qa, z, z, z, ka_all, va_all, kidx_all, bias_tiles)


def _causal_classes(s, lp, q0):
    if q0 == 0 and s % CLASS_ROWS == 0 and lp == s:
        return [(c * CLASS_ROWS, CLASS_ROWS, (c + 1) * CLASS_ROWS) for c in range(s // CLASS_ROWS)]
    return [(0, s, lp)]


def _mla_kernel(qb_ref, bg_ref, kb_ref, vb_ref, o_ref, *, bq, lp, l_true, q0, blk0, n_full):
    q_start = q0 + (pl.program_id(1) + blk0) * bq
    qpos = q_start + lax.broadcasted_iota(jnp.int32, (bq, 1), 0)
    kpos = n_full + lax.broadcasted_iota(jnp.int32, (1, lp - n_full), 1)
    valid = ((kpos >> CHUNK_SHIFT) <= (qpos >> CHUNK_SHIFT)) & (kpos < l_true)
    for h in range(H_B):
        q = qb_ref[:, h * 256:(h + 1) * 256]
        parts = []
        if n_full:
            parts.append((lax.dot_general(q, kb_ref[:n_full, h * 256:(h + 1) * 256], NT_DIMS,
                                          preferred_element_type=F32), vb_ref[:n_full, h * V_DIM:(h + 1) * V_DIM]))
        lg = lax.dot_general(q, kb_ref[n_full:, h * 256:(h + 1) * 256], NT_DIMS, preferred_element_type=F32)
        parts.append((jnp.where(valid, lg, -jnp.inf), vb_ref[n_full:, h * V_DIM:(h + 1) * V_DIM]))
        o = _softmax_pv(parts, QK_B ** -0.5)
        gate = bg_ref[:, h * V_DIM:(h + 1) * V_DIM]
        o_ref[:, h * V_DIM:(h + 1) * V_DIM] = (o * _silu(gate)).astype(BF16)


def _mla(qb, z, kb, vb, *, bq, l_true, q0, row0, rows, lp):
    b = qb.shape[0]
    blk0 = row0 // bq
    n_full = min(min(q0 + row0 + CHUNK, l_true) // LANE * LANE, lp - LANE)
    kern = functools.partial(_mla_kernel, bq=bq, lp=lp, l_true=l_true, q0=q0, blk0=blk0, n_full=n_full)
    return pl.pallas_call(
        kern,
        grid=(b, rows // bq),
        in_specs=[pl.BlockSpec((None, bq, 2048), lambda bi, qi: (bi, qi + blk0, 0)),
                  pl.BlockSpec((None, bq, 1024), lambda bi, qi: (bi, qi + blk0, E_BG // 1024)),
                  pl.BlockSpec((None, lp, 2048), lambda bi, qi: (bi, 0, 0)),
                  pl.BlockSpec((None, lp, 1024), lambda bi, qi: (bi, 0, 0))],
        out_specs=pl.BlockSpec((None, bq, 1024), lambda bi, qi: (bi, qi, 0)),
        out_shape=jax.ShapeDtypeStruct((b, rows, 1024), BF16),
        compiler_params=_params("parallel", "arbitrary"),
        name="mla",
    )(qb, z, kb, vb)


def _out_ple_kernel(h_ref, oa_ref, ob_ref, p_ref, wo_ref, wg_ref, wp_ref, o_ref):
    half = oa_ref.shape[-1]
    h1 = (h_ref[...]
          + jnp.dot(oa_ref[...], wo_ref[:half, :], preferred_element_type=F32)
          + jnp.dot(ob_ref[...], wo_ref[half:, :], preferred_element_type=F32))
    r = (h1 * _rs(h1)).astype(BF16)
    gate = 1.0 / (1.0 + jnp.exp(-jnp.dot(r, wg_ref[...], preferred_element_type=F32)))
    o_ref[...] = h1 + gate * jnp.dot(p_ref[...].astype(BF16), wp_ref[...], preferred_element_type=F32)


def _out_ple(h, oa, ob, p, wo, wg, wp, bm):
    m, d = h.shape
    half = oa.shape[1]

    def rows(width):
        return pl.BlockSpec((bm, width), lambda i: (i, 0))

    return pl.pallas_call(
        _out_ple_kernel,
        grid=(m // bm,),
        in_specs=[rows(d), rows(half), rows(half), rows(PLE_DIM),
                  _const_spec((2 * half, d)), _const_spec((d, d)), _const_spec((PLE_DIM, d))],
        out_specs=rows(d),
        out_shape=jax.ShapeDtypeStruct((m, d), F32),
        compiler_params=_params("parallel"),
        name="out_ple",
    )(h, oa, ob, p, wo, wg, wp)


def _odd_prep_kernel(cq_ref, ck_ref, cv_ref, dv_ref, cqn_ref, ckn_ref, dg_ref, db_ref,
                     qc_o, kc_o, kcb_o, vcb_o, dvn_o):
    for h in range(H_C):
        sl = slice(h * DH_C, (h + 1) * DH_C)
        x = cq_ref[:, sl]
        qc_o[:, sl] = (x * _rs(x) * cqn_ref[...]).astype(BF16)
        x = ck_ref[:, sl]
        kn = x * _rs(x) * ckn_ref[...]
        kc_o[:, sl] = kn
        kcb_o[:, sl] = kn.astype(BF16)
    vcb_o[...] = cv_ref[...].astype(BF16)
    dv = dv_ref[...]
    xc = dv - jnp.mean(dv, -1, keepdims=True)
    var = jnp.mean(xc * xc, -1, keepdims=True)
    dvn_o[...] = xc * lax.rsqrt(var + 1e-5) * dg_ref[...] + db_ref[...]


def _odd_prep(z, cqn, ckn, dg, db, bs):
    b, s, _ = z.shape

    def zspec(off):
        return pl.BlockSpec((None, bs, 1024), lambda bi, si: (bi, si, off // 1024))

    ospec = pl.BlockSpec((None, bs, 1024), lambda bi, si: (bi, si, 0))
    return pl.pallas_call(
        _odd_prep_kernel,
        grid=(b, s // bs),
        in_specs=[zspec(O_CQ), zspec(O_CK), zspec(O_CV), zspec(O_DV),
                  _const_spec((1, DH_C)), _const_spec((1, DH_C)), _const_spec((1, W_D)), _const_spec((1, W_D))],
        out_specs=[ospec] * 5,
        out_shape=[jax.ShapeDtypeStruct((b, s, 1024), BF16), jax.ShapeDtypeStruct((b, s, 1024), F32),
                   jax.ShapeDtypeStruct((b, s, 1024), BF16), jax.ShapeDtypeStruct((b, s, 1024), BF16),
                   jax.ShapeDtypeStruct((b, s, 1024), F32)],
        compiler_params=_params("parallel", "arbitrary"),
        name="odd_prep",
    )(z, z, z, z, cqn, ckn, dg, db)


BACK_TILES = C_BACK * CHUNK // LANE


def _band_width(bq):
    return (BACK_TILES + -(-bq // LANE)) * LANE


def _band_kernel(q_ref, cg_ref, k_ref, v_ref, bias_ref, o_ref, *, bq, sliding, q0, k0, k_end):
    i = pl.program_id(1)
    q_start = q0 + i * bq
    bw = _band_width(bq)
    if sliding:
        first = i * (bq // LANE) - BACK_TILES
        win_start = first * LANE
        rows = [pl.ds(pl.multiple_of(jnp.maximum(first + t, 0) * LANE, LANE), LANE) for t in range(bw // LANE)]
    else:
        win_start = k0
        rows = [pl.ds(t * LANE, LANE) for t in range(bw // LANE)]
    kw = jnp.concatenate([k_ref[r, :] for r in rows], axis=0)
    vw = jnp.concatenate([v_ref[r, :] for r in rows], axis=0)
    qpos = q_start + lax.broadcasted_iota(jnp.int32, (bq, 1), 0)
    kpos = win_start + lax.broadcasted_iota(jnp.int32, (1, bw), 1)
    dc = (qpos >> CHUNK_SHIFT) - (kpos >> CHUNK_SHIFT)
    valid = (dc >= 0) & (dc <= C_BACK) & (kpos >= 0) & (kpos < k_end)
    for h in range(H_C):
        sl = slice(h * DH_C, (h + 1) * DH_C)
        lg = lax.dot_general(q_ref[:, sl], kw[:, sl], NT_DIMS, preferred_element_type=F32) * DH_C ** -0.5
        lg = jnp.where(valid, lg + bias_ref[h], -jnp.inf)
        o = _softmax_pv([(lg, vw[:, sl])])
        o_ref[:, sl] = (o * _silu(cg_ref[:, sl])).astype(BF16)


def _band(qc, z, kcb, vcb, bias, *, bq, sliding, q0, k0, k_end):
    b, s, _ = qc.shape
    lk = kcb.shape[1]
    if sliding:
        assert bq % LANE == 0 and q0 == 0 and k0 == 0
    else:
        assert lk == _band_width(bq) and s == bq
    kern = functools.partial(_band_kernel, bq=bq, sliding=sliding, q0=q0, k0=k0, k_end=k_end)
    return pl.pallas_call(
        kern,
        grid=(b, s // bq),
        in_specs=[pl.BlockSpec((None, bq, 1024), lambda bi, qi: (bi, qi, 0)),
                  pl.BlockSpec((None, bq, 1024), lambda bi, qi: (bi, qi, O_CG // 1024)),
                  pl.BlockSpec((None, lk, 1024), lambda bi, qi: (bi, 0, 0)),
                  pl.BlockSpec((None, lk, 1024), lambda bi, qi: (bi, 0, 0)),
                  _const_spec(bias.shape)],
        out_specs=pl.BlockSpec((None, bq, 1024), lambda bi, qi: (bi, qi, 0)),
        out_shape=jax.ShapeDtypeStruct((b, s, 1024), BF16),
        compiler_params=_params("parallel", "arbitrary"),
        name="band",
    )(qc, z, kcb, vcb, bias)


def _sgate_kernel(dvn_ref, du_ref, dg_ref, ws_ref, bs_ref, o_ref, *, n):
    row = lax.broadcasted_iota(jnp.int32, (n, n), 0)
    col = lax.broadcasted_iota(jnp.int32, (n, n), 1)
    tril = col <= row
    for g in range(G_D):
        sl = slice(g * DG_D, (g + 1) * DG_D)
        w = jnp.where(tril, ws_ref[g], 0.0).astype(BF16)
        for c in range(dvn_ref.shape[0] // n):
            rs = slice(c * n, (c + 1) * n)
            sg = jnp.dot(w, dvn_ref[rs, sl].astype(BF16), preferred_element_type=F32) + bs_ref[:, g:g + 1]
            o_ref[rs, sl] = (du_ref[rs, sl] * sg * _silu(dg_ref[rs, sl])).astype(BF16)


def _sgate(dvn, z, ws, bs_t, n):
    b, s, _ = dvn.shape
    br = n * math.gcd(s // n, 4)

    def zspec(off):
        return pl.BlockSpec((None, br, 1024), lambda bi, ci: (bi, ci, off // 1024))

    return pl.pallas_call(
        functools.partial(_sgate_kernel, n=n),
        grid=(b, s // br),
        in_specs=[pl.BlockSpec((None, br, 1024), lambda bi, ci: (bi, ci, 0)), zspec(O_DU), zspec(O_DG),
                  _const_spec((G_D, n, n)), _const_spec((n, G_D))],
        out_specs=pl.BlockSpec((None, br, 1024), lambda bi, ci: (bi, ci, 0)),
        out_shape=jax.ShapeDtypeStruct((b, s, 1024), BF16),
        compiler_params=_params("parallel", "arbitrary"),
        name="sgate",
    )(dvn, z, z, ws, bs_t)


def _rope_tables(pos):
    half = ROPE_DIM // 2
    freq = ROPE_BASE ** (-jnp.arange(half, dtype=F32) / half)
    ang = pos.astype(F32)[:, None] * freq[None, :]
    cos, sin = jnp.cos(ang), jnp.sin(ang)
    z = jnp.zeros_like(cos)
    return jnp.concatenate([cos, z, cos, z], 1), jnp.concatenate([-sin, z, sin, z], 1)


def _rope_lanes(x):
    half = ROPE_DIM // 2
    z = jnp.zeros(x.shape[:-1] + (half,), x.dtype)
    return jnp.concatenate([x[..., :half], z, x[..., half:], z], -1)


def _rope_unlanes(x):
    half = ROPE_DIM // 2
    return jnp.concatenate([x[..., :half], x[..., 2 * half:3 * half]], -1)


def _t5_bucket_np(rel):
    nb = T5_BUCKETS // 2
    max_exact = nb // 2
    n = np.abs(rel)
    nf = np.maximum(n, 1).astype(np.float64)
    large = max_exact + (np.log(nf / max_exact) / math.log(T5_MAX_DIST / max_exact) * (nb - max_exact)).astype(np.int64)
    large = np.minimum(large, nb - 1)
    return np.where(rel > 0, nb, 0) + np.where(n < max_exact, n, large)


def _toeplitz(w, rows, width, cols):
    flat = jnp.tile(w, (1,) * (w.ndim - 1) + (rows,))[..., :rows * width]
    return flat.reshape(w.shape[:-1] + (rows, width))[..., :cols]


def _t5_tiles(t5_bias, bq, nd):
    width = 2 * LANE
    k = np.arange(width + 1)
    delta = np.where(k < LANE, k, k - (width + 1))
    rel = (np.arange(nd) - (nd - 1))[:, None] * LANE + delta[None, :]
    w = jnp.transpose(t5_bias[_t5_bucket_np(rel)], (2, 0, 1))
    return _toeplitz(w, bq, width, LANE)


def _band_bias(rel_tab, bq, qk_off):
    bw = _band_width(bq)
    width = 2 * bw
    k = np.arange(width + 1)
    delta = np.where(k < bw, k, k - (width + 1))
    idx = np.clip(qk_off - delta, -(CHUNK - 1), REL_CLIP) + (CHUNK - 1)
    return _toeplitz(jnp.transpose(rel_tab[idx], (1, 0)), bq, width, bw)


def _even_weights(w_in, b_wuq, b_wukv, b_qn, b_kn):
    d = w_in.shape[0]
    offs = np.cumsum((0,) + EVEN_SPLITS)
    aq, ak, av, ag, iq, ik, iw, bcq, bckv, bkpe, bg = [w_in[:, offs[t]:offs[t + 1]] for t in range(11)]
    slab_idx = jnp.concatenate([ik, iw, jnp.zeros((d, LANE - D_IDX - H_IDX), w_in.dtype)], 1)
    w = jnp.concatenate([aq, ag, iq, bg, bcq, ak, av, bckv, slab_idx, _rope_lanes(bkpe)], 1).astype(BF16)
    uq = b_wuq.reshape(Q_LORA, H_B, QK_B)
    uq = jnp.concatenate([uq[..., :NOPE], _rope_lanes(uq[..., NOPE:])], -1).reshape(Q_LORA, H_B * 256).astype(BF16)
    ukv = b_wukv.reshape(KV_LORA, H_B, NOPE + V_DIM)
    ukv = jnp.concatenate([ukv[..., :NOPE].reshape(KV_LORA, H_B * NOPE),
                           ukv[..., NOPE:].reshape(KV_LORA, H_B * V_DIM)], 1).astype(BF16)
    pad_gain = lambda g: jnp.concatenate([g[:NOPE], _rope_lanes(g[NOPE:])])[None, :]
    return w, uq, ukv, pad_gain(b_qn), pad_gain(b_kn)


def _pad_rows(x, lp):
    return jnp.pad(x, ((0, 0), (0, lp - x.shape[1]), (0, 0)))


def _even_layer(h, p, past, q0, ln_g, w_in, uq, ukv, a_qn, a_kn, t5_bias, b_qln, b_kvln, bqn, bkn, wo, wg, wp):
    b, s, d = h.shape
    m = b * s
    z = _norm_mm(h.reshape(m, d), ln_g[None, :], w_in, min(m, 1024), 512).reshape(b, s, E_END)
    cos, sin = _rope_tables(q0 + jnp.arange(s, dtype=jnp.int32))
    bs = min(s, 256)
    qa, ka, qb, ckv, kpe_l = _even_prep(z, cos, sin, a_qn[None, :], a_kn[None, :], b_qln[None, :], b_kvln[None, :],
                                        uq, bqn, bs)
    av = z[..., E_AV:E_AV + 256]
    kidx = z[..., E_IDX:E_IDX + LANE]
    new = (ka.reshape(b, s, HKV_A, DH_A), av.reshape(b, s, HKV_A, DH_A), kidx[..., :D_IDX], ckv, _rope_unlanes(kpe_l))
    if past is None:
        l_true = s
        ka_all, va_all, kidx_all, ckv_all, kpe_all = ka, av, kidx, ckv, kpe_l
    else:
        c_k, c_v, c_ik, c_ckv, c_kpe = past
        pl_ = c_k.shape[1]
        l_true = pl_ + s
        lp = -(-l_true // LANE) * LANE
        cat = lambda c, n_: _pad_rows(jnp.concatenate([c, n_], 1), lp)
        ka_all = cat(c_k.reshape(b, pl_, 256), ka)
        va_all = cat(c_v.reshape(b, pl_, 256), av)
        kidx_all = cat(jnp.pad(c_ik, ((0, 0), (0, 0), (0, LANE - D_IDX))), kidx)
        ckv_all = cat(c_ckv, ckv)
        kpe_all = cat(_rope_lanes(c_kpe), kpe_l)
    lp = ka_all.shape[1]
    ksel = min(TOPK_MAX, l_true // 4)
    bq = min(s, LANE)
    nd = (q0 + s - bq) // LANE + 1
    tiles = _t5_tiles(t5_bias, bq, nd)
    classes = _causal_classes(s, lp, q0)
    o_a = jnp.concatenate([_dsa(qa, z, ka_all, va_all, kidx_all, tiles, bq=bq, l_true=l_true, q0=q0, ksel=ksel,
                                row0=r0, rows=nr, lp=lc) for r0, nr, lc in classes], axis=1)
    kb, vb = _mla_kv(ckv_all, kpe_all, ukv, bkn, _key_chunk(lp))
    o_b = jnp.concatenate([_mla(qb, z, kb, vb, bq=min(s, 256), l_true=l_true, q0=q0,
                                row0=r0, rows=nr, lp=lc) for r0, nr, lc in classes], axis=1)
    y = _out_ple(h.reshape(m, d), o_a.reshape(m, -1), o_b.reshape(m, -1), p.reshape(m, -1), wo, wg, wp, min(m, 256))
    return y.reshape(b, s, d), new


def _odd_layer(h, p, past, q0, ln_g, w_in, c_qn, c_kn, c_rel, d_g, d_b, d_ws, d_bs, wo, wg, wp):
    b, s, d = h.shape
    m = b * s
    z = _norm_mm(h.reshape(m, d), ln_g[None, :], w_in, min(m, 1024), 1024).reshape(b, s, O_END)
    qc, kc, kcb, vcb, dvn = _odd_prep(z, c_qn[None, :], c_kn[None, :], d_g[None, :], d_b[None, :], min(s, 256))
    cv = z[..., O_CV:O_CV + 1024]
    if past is None:
        keep = min(C_BACK * CHUNK, s)
        c_new = (kc[:, s - keep:].reshape(b, keep, H_C, DH_C), cv[:, s - keep:].reshape(b, keep, H_C, DH_C))
        bqc = 2 * LANE
        bias = _band_bias(c_rel, bqc, C_BACK * CHUNK)
        o_c = _band(qc, z, kcb, vcb, bias, bq=bqc, sliding=True, q0=0, k0=0, k_end=s)
    else:
        nc = past[0].shape[1]
        c_new = (kc.reshape(b, s, H_C, DH_C), cv.reshape(b, s, H_C, DH_C))
        cat = lambda c, n_: _pad_rows(jnp.concatenate([c.reshape(b, nc, 1024).astype(BF16), n_], 1), _band_width(s))
        bias = _band_bias(c_rel, s, nc)
        o_c = _band(qc, z, cat(past[0], kcb), cat(past[1], vcb), bias,
                    bq=s, sliding=False, q0=q0, k0=q0 - nc, k_end=q0 + s)
    n = min(s, D_CHUNK)
    o_d = _sgate(dvn, z, d_ws[:, :n, :n], d_bs[:, :n].T, n)
    y = _out_ple(h.reshape(m, d), o_c.reshape(m, -1), o_d.reshape(m, -1), p.reshape(m, -1), wo, wg, wp, min(m, 256))
    return y.reshape(b, s, d), c_new, dvn


def kernel(x_prompt, x_sample, cache_a_k, cache_a_v, cache_a_idx_k, cache_b_ckv, cache_b_kpe, cache_c_k, cache_c_v, p_prompt, p_sample, ln_g, w_in_even, a_q_norm, a_k_norm, t5_bias, b_q_lora_norm, b_kv_lora_norm, b_w_uq, b_w_ukv, b_q_norm, b_k_norm, w_out_even, w_in_odd, c_q_norm, c_k_norm, c_rel_bias, d_ln_g, d_ln_b, d_w_s, d_b_s, w_out_odd, ple_proj, ple_gate):
    depth = ln_g.shape[0]
    past_len = cache_a_k.shape[2]
    hp, hs = x_prompt, x_sample
    ev_p, ev_s, od_p, od_s, dv_s = [], [], [], [], []
    for i in range(depth):
        j = i // 2
        wg = ple_gate[i].astype(BF16)
        wp = ple_proj[i].astype(BF16)
        if i % 2 == 0:
            w_in, uq, ukv, bqn, bkn = _even_weights(w_in_even[j], b_w_uq[j], b_w_ukv[j], b_q_norm[j], b_k_norm[j])
            w = (ln_g[i], w_in, uq, ukv, a_q_norm[j], a_k_norm[j], t5_bias, b_q_lora_norm[j], b_kv_lora_norm[j],
                 bqn, bkn, w_out_even[j].astype(BF16), wg, wp)
            hp, sp = _even_layer(hp, p_prompt[i], None, 0, *w)
            past = (cache_a_k[j], cache_a_v[j], cache_a_idx_k[j], cache_b_ckv[j], cache_b_kpe[j])
            hs, ss = _even_layer(hs, p_sample[i], past, past_len, *w)
            ev_p.append(sp)
            ev_s.append(ss)
        else:
            w = (ln_g[i], w_in_odd[j].astype(BF16), c_q_norm[j], c_k_norm[j], c_rel_bias[j], d_ln_g[j], d_ln_b[j],
                 d_w_s[j], d_b_s[j], w_out_odd[j].astype(BF16), wg, wp)
            hp, sp, _ = _odd_layer(hp, p_prompt[i], None, 0, *w)
            hs, ss, dvs = _odd_layer(hs, p_sample[i], (cache_c_k[j], cache_c_v[j]), past_len, *w)
            od_p.append(sp)
            od_s.append(ss)
            dv_s.append(dvs)
    st = lambda lst, n_: jnp.stack([e[n_] for e in lst], 0)
    return (hp, hs, st(ev_p, 0), st(ev_p, 1), st(ev_p, 2), st(ev_p, 3), st(ev_p, 4), st(od_p, 0), st(od_p, 1),
            st(ev_s, 0), st(ev_s, 1), st(ev_s, 2), st(ev_s, 3), st(ev_s, 4), st(od_s, 0), st(od_s, 1),
            jnp.stack(dv_s, 0))
```

```python
import functools
import math

import numpy as np
import jax
import jax.numpy as jnp
from jax import lax
from jax.experimental import pallas as pl
from jax.experimental.pallas import tpu as pltpu

F32 = jnp.float32
BF16 = jnp.bfloat16
INT_MIN = -2 ** 31
LOG2E = math.log2(math.e)
CLASS_ROWS = 512
SEARCH_GROUPS = 4

D_MODEL = 2048
CHUNK = 64
CHUNK_SHIFT = 6
LANE = 128
H_A, HKV_A, GROUP_A, DH_A = 8, 2, 4, 128
H_IDX, D_IDX = 16, 64
TOPK_MAX = 256
T5_BUCKETS, T5_MAX_DIST = 32, 128
H_B, Q_LORA, KV_LORA, NOPE, ROPE_DIM, V_DIM = 8, 512, 256, 128, 64, 128
ROPE_BASE = 10000.0
QK_B = NOPE + ROPE_DIM
H_C, DH_C, C_BACK, REL_CLIP = 8, 128, 8, 128
W_D, G_D, DG_D, D_CHUNK = 1024, 8, 128, 128
PLE_DIM = 256

EVEN_SPLITS = (H_A * DH_A, HKV_A * DH_A, HKV_A * DH_A, H_A * DH_A, H_IDX * D_IDX, D_IDX, H_IDX,
               Q_LORA, KV_LORA, ROPE_DIM, H_B * V_DIM)
E_AQ, E_AG, E_IQ, E_BG, E_BCQ, E_AK, E_AV, E_CKV, E_IDX, E_KPE, E_END = (
    0, 1024, 2048, 3072, 4096, 4608, 4864, 5120, 5376, 5504, 5632)
O_CQ, O_CK, O_CV, O_CG, O_DU, O_DV, O_DG, O_END = 0, 1024, 2048, 3072, 4096, 5120, 6144, 7168

VMEM_LIMIT_BYTES = 56 * 1024 * 1024
NT_DIMS = (((1,), (1,)), ((), ()))


def _params(*sem):
    return pltpu.CompilerParams(dimension_semantics=sem, vmem_limit_bytes=VMEM_LIMIT_BYTES)


def _const_spec(shape):
    zeros = (0,) * len(shape)
    return pl.BlockSpec(shape, lambda *_: zeros, pipeline_mode=pl.Buffered(1))


def _rs(x, n=None, eps=1e-6):
    n = x.shape[-1] if n is None else n
    return lax.rsqrt(jnp.sum(x * x, axis=-1, keepdims=True) / n + eps)


def _silu(x):
    return x * (1.0 / (1.0 + jnp.exp(-x)))


def _softmax_pv(parts, scale=1.0):
    m = functools.reduce(jnp.maximum, [jnp.max(lg, axis=-1, keepdims=True) for lg, _ in parts])
    o = s = None
    for lg, v in parts:
        p = jnp.exp2((lg - m) * (scale * LOG2E))
        ps = jnp.sum(p, axis=-1, keepdims=True)
        po = jnp.dot(p.astype(BF16), v, preferred_element_type=F32)
        o, s = (po, ps) if o is None else (o + po, s + ps)
    return o / s


def _norm_mm_kernel(x_ref, g_ref, w_ref, o_ref, xn_ref):
    @pl.when(pl.program_id(1) == 0)
    def _():
        x = x_ref[...]
        xn_ref[...] = (x * _rs(x) * g_ref[...]).astype(BF16)

    o_ref[...] = jnp.dot(xn_ref[...], w_ref[...], preferred_element_type=F32)


def _norm_mm(x, g, w, bm, bn):
    m, d = x.shape
    n = w.shape[1]
    return pl.pallas_call(
        _norm_mm_kernel,
        grid=(m // bm, n // bn),
        in_specs=[pl.BlockSpec((bm, d), lambda i, j: (i, 0)),
                  pl.BlockSpec((1, d), lambda i, j: (0, 0)),
                  pl.BlockSpec((d, bn), lambda i, j: (0, j))],
        out_specs=pl.BlockSpec((bm, bn), lambda i, j: (i, j)),
        out_shape=jax.ShapeDtypeStruct((m, n), F32),
        scratch_shapes=[pltpu.VMEM((bm, d), BF16)],
        compiler_params=_params("parallel", "arbitrary"),
        name="norm_mm",
    )(x, g, w)


def _rope(x, cos, sin):
    return x * cos + pltpu.roll(x, 64, 1) * sin


def _even_prep_kernel(aq_ref, bcq_ref, ak_ref, ckv_ref, kpe_ref, cos_ref, sin_ref,
                      aqn_ref, akn_ref, qln_ref, kvln_ref, wuq_ref, bqn_ref,
                      qa_o, ka_o, qb_o, ckv_o, kpe_o):
    for h in range(H_A):
        x = aq_ref[:, h * DH_A:(h + 1) * DH_A]
        qa_o[:, h * DH_A:(h + 1) * DH_A] = (x * _rs(x) * aqn_ref[...]).astype(BF16)
    for n in range(HKV_A):
        x = ak_ref[:, n * DH_A:(n + 1) * DH_A]
        ka_o[:, n * DH_A:(n + 1) * DH_A] = x * _rs(x) * akn_ref[...]
    c = ckv_ref[...]
    ckv_o[...] = c * _rs(c) * kvln_ref[...]
    cos = cos_ref[...]
    sin = sin_ref[...]
    kpe_o[...] = _rope(kpe_ref[...], cos, sin)
    cq = bcq_ref[...]
    cqn = (cq * _rs(cq) * qln_ref[...]).astype(BF16)
    qb = jnp.dot(cqn, wuq_ref[...], preferred_element_type=F32)
    g = bqn_ref[...]
    for h in range(H_B):
        nope = qb[:, h * 256:h * 256 + 128]
        rot = _rope(qb[:, h * 256 + 128:(h + 1) * 256], cos, sin)
        ss = jnp.sum(nope * nope, -1, keepdims=True) + jnp.sum(rot * rot, -1, keepdims=True)
        r = lax.rsqrt(ss / QK_B + 1e-6)
        qb_o[:, h * 256:h * 256 + 128] = (nope * r * g[:, :128]).astype(BF16)
        qb_o[:, h * 256 + 128:(h + 1) * 256] = (rot * r * g[:, 128:]).astype(BF16)


def _even_prep(z, cos, sin, aqn, akn, qln, kvln, wuq, bqn, bs):
    b, s, _ = z.shape

    def zspec(width, off):
        return pl.BlockSpec((None, bs, width), lambda bi, si: (bi, si, off // width))

    def ospec(width):
        return pl.BlockSpec((None, bs, width), lambda bi, si: (bi, si, 0))

    pos_spec = pl.BlockSpec((bs, LANE), lambda bi, si: (si, 0))
    return pl.pallas_call(
        _even_prep_kernel,
        grid=(b, s // bs),
        in_specs=[zspec(1024, E_AQ), zspec(512, E_BCQ), zspec(256, E_AK), zspec(256, E_CKV),
                  zspec(128, E_KPE), pos_spec, pos_spec,
                  _const_spec((1, DH_A)), _const_spec((1, DH_A)), _const_spec((1, Q_LORA)),
                  _const_spec((1, KV_LORA)), _const_spec((Q_LORA, H_B * 256)), _const_spec((1, 256))],
        out_specs=[ospec(1024), ospec(256), ospec(2048), ospec(256), ospec(128)],
        out_shape=[jax.ShapeDtypeStruct((b, s, 1024), BF16), jax.ShapeDtypeStruct((b, s, 256), F32),
                   jax.ShapeDtypeStruct((b, s, 2048), BF16), jax.ShapeDtypeStruct((b, s, 256), F32),
                   jax.ShapeDtypeStruct((b, s, 128), F32)],
        compiler_params=_params("parallel", "arbitrary"),
        name="even_prep",
    )(z, z, z, z, z, cos, sin, aqn, akn, qln, kvln, wuq, bqn)


def _mla_kv_kernel(ckv_ref, kpe_ref, w_ref, g_ref, kb_o, vb_o):
    kv = jnp.dot(ckv_ref[...].astype(BF16), w_ref[...], preferred_element_type=F32)
    kp = kpe_ref[...]
    skp = jnp.sum(kp * kp, -1, keepdims=True)
    g = g_ref[...]
    for h in range(H_B):
        nope = kv[:, h * NOPE:(h + 1) * NOPE]
        r = lax.rsqrt((jnp.sum(nope * nope, -1, keepdims=True) + skp) / QK_B + 1e-6)
        kb_o[:, h * 256:h * 256 + 128] = (nope * r * g[:, :128]).astype(BF16)
        kb_o[:, h * 256 + 128:(h + 1) * 256] = (kp * r * g[:, 128:]).astype(BF16)
    vb_o[...] = kv[:, H_B * NOPE:].astype(BF16)


def _mla_kv(ckv_all, kpe_all, wukv, bkn, bl):
    b, lp, _ = ckv_all.shape
    return pl.pallas_call(
        _mla_kv_kernel,
        grid=(b, lp // bl),
        in_specs=[pl.BlockSpec((None, bl, KV_LORA), lambda bi, li: (bi, li, 0)),
                  pl.BlockSpec((None, bl, LANE), lambda bi, li: (bi, li, 0)),
                  _const_spec((KV_LORA, 2048)), _const_spec((1, 256))],
        out_specs=[pl.BlockSpec((None, bl, 2048), lambda bi, li: (bi, li, 0)),
                   pl.BlockSpec((None, bl, 1024), lambda bi, li: (bi, li, 0))],
        out_shape=[jax.ShapeDtypeStruct((b, lp, 2048), BF16), jax.ShapeDtypeStruct((b, lp, 1024), BF16)],
        compiler_params=_params("parallel", "arbitrary"),
        name="mla_kv",
    )(ckv_all, kpe_all, wukv, bkn)


def _key_chunk(lp):
    for c in (512, 384, 256, 128):
        if lp % c == 0:
            return c
    raise ValueError(lp)


def _dsa_kernel(qa_ref, iq_ref, sa_ref, ag_ref, ka_ref, va_ref, kidx_ref, bias_ref, o_ref, key_ref, hi_ref, lo_ref,
                *, bq, lp, l_true, q0, ksel, nd, blk0):
    i = pl.program_id(1) + blk0
    q_start = q0 + i * bq
    qpos = q_start + lax.broadcasted_iota(jnp.int32, (bq, 1), 0)
    ck = _key_chunk(lp)

    iq = iq_ref[...].astype(BF16)
    a = jnp.concatenate([iq[:, p * LANE:(p + 1) * LANE] for p in range(H_IDX // 2)], axis=0)
    wi = sa_ref[...] * (H_IDX ** -0.5 * D_IDX ** -0.5)
    for c0 in range(0, lp, ck):
        kk = kidx_ref[c0:c0 + ck, :]
        lane = lax.broadcasted_iota(jnp.int32, kk.shape, 1)
        k_lo = jnp.where(lane < D_IDX, kk, 0.0).astype(BF16)
        k_hi = jnp.where(lane >= D_IDX, pltpu.roll(kk, D_IDX, 1), 0.0).astype(BF16)
        s_lo = lax.dot_general(a, k_lo, NT_DIMS, preferred_element_type=F32)
        s_hi = lax.dot_general(a, k_hi, NT_DIMS, preferred_element_type=F32)
        sc = jnp.zeros((bq, ck), F32)
        for p in range(H_IDX // 2):
            w0 = wi[:, D_IDX + 2 * p:D_IDX + 2 * p + 1]
            w1 = wi[:, D_IDX + 2 * p + 1:D_IDX + 2 * p + 2]
            sc = sc + w0 * jnp.maximum(s_lo[p * bq:(p + 1) * bq], 0.0)
            sc = sc + w1 * jnp.maximum(s_hi[p * bq:(p + 1) * bq], 0.0)
        kpos = c0 + lax.broadcasted_iota(jnp.int32, (1, ck), 1)
        valid = ((kpos >> CHUNK_SHIFT) <= (qpos >> CHUNK_SHIFT)) & (kpos < l_true)
        bits = lax.bitcast_convert_type(sc, jnp.int32)
        key = jnp.where(bits < 0, bits ^ 0x7FFFFFFF, bits)
        key = jnp.where(bits == INT_MIN, 0, key)
        key = jnp.where(valid, key, INT_MIN)
        key_ref[:, c0:c0 + ck] = key
        hi_ref[:, c0:c0 + ck] = (key >> 16) + 2 ** 15
        lo_ref[:, c0:c0 + ck] = key & 0xFFFF

    ng = SEARCH_GROUPS
    gr = bq // ng
    groups = [slice(r * gr, (r + 1) * gr) for r in range(ng)]

    def count_ge(ref, rows, cand):
        acc = None
        for j in range(lp // LANE):
            d = (ref[rows, j * LANE:(j + 1) * LANE] - cand) >> 31
            acc = d if acc is None else acc + d
        return lp + jnp.sum(acc.astype(F32), axis=-1, keepdims=True)

    def search16(ref, need):
        ts = [jnp.zeros((gr, 1), jnp.int32)] * ng
        for bit in reversed(range(16)):
            for r in range(ng):
                cand = ts[r] | (1 << bit)
                ts[r] = jnp.where(count_ge(ref, groups[r], cand) >= need[r], cand, ts[r])
        return ts

    t_hi = search16(hi_ref, (float(ksel),) * ng)
    need_lo = []
    for rows, t in zip(groups, t_hi):
        need_lo.append(ksel - count_ge(hi_ref, rows, t + 1))
        lo_ref[rows, :] = jnp.where(hi_ref[rows, :] == t, lo_ref[rows, :], -1)
    t_lo = search16(lo_ref, need_lo)
    t_s = jnp.concatenate([lax.shift_left(h - 2 ** 15, 16) | l for h, l in zip(t_hi, t_lo)], axis=0)
    key = key_ref[...]
    sel = (key >= t_s) & (key > INT_MIN)

    d_base = (nd - 1) - (q0 // LANE + (i if bq == LANE else 0))
    nk = lp // LANE
    for n in range(HKV_A):
        q4 = jnp.concatenate([qa_ref[:, (n * GROUP_A + g) * DH_A:(n * GROUP_A + g + 1) * DH_A]
                              for g in range(GROUP_A)], axis=0)
        k_n = ka_ref[:, n * DH_A:(n + 1) * DH_A].astype(BF16)
        v_n = va_ref[:, n * DH_A:(n + 1) * DH_A].astype(BF16)
        lg4 = lax.dot_general(q4, k_n, NT_DIMS, preferred_element_type=F32) * DH_A ** -0.5
        ps, ss = [], []
        for g in range(GROUP_A):
            bias = jnp.concatenate([bias_ref[n * GROUP_A + g, jnp.minimum(d_base + j, nd - 1)]
                                    for j in range(nk)], axis=1)
            lg = jnp.where(sel, lg4[g * bq:(g + 1) * bq] + bias, -jnp.inf)
            p = jnp.exp(lg - jnp.max(lg, axis=-1, keepdims=True))
            ss.append(jnp.sum(p, axis=-1, keepdims=True))
            ps.append(p.astype(BF16))
        o4 = jnp.dot(jnp.concatenate(ps, axis=0), v_n, preferred_element_type=F32)
        for g in range(GROUP_A):
            h = n * GROUP_A + g
            gate = ag_ref[:, h * DH_A:(h + 1) * DH_A]
            o_ref[:, h * DH_A:(h + 1) * DH_A] = (o4[g * bq:(g + 1) * bq] / ss[g] * _silu(gate)).astype(BF16)


def _dsa(qa, z, ka_all, va_all, kidx_all, bias_tiles, *, bq, l_true, q0, ksel, row0, rows, lp):
    b, s, _ = qa.shape
    nd = bias_tiles.shape[1]
    blk0 = row0 // bq
    assert q0 % LANE == 0 and (bq == LANE or s == bq)
    assert nd == (q0 + s - bq) // LANE + 1
    kern = functools.partial(_dsa_kernel, bq=bq, lp=lp, l_true=l_true, q0=q0, ksel=ksel, nd=nd, blk0=blk0)

    def zspec(width, off):
        return pl.BlockSpec((None, bq, width), lambda bi, qi: (bi, qi + blk0, off // width))

    def kspec(width):
        return pl.BlockSpec((None, lp, width), lambda bi, qi: (bi, 0, 0))

    return pl.pallas_call(
        kern,
        grid=(b, rows // bq),
        in_specs=[pl.BlockSpec((None, bq, 1024), lambda bi, qi: (bi, qi + blk0, 0)),
                  zspec(1024, E_IQ), zspec(128, E_IDX), zspec(1024, E_AG),
                  kspec(256), kspec(256), kspec(128),
                  _const_spec(bias_tiles.shape)],
        out_specs=pl.BlockSpec((None, bq, 1024), lambda bi, qi: (bi, qi, 0)),
        out_shape=jax.ShapeDtypeStruct((b, rows, 1024), BF16),
        scratch_shapes=[pltpu.VMEM((bq, lp), jnp.int32)] * 3,
        compiler_params=_params("parallel", "arbitrary"),
        name="dsa",
    )(qa, z, z, z, ka_all, va_all, kidx_all, bias_tiles)


def _causal_classes(s, lp, q0):
    if q0 == 0 and s % CLASS_ROWS == 0 and lp == s:
        return [(c * CLASS_ROWS, CLASS_ROWS, (c + 1) * CLASS_ROWS) for c in range(s // CLASS_ROWS)]
    return [(0, s, lp)]


def _mla_kernel(qb_ref, bg_ref, kb_ref, vb_ref, o_ref, *, bq, lp, l_true, q0, blk0, n_full):
    q_start = q0 + (pl.program_id(1) + blk0) * bq
    qpos = q_start + lax.broadcasted_iota(jnp.int32, (bq, 1), 0)
    kpos = n_full + lax.broadcasted_iota(jnp.int32, (1, lp - n_full), 1)
    valid = ((kpos >> CHUNK_SHIFT) <= (qpos >> CHUNK_SHIFT)) & (kpos < l_true)
    for h in range(H_B):
        q = qb_ref[:, h * 256:(h + 1) * 256]
        parts = []
        if n_full:
            parts.append((lax.dot_general(q, kb_ref[:n_full, h * 256:(h + 1) * 256], NT_DIMS,
                                          preferred_element_type=F32), vb_ref[:n_full, h * V_DIM:(h + 1) * V_DIM]))
        lg = lax.dot_general(q, kb_ref[n_full:, h * 256:(h + 1) * 256], NT_DIMS, preferred_element_type=F32)
        parts.append((jnp.where(valid, lg, -jnp.inf), vb_ref[n_full:, h * V_DIM:(h + 1) * V_DIM]))
        o = _softmax_pv(parts, QK_B ** -0.5)
        gate = bg_ref[:, h * V_DIM:(h + 1) * V_DIM]
        o_ref[:, h * V_DIM:(h + 1) * V_DIM] = (o * _silu(gate)).astype(BF16)


def _mla(qb, z, kb, vb, *, bq, l_true, q0, row0, rows, lp):
    b = qb.shape[0]
    blk0 = row0 // bq
    n_full = min(min(q0 + row0 + CHUNK, l_true) // LANE * LANE, lp - LANE)
    kern = functools.partial(_mla_kernel, bq=bq, lp=lp, l_true=l_true, q0=q0, blk0=blk0, n_full=n_full)
    return pl.pallas_call(
        kern,
        grid=(b, rows // bq),
        in_specs=[pl.BlockSpec((None, bq, 2048), lambda bi, qi: (bi, qi + blk0, 0)),
                  pl.BlockSpec((None, bq, 1024), lambda bi, qi: (bi, qi + blk0, E_BG // 1024)),
                  pl.BlockSpec((None, lp, 2048), lambda bi, qi: (bi, 0, 0)),
                  pl.BlockSpec((None, lp, 1024), lambda bi, qi: (bi, 0, 0))],
        out_specs=pl.BlockSpec((None, bq, 1024), lambda bi, qi: (bi, qi, 0)),
        out_shape=jax.ShapeDtypeStruct((b, rows, 1024), BF16),
        compiler_params=_params("parallel", "arbitrary"),
        name="mla",
    )(qb, z, kb, vb)


def _out_ple_kernel(h_ref, oa_ref, ob_ref, p_ref, wo_ref, wg_ref, wp_ref, o_ref):
    half = oa_ref.shape[-1]
    h1 = (h_ref[...]
          + jnp.dot(oa_ref[...], wo_ref[:half, :], preferred_element_type=F32)
          + jnp.dot(ob_ref[...], wo_ref[half:, :], preferred_element_type=F32))
    r = (h1 * _rs(h1)).astype(BF16)
    gate = 1.0 / (1.0 + jnp.exp(-jnp.dot(r, wg_ref[...], preferred_element_type=F32)))
    o_ref[...] = h1 + gate * jnp.dot(p_ref[...].astype(BF16), wp_ref[...], preferred_element_type=F32)


def _out_ple(h, oa, ob, p, wo, wg, wp, bm):
    m, d = h.shape
    half = oa.shape[1]

    def rows(width):
        return pl.BlockSpec((bm, width), lambda i: (i, 0))

    return pl.pallas_call(
        _out_ple_kernel,
        grid=(m // bm,),
        in_specs=[rows(d), rows(half), rows(half), rows(PLE_DIM),
                  _const_spec((2 * half, d)), _const_spec((d, d)), _const_spec((PLE_DIM, d))],
        out_specs=rows(d),
        out_shape=jax.ShapeDtypeStruct((m, d), F32),
        compiler_params=_params("parallel"),
        name="out_ple",
    )(h, oa, ob, p, wo, wg, wp)


def _odd_prep_kernel(cq_ref, ck_ref, cv_ref, dv_ref, du_ref, dgate_ref, cqn_ref, ckn_ref, dg_ref, db_ref,
                     ws_ref, bs_ref, qc_o, kc_o, kcb_o, vcb_o, od_o, dvn_o, *, n):
    for h in range(H_C):
        sl = slice(h * DH_C, (h + 1) * DH_C)
        x = cq_ref[:, sl]
        qc_o[:, sl] = (x * _rs(x) * cqn_ref[...]).astype(BF16)
        x = ck_ref[:, sl]
        kn = x * _rs(x) * ckn_ref[...]
        kc_o[:, sl] = kn
        kcb_o[:, sl] = kn.astype(BF16)
    vcb_o[...] = cv_ref[...].astype(BF16)
    dv = dv_ref[...]
    xc = dv - jnp.mean(dv, -1, keepdims=True)
    var = jnp.mean(xc * xc, -1, keepdims=True)
    dvn = xc * lax.rsqrt(var + 1e-5) * dg_ref[...] + db_ref[...]
    dvn_o[...] = dvn
    dvn = dvn.astype(BF16)
    row = lax.broadcasted_iota(jnp.int32, (n, n), 0)
    col = lax.broadcasted_iota(jnp.int32, (n, n), 1)
    for g in range(G_D):
        sl = slice(g * DG_D, (g + 1) * DG_D)
        w = jnp.where(col <= row, ws_ref[g], 0.0).astype(BF16)
        for c in range(dv.shape[0] // n):
            rs = slice(c * n, (c + 1) * n)
            sg = jnp.dot(w, dvn[rs, sl], preferred_element_type=F32) + bs_ref[:, g:g + 1]
            od_o[rs, sl] = (du_ref[rs, sl] * sg * _silu(dgate_ref[rs, sl])).astype(BF16)


def _odd_prep(z, cqn, ckn, dg, db, ws, bs_t, bs, n):
    b, s, _ = z.shape

    def zspec(off):
        return pl.BlockSpec((None, bs, 1024), lambda bi, si: (bi, si, off // 1024))

    ospec = pl.BlockSpec((None, bs, 1024), lambda bi, si: (bi, si, 0))
    act = lambda dt: jax.ShapeDtypeStruct((b, s, 1024), dt)
    return pl.pallas_call(
        functools.partial(_odd_prep_kernel, n=n),
        grid=(b, s // bs),
        in_specs=[zspec(O_CQ), zspec(O_CK), zspec(O_CV), zspec(O_DV), zspec(O_DU), zspec(O_DG),
                  _const_spec((1, DH_C)), _const_spec((1, DH_C)), _const_spec((1, W_D)), _const_spec((1, W_D)),
                  _const_spec((G_D, n, n)), _const_spec((n, G_D))],
        out_specs=[ospec] * 6,
        out_shape=[act(BF16), act(F32), act(BF16), act(BF16), act(BF16), act(F32)],
        compiler_params=_params("parallel", "arbitrary"),
        name="odd_prep",
    )(z, z, z, z, z, z, cqn, ckn, dg, db, ws, bs_t)


BACK_TILES = C_BACK * CHUNK // LANE


def _band_width(bq):
    return (BACK_TILES + -(-bq // LANE)) * LANE


def _band_kernel(q_ref, cg_ref, k_ref, v_ref, bias_ref, o_ref, *, bq, sliding, q0, k0, k_end):
    i = pl.program_id(1)
    q_start = q0 + i * bq
    bw = _band_width(bq)
    if sliding:
        first = i * (bq // LANE) - BACK_TILES
        win_start = first * LANE
        rows = [pl.ds(pl.multiple_of(jnp.maximum(first + t, 0) * LANE, LANE), LANE) for t in range(bw // LANE)]
    else:
        win_start = k0
        rows = [pl.ds(t * LANE, LANE) for t in range(bw // LANE)]
    kw = jnp.concatenate([k_ref[r, :] for r in rows], axis=0)
    vw = jnp.concatenate([v_ref[r, :] for r in rows], axis=0)
    qpos = q_start + lax.broadcasted_iota(jnp.int32, (bq, 1), 0)
    kpos = win_start + lax.broadcasted_iota(jnp.int32, (1, bw), 1)
    dc = (qpos >> CHUNK_SHIFT) - (kpos >> CHUNK_SHIFT)
    valid = (dc >= 0) & (dc <= C_BACK) & (kpos >= 0) & (kpos < k_end)
    for h in range(H_C):
        sl = slice(h * DH_C, (h + 1) * DH_C)
        lg = lax.dot_general(q_ref[:, sl], kw[:, sl], NT_DIMS, preferred_element_type=F32) * DH_C ** -0.5
        lg = jnp.where(valid, lg + bias_ref[h], -jnp.inf)
        o = _softmax_pv([(lg, vw[:, sl])])
        o_ref[:, sl] = (o * _silu(cg_ref[:, sl])).astype(BF16)


def _band(qc, z, kcb, vcb, bias, *, bq, sliding, q0, k0, k_end):
    b, s, _ = qc.shape
    lk = kcb.shape[1]
    if sliding:
        assert bq % LANE == 0 and q0 == 0 and k0 == 0
    else:
        assert lk == _band_width(bq) and s == bq
    kern = functools.partial(_band_kernel, bq=bq, sliding=sliding, q0=q0, k0=k0, k_end=k_end)
    return pl.pallas_call(
        kern,
        grid=(b, s // bq),
        in_specs=[pl.BlockSpec((None, bq, 1024), lambda bi, qi: (bi, qi, 0)),
                  pl.BlockSpec((None, bq, 1024), lambda bi, qi: (bi, qi, O_CG // 1024)),
                  pl.BlockSpec((None, lk, 1024), lambda bi, qi: (bi, 0, 0)),
                  pl.BlockSpec((None, lk, 1024), lambda bi, qi: (bi, 0, 0)),
                  _const_spec(bias.shape)],
        out_specs=pl.BlockSpec((None, bq, 1024), lambda bi, qi: (bi, qi, 0)),
        out_shape=jax.ShapeDtypeStruct((b, s, 1024), BF16),
        compiler_params=_params("parallel", "arbitrary"),
        name="band",
    )(qc, z, kcb, vcb, bias)


def _rope_tables(pos):
    half = ROPE_DIM // 2
    freq = ROPE_BASE ** (-jnp.arange(half, dtype=F32) / half)
    ang = pos.astype(F32)[:, None] * freq[None, :]
    cos, sin = jnp.cos(ang), jnp.sin(ang)
    z = jnp.zeros_like(cos)
    return jnp.concatenate([cos, z, cos, z], 1), jnp.concatenate([-sin, z, sin, z], 1)


def _rope_lanes(x):
    half = ROPE_DIM // 2
    z = jnp.zeros(x.shape[:-1] + (half,), x.dtype)
    return jnp.concatenate([x[..., :half], z, x[..., half:], z], -1)


def _rope_unlanes(x):
    half = ROPE_DIM // 2
    return jnp.concatenate([x[..., :half], x[..., 2 * half:3 * half]], -1)


def _t5_bucket_np(rel):
    nb = T5_BUCKETS // 2
    max_exact = nb // 2
    n = np.abs(rel)
    nf = np.maximum(n, 1).astype(np.float64)
    large = max_exact + (np.log(nf / max_exact) / math.log(T5_MAX_DIST / max_exact) * (nb - max_exact)).astype(np.int64)
    large = np.minimum(large, nb - 1)
    return np.where(rel > 0, nb, 0) + np.where(n < max_exact, n, large)


def _toeplitz(w, rows, width, cols):
    flat = jnp.tile(w, (1,) * (w.ndim - 1) + (rows,))[..., :rows * width]
    return flat.reshape(w.shape[:-1] + (rows, width))[..., :cols]


def _t5_tiles(t5_bias, bq, nd):
    width = 2 * LANE
    k = np.arange(width + 1)
    delta = np.where(k < LANE, k, k - (width + 1))
    rel = (np.arange(nd) - (nd - 1))[:, None] * LANE + delta[None, :]
    w = jnp.transpose(t5_bias[_t5_bucket_np(rel)], (2, 0, 1))
    return _toeplitz(w, bq, width, LANE)


def _band_bias(rel_tab, bq, qk_off):
    bw = _band_width(bq)
    width = 2 * bw
    k = np.arange(width + 1)
    delta = np.where(k < bw, k, k - (width + 1))
    idx = np.clip(qk_off - delta, -(CHUNK - 1), REL_CLIP) + (CHUNK - 1)
    return _toeplitz(jnp.transpose(rel_tab[idx], (1, 0)), bq, width, bw)


def _even_weights(w_in, b_wuq, b_wukv, b_qn, b_kn):
    d = w_in.shape[0]
    offs = np.cumsum((0,) + EVEN_SPLITS)
    aq, ak, av, ag, iq, ik, iw, bcq, bckv, bkpe, bg = [w_in[:, offs[t]:offs[t + 1]] for t in range(11)]
    slab_idx = jnp.concatenate([ik, iw, jnp.zeros((d, LANE - D_IDX - H_IDX), w_in.dtype)], 1)
    w = jnp.concatenate([aq, ag, iq, bg, bcq, ak, av, bckv, slab_idx, _rope_lanes(bkpe)], 1).astype(BF16)
    uq = b_wuq.reshape(Q_LORA, H_B, QK_B)
    uq = jnp.concatenate([uq[..., :NOPE], _rope_lanes(uq[..., NOPE:])], -1).reshape(Q_LORA, H_B * 256).astype(BF16)
    ukv = b_wukv.reshape(KV_LORA, H_B, NOPE + V_DIM)
    ukv = jnp.concatenate([ukv[..., :NOPE].reshape(KV_LORA, H_B * NOPE),
                           ukv[..., NOPE:].reshape(KV_LORA, H_B * V_DIM)], 1).astype(BF16)
    pad_gain = lambda g: jnp.concatenate([g[:NOPE], _rope_lanes(g[NOPE:])])[None, :]
    return w, uq, ukv, pad_gain(b_qn), pad_gain(b_kn)


def _pad_rows(x, lp):
    return jnp.pad(x, ((0, 0), (0, lp - x.shape[1]), (0, 0)))


def _even_layer(h, p, past, q0, ln_g, w_in, uq, ukv, a_qn, a_kn, t5_bias, b_qln, b_kvln, bqn, bkn, wo, wg, wp):
    b, s, d = h.shape
    m = b * s
    z = _norm_mm(h.reshape(m, d), ln_g[None, :], w_in, min(m, 1024), 512).reshape(b, s, E_END)
    cos, sin = _rope_tables(q0 + jnp.arange(s, dtype=jnp.int32))
    bs = min(s, 256)
    qa, ka, qb, ckv, kpe_l = _even_prep(z, cos, sin, a_qn[None, :], a_kn[None, :], b_qln[None, :], b_kvln[None, :],
                                        uq, bqn, bs)
    av = z[..., E_AV:E_AV + 256]
    kidx = z[..., E_IDX:E_IDX + LANE]
    new = (ka.reshape(b, s, HKV_A, DH_A), av.reshape(b, s, HKV_A, DH_A), kidx[..., :D_IDX], ckv, _rope_unlanes(kpe_l))
    if past is None:
        l_true = s
        ka_all, va_all, kidx_all, ckv_all, kpe_all = ka, av, kidx, ckv, kpe_l
    else:
        c_k, c_v, c_ik, c_ckv, c_kpe = past
        pl_ = c_k.shape[1]
        l_true = pl_ + s
        lp = -(-l_true // LANE) * LANE
        cat = lambda c, n_: _pad_rows(jnp.concatenate([c, n_], 1), lp)
        ka_all = cat(c_k.reshape(b, pl_, 256), ka)
        va_all = cat(c_v.reshape(b, pl_, 256), av)
        kidx_all = cat(jnp.pad(c_ik, ((0, 0), (0, 0), (0, LANE - D_IDX))), kidx)
        ckv_all = cat(c_ckv, ckv)
        kpe_all = cat(_rope_lanes(c_kpe), kpe_l)
    lp = ka_all.shape[1]
    ksel = min(TOPK_MAX, l_true // 4)
    bq = min(s, LANE)
    nd = (q0 + s - bq) // LANE + 1
    tiles = _t5_tiles(t5_bias, bq, nd)
    classes = _causal_classes(s, lp, q0)
    o_a = jnp.concatenate([_dsa(qa, z, ka_all, va_all, kidx_all, tiles, bq=bq, l_true=l_true, q0=q0, ksel=ksel,
                                row0=r0, rows=nr, lp=lc) for r0, nr, lc in classes], axis=1)
    kb, vb = _mla_kv(ckv_all, kpe_all, ukv, bkn, _key_chunk(lp))
    o_b = jnp.concatenate([_mla(qb, z, kb, vb, bq=min(s, 256), l_true=l_true, q0=q0,
                                row0=r0, rows=nr, lp=lc) for r0, nr, lc in classes], axis=1)
    y = _out_ple(h.reshape(m, d), o_a.reshape(m, -1), o_b.reshape(m, -1), p.reshape(m, -1), wo, wg, wp, min(m, 256))
    return y.reshape(b, s, d), new


def _odd_layer(h, p, past, q0, ln_g, w_in, c_qn, c_kn, c_rel, d_g, d_b, d_ws, d_bs, wo, wg, wp):
    b, s, d = h.shape
    m = b * s
    z = _norm_mm(h.reshape(m, d), ln_g[None, :], w_in, min(m, 1024), 1024).reshape(b, s, O_END)
    n = min(s, D_CHUNK)
    qc, kc, kcb, vcb, o_d, dvn = _odd_prep(z, c_qn[None, :], c_kn[None, :], d_g[None, :], d_b[None, :],
                                           d_ws[:, :n, :n], d_bs[:, :n].T, min(s, 256), n)
    cv = z[..., O_CV:O_CV + 1024]
    if past is None:
        keep = min(C_BACK * CHUNK, s)
        c_new = (kc[:, s - keep:].reshape(b, keep, H_C, DH_C), cv[:, s - keep:].reshape(b, keep, H_C, DH_C))
        bqc = 2 * LANE
        bias = _band_bias(c_rel, bqc, C_BACK * CHUNK)
        o_c = _band(qc, z, kcb, vcb, bias, bq=bqc, sliding=True, q0=0, k0=0, k_end=s)
    else:
        nc = past[0].shape[1]
        c_new = (kc.reshape(b, s, H_C, DH_C), cv.reshape(b, s, H_C, DH_C))
        cat = lambda c, n_: _pad_rows(jnp.concatenate([c.reshape(b, nc, 1024).astype(BF16), n_], 1), _band_width(s))
        bias = _band_bias(c_rel, s, nc)
        o_c = _band(qc, z, cat(past[0], kcb), cat(past[1], vcb), bias,
                    bq=s, sliding=False, q0=q0, k0=q0 - nc, k_end=q0 + s)
    y = _out_ple(h.reshape(m, d), o_c.reshape(m, -1), o_d.reshape(m, -1), p.reshape(m, -1), wo, wg, wp, min(m, 256))
    return y.reshape(b, s, d), c_new, dvn


def kernel(x_prompt, x_sample, cache_a_k, cache_a_v, cache_a_idx_k, cache_b_ckv, cache_b_kpe, cache_c_k, cache_c_v, p_prompt, p_sample, ln_g, w_in_even, a_q_norm, a_k_norm, t5_bias, b_q_lora_norm, b_kv_lora_norm, b_w_uq, b_w_ukv, b_q_norm, b_k_norm, w_out_even, w_in_odd, c_q_norm, c_k_norm, c_rel_bias, d_ln_g, d_ln_b, d_w_s, d_b_s, w_out_odd, ple_proj, ple_gate):
    depth = ln_g.shape[0]
    past_len = cache_a_k.shape[2]
    hp, hs = x_prompt, x_sample
    ev_p, ev_s, od_p, od_s, dv_s = [], [], [], [], []
    for i in range(depth):
        j = i // 2
        wg = ple_gate[i].astype(BF16)
        wp = ple_proj[i].astype(BF16)
        if i % 2 == 0:
            w_in, uq, ukv, bqn, bkn = _even_weights(w_in_even[j], b_w_uq[j], b_w_ukv[j], b_q_norm[j], b_k_norm[j])
            w = (ln_g[i], w_in, uq, ukv, a_q_norm[j], a_k_norm[j], t5_bias, b_q_lora_norm[j], b_kv_lora_norm[j],
                 bqn, bkn, w_out_even[j].astype(BF16), wg, wp)
            hp, sp = _even_layer(hp, p_prompt[i], None, 0, *w)
            past = (cache_a_k[j], cache_a_v[j], cache_a_idx_k[j], cache_b_ckv[j], cache_b_kpe[j])
            hs, ss = _even_layer(hs, p_sample[i], past, past_len, *w)
            ev_p.append(sp)
            ev_s.append(ss)
        else:
            w = (ln_g[i], w_in_odd[j].astype(BF16), c_q_norm[j], c_k_norm[j], c_rel_bias[j], d_ln_g[j], d_ln_b[j],
                 d_w_s[j], d_b_s[j], w_out_odd[j].astype(BF16), wg, wp)
            hp, sp, _ = _odd_layer(hp, p_prompt[i], None, 0, *w)
            hs, ss, dvs = _odd_layer(hs, p_sample[i], (cache_c_k[j], cache_c_v[j]), past_len, *w)
            od_p.append(sp)
            od_s.append(ss)
            dv_s.append(dvs)
    st = lambda lst, n_: jnp.stack([e[n_] for e in lst], 0)
    return (hp, hs, st(ev_p, 0), st(ev_p, 1), st(ev_p, 2), st(ev_p, 3), st(ev_p, 4), st(od_p, 0), st(od_p, 1),
            st(ev_s, 0), st(ev_s, 1), st(ev_s, 2), st(ev_s, 3), st(ev_s, 4), st(od_s, 0), st(od_s, 1),
            jnp.stack(dv_s, 0))
```

```python
import functools
import math

import numpy as np
import jax
import jax.numpy as jnp
from jax import lax
from jax.experimental import pallas as pl
from jax.experimental.pallas import tpu as pltpu

F32 = jnp.float32
BF16 = jnp.bfloat16
INT_MIN = -2 ** 31
LOG2E = math.log2(math.e)
CLASS_ROWS = 256
SEARCH_GROUPS = 4

D_MODEL = 2048
CHUNK = 64
CHUNK_SHIFT = 6
LANE = 128
H_A, HKV_A, GROUP_A, DH_A = 8, 2, 4, 128
H_IDX, D_IDX = 16, 64
TOPK_MAX = 256
T5_BUCKETS, T5_MAX_DIST = 32, 128
H_B, Q_LORA, KV_LORA, NOPE, ROPE_DIM, V_DIM = 8, 512, 256, 128, 64, 128
ROPE_BASE = 10000.0
QK_B = NOPE + ROPE_DIM
H_C, DH_C, C_BACK, REL_CLIP = 8, 128, 8, 128
W_D, G_D, DG_D, D_CHUNK = 1024, 8, 128, 128
PLE_DIM = 256

EVEN_SPLITS = (H_A * DH_A, HKV_A * DH_A, HKV_A * DH_A, H_A * DH_A, H_IDX * D_IDX, D_IDX, H_IDX,
               Q_LORA, KV_LORA, ROPE_DIM, H_B * V_DIM)
E_AQ, E_AG, E_IQ, E_BG, E_BCQ, E_AK, E_AV, E_CKV, E_IDX, E_KPE, E_END = (
    0, 1024, 2048, 3072, 4096, 4608, 4864, 5120, 5376, 5504, 5632)
O_CQ, O_CK, O_CV, O_CG, O_DU, O_DV, O_DG, O_END = 0, 1024, 2048, 3072, 4096, 5120, 6144, 7168

VMEM_LIMIT_BYTES = 56 * 1024 * 1024
NT_DIMS = (((1,), (1,)), ((), ()))


def _params(*sem):
    return pltpu.CompilerParams(dimension_semantics=sem, vmem_limit_bytes=VMEM_LIMIT_BYTES)


def _const_spec(shape):
    zeros = (0,) * len(shape)
    return pl.BlockSpec(shape, lambda *_: zeros, pipeline_mode=pl.Buffered(1))


def _rs(x, n=None, eps=1e-6):
    n = x.shape[-1] if n is None else n
    return lax.rsqrt(jnp.sum(x * x, axis=-1, keepdims=True) / n + eps)


def _silu(x):
    return x * (1.0 / (1.0 + jnp.exp(-x)))


def _softmax_pv(parts, scale=1.0):
    m = functools.reduce(jnp.maximum, [jnp.max(lg, axis=-1, keepdims=True) for lg, _ in parts])
    o = s = None
    for lg, v in parts:
        p = jnp.exp2((lg - m) * (scale * LOG2E))
        ps = jnp.sum(p, axis=-1, keepdims=True)
        po = jnp.dot(p.astype(BF16), v, preferred_element_type=F32)
        o, s = (po, ps) if o is None else (o + po, s + ps)
    return o / s


def _norm_mm_kernel(x_ref, g_ref, w_ref, o_ref, xn_ref):
    @pl.when(pl.program_id(1) == 0)
    def _():
        x = x_ref[...]
        xn_ref[...] = (x * _rs(x) * g_ref[...]).astype(BF16)

    o_ref[...] = jnp.dot(xn_ref[...], w_ref[...], preferred_element_type=F32)


def _norm_mm(x, g, w, bm, bn):
    m, d = x.shape
    n = w.shape[1]
    return pl.pallas_call(
        _norm_mm_kernel,
        grid=(m // bm, n // bn),
        in_specs=[pl.BlockSpec((bm, d), lambda i, j: (i, 0)),
                  pl.BlockSpec((1, d), lambda i, j: (0, 0)),
                  pl.BlockSpec((d, bn), lambda i, j: (0, j))],
        out_specs=pl.BlockSpec((bm, bn), lambda i, j: (i, j)),
        out_shape=jax.ShapeDtypeStruct((m, n), F32),
        scratch_shapes=[pltpu.VMEM((bm, d), BF16)],
        compiler_params=_params("parallel", "arbitrary"),
        name="norm_mm",
    )(x, g, w)


def _rope(x, cos, sin):
    return x * cos + pltpu.roll(x, 64, 1) * sin


def _even_prep_kernel(aq_ref, bcq_ref, ak_ref, ckv_ref, kpe_ref, cos_ref, sin_ref,
                      aqn_ref, akn_ref, qln_ref, kvln_ref, wuq_ref, bqn_ref,
                      qa_o, ka_o, qb_o, ckv_o, kpe_o):
    for h in range(H_A):
        x = aq_ref[:, h * DH_A:(h + 1) * DH_A]
        qa_o[:, h * DH_A:(h + 1) * DH_A] = (x * _rs(x) * aqn_ref[...]).astype(BF16)
    for n in range(HKV_A):
        x = ak_ref[:, n * DH_A:(n + 1) * DH_A]
        ka_o[:, n * DH_A:(n + 1) * DH_A] = x * _rs(x) * akn_ref[...]
    c = ckv_ref[...]
    ckv_o[...] = c * _rs(c) * kvln_ref[...]
    cos = cos_ref[...]
    sin = sin_ref[...]
    kpe_o[...] = _rope(kpe_ref[...], cos, sin)
    cq = bcq_ref[...]
    cqn = (cq * _rs(cq) * qln_ref[...]).astype(BF16)
    qb = jnp.dot(cqn, wuq_ref[...], preferred_element_type=F32)
    g = bqn_ref[...]
    for h in range(H_B):
        nope = qb[:, h * 256:h * 256 + 128]
        rot = _rope(qb[:, h * 256 + 128:(h + 1) * 256], cos, sin)
        ss = jnp.sum(nope * nope, -1, keepdims=True) + jnp.sum(rot * rot, -1, keepdims=True)
        r = lax.rsqrt(ss / QK_B + 1e-6)
        qb_o[:, h * 256:h * 256 + 128] = (nope * r * g[:, :128]).astype(BF16)
        qb_o[:, h * 256 + 128:(h + 1) * 256] = (rot * r * g[:, 128:]).astype(BF16)


def _even_prep(z, cos, sin, aqn, akn, qln, kvln, wuq, bqn, bs):
    b, s, _ = z.shape

    def zspec(width, off):
        return pl.BlockSpec((None, bs, width), lambda bi, si: (bi, si, off // width))

    def ospec(width):
        return pl.BlockSpec((None, bs, width), lambda bi, si: (bi, si, 0))

    pos_spec = pl.BlockSpec((bs, LANE), lambda bi, si: (si, 0))
    return pl.pallas_call(
        _even_prep_kernel,
        grid=(b, s // bs),
        in_specs=[zspec(1024, E_AQ), zspec(512, E_BCQ), zspec(256, E_AK), zspec(256, E_CKV),
                  zspec(128, E_KPE), pos_spec, pos_spec,
                  _const_spec((1, DH_A)), _const_spec((1, DH_A)), _const_spec((1, Q_LORA)),
                  _const_spec((1, KV_LORA)), _const_spec((Q_LORA, H_B * 256)), _const_spec((1, 256))],
        out_specs=[ospec(1024), ospec(256), ospec(2048), ospec(256), ospec(128)],
        out_shape=[jax.ShapeDtypeStruct((b, s, 1024), BF16), jax.ShapeDtypeStruct((b, s, 256), F32),
                   jax.ShapeDtypeStruct((b, s, 2048), BF16), jax.ShapeDtypeStruct((b, s, 256), F32),
                   jax.ShapeDtypeStruct((b, s, 128), F32)],
        compiler_params=_params("parallel", "arbitrary"),
        name="even_prep",
    )(z, z, z, z, z, cos, sin, aqn, akn, qln, kvln, wuq, bqn)


def _mla_kv_kernel(ckv_ref, kpe_ref, w_ref, g_ref, kb_o, vb_o):
    kv = jnp.dot(ckv_ref[...].astype(BF16), w_ref[...], preferred_element_type=F32)
    kp = kpe_ref[...]
    skp = jnp.sum(kp * kp, -1, keepdims=True)
    g = g_ref[...]
    for h in range(H_B):
        nope = kv[:, h * NOPE:(h + 1) * NOPE]
        r = lax.rsqrt((jnp.sum(nope * nope, -1, keepdims=True) + skp) / QK_B + 1e-6)
        kb_o[:, h * 256:h * 256 + 128] = (nope * r * g[:, :128]).astype(BF16)
        kb_o[:, h * 256 + 128:(h + 1) * 256] = (kp * r * g[:, 128:]).astype(BF16)
    vb_o[...] = kv[:, H_B * NOPE:].astype(BF16)


def _mla_kv(ckv_all, kpe_all, wukv, bkn, bl):
    b, lp, _ = ckv_all.shape
    return pl.pallas_call(
        _mla_kv_kernel,
        grid=(b, lp // bl),
        in_specs=[pl.BlockSpec((None, bl, KV_LORA), lambda bi, li: (bi, li, 0)),
                  pl.BlockSpec((None, bl, LANE), lambda bi, li: (bi, li, 0)),
                  _const_spec((KV_LORA, 2048)), _const_spec((1, 256))],
        out_specs=[pl.BlockSpec((None, bl, 2048), lambda bi, li: (bi, li, 0)),
                   pl.BlockSpec((None, bl, 1024), lambda bi, li: (bi, li, 0))],
        out_shape=[jax.ShapeDtypeStruct((b, lp, 2048), BF16), jax.ShapeDtypeStruct((b, lp, 1024), BF16)],
        compiler_params=_params("parallel", "arbitrary"),
        name="mla_kv",
    )(ckv_all, kpe_all, wukv, bkn)


def _key_chunk(lp):
    for c in (512, 384, 256, 128):
        if lp % c == 0:
            return c
    raise ValueError(lp)


def _dsa_kernel(qa_ref, iq_ref, sa_ref, ag_ref, ka_ref, va_ref, kidx_ref, bias_ref, o_ref, key_ref, hi_ref, lo_ref,
                *, bq, lp, l_true, q0, ksel, nd, blk0):
    i = pl.program_id(1) + blk0
    q_start = q0 + i * bq
    qpos = q_start + lax.broadcasted_iota(jnp.int32, (bq, 1), 0)
    ck = _key_chunk(lp)

    iq = iq_ref[...].astype(BF16)
    a = jnp.concatenate([iq[:, p * LANE:(p + 1) * LANE] for p in range(H_IDX // 2)], axis=0)
    wi = sa_ref[...] * (H_IDX ** -0.5 * D_IDX ** -0.5)
    for c0 in range(0, lp, ck):
        kk = kidx_ref[c0:c0 + ck, :]
        lane = lax.broadcasted_iota(jnp.int32, kk.shape, 1)
        k_lo = jnp.where(lane < D_IDX, kk, 0.0).astype(BF16)
        k_hi = jnp.where(lane >= D_IDX, pltpu.roll(kk, D_IDX, 1), 0.0).astype(BF16)
        s_lo = lax.dot_general(a, k_lo, NT_DIMS, preferred_element_type=F32)
        s_hi = lax.dot_general(a, k_hi, NT_DIMS, preferred_element_type=F32)
        sc = jnp.zeros((bq, ck), F32)
        for p in range(H_IDX // 2):
            w0 = wi[:, D_IDX + 2 * p:D_IDX + 2 * p + 1]
            w1 = wi[:, D_IDX + 2 * p + 1:D_IDX + 2 * p + 2]
            sc = sc + w0 * jnp.maximum(s_lo[p * bq:(p + 1) * bq], 0.0)
            sc = sc + w1 * jnp.maximum(s_hi[p * bq:(p + 1) * bq], 0.0)
        kpos = c0 + lax.broadcasted_iota(jnp.int32, (1, ck), 1)
        valid = ((kpos >> CHUNK_SHIFT) <= (qpos >> CHUNK_SHIFT)) & (kpos < l_true)
        bits = lax.bitcast_convert_type(sc, jnp.int32)
        key = jnp.where(bits < 0, bits ^ 0x7FFFFFFF, bits)
        key = jnp.where(bits == INT_MIN, 0, key)
        key = jnp.where(valid, key, INT_MIN)
        key_ref[:, c0:c0 + ck] = key
        hi_ref[:, c0:c0 + ck] = (key >> 16) + 2 ** 15
        lo_ref[:, c0:c0 + ck] = key & 0xFFFF

    ng = SEARCH_GROUPS
    gr = bq // ng
    groups = [slice(r * gr, (r + 1) * gr) for r in range(ng)]

    def count_ge(ref, rows, cand):
        acc = None
        for j in range(lp // LANE):
            d = (ref[rows, j * LANE:(j + 1) * LANE] - cand) >> 31
            acc = d if acc is None else acc + d
        return lp + jnp.sum(acc.astype(F32), axis=-1, keepdims=True)

    def search16(ref, need):
        ts = [jnp.zeros((gr, 1), jnp.int32)] * ng
        for bit in reversed(range(16)):
            for r in range(ng):
                cand = ts[r] | (1 << bit)
                ts[r] = jnp.where(count_ge(ref, groups[r], cand) >= need[r], cand, ts[r])
        return ts

    t_hi = search16(hi_ref, (float(ksel),) * ng)
    need_lo = []
    for rows, t in zip(groups, t_hi):
        need_lo.append(ksel - count_ge(hi_ref, rows, t + 1))
        lo_ref[rows, :] = jnp.where(hi_ref[rows, :] == t, lo_ref[rows, :], -1)
    t_lo = search16(lo_ref, need_lo)
    t_s = jnp.concatenate([lax.shift_left(h - 2 ** 15, 16) | l for h, l in zip(t_hi, t_lo)], axis=0)
    key = key_ref[...]
    sel = (key >= t_s) & (key > INT_MIN)

    d_base = (nd - 1) - (q0 // LANE + (i if bq == LANE else 0))
    nk = lp // LANE
    for n in range(HKV_A):
        q4 = jnp.concatenate([qa_ref[:, (n * GROUP_A + g) * DH_A:(n * GROUP_A + g + 1) * DH_A]
                              for g in range(GROUP_A)], axis=0)
        k_n = ka_ref[:, n * DH_A:(n + 1) * DH_A].astype(BF16)
        v_n = va_ref[:, n * DH_A:(n + 1) * DH_A].astype(BF16)
        lg4 = lax.dot_general(q4, k_n, NT_DIMS, preferred_element_type=F32) * DH_A ** -0.5
        ps, ss = [], []
        for g in range(GROUP_A):
            bias = jnp.concatenate([bias_ref[n * GROUP_A + g, jnp.minimum(d_base + j, nd - 1)]
                                    for j in range(nk)], axis=1)
            lg = jnp.where(sel, lg4[g * bq:(g + 1) * bq] + bias, -jnp.inf)
            p = jnp.exp(lg - jnp.max(lg, axis=-1, keepdims=True))
            ss.append(jnp.sum(p, axis=-1, keepdims=True))
            ps.append(p.astype(BF16))
        o4 = jnp.dot(jnp.concatenate(ps, axis=0), v_n, preferred_element_type=F32)
        for g in range(GROUP_A):
            h = n * GROUP_A + g
            gate = ag_ref[:, h * DH_A:(h + 1) * DH_A]
            o_ref[:, h * DH_A:(h + 1) * DH_A] = (o4[g * bq:(g + 1) * bq] / ss[g] * _silu(gate)).astype(BF16)


def _dsa(qa, z, ka_all, va_all, kidx_all, bias_tiles, *, bq, l_true, q0, ksel, row0, rows, lp):
    b, s, _ = qa.shape
    nd = bias_tiles.shape[1]
    blk0 = row0 // bq
    assert q0 % LANE == 0 and (bq == LANE or s == bq)
    assert nd == (q0 + s - bq) // LANE + 1
    kern = functools.partial(_dsa_kernel, bq=bq, lp=lp, l_true=l_true, q0=q0, ksel=ksel, nd=nd, blk0=blk0)

    def zspec(width, off):
        return pl.BlockSpec((None, bq, width), lambda bi, qi: (bi, qi + blk0, off // width))

    def kspec(width):
        return pl.BlockSpec((None, lp, width), lambda bi, qi: (bi, 0, 0))

    return pl.pallas_call(
        kern,
        grid=(b, rows // bq),
        in_specs=[pl.BlockSpec((None, bq, 1024), lambda bi, qi: (bi, qi + blk0, 0)),
                  zspec(1024, E_IQ), zspec(128, E_IDX), zspec(1024, E_AG),
                  kspec(256), kspec(256), kspec(128),
                  _const_spec(bias_tiles.shape)],
        out_specs=pl.BlockSpec((None, bq, 1024), lambda bi, qi: (bi, qi, 0)),
        out_shape=jax.ShapeDtypeStruct((b, rows, 1024), BF16),
        scratch_shapes=[pltpu.VMEM((bq, lp), jnp.int32)] * 3,
        compiler_params=_params("parallel", "arbitrary"),
        name="dsa",
    )(qa, z, z, z, ka_all, va_all, kidx_all, bias_tiles)


def _causal_classes(s, lp, q0):
    if q0 == 0 and s % CLASS_ROWS == 0 and lp == s:
        return [(c * CLASS_ROWS, CLASS_ROWS, (c + 1) * CLASS_ROWS) for c in range(s // CLASS_ROWS)]
    return [(0, s, lp)]


def _mla_kernel(qb_ref, bg_ref, kb_ref, vb_ref, o_ref, *, bq, lp, l_true, q0, blk0, n_full):
    q_start = q0 + (pl.program_id(1) + blk0) * bq
    qpos = q_start + lax.broadcasted_iota(jnp.int32, (bq, 1), 0)
    kpos = n_full + lax.broadcasted_iota(jnp.int32, (1, lp - n_full), 1)
    valid = ((kpos >> CHUNK_SHIFT) <= (qpos >> CHUNK_SHIFT)) & (kpos < l_true)
    for h in range(H_B):
        q = qb_ref[:, h * 256:(h + 1) * 256]
        parts = []
        if n_full:
            parts.append((lax.dot_general(q, kb_ref[:n_full, h * 256:(h + 1) * 256], NT_DIMS,
                                          preferred_element_type=F32), vb_ref[:n_full, h * V_DIM:(h + 1) * V_DIM]))
        lg = lax.dot_general(q, kb_ref[n_full:, h * 256:(h + 1) * 256], NT_DIMS, preferred_element_type=F32)
        parts.append((jnp.where(valid, lg, -jnp.inf), vb_ref[n_full:, h * V_DIM:(h + 1) * V_DIM]))
        o = _softmax_pv(parts, QK_B ** -0.5)
        gate = bg_ref[:, h * V_DIM:(h + 1) * V_DIM]
        o_ref[:, h * V_DIM:(h + 1) * V_DIM] = (o * _silu(gate)).astype(BF16)


def _mla(qb, z, kb, vb, *, bq, l_true, q0, row0, rows, lp):
    b = qb.shape[0]
    blk0 = row0 // bq
    n_full = min(min(q0 + row0 + CHUNK, l_true) // LANE * LANE, lp - LANE)
    kern = functools.partial(_mla_kernel, bq=bq, lp=lp, l_true=l_true, q0=q0, blk0=blk0, n_full=n_full)
    return pl.pallas_call(
        kern,
        grid=(b, rows // bq),
        in_specs=[pl.BlockSpec((None, bq, 2048), lambda bi, qi: (bi, qi + blk0, 0)),
                  pl.BlockSpec((None, bq, 1024), lambda bi, qi: (bi, qi + blk0, E_BG // 1024)),
                  pl.BlockSpec((None, lp, 2048), lambda bi, qi: (bi, 0, 0)),
                  pl.BlockSpec((None, lp, 1024), lambda bi, qi: (bi, 0, 0))],
        out_specs=pl.BlockSpec((None, bq, 1024), lambda bi, qi: (bi, qi, 0)),
        out_shape=jax.ShapeDtypeStruct((b, rows, 1024), BF16),
        compiler_params=_params("parallel", "arbitrary"),
        name="mla",
    )(qb, z, kb, vb)


def _out_ple_kernel(h_ref, oa_ref, ob_ref, p_ref, wo_ref, wg_ref, wp_ref, o_ref):
    half = oa_ref.shape[-1]
    h1 = (h_ref[...]
          + jnp.dot(oa_ref[...], wo_ref[:half, :], preferred_element_type=F32)
          + jnp.dot(ob_ref[...], wo_ref[half:, :], preferred_element_type=F32))
    r = (h1 * _rs(h1)).astype(BF16)
    gate = 1.0 / (1.0 + jnp.exp(-jnp.dot(r, wg_ref[...], preferred_element_type=F32)))
    o_ref[...] = h1 + gate * jnp.dot(p_ref[...].astype(BF16), wp_ref[...], preferred_element_type=F32)


def _out_ple(h, oa, ob, p, wo, wg, wp, bm):
    m, d = h.shape
    half = oa.shape[1]

    def rows(width):
        return pl.BlockSpec((bm, width), lambda i: (i, 0))

    return pl.pallas_call(
        _out_ple_kernel,
        grid=(m // bm,),
        in_specs=[rows(d), rows(half), rows(half), rows(PLE_DIM),
                  _const_spec((2 * half, d)), _const_spec((d, d)), _const_spec((PLE_DIM, d))],
        out_specs=rows(d),
        out_shape=jax.ShapeDtypeStruct((m, d), F32),
        compiler_params=_params("parallel"),
        name="out_ple",
    )(h, oa, ob, p, wo, wg, wp)


def _odd_prep_kernel(cq_ref, ck_ref, cv_ref, dv_ref, du_ref, dgate_ref, cqn_ref, ckn_ref, dg_ref, db_ref,
                     ws_ref, bs_ref, qc_o, kc_o, kcb_o, vcb_o, od_o, dvn_o, *, n):
    for h in range(H_C):
        sl = slice(h * DH_C, (h + 1) * DH_C)
        x = cq_ref[:, sl]
        qc_o[:, sl] = (x * _rs(x) * cqn_ref[...]).astype(BF16)
        x = ck_ref[:, sl]
        kn = x * _rs(x) * ckn_ref[...]
        kc_o[:, sl] = kn
        kcb_o[:, sl] = kn.astype(BF16)
    vcb_o[...] = cv_ref[...].astype(BF16)
    dv = dv_ref[...]
    xc = dv - jnp.mean(dv, -1, keepdims=True)
    var = jnp.mean(xc * xc, -1, keepdims=True)
    dvn = xc * lax.rsqrt(var + 1e-5) * dg_ref[...] + db_ref[...]
    dvn_o[...] = dvn
    dvn = dvn.astype(BF16)
    row = lax.broadcasted_iota(jnp.int32, (n, n), 0)
    col = lax.broadcasted_iota(jnp.int32, (n, n), 1)
    for g in range(G_D):
        sl = slice(g * DG_D, (g + 1) * DG_D)
        w = jnp.where(col <= row, ws_ref[g], 0.0).astype(BF16)
        for c in range(dv.shape[0] // n):
            rs = slice(c * n, (c + 1) * n)
            sg = jnp.dot(w, dvn[rs, sl], preferred_element_type=F32) + bs_ref[:, g:g + 1]
            od_o[rs, sl] = (du_ref[rs, sl] * sg * _silu(dgate_ref[rs, sl])).astype(BF16)


def _odd_prep(z, cqn, ckn, dg, db, ws, bs_t, bs, n):
    b, s, _ = z.shape

    def zspec(off):
        return pl.BlockSpec((None, bs, 1024), lambda bi, si: (bi, si, off // 1024))

    ospec = pl.BlockSpec((None, bs, 1024), lambda bi, si: (bi, si, 0))
    act = lambda dt: jax.ShapeDtypeStruct((b, s, 1024), dt)
    return pl.pallas_call(
        functools.partial(_odd_prep_kernel, n=n),
        grid=(b, s // bs),
        in_specs=[zspec(O_CQ), zspec(O_CK), zspec(O_CV), zspec(O_DV), zspec(O_DU), zspec(O_DG),
                  _const_spec((1, DH_C)), _const_spec((1, DH_C)), _const_spec((1, W_D)), _const_spec((1, W_D)),
                  _const_spec((G_D, n, n)), _const_spec((n, G_D))],
        out_specs=[ospec] * 6,
        out_shape=[act(BF16), act(F32), act(BF16), act(BF16), act(BF16), act(F32)],
        compiler_params=_params("parallel", "arbitrary"),
        name="odd_prep",
    )(z, z, z, z, z, z, cqn, ckn, dg, db, ws, bs_t)


BACK_TILES = C_BACK * CHUNK // LANE


def _band_width(bq):
    return (BACK_TILES + -(-bq // LANE)) * LANE


def _band_kernel(q_ref, cg_ref, k_ref, v_ref, bias_ref, o_ref, *, bq, sliding, q0, k0, k_end):
    i = pl.program_id(1)
    q_start = q0 + i * bq
    bw = _band_width(bq)
    if sliding:
        first = i * (bq // LANE) - BACK_TILES
        win_start = first * LANE
        rows = [pl.ds(pl.multiple_of(jnp.maximum(first + t, 0) * LANE, LANE), LANE) for t in range(bw // LANE)]
    else:
        win_start = k0
        rows = [pl.ds(t * LANE, LANE) for t in range(bw // LANE)]
    kw = jnp.concatenate([k_ref[r, :] for r in rows], axis=0)
    vw = jnp.concatenate([v_ref[r, :] for r in rows], axis=0)
    qpos = q_start + lax.broadcasted_iota(jnp.int32, (bq, 1), 0)
    kpos = win_start + lax.broadcasted_iota(jnp.int32, (1, bw), 1)
    dc = (qpos >> CHUNK_SHIFT) - (kpos >> CHUNK_SHIFT)
    valid = (dc >= 0) & (dc <= C_BACK) & (kpos >= 0) & (kpos < k_end)
    for h in range(H_C):
        sl = slice(h * DH_C, (h + 1) * DH_C)
        lg = lax.dot_general(q_ref[:, sl], kw[:, sl], NT_DIMS, preferred_element_type=F32) * DH_C ** -0.5
        lg = jnp.where(valid, lg + bias_ref[h], -jnp.inf)
        o = _softmax_pv([(lg, vw[:, sl])])
        o_ref[:, sl] = (o * _silu(cg_ref[:, sl])).astype(BF16)


def _band(qc, z, kcb, vcb, bias, *, bq, sliding, q0, k0, k_end):
    b, s, _ = qc.shape
    lk = kcb.shape[1]
    if sliding:
        assert bq % LANE == 0 and q0 == 0 and k0 == 0
    else:
        assert lk == _band_width(bq) and s == bq
    kern = functools.partial(_band_kernel, bq=bq, sliding=sliding, q0=q0, k0=k0, k_end=k_end)
    return pl.pallas_call(
        kern,
        grid=(b, s // bq),
        in_specs=[pl.BlockSpec((None, bq, 1024), lambda bi, qi: (bi, qi, 0)),
                  pl.BlockSpec((None, bq, 1024), lambda bi, qi: (bi, qi, O_CG // 1024)),
                  pl.BlockSpec((None, lk, 1024), lambda bi, qi: (bi, 0, 0)),
                  pl.BlockSpec((None, lk, 1024), lambda bi, qi: (bi, 0, 0)),
                  _const_spec(bias.shape)],
        out_specs=pl.BlockSpec((None, bq, 1024), lambda bi, qi: (bi, qi, 0)),
        out_shape=jax.ShapeDtypeStruct((b, s, 1024), BF16),
        compiler_params=_params("parallel", "arbitrary"),
        name="band",
    )(qc, z, kcb, vcb, bias)


def _rope_tables(pos):
    half = ROPE_DIM // 2
    freq = ROPE_BASE ** (-jnp.arange(half, dtype=F32) / half)
    ang = pos.astype(F32)[:, None] * freq[None, :]
    cos, sin = jnp.cos(ang), jnp.sin(ang)
    z = jnp.zeros_like(cos)
    return jnp.concatenate([cos, z, cos, z], 1), jnp.concatenate([-sin, z, sin, z], 1)


def _rope_lanes(x):
    half = ROPE_DIM // 2
    z = jnp.zeros(x.shape[:-1] + (half,), x.dtype)
    return jnp.concatenate([x[..., :half], z, x[..., half:], z], -1)


def _rope_unlanes(x):
    half = ROPE_DIM // 2
    return jnp.concatenate([x[..., :half], x[..., 2 * half:3 * half]], -1)


def _t5_bucket_np(rel):
    nb = T5_BUCKETS // 2
    max_exact = nb // 2
    n = np.abs(rel)
    nf = np.maximum(n, 1).astype(np.float64)
    large = max_exact + (np.log(nf / max_exact) / math.log(T5_MAX_DIST / max_exact) * (nb - max_exact)).astype(np.int64)
    large = np.minimum(large, nb - 1)
    return np.where(rel > 0, nb, 0) + np.where(n < max_exact, n, large)


def _toeplitz(w, rows, width, cols):
    flat = jnp.tile(w, (1,) * (w.ndim - 1) + (rows,))[..., :rows * width]
    return flat.reshape(w.shape[:-1] + (rows, width))[..., :cols]


def _t5_tiles(t5_bias, bq, nd):
    width = 2 * LANE
    k = np.arange(width + 1)
    delta = np.where(k < LANE, k, k - (width + 1))
    rel = (np.arange(nd) - (nd - 1))[:, None] * LANE + delta[None, :]
    w = jnp.transpose(t5_bias[_t5_bucket_np(rel)], (2, 0, 1))
    return _toeplitz(w, bq, width, LANE)


def _band_bias(rel_tab, bq, qk_off):
    bw = _band_width(bq)
    width = 2 * bw
    k = np.arange(width + 1)
    delta = np.where(k < bw, k, k - (width + 1))
    idx = np.clip(qk_off - delta, -(CHUNK - 1), REL_CLIP) + (CHUNK - 1)
    return _toeplitz(jnp.transpose(rel_tab[idx], (1, 0)), bq, width, bw)


def _even_weights(w_in, b_wuq, b_wukv, b_qn, b_kn):
    d = w_in.shape[0]
    offs = np.cumsum((0,) + EVEN_SPLITS)
    aq, ak, av, ag, iq, ik, iw, bcq, bckv, bkpe, bg = [w_in[:, offs[t]:offs[t + 1]] for t in range(11)]
    slab_idx = jnp.concatenate([ik, iw, jnp.zeros((d, LANE - D_IDX - H_IDX), w_in.dtype)], 1)
    w = jnp.concatenate([aq, ag, iq, bg, bcq, ak, av, bckv, slab_idx, _rope_lanes(bkpe)], 1).astype(BF16)
    uq = b_wuq.reshape(Q_LORA, H_B, QK_B)
    uq = jnp.concatenate([uq[..., :NOPE], _rope_lanes(uq[..., NOPE:])], -1).reshape(Q_LORA, H_B * 256).astype(BF16)
    ukv = b_wukv.reshape(KV_LORA, H_B, NOPE + V_DIM)
    ukv = jnp.concatenate([ukv[..., :NOPE].reshape(KV_LORA, H_B * NOPE),
                           ukv[..., NOPE:].reshape(KV_LORA, H_B * V_DIM)], 1).astype(BF16)
    pad_gain = lambda g: jnp.concatenate([g[:NOPE], _rope_lanes(g[NOPE:])])[None, :]
    return w, uq, ukv, pad_gain(b_qn), pad_gain(b_kn)


def _pad_rows(x, lp):
    return jnp.pad(x, ((0, 0), (0, lp - x.shape[1]), (0, 0)))


def _even_layer(h, p, past, q0, ln_g, w_in, uq, ukv, a_qn, a_kn, t5_bias, b_qln, b_kvln, bqn, bkn, wo, wg, wp):
    b, s, d = h.shape
    m = b * s
    z = _norm_mm(h.reshape(m, d), ln_g[None, :], w_in, min(m, 1024), E_END // 4).reshape(b, s, E_END)
    cos, sin = _rope_tables(q0 + jnp.arange(s, dtype=jnp.int32))
    bs = min(s, 256)
    qa, ka, qb, ckv, kpe_l = _even_prep(z, cos, sin, a_qn[None, :], a_kn[None, :], b_qln[None, :], b_kvln[None, :],
                                        uq, bqn, bs)
    av = z[..., E_AV:E_AV + 256]
    kidx = z[..., E_IDX:E_IDX + LANE]
    new = (ka.reshape(b, s, HKV_A, DH_A), av.reshape(b, s, HKV_A, DH_A), kidx[..., :D_IDX], ckv, _rope_unlanes(kpe_l))
    if past is None:
        l_true = s
        ka_all, va_all, kidx_all, ckv_all, kpe_all = ka, av, kidx, ckv, kpe_l
    else:
        c_k, c_v, c_ik, c_ckv, c_kpe = past
        pl_ = c_k.shape[1]
        l_true = pl_ + s
        lp = -(-l_true // LANE) * LANE
        cat = lambda c, n_: _pad_rows(jnp.concatenate([c, n_], 1), lp)
        ka_all = cat(c_k.reshape(b, pl_, 256), ka)
        va_all = cat(c_v.reshape(b, pl_, 256), av)
        kidx_all = cat(jnp.pad(c_ik, ((0, 0), (0, 0), (0, LANE - D_IDX))), kidx)
        ckv_all = cat(c_ckv, ckv)
        kpe_all = cat(_rope_lanes(c_kpe), kpe_l)
    lp = ka_all.shape[1]
    ksel = min(TOPK_MAX, l_true // 4)
    bq = min(s, LANE)
    nd = (q0 + s - bq) // LANE + 1
    tiles = _t5_tiles(t5_bias, bq, nd)
    classes = _causal_classes(s, lp, q0)
    o_a = jnp.concatenate([_dsa(qa, z, ka_all, va_all, kidx_all, tiles, bq=bq, l_true=l_true, q0=q0, ksel=ksel,
                                row0=r0, rows=nr, lp=lc) for r0, nr, lc in classes], axis=1)
    kb, vb = _mla_kv(ckv_all, kpe_all, ukv, bkn, _key_chunk(lp))
    o_b = jnp.concatenate([_mla(qb, z, kb, vb, bq=min(s, 256), l_true=l_true, q0=q0,
                                row0=r0, rows=nr, lp=lc) for r0, nr, lc in classes], axis=1)
    y = _out_ple(h.reshape(m, d), o_a.reshape(m, -1), o_b.reshape(m, -1), p.reshape(m, -1), wo, wg, wp, min(m, 256))
    return y.reshape(b, s, d), new


def _odd_layer(h, p, past, q0, ln_g, w_in, c_qn, c_kn, c_rel, d_g, d_b, d_ws, d_bs, wo, wg, wp):
    b, s, d = h.shape
    m = b * s
    z = _norm_mm(h.reshape(m, d), ln_g[None, :], w_in, min(m, 1024), 1024).reshape(b, s, O_END)
    n = min(s, D_CHUNK)
    qc, kc, kcb, vcb, o_d, dvn = _odd_prep(z, c_qn[None, :], c_kn[None, :], d_g[None, :], d_b[None, :],
                                           d_ws[:, :n, :n], d_bs[:, :n].T, min(s, 256), n)
    cv = z[..., O_CV:O_CV + 1024]
    if past is None:
        keep = min(C_BACK * CHUNK, s)
        c_new = (kc[:, s - keep:].reshape(b, keep, H_C, DH_C), cv[:, s - keep:].reshape(b, keep, H_C, DH_C))
        bqc = 2 * LANE
        bias = _band_bias(c_rel, bqc, C_BACK * CHUNK)
        o_c = _band(qc, z, kcb, vcb, bias, bq=bqc, sliding=True, q0=0, k0=0, k_end=s)
    else:
        nc = past[0].shape[1]
        c_new = (kc.reshape(b, s, H_C, DH_C), cv.reshape(b, s, H_C, DH_C))
        cat = lambda c, n_: _pad_rows(jnp.concatenate([c.reshape(b, nc, 1024).astype(BF16), n_], 1), _band_width(s))
        bias = _band_bias(c_rel, s, nc)
        o_c = _band(qc, z, cat(past[0], kcb), cat(past[1], vcb), bias,
                    bq=s, sliding=False, q0=q0, k0=q0 - nc, k_end=q0 + s)
    y = _out_ple(h.reshape(m, d), o_c.reshape(m, -1), o_d.reshape(m, -1), p.reshape(m, -1), wo, wg, wp, min(m, 256))
    return y.reshape(b, s, d), c_new, dvn


def kernel(x_prompt, x_sample, cache_a_k, cache_a_v, cache_a_idx_k, cache_b_ckv, cache_b_kpe, cache_c_k, cache_c_v, p_prompt, p_sample, ln_g, w_in_even, a_q_norm, a_k_norm, t5_bias, b_q_lora_norm, b_kv_lora_norm, b_w_uq, b_w_ukv, b_q_norm, b_k_norm, w_out_even, w_in_odd, c_q_norm, c_k_norm, c_rel_bias, d_ln_g, d_ln_b, d_w_s, d_b_s, w_out_odd, ple_proj, ple_gate):
    depth = ln_g.shape[0]
    past_len = cache_a_k.shape[2]
    hp, hs = x_prompt, x_sample
    ev_p, ev_s, od_p, od_s, dv_s = [], [], [], [], []
    for i in range(depth):
        j = i // 2
        wg = ple_gate[i].astype(BF16)
        wp = ple_proj[i].astype(BF16)
        if i % 2 == 0:
            w_in, uq, ukv, bqn, bkn = _even_weights(w_in_even[j], b_w_uq[j], b_w_ukv[j], b_q_norm[j], b_k_norm[j])
            w = (ln_g[i], w_in, uq, ukv, a_q_norm[j], a_k_norm[j], t5_bias, b_q_lora_norm[j], b_kv_lora_norm[j],
                 bqn, bkn, w_out_even[j].astype(BF16), wg, wp)
            hp, sp = _even_layer(hp, p_prompt[i], None, 0, *w)
            past = (cache_a_k[j], cache_a_v[j], cache_a_idx_k[j], cache_b_ckv[j], cache_b_kpe[j])
            hs, ss = _even_layer(hs, p_sample[i], past, past_len, *w)
            ev_p.append(sp)
            ev_s.append(ss)
        else:
            w = (ln_g[i], w_in_odd[j].astype(BF16), c_q_norm[j], c_k_norm[j], c_rel_bias[j], d_ln_g[j], d_ln_b[j],
                 d_w_s[j], d_b_s[j], w_out_odd[j].astype(BF16), wg, wp)
            hp, sp, _ = _odd_layer(hp, p_prompt[i], None, 0, *w)
            hs, ss, dvs = _odd_layer(hs, p_sample[i], (cache_c_k[j], cache_c_v[j]), past_len, *w)
            od_p.append(sp)
            od_s.append(ss)
            dv_s.append(dvs)
    st = lambda lst, n_: jnp.stack([e[n_] for e in lst], 0)
    return (hp, hs, st(ev_p, 0), st(ev_p, 1), st(ev_p, 2), st(ev_p, 3), st(ev_p, 4), st(od_p, 0), st(od_p, 1),
            st(ev_s, 0), st(ev_s, 1), st(ev_s, 2), st(ev_s, 3), st(ev_s, 4), st(od_s, 0), st(od_s, 1),
            jnp.stack(dv_s, 0))
```

```python
import functools
import math

import numpy as np
import jax
import jax.numpy as jnp
from jax import lax
from jax.experimental import pallas as pl
from jax.experimental.pallas import tpu as pltpu

F32 = jnp.float32
BF16 = jnp.bfloat16
INT_MIN = -2 ** 31
LOG2E = math.log2(math.e)
CLASS_ROWS = 256
SEARCH_GROUPS = 4

D_MODEL = 2048
CHUNK = 64
CHUNK_SHIFT = 6
LANE = 128
H_A, HKV_A, GROUP_A, DH_A = 8, 2, 4, 128
H_IDX, D_IDX = 16, 64
TOPK_MAX = 256
T5_BUCKETS, T5_MAX_DIST = 32, 128
H_B, Q_LORA, KV_LORA, NOPE, ROPE_DIM, V_DIM = 8, 512, 256, 128, 64, 128
ROPE_BASE = 10000.0
QK_B = NOPE + ROPE_DIM
H_C, DH_C, C_BACK, REL_CLIP = 8, 128, 8, 128
W_D, G_D, DG_D, D_CHUNK = 1024, 8, 128, 128
PLE_DIM = 256

EVEN_SPLITS = (H_A * DH_A, HKV_A * DH_A, HKV_A * DH_A, H_A * DH_A, H_IDX * D_IDX, D_IDX, H_IDX,
               Q_LORA, KV_LORA, ROPE_DIM, H_B * V_DIM)
E_AQ, E_AG, E_IQ, E_BG, E_BCQ, E_AK, E_AV, E_CKV, E_IDX, E_KPE, E_END = (
    0, 1024, 2048, 3072, 4096, 4608, 4864, 5120, 5376, 5504, 5632)
O_CQ, O_CK, O_CV, O_CG, O_DU, O_DV, O_DG, O_END = 0, 1024, 2048, 3072, 4096, 5120, 6144, 7168

VMEM_LIMIT_BYTES = 56 * 1024 * 1024
NT_DIMS = (((1,), (1,)), ((), ()))


def _params(*sem):
    return pltpu.CompilerParams(dimension_semantics=sem, vmem_limit_bytes=VMEM_LIMIT_BYTES)


def _const_spec(shape):
    zeros = (0,) * len(shape)
    return pl.BlockSpec(shape, lambda *_: zeros, pipeline_mode=pl.Buffered(1))


def _rs(x, n=None, eps=1e-6):
    n = x.shape[-1] if n is None else n
    return lax.rsqrt(jnp.sum(x * x, axis=-1, keepdims=True) / n + eps)


def _silu(x):
    return x * (1.0 / (1.0 + jnp.exp(-x)))


def _softmax_pv(parts, scale=1.0):
    m = functools.reduce(jnp.maximum, [jnp.max(lg, axis=-1, keepdims=True) for lg, _ in parts])
    o = s = None
    for lg, v in parts:
        p = jnp.exp2((lg - m) * (scale * LOG2E))
        ps = jnp.sum(p, axis=-1, keepdims=True)
        po = jnp.dot(p.astype(BF16), v, preferred_element_type=F32)
        o, s = (po, ps) if o is None else (o + po, s + ps)
    return o / s


def _norm_mm_kernel(x_ref, g_ref, w_ref, o_ref, xn_ref):
    @pl.when(pl.program_id(1) == 0)
    def _():
        x = x_ref[...]
        xn_ref[...] = (x * _rs(x) * g_ref[...]).astype(BF16)

    o_ref[...] = jnp.dot(xn_ref[...], w_ref[...], preferred_element_type=F32)


def _norm_mm(x, g, w, bm, bn):
    m, d = x.shape
    n = w.shape[1]
    return pl.pallas_call(
        _norm_mm_kernel,
        grid=(m // bm, n // bn),
        in_specs=[pl.BlockSpec((bm, d), lambda i, j: (i, 0)),
                  pl.BlockSpec((1, d), lambda i, j: (0, 0)),
                  pl.BlockSpec((d, bn), lambda i, j: (0, j))],
        out_specs=pl.BlockSpec((bm, bn), lambda i, j: (i, j)),
        out_shape=jax.ShapeDtypeStruct((m, n), F32),
        scratch_shapes=[pltpu.VMEM((bm, d), BF16)],
        compiler_params=_params("parallel", "arbitrary"),
        name="norm_mm",
    )(x, g, w)


def _rope(x, cos, sin):
    return x * cos + pltpu.roll(x, 64, 1) * sin


def _even_prep_kernel(aq_ref, bcq_ref, ak_ref, ckv_ref, kpe_ref, cos_ref, sin_ref,
                      aqn_ref, akn_ref, qln_ref, kvln_ref, wuq_ref, bqn_ref,
                      qa_o, ka_o, qb_o, ckv_o, kpe_o):
    for h in range(H_A):
        x = aq_ref[:, h * DH_A:(h + 1) * DH_A]
        qa_o[:, h * DH_A:(h + 1) * DH_A] = (x * _rs(x) * aqn_ref[...]).astype(BF16)
    for n in range(HKV_A):
        x = ak_ref[:, n * DH_A:(n + 1) * DH_A]
        ka_o[:, n * DH_A:(n + 1) * DH_A] = x * _rs(x) * akn_ref[...]
    c = ckv_ref[...]
    ckv_o[...] = c * _rs(c) * kvln_ref[...]
    cos = cos_ref[...]
    sin = sin_ref[...]
    kpe_o[...] = _rope(kpe_ref[...], cos, sin)
    cq = bcq_ref[...]
    cqn = (cq * _rs(cq) * qln_ref[...]).astype(BF16)
    qb = jnp.dot(cqn, wuq_ref[...], preferred_element_type=F32)
    g = bqn_ref[...]
    for h in range(H_B):
        nope = qb[:, h * 256:h * 256 + 128]
        rot = _rope(qb[:, h * 256 + 128:(h + 1) * 256], cos, sin)
        ss = jnp.sum(nope * nope, -1, keepdims=True) + jnp.sum(rot * rot, -1, keepdims=True)
        r = lax.rsqrt(ss / QK_B + 1e-6)
        qb_o[:, h * 256:h * 256 + 128] = (nope * r * g[:, :128]).astype(BF16)
        qb_o[:, h * 256 + 128:(h + 1) * 256] = (rot * r * g[:, 128:]).astype(BF16)


def _even_prep(z, cos, sin, aqn, akn, qln, kvln, wuq, bqn, bs):
    b, s, _ = z.shape

    def zspec(width, off):
        return pl.BlockSpec((None, bs, width), lambda bi, si: (bi, si, off // width))

    def ospec(width):
        return pl.BlockSpec((None, bs, width), lambda bi, si: (bi, si, 0))

    pos_spec = pl.BlockSpec((bs, LANE), lambda bi, si: (si, 0))
    return pl.pallas_call(
        _even_prep_kernel,
        grid=(b, s // bs),
        in_specs=[zspec(1024, E_AQ), zspec(512, E_BCQ), zspec(256, E_AK), zspec(256, E_CKV),
                  zspec(128, E_KPE), pos_spec, pos_spec,
                  _const_spec((1, DH_A)), _const_spec((1, DH_A)), _const_spec((1, Q_LORA)),
                  _const_spec((1, KV_LORA)), _const_spec((Q_LORA, H_B * 256)), _const_spec((1, 256))],
        out_specs=[ospec(1024), ospec(256), ospec(2048), ospec(256), ospec(128)],
        out_shape=[jax.ShapeDtypeStruct((b, s, 1024), BF16), jax.ShapeDtypeStruct((b, s, 256), F32),
                   jax.ShapeDtypeStruct((b, s, 2048), BF16), jax.ShapeDtypeStruct((b, s, 256), F32),
                   jax.ShapeDtypeStruct((b, s, 128), F32)],
        compiler_params=_params("parallel", "arbitrary"),
        name="even_prep",
    )(z, z, z, z, z, cos, sin, aqn, akn, qln, kvln, wuq, bqn)


def _mla_kv_kernel(ckv_ref, kpe_ref, w_ref, g_ref, kb_o, vb_o):
    kv = jnp.dot(ckv_ref[...].astype(BF16), w_ref[...], preferred_element_type=F32)
    kp = kpe_ref[...]
    skp = jnp.sum(kp * kp, -1, keepdims=True)
    g = g_ref[...]
    for h in range(H_B):
        nope = kv[:, h * NOPE:(h + 1) * NOPE]
        r = lax.rsqrt((jnp.sum(nope * nope, -1, keepdims=True) + skp) / QK_B + 1e-6)
        kb_o[:, h * 256:h * 256 + 128] = (nope * r * g[:, :128]).astype(BF16)
        kb_o[:, h * 256 + 128:(h + 1) * 256] = (kp * r * g[:, 128:]).astype(BF16)
    vb_o[...] = kv[:, H_B * NOPE:].astype(BF16)


def _mla_kv(ckv_all, kpe_all, wukv, bkn, bl):
    b, lp, _ = ckv_all.shape
    return pl.pallas_call(
        _mla_kv_kernel,
        grid=(b, lp // bl),
        in_specs=[pl.BlockSpec((None, bl, KV_LORA), lambda bi, li: (bi, li, 0)),
                  pl.BlockSpec((None, bl, LANE), lambda bi, li: (bi, li, 0)),
                  _const_spec((KV_LORA, 2048)), _const_spec((1, 256))],
        out_specs=[pl.BlockSpec((None, bl, 2048), lambda bi, li: (bi, li, 0)),
                   pl.BlockSpec((None, bl, 1024), lambda bi, li: (bi, li, 0))],
        out_shape=[jax.ShapeDtypeStruct((b, lp, 2048), BF16), jax.ShapeDtypeStruct((b, lp, 1024), BF16)],
        compiler_params=_params("parallel", "arbitrary"),
        name="mla_kv",
    )(ckv_all, kpe_all, wukv, bkn)


def _key_chunk(lp):
    for c in (512, 384, 256, 128):
        if lp % c == 0:
            return c
    raise ValueError(lp)


def _dsa_kernel(qa_ref, iq_ref, sa_ref, ag_ref, ka_ref, va_ref, kidx_ref, bias_ref, acc_ref, o_ref,
                key_ref, hi_ref, lo_ref, *, bq, lp, l_true, q0, ksel, nd, blk0):
    i = pl.program_id(1) + blk0
    q_start = q0 + i * bq
    qpos = q_start + lax.broadcasted_iota(jnp.int32, (bq, 1), 0)
    ck = _key_chunk(lp)

    iq = iq_ref[...].astype(BF16)
    a = jnp.concatenate([iq[:, p * LANE:(p + 1) * LANE] for p in range(H_IDX // 2)], axis=0)
    wi = sa_ref[...] * (H_IDX ** -0.5 * D_IDX ** -0.5)
    for c0 in range(0, lp, ck):
        kk = kidx_ref[c0:c0 + ck, :]
        lane = lax.broadcasted_iota(jnp.int32, kk.shape, 1)
        k_lo = jnp.where(lane < D_IDX, kk, 0.0).astype(BF16)
        k_hi = jnp.where(lane >= D_IDX, pltpu.roll(kk, D_IDX, 1), 0.0).astype(BF16)
        s_lo = lax.dot_general(a, k_lo, NT_DIMS, preferred_element_type=F32)
        s_hi = lax.dot_general(a, k_hi, NT_DIMS, preferred_element_type=F32)
        sc = jnp.zeros((bq, ck), F32)
        for p in range(H_IDX // 2):
            w0 = wi[:, D_IDX + 2 * p:D_IDX + 2 * p + 1]
            w1 = wi[:, D_IDX + 2 * p + 1:D_IDX + 2 * p + 2]
            sc = sc + w0 * jnp.maximum(s_lo[p * bq:(p + 1) * bq], 0.0)
            sc = sc + w1 * jnp.maximum(s_hi[p * bq:(p + 1) * bq], 0.0)
        kpos = c0 + lax.broadcasted_iota(jnp.int32, (1, ck), 1)
        valid = ((kpos >> CHUNK_SHIFT) <= (qpos >> CHUNK_SHIFT)) & (kpos < l_true)
        bits = lax.bitcast_convert_type(sc, jnp.int32)
        key = jnp.where(bits < 0, bits ^ 0x7FFFFFFF, bits)
        key = jnp.where(bits == INT_MIN, 0, key)
        key = jnp.where(valid, key, INT_MIN)
        key_ref[:, c0:c0 + ck] = key
        hi_ref[:, c0:c0 + ck] = (key >> 16) + 2 ** 15
        lo_ref[:, c0:c0 + ck] = key & 0xFFFF

    ng = SEARCH_GROUPS
    gr = bq // ng
    groups = [slice(r * gr, (r + 1) * gr) for r in range(ng)]

    def count_ge(ref, rows, cand):
        acc = None
        for j in range(lp // LANE):
            d = (ref[rows, j * LANE:(j + 1) * LANE] - cand) >> 31
            acc = d if acc is None else acc + d
        return lp + jnp.sum(acc.astype(F32), axis=-1, keepdims=True)

    def search16(ref, need):
        ts = [jnp.zeros((gr, 1), jnp.int32)] * ng
        for bit in reversed(range(16)):
            for r in range(ng):
                cand = ts[r] | (1 << bit)
                ts[r] = jnp.where(count_ge(ref, groups[r], cand) >= need[r], cand, ts[r])
        return ts

    t_hi = search16(hi_ref, (float(ksel),) * ng)
    need_lo = []
    for rows, t in zip(groups, t_hi):
        need_lo.append(ksel - count_ge(hi_ref, rows, t + 1))
        lo_ref[rows, :] = jnp.where(hi_ref[rows, :] == t, lo_ref[rows, :], -1)
    t_lo = search16(lo_ref, need_lo)
    t_s = jnp.concatenate([lax.shift_left(h - 2 ** 15, 16) | l for h, l in zip(t_hi, t_lo)], axis=0)
    key = key_ref[...]
    sel = (key >= t_s) & (key > INT_MIN)

    d_base = (nd - 1) - (q0 // LANE + (i if bq == LANE else 0))
    nk = lp // LANE
    for n in range(HKV_A):
        q4 = jnp.concatenate([qa_ref[:, (n * GROUP_A + g) * DH_A:(n * GROUP_A + g + 1) * DH_A]
                              for g in range(GROUP_A)], axis=0)
        k_n = ka_ref[:, n * DH_A:(n + 1) * DH_A].astype(BF16)
        v_n = va_ref[:, n * DH_A:(n + 1) * DH_A].astype(BF16)
        lg4 = lax.dot_general(q4, k_n, NT_DIMS, preferred_element_type=F32) * DH_A ** -0.5
        ps, ss = [], []
        for g in range(GROUP_A):
            bias = jnp.concatenate([bias_ref[n * GROUP_A + g, jnp.minimum(d_base + j, nd - 1)]
                                    for j in range(nk)], axis=1)
            lg = jnp.where(sel, lg4[g * bq:(g + 1) * bq] + bias, -jnp.inf)
            p = jnp.exp(lg - jnp.max(lg, axis=-1, keepdims=True))
            ss.append(jnp.sum(p, axis=-1, keepdims=True))
            ps.append(p.astype(BF16))
        o4 = jnp.dot(jnp.concatenate(ps, axis=0), v_n, preferred_element_type=F32)
        for g in range(GROUP_A):
            h = n * GROUP_A + g
            gate = ag_ref[:, h * DH_A:(h + 1) * DH_A]
            o_ref[:, h * DH_A:(h + 1) * DH_A] = (o4[g * bq:(g + 1) * bq] / ss[g] * _silu(gate)).astype(BF16)


def _dsa(qa, z, ka_all, va_all, kidx_all, kidx_col, bias_tiles, acc, *, bq, l_true, q0, ksel, row0, rows, lp):
    b, s, _ = qa.shape
    nd = bias_tiles.shape[1]
    blk0 = row0 // bq
    assert q0 % LANE == 0 and (bq == LANE or s == bq)
    assert nd == (q0 + s - bq) // LANE + 1
    kern = functools.partial(_dsa_kernel, bq=bq, lp=lp, l_true=l_true, q0=q0, ksel=ksel, nd=nd, blk0=blk0)

    def zspec(width, off):
        return pl.BlockSpec((None, bq, width), lambda bi, qi: (bi, qi + blk0, off // width))

    def kspec(width):
        return pl.BlockSpec((None, lp, width), lambda bi, qi: (bi, 0, 0))

    return pl.pallas_call(
        kern,
        grid=(b, rows // bq),
        in_specs=[pl.BlockSpec((None, bq, 1024), lambda bi, qi: (bi, qi + blk0, 0)),
                  zspec(1024, E_IQ), zspec(128, E_IDX), zspec(1024, E_AG),
                  kspec(256), kspec(256),
                  pl.BlockSpec((None, lp, LANE), lambda bi, qi: (bi, 0, kidx_col)),
                  _const_spec(bias_tiles.shape),
                  pl.BlockSpec(memory_space=pl.ANY)],
        out_specs=pl.BlockSpec((None, bq, 1024), lambda bi, qi: (bi, qi + blk0, 0)),
        out_shape=jax.ShapeDtypeStruct(acc.shape, acc.dtype),
        input_output_aliases={8: 0},
        scratch_shapes=[pltpu.VMEM((bq, lp), jnp.int32)] * 3,
        compiler_params=_params("parallel", "arbitrary"),
        name="dsa",
    )(qa, z, z, z, ka_all, va_all, kidx_all, bias_tiles, acc)


def _causal_classes(s, lp, q0):
    if q0 == 0 and s % CLASS_ROWS == 0 and lp == s:
        return [(c * CLASS_ROWS, CLASS_ROWS, (c + 1) * CLASS_ROWS) for c in range(s // CLASS_ROWS)]
    return [(0, s, lp)]


def _mla_kernel(qb_ref, bg_ref, kb_ref, vb_ref, acc_ref, o_ref, *, bq, lp, l_true, q0, blk0, n_full):
    q_start = q0 + (pl.program_id(1) + blk0) * bq
    qpos = q_start + lax.broadcasted_iota(jnp.int32, (bq, 1), 0)
    kpos = n_full + lax.broadcasted_iota(jnp.int32, (1, lp - n_full), 1)
    valid = ((kpos >> CHUNK_SHIFT) <= (qpos >> CHUNK_SHIFT)) & (kpos < l_true)
    for h in range(H_B):
        q = qb_ref[:, h * 256:(h + 1) * 256]
        parts = []
        if n_full:
            parts.append((lax.dot_general(q, kb_ref[:n_full, h * 256:(h + 1) * 256], NT_DIMS,
                                          preferred_element_type=F32), vb_ref[:n_full, h * V_DIM:(h + 1) * V_DIM]))
        lg = lax.dot_general(q, kb_ref[n_full:, h * 256:(h + 1) * 256], NT_DIMS, preferred_element_type=F32)
        parts.append((jnp.where(valid, lg, -jnp.inf), vb_ref[n_full:, h * V_DIM:(h + 1) * V_DIM]))
        o = _softmax_pv(parts, QK_B ** -0.5)
        gate = bg_ref[:, h * V_DIM:(h + 1) * V_DIM]
        o_ref[:, h * V_DIM:(h + 1) * V_DIM] = (o * _silu(gate)).astype(BF16)


def _mla(qb, z, kb, vb, acc, *, bq, l_true, q0, row0, rows, lp):
    b = qb.shape[0]
    blk0 = row0 // bq
    n_full = min(min(q0 + row0 + CHUNK, l_true) // LANE * LANE, lp - LANE)
    kern = functools.partial(_mla_kernel, bq=bq, lp=lp, l_true=l_true, q0=q0, blk0=blk0, n_full=n_full)
    return pl.pallas_call(
        kern,
        grid=(b, rows // bq),
        in_specs=[pl.BlockSpec((None, bq, 2048), lambda bi, qi: (bi, qi + blk0, 0)),
                  pl.BlockSpec((None, bq, 1024), lambda bi, qi: (bi, qi + blk0, E_BG // 1024)),
                  pl.BlockSpec((None, lp, 2048), lambda bi, qi: (bi, 0, 0)),
                  pl.BlockSpec((None, lp, 1024), lambda bi, qi: (bi, 0, 0)),
                  pl.BlockSpec(memory_space=pl.ANY)],
        out_specs=pl.BlockSpec((None, bq, 1024), lambda bi, qi: (bi, qi + blk0, 0)),
        out_shape=jax.ShapeDtypeStruct(acc.shape, acc.dtype),
        input_output_aliases={4: 0},
        compiler_params=_params("parallel", "arbitrary"),
        name="mla",
    )(qb, z, kb, vb, acc)


def _out_ple_kernel(h_ref, oa_ref, ob_ref, p_ref, wo_ref, wg_ref, wp_ref, o_ref):
    half = oa_ref.shape[-1]
    h1 = (h_ref[...]
          + jnp.dot(oa_ref[...], wo_ref[:half, :], preferred_element_type=F32)
          + jnp.dot(ob_ref[...], wo_ref[half:, :], preferred_element_type=F32))
    r = (h1 * _rs(h1)).astype(BF16)
    gate = 1.0 / (1.0 + jnp.exp(-jnp.dot(r, wg_ref[...], preferred_element_type=F32)))
    o_ref[...] = h1 + gate * jnp.dot(p_ref[...].astype(BF16), wp_ref[...], preferred_element_type=F32)


def _out_ple(h, oa, ob, p, wo, wg, wp, bm):
    m, d = h.shape
    half = oa.shape[1]

    def rows(width):
        return pl.BlockSpec((bm, width), lambda i: (i, 0))

    return pl.pallas_call(
        _out_ple_kernel,
        grid=(m // bm,),
        in_specs=[rows(d), rows(half), rows(half), rows(PLE_DIM),
                  _const_spec((2 * half, d)), _const_spec((d, d)), _const_spec((PLE_DIM, d))],
        out_specs=rows(d),
        out_shape=jax.ShapeDtypeStruct((m, d), F32),
        compiler_params=_params("parallel"),
        name="out_ple",
    )(h, oa, ob, p, wo, wg, wp)


def _odd_prep_kernel(cq_ref, ck_ref, cv_ref, dv_ref, du_ref, dgate_ref, cqn_ref, ckn_ref, dg_ref, db_ref,
                     ws_ref, bs_ref, qc_o, kcb_o, vcb_o, od_o, kct_o, cvt_o, *maybe_dvn_o, n):
    for h in range(H_C):
        sl = slice(h * DH_C, (h + 1) * DH_C)
        x = cq_ref[:, sl]
        qc_o[:, sl] = (x * _rs(x) * cqn_ref[...]).astype(BF16)
        x = ck_ref[:, sl]
        kn = x * _rs(x) * ckn_ref[...]
        kct_o[:, sl] = kn
        kcb_o[:, sl] = kn.astype(BF16)
    cv = cv_ref[...]
    cvt_o[...] = cv
    vcb_o[...] = cv.astype(BF16)
    dv = dv_ref[...]
    xc = dv - jnp.mean(dv, -1, keepdims=True)
    var = jnp.mean(xc * xc, -1, keepdims=True)
    dvn = xc * lax.rsqrt(var + 1e-5) * dg_ref[...] + db_ref[...]
    for dvn_o in maybe_dvn_o:
        dvn_o[...] = dvn
    dvn = dvn.astype(BF16)
    row = lax.broadcasted_iota(jnp.int32, (n, n), 0)
    col = lax.broadcasted_iota(jnp.int32, (n, n), 1)
    for g in range(G_D):
        sl = slice(g * DG_D, (g + 1) * DG_D)
        w = jnp.where(col <= row, ws_ref[g], 0.0).astype(BF16)
        for c in range(dv.shape[0] // n):
            rs = slice(c * n, (c + 1) * n)
            sg = jnp.dot(w, dvn[rs, sl], preferred_element_type=F32) + bs_ref[:, g:g + 1]
            od_o[rs, sl] = (du_ref[rs, sl] * sg * _silu(dgate_ref[rs, sl])).astype(BF16)


def _odd_prep(z, cqn, ckn, dg, db, ws, bs_t, bs, n, keep, want_dvn):
    b, s, _ = z.shape
    first_tail = (s - keep) // bs
    assert keep % bs == 0

    def zspec(off):
        return pl.BlockSpec((None, bs, 1024), lambda bi, si: (bi, si, off // 1024))

    ospec = pl.BlockSpec((None, bs, 1024), lambda bi, si: (bi, si, 0))
    tspec = pl.BlockSpec((None, bs, 1024), lambda bi, si: (bi, jnp.maximum(si - first_tail, 0), 0))
    act = lambda dt: jax.ShapeDtypeStruct((b, s, 1024), dt)
    tail = jax.ShapeDtypeStruct((b, keep, 1024), F32)
    return pl.pallas_call(
        functools.partial(_odd_prep_kernel, n=n),
        grid=(b, s // bs),
        in_specs=[zspec(O_CQ), zspec(O_CK), zspec(O_CV), zspec(O_DV), zspec(O_DU), zspec(O_DG),
                  _const_spec((1, DH_C)), _const_spec((1, DH_C)), _const_spec((1, W_D)), _const_spec((1, W_D)),
                  _const_spec((G_D, n, n)), _const_spec((n, G_D))],
        out_specs=[ospec] * 4 + [tspec] * 2 + [ospec] * want_dvn,
        out_shape=[act(BF16)] * 4 + [tail] * 2 + [act(F32)] * want_dvn,
        compiler_params=_params("parallel", "arbitrary"),
        name="odd_prep",
    )(z, z, z, z, z, z, cqn, ckn, dg, db, ws, bs_t)


BACK_TILES = C_BACK * CHUNK // LANE


def _band_width(bq):
    return (BACK_TILES + -(-bq // LANE)) * LANE


def _band_kernel(q_ref, cg_ref, k_ref, v_ref, bias_ref, o_ref, *, bq, sliding, q0, k0, k_end):
    i = pl.program_id(1)
    q_start = q0 + i * bq
    bw = _band_width(bq)
    if sliding:
        first = i * (bq // LANE) - BACK_TILES
        win_start = first * LANE
        rows = [pl.ds(pl.multiple_of(jnp.maximum(first + t, 0) * LANE, LANE), LANE) for t in range(bw // LANE)]
    else:
        win_start = k0
        rows = [pl.ds(t * LANE, LANE) for t in range(bw // LANE)]
    kw = jnp.concatenate([k_ref[r, :] for r in rows], axis=0)
    vw = jnp.concatenate([v_ref[r, :] for r in rows], axis=0)
    qpos = q_start + lax.broadcasted_iota(jnp.int32, (bq, 1), 0)
    kpos = win_start + lax.broadcasted_iota(jnp.int32, (1, bw), 1)
    dc = (qpos >> CHUNK_SHIFT) - (kpos >> CHUNK_SHIFT)
    valid = (dc >= 0) & (dc <= C_BACK) & (kpos >= 0) & (kpos < k_end)
    for h in range(H_C):
        sl = slice(h * DH_C, (h + 1) * DH_C)
        lg = lax.dot_general(q_ref[:, sl], kw[:, sl], NT_DIMS, preferred_element_type=F32) * DH_C ** -0.5
        lg = jnp.where(valid, lg + bias_ref[h], -jnp.inf)
        o = _softmax_pv([(lg, vw[:, sl])])
        o_ref[:, sl] = (o * _silu(cg_ref[:, sl])).astype(BF16)


def _band(qc, z, kcb, vcb, bias, *, bq, sliding, q0, k0, k_end):
    b, s, _ = qc.shape
    lk = kcb.shape[1]
    if sliding:
        assert bq % LANE == 0 and q0 == 0 and k0 == 0
    else:
        assert lk == _band_width(bq) and s == bq
    kern = functools.partial(_band_kernel, bq=bq, sliding=sliding, q0=q0, k0=k0, k_end=k_end)
    return pl.pallas_call(
        kern,
        grid=(b, s // bq),
        in_specs=[pl.BlockSpec((None, bq, 1024), lambda bi, qi: (bi, qi, 0)),
                  pl.BlockSpec((None, bq, 1024), lambda bi, qi: (bi, qi, O_CG // 1024)),
                  pl.BlockSpec((None, lk, 1024), lambda bi, qi: (bi, 0, 0)),
                  pl.BlockSpec((None, lk, 1024), lambda bi, qi: (bi, 0, 0)),
                  _const_spec(bias.shape)],
        out_specs=pl.BlockSpec((None, bq, 1024), lambda bi, qi: (bi, qi, 0)),
        out_shape=jax.ShapeDtypeStruct((b, s, 1024), BF16),
        compiler_params=_params("parallel", "arbitrary"),
        name="band",
    )(qc, z, kcb, vcb, bias)


def _rope_tables(pos):
    half = ROPE_DIM // 2
    freq = ROPE_BASE ** (-jnp.arange(half, dtype=F32) / half)
    ang = pos.astype(F32)[:, None] * freq[None, :]
    cos, sin = jnp.cos(ang), jnp.sin(ang)
    z = jnp.zeros_like(cos)
    return jnp.concatenate([cos, z, cos, z], 1), jnp.concatenate([-sin, z, sin, z], 1)


def _rope_lanes(x):
    half = ROPE_DIM // 2
    z = jnp.zeros(x.shape[:-1] + (half,), x.dtype)
    return jnp.concatenate([x[..., :half], z, x[..., half:], z], -1)


def _rope_unlanes(x):
    half = ROPE_DIM // 2
    return jnp.concatenate([x[..., :half], x[..., 2 * half:3 * half]], -1)


def _t5_bucket_np(rel):
    nb = T5_BUCKETS // 2
    max_exact = nb // 2
    n = np.abs(rel)
    nf = np.maximum(n, 1).astype(np.float64)
    large = max_exact + (np.log(nf / max_exact) / math.log(T5_MAX_DIST / max_exact) * (nb - max_exact)).astype(np.int64)
    large = np.minimum(large, nb - 1)
    return np.where(rel > 0, nb, 0) + np.where(n < max_exact, n, large)


def _toeplitz(w, rows, width, cols):
    flat = jnp.tile(w, (1,) * (w.ndim - 1) + (rows,))[..., :rows * width]
    return flat.reshape(w.shape[:-1] + (rows, width))[..., :cols]


def _t5_tiles(t5_bias, bq, nd):
    width = 2 * LANE
    k = np.arange(width + 1)
    delta = np.where(k < LANE, k, k - (width + 1))
    rel = (np.arange(nd) - (nd - 1))[:, None] * LANE + delta[None, :]
    w = jnp.transpose(t5_bias[_t5_bucket_np(rel)], (2, 0, 1))
    return _toeplitz(w, bq, width, LANE)


def _band_bias(rel_tab, bq, qk_off):
    bw = _band_width(bq)
    width = 2 * bw
    k = np.arange(width + 1)
    delta = np.where(k < bw, k, k - (width + 1))
    idx = np.clip(qk_off - delta, -(CHUNK - 1), REL_CLIP) + (CHUNK - 1)
    return _toeplitz(jnp.transpose(rel_tab[idx], (1, 0)), bq, width, bw)


def _even_weights(w_in, b_wuq, b_wukv, b_qn, b_kn):
    d = w_in.shape[0]
    offs = np.cumsum((0,) + EVEN_SPLITS)
    w16 = w_in.astype(BF16)
    aq, ak, av, ag, iq, ik, iw, bcq, bckv, bkpe, bg = [w16[:, offs[t]:offs[t + 1]] for t in range(11)]
    slab_idx = jnp.concatenate([ik, iw, jnp.zeros((d, LANE - D_IDX - H_IDX), BF16)], 1)
    w = jnp.concatenate([aq, ag, iq, bg, bcq, ak, av, bckv, slab_idx, _rope_lanes(bkpe)], 1)
    uq = b_wuq.reshape(Q_LORA, H_B, QK_B)
    uq = jnp.concatenate([uq[..., :NOPE], _rope_lanes(uq[..., NOPE:])], -1).reshape(Q_LORA, H_B * 256).astype(BF16)
    ukv = b_wukv.reshape(KV_LORA, H_B, NOPE + V_DIM)
    ukv = jnp.concatenate([ukv[..., :NOPE].reshape(KV_LORA, H_B * NOPE),
                           ukv[..., NOPE:].reshape(KV_LORA, H_B * V_DIM)], 1).astype(BF16)
    pad_gain = lambda g: jnp.concatenate([g[:NOPE], _rope_lanes(g[NOPE:])])[None, :]
    return w, uq, ukv, pad_gain(b_qn), pad_gain(b_kn)


def _pad_rows(x, lp):
    return jnp.pad(x, ((0, 0), (0, lp - x.shape[1]), (0, 0)))


def _even_layer(h, p, past, q0, ln_g, w_in, uq, ukv, a_qn, a_kn, t5_bias, b_qln, b_kvln, bqn, bkn, wo, wg, wp):
    b, s, d = h.shape
    m = b * s
    z = _norm_mm(h.reshape(m, d), ln_g[None, :], w_in, min(m, 1024), E_END // 4).reshape(b, s, E_END)
    cos, sin = _rope_tables(q0 + jnp.arange(s, dtype=jnp.int32))
    bs = min(s, 256)
    qa, ka, qb, ckv, kpe_l = _even_prep(z, cos, sin, a_qn[None, :], a_kn[None, :], b_qln[None, :], b_kvln[None, :],
                                        uq, bqn, bs)
    av = z[..., E_AV:E_AV + 256]
    new = (ka.reshape(b, s, HKV_A, DH_A), av.reshape(b, s, HKV_A, DH_A), z[..., E_IDX:E_IDX + D_IDX], ckv,
           _rope_unlanes(kpe_l))
    if past is None:
        l_true = s
        ka_all, va_all, ckv_all, kpe_all = ka, av, ckv, kpe_l
        kidx_all, kidx_col = z, E_IDX // LANE
    else:
        kidx = z[..., E_IDX:E_IDX + LANE]
        kidx_col = 0
        c_k, c_v, c_ik, c_ckv, c_kpe = past
        pl_ = c_k.shape[1]
        l_true = pl_ + s
        lp = -(-l_true // LANE) * LANE
        cat = lambda c, n_: _pad_rows(jnp.concatenate([c, n_], 1), lp)
        ka_all = cat(c_k.reshape(b, pl_, 256), ka)
        va_all = cat(c_v.reshape(b, pl_, 256), av)
        kidx_all = cat(jnp.pad(c_ik, ((0, 0), (0, 0), (0, LANE - D_IDX))), kidx)
        ckv_all = cat(c_ckv, ckv)
        kpe_all = cat(_rope_lanes(c_kpe), kpe_l)
    lp = ka_all.shape[1]
    ksel = min(TOPK_MAX, l_true // 4)
    bq = min(s, LANE)
    nd = (q0 + s - bq) // LANE + 1
    tiles = _t5_tiles(t5_bias, bq, nd)
    classes = _causal_classes(s, lp, q0)
    o_a = jnp.zeros((b, s, H_A * DH_A), BF16)
    for r0, nr, lc in classes:
        o_a = _dsa(qa, z, ka_all, va_all, kidx_all, kidx_col, tiles, o_a, bq=bq, l_true=l_true, q0=q0, ksel=ksel,
                   row0=r0, rows=nr, lp=lc)
    kb, vb = _mla_kv(ckv_all, kpe_all, ukv, bkn, _key_chunk(lp))
    o_b = jnp.zeros((b, s, H_B * V_DIM), BF16)
    for r0, nr, lc in classes:
        o_b = _mla(qb, z, kb, vb, o_b, bq=min(s, 256), l_true=l_true, q0=q0, row0=r0, rows=nr, lp=lc)
    y = _out_ple(h.reshape(m, d), o_a.reshape(m, -1), o_b.reshape(m, -1), p.reshape(m, -1), wo, wg, wp, min(m, 256))
    return y.reshape(b, s, d), new


def _odd_layer(h, p, past, q0, ln_g, w_in, c_qn, c_kn, c_rel, d_g, d_b, d_ws, d_bs, wo, wg, wp):
    b, s, d = h.shape
    m = b * s
    z = _norm_mm(h.reshape(m, d), ln_g[None, :], w_in, min(m, 1024), 1024).reshape(b, s, O_END)
    n = min(s, D_CHUNK)
    keep = min(C_BACK * CHUNK, s) if past is None else s
    qc, kcb, vcb, o_d, kct, cvt, *dvn = _odd_prep(z, c_qn[None, :], c_kn[None, :], d_g[None, :], d_b[None, :],
                                                  d_ws[:, :n, :n], d_bs[:, :n].T, min(s, 256), n, keep,
                                                  want_dvn=past is not None)
    c_new = (kct.reshape(b, keep, H_C, DH_C), cvt.reshape(b, keep, H_C, DH_C))
    if past is None:
        bqc = 2 * LANE
        bias = _band_bias(c_rel, bqc, C_BACK * CHUNK)
        o_c = _band(qc, z, kcb, vcb, bias, bq=bqc, sliding=True, q0=0, k0=0, k_end=s)
    else:
        nc = past[0].shape[1]
        cat = lambda c, n_: _pad_rows(jnp.concatenate([c.reshape(b, nc, 1024).astype(BF16), n_], 1), _band_width(s))
        bias = _band_bias(c_rel, s, nc)
        o_c = _band(qc, z, cat(past[0], kcb), cat(past[1], vcb), bias,
                    bq=s, sliding=False, q0=q0, k0=q0 - nc, k_end=q0 + s)
    y = _out_ple(h.reshape(m, d), o_c.reshape(m, -1), o_d.reshape(m, -1), p.reshape(m, -1), wo, wg, wp, min(m, 256))
    return y.reshape(b, s, d), c_new, (dvn[0] if dvn else None)


def kernel(x_prompt, x_sample, cache_a_k, cache_a_v, cache_a_idx_k, cache_b_ckv, cache_b_kpe, cache_c_k, cache_c_v, p_prompt, p_sample, ln_g, w_in_even, a_q_norm, a_k_norm, t5_bias, b_q_lora_norm, b_kv_lora_norm, b_w_uq, b_w_ukv, b_q_norm, b_k_norm, w_out_even, w_in_odd, c_q_norm, c_k_norm, c_rel_bias, d_ln_g, d_ln_b, d_w_s, d_b_s, w_out_odd, ple_proj, ple_gate):
    depth = ln_g.shape[0]
    past_len = cache_a_k.shape[2]
    hp, hs = x_prompt, x_sample
    ev_p, ev_s, od_p, od_s, dv_s = [], [], [], [], []
    for i in range(depth):
        j = i // 2
        wg = ple_gate[i].astype(BF16)
        wp = ple_proj[i].astype(BF16)
        if i % 2 == 0:
            w_in, uq, ukv, bqn, bkn = _even_weights(w_in_even[j], b_w_uq[j], b_w_ukv[j], b_q_norm[j], b_k_norm[j])
            w = (ln_g[i], w_in, uq, ukv, a_q_norm[j], a_k_norm[j], t5_bias, b_q_lora_norm[j], b_kv_lora_norm[j],
                 bqn, bkn, w_out_even[j].astype(BF16), wg, wp)
            hp, sp = _even_layer(hp, p_prompt[i], None, 0, *w)
            past = (cache_a_k[j], cache_a_v[j], cache_a_idx_k[j], cache_b_ckv[j], cache_b_kpe[j])
            hs, ss = _even_layer(hs, p_sample[i], past, past_len, *w)
            ev_p.append(sp)
            ev_s.append(ss)
        else:
            w = (ln_g[i], w_in_odd[j].astype(BF16), c_q_norm[j], c_k_norm[j], c_rel_bias[j], d_ln_g[j], d_ln_b[j],
                 d_w_s[j], d_b_s[j], w_out_odd[j].astype(BF16), wg, wp)
            hp, sp, _ = _odd_layer(hp, p_prompt[i], None, 0, *w)
            hs, ss, dvs = _odd_layer(hs, p_sample[i], (cache_c_k[j], cache_c_v[j]), past_len, *w)
            od_p.append(sp)
            od_s.append(ss)
            dv_s.append(dvs)
    st = lambda lst, n_: jnp.stack([e[n_] for e in lst], 0)
    return (hp, hs, st(ev_p, 0), st(ev_p, 1), st(ev_p, 2), st(ev_p, 3), st(ev_p, 4), st(od_p, 0), st(od_p, 1),
            st(ev_s, 0), st(ev_s, 1), st(ev_s, 2), st(ev_s, 3), st(ev_s, 4), st(od_s, 0), st(od_s, 1),
            jnp.stack(dv_s, 0))
```

```python
import functools
import math

import numpy as np
import jax
import jax.numpy as jnp
from jax import lax
from jax.experimental import pallas as pl
from jax.experimental.pallas import tpu as pltpu

F32 = jnp.float32
BF16 = jnp.bfloat16
INT_MIN = -2 ** 31
LOG2E = math.log2(math.e)
CLASS_ROWS = 256
SEARCH_GROUPS = 4

D_MODEL = 2048
CHUNK = 64
CHUNK_SHIFT = 6
LANE = 128
H_A, HKV_A, GROUP_A, DH_A = 8, 2, 4, 128
H_IDX, D_IDX = 16, 64
TOPK_MAX = 256
T5_BUCKETS, T5_MAX_DIST = 32, 128
H_B, Q_LORA, KV_LORA, NOPE, ROPE_DIM, V_DIM = 8, 512, 256, 128, 64, 128
ROPE_BASE = 10000.0
QK_B = NOPE + ROPE_DIM
H_C, DH_C, C_BACK, REL_CLIP = 8, 128, 8, 128
W_D, G_D, DG_D, D_CHUNK = 1024, 8, 128, 128
PLE_DIM = 256

EVEN_SPLITS = (H_A * DH_A, HKV_A * DH_A, HKV_A * DH_A, H_A * DH_A, H_IDX * D_IDX, D_IDX, H_IDX,
               Q_LORA, KV_LORA, ROPE_DIM, H_B * V_DIM)
E_AQ, E_AG, E_IQ, E_BG, E_BCQ, E_AK, E_AV, E_CKV, E_IDX, E_KPE, E_END = (
    0, 1024, 2048, 3072, 4096, 4608, 4864, 5120, 5376, 5504, 5632)
O_CQ, O_CK, O_CV, O_CG, O_DU, O_DV, O_DG, O_END = 0, 1024, 2048, 3072, 4096, 5120, 6144, 7168

VMEM_LIMIT_BYTES = 56 * 1024 * 1024
NT_DIMS = (((1,), (1,)), ((), ()))


def _params(*sem):
    return pltpu.CompilerParams(dimension_semantics=sem, vmem_limit_bytes=VMEM_LIMIT_BYTES)


def _const_spec(shape):
    zeros = (0,) * len(shape)
    return pl.BlockSpec(shape, lambda *_: zeros, pipeline_mode=pl.Buffered(1))


def _rs(x, n=None, eps=1e-6):
    n = x.shape[-1] if n is None else n
    return lax.rsqrt(jnp.sum(x * x, axis=-1, keepdims=True) / n + eps)


def _silu(x):
    return x * (1.0 / (1.0 + jnp.exp(-x)))


def _softmax_pv(parts, scale=1.0):
    m = functools.reduce(jnp.maximum, [jnp.max(lg, axis=-1, keepdims=True) for lg, _ in parts])
    o = s = None
    for lg, v in parts:
        p = jnp.exp2((lg - m) * (scale * LOG2E))
        ps = jnp.sum(p, axis=-1, keepdims=True)
        po = jnp.dot(p.astype(BF16), v, preferred_element_type=F32)
        o, s = (po, ps) if o is None else (o + po, s + ps)
    return o / s


def _norm_mm_kernel(x_ref, g_ref, w_ref, o_ref, xn_ref):
    @pl.when(pl.program_id(1) == 0)
    def _():
        x = x_ref[...]
        xn_ref[...] = (x * _rs(x) * g_ref[...]).astype(BF16)

    o_ref[...] = jnp.dot(xn_ref[...], w_ref[...], preferred_element_type=F32)


def _norm_mm(x, g, w, bm, bn):
    m, d = x.shape
    n = w.shape[1]
    return pl.pallas_call(
        _norm_mm_kernel,
        grid=(m // bm, n // bn),
        in_specs=[pl.BlockSpec((bm, d), lambda i, j: (i, 0)),
                  pl.BlockSpec((1, d), lambda i, j: (0, 0)),
                  pl.BlockSpec((d, bn), lambda i, j: (0, j))],
        out_specs=pl.BlockSpec((bm, bn), lambda i, j: (i, j)),
        out_shape=jax.ShapeDtypeStruct((m, n), F32),
        scratch_shapes=[pltpu.VMEM((bm, d), BF16)],
        compiler_params=_params("parallel", "arbitrary"),
        name="norm_mm",
    )(x, g, w)


def _rope(x, cos, sin):
    return x * cos + pltpu.roll(x, 64, 1) * sin


def _even_prep_kernel(aq_ref, bcq_ref, ak_ref, ckv_ref, kpe_ref, idx_ref, cos_ref, sin_ref,
                      aqn_ref, akn_ref, qln_ref, kvln_ref, wuq_ref, bqn_ref,
                      qa_o, ka_o, qb_o, ckv_o, kpe_o, kpe64_o, idx64_o):
    for h in range(H_A):
        x = aq_ref[:, h * DH_A:(h + 1) * DH_A]
        qa_o[:, h * DH_A:(h + 1) * DH_A] = (x * _rs(x) * aqn_ref[...]).astype(BF16)
    for n in range(HKV_A):
        x = ak_ref[:, n * DH_A:(n + 1) * DH_A]
        ka_o[:, n * DH_A:(n + 1) * DH_A] = x * _rs(x) * akn_ref[...]
    c = ckv_ref[...]
    ckv_o[...] = c * _rs(c) * kvln_ref[...]
    cos = cos_ref[...]
    sin = sin_ref[...]
    kpe = _rope(kpe_ref[...], cos, sin)
    kpe_o[...] = kpe
    half = ROPE_DIM // 2
    kpe64_o[...] = jnp.concatenate([kpe[:, :half], kpe[:, 2 * half:3 * half]], axis=-1)
    idx64_o[...] = idx_ref[:, :D_IDX]
    cq = bcq_ref[...]
    cqn = (cq * _rs(cq) * qln_ref[...]).astype(BF16)
    qb = jnp.dot(cqn, wuq_ref[...], preferred_element_type=F32)
    g = bqn_ref[...]
    for h in range(H_B):
        nope = qb[:, h * 256:h * 256 + 128]
        rot = _rope(qb[:, h * 256 + 128:(h + 1) * 256], cos, sin)
        ss = jnp.sum(nope * nope, -1, keepdims=True) + jnp.sum(rot * rot, -1, keepdims=True)
        r = lax.rsqrt(ss / QK_B + 1e-6)
        qb_o[:, h * 256:h * 256 + 128] = (nope * r * g[:, :128]).astype(BF16)
        qb_o[:, h * 256 + 128:(h + 1) * 256] = (rot * r * g[:, 128:]).astype(BF16)


def _even_prep(z, cos, sin, aqn, akn, qln, kvln, wuq, bqn, bs):
    b, s, _ = z.shape

    def zspec(width, off):
        return pl.BlockSpec((None, bs, width), lambda bi, si: (bi, si, off // width))

    def ospec(width):
        return pl.BlockSpec((None, bs, width), lambda bi, si: (bi, si, 0))

    pos_spec = pl.BlockSpec((bs, LANE), lambda bi, si: (si, 0))
    return pl.pallas_call(
        _even_prep_kernel,
        grid=(b, s // bs),
        in_specs=[zspec(1024, E_AQ), zspec(512, E_BCQ), zspec(256, E_AK), zspec(256, E_CKV),
                  zspec(128, E_KPE), zspec(128, E_IDX), pos_spec, pos_spec,
                  _const_spec((1, DH_A)), _const_spec((1, DH_A)), _const_spec((1, Q_LORA)),
                  _const_spec((1, KV_LORA)), _const_spec((Q_LORA, H_B * 256)), _const_spec((1, 256))],
        out_specs=[ospec(1024), ospec(256), ospec(2048), ospec(256), ospec(128), ospec(ROPE_DIM), ospec(D_IDX)],
        out_shape=[jax.ShapeDtypeStruct((b, s, 1024), BF16), jax.ShapeDtypeStruct((b, s, 256), F32),
                   jax.ShapeDtypeStruct((b, s, 2048), BF16), jax.ShapeDtypeStruct((b, s, 256), F32),
                   jax.ShapeDtypeStruct((b, s, 128), F32), jax.ShapeDtypeStruct((b, s, ROPE_DIM), F32),
                   jax.ShapeDtypeStruct((b, s, D_IDX), F32)],
        compiler_params=_params("parallel", "arbitrary"),
        name="even_prep",
    )(z, z, z, z, z, z, cos, sin, aqn, akn, qln, kvln, wuq, bqn)


def _mla_kv_kernel(ckv_ref, kpe_ref, w_ref, g_ref, kb_o, vb_o):
    kv = jnp.dot(ckv_ref[...].astype(BF16), w_ref[...], preferred_element_type=F32)
    kp = kpe_ref[...]
    skp = jnp.sum(kp * kp, -1, keepdims=True)
    g = g_ref[...]
    for h in range(H_B):
        nope = kv[:, h * NOPE:(h + 1) * NOPE]
        r = lax.rsqrt((jnp.sum(nope * nope, -1, keepdims=True) + skp) / QK_B + 1e-6)
        kb_o[:, h * 256:h * 256 + 128] = (nope * r * g[:, :128]).astype(BF16)
        kb_o[:, h * 256 + 128:(h + 1) * 256] = (kp * r * g[:, 128:]).astype(BF16)
    vb_o[...] = kv[:, H_B * NOPE:].astype(BF16)


def _mla_kv(ckv_all, kpe_all, wukv, bkn, bl):
    b, lp, _ = ckv_all.shape
    return pl.pallas_call(
        _mla_kv_kernel,
        grid=(b, lp // bl),
        in_specs=[pl.BlockSpec((None, bl, KV_LORA), lambda bi, li: (bi, li, 0)),
                  pl.BlockSpec((None, bl, LANE), lambda bi, li: (bi, li, 0)),
                  _const_spec((KV_LORA, 2048)), _const_spec((1, 256))],
        out_specs=[pl.BlockSpec((None, bl, 2048), lambda bi, li: (bi, li, 0)),
                   pl.BlockSpec((None, bl, 1024), lambda bi, li: (bi, li, 0))],
        out_shape=[jax.ShapeDtypeStruct((b, lp, 2048), BF16), jax.ShapeDtypeStruct((b, lp, 1024), BF16)],
        compiler_params=_params("parallel", "arbitrary"),
        name="mla_kv",
    )(ckv_all, kpe_all, wukv, bkn)


def _key_chunk(lp):
    for c in (512, 384, 256, 128):
        if lp % c == 0:
            return c
    raise ValueError(lp)


def _dsa_kernel(qa_ref, iq_ref, sa_ref, ag_ref, ka_ref, va_ref, kidx_ref, bias_ref, acc_ref, o_ref,
                key_ref, hi_ref, lo_ref, *, bq, lp, l_true, q0, ksel, nd, far, blk0):
    i = pl.program_id(1) + blk0
    q_start = q0 + i * bq
    qpos = q_start + lax.broadcasted_iota(jnp.int32, (bq, 1), 0)
    ck = _key_chunk(lp)

    iq = iq_ref[...].astype(BF16)
    a = jnp.concatenate([iq[:, p * LANE:(p + 1) * LANE] for p in range(H_IDX // 2)], axis=0)
    wi = sa_ref[...] * (H_IDX ** -0.5 * D_IDX ** -0.5)
    for c0 in range(0, lp, ck):
        kk = kidx_ref[c0:c0 + ck, :]
        lane = lax.broadcasted_iota(jnp.int32, kk.shape, 1)
        k_lo = jnp.where(lane < D_IDX, kk, 0.0).astype(BF16)
        k_hi = jnp.where(lane >= D_IDX, pltpu.roll(kk, D_IDX, 1), 0.0).astype(BF16)
        s_lo = lax.dot_general(a, k_lo, NT_DIMS, preferred_element_type=F32)
        s_hi = lax.dot_general(a, k_hi, NT_DIMS, preferred_element_type=F32)
        sc = jnp.zeros((bq, ck), F32)
        for p in range(H_IDX // 2):
            w0 = wi[:, D_IDX + 2 * p:D_IDX + 2 * p + 1]
            w1 = wi[:, D_IDX + 2 * p + 1:D_IDX + 2 * p + 2]
            sc = sc + w0 * jnp.maximum(s_lo[p * bq:(p + 1) * bq], 0.0)
            sc = sc + w1 * jnp.maximum(s_hi[p * bq:(p + 1) * bq], 0.0)
        kpos = c0 + lax.broadcasted_iota(jnp.int32, (1, ck), 1)
        valid = ((kpos >> CHUNK_SHIFT) <= (qpos >> CHUNK_SHIFT)) & (kpos < l_true)
        bits = lax.bitcast_convert_type(sc, jnp.int32)
        key = jnp.where(bits < 0, bits ^ 0x7FFFFFFF, bits)
        key = jnp.where(bits == INT_MIN, 0, key)
        key = jnp.where(valid, key, INT_MIN)
        key_ref[:, c0:c0 + ck] = key
        hi_ref[:, c0:c0 + ck] = (key >> 16) + 2 ** 15
        lo_ref[:, c0:c0 + ck] = key & 0xFFFF

    ng = SEARCH_GROUPS
    gr = bq // ng
    groups = [slice(r * gr, (r + 1) * gr) for r in range(ng)]

    def count_ge(ref, rows, cand):
        acc = None
        for j in range(lp // LANE):
            d = (ref[rows, j * LANE:(j + 1) * LANE] - cand) >> 31
            acc = d if acc is None else acc + d
        return lp + jnp.sum(acc.astype(F32), axis=-1, keepdims=True)

    def search16(ref, need):
        ts = [jnp.zeros((gr, 1), jnp.int32)] * ng
        for bit in reversed(range(16)):
            for r in range(ng):
                cand = ts[r] | (1 << bit)
                ts[r] = jnp.where(count_ge(ref, groups[r], cand) >= need[r], cand, ts[r])
        return ts

    t_hi = search16(hi_ref, (float(ksel),) * ng)
    need_lo = []
    for rows, t in zip(groups, t_hi):
        need_lo.append(ksel - count_ge(hi_ref, rows, t + 1))
        lo_ref[rows, :] = jnp.where(hi_ref[rows, :] == t, lo_ref[rows, :], -1)
    t_lo = search16(lo_ref, need_lo)
    t_s = jnp.concatenate([lax.shift_left(h - 2 ** 15, 16) | l for h, l in zip(t_hi, t_lo)], axis=0)
    key = key_ref[...]
    sel = (key >= t_s) & (key > INT_MIN)

    d_base = (nd - 1) - (q0 // LANE + (i if bq == LANE else 0))
    nk = lp // LANE
    for n in range(HKV_A):
        q4 = jnp.concatenate([qa_ref[:, (n * GROUP_A + g) * DH_A:(n * GROUP_A + g + 1) * DH_A]
                              for g in range(GROUP_A)], axis=0)
        k_n = ka_ref[:, n * DH_A:(n + 1) * DH_A].astype(BF16)
        v_n = va_ref[:, n * DH_A:(n + 1) * DH_A].astype(BF16)
        lg4 = lax.dot_general(q4, k_n, NT_DIMS, preferred_element_type=F32) * DH_A ** -0.5
        ps, ss = [], []
        for g in range(GROUP_A):
            bias = jnp.concatenate([bias_ref[n * GROUP_A + g, jnp.maximum(jnp.minimum(d_base + j, nd - 1) - far, 0)]
                                    for j in range(nk)], axis=1)
            lg = jnp.where(sel, lg4[g * bq:(g + 1) * bq] + bias, -jnp.inf)
            p = jnp.exp(lg - jnp.max(lg, axis=-1, keepdims=True))
            ss.append(jnp.sum(p, axis=-1, keepdims=True))
            ps.append(p.astype(BF16))
        o4 = jnp.dot(jnp.concatenate(ps, axis=0), v_n, preferred_element_type=F32)
        for g in range(GROUP_A):
            h = n * GROUP_A + g
            gate = ag_ref[:, h * DH_A:(h + 1) * DH_A]
            o_ref[:, h * DH_A:(h + 1) * DH_A] = (o4[g * bq:(g + 1) * bq] / ss[g] * _silu(gate)).astype(BF16)


def _dsa(qa, z, ka_all, va_all, kidx_all, kidx_col, bias_tiles, acc, *, bq, l_true, q0, ksel, row0, rows, lp):
    b, s, _ = qa.shape
    bias_tiles, far = bias_tiles
    nd = bias_tiles.shape[1] + far
    blk0 = row0 // bq
    assert q0 % LANE == 0 and (bq == LANE or s == bq)
    assert nd == (q0 + s - bq) // LANE + 1
    kern = functools.partial(_dsa_kernel, bq=bq, lp=lp, l_true=l_true, q0=q0, ksel=ksel, nd=nd, far=far, blk0=blk0)

    def zspec(width, off):
        return pl.BlockSpec((None, bq, width), lambda bi, qi: (bi, qi + blk0, off // width))

    def kspec(width):
        return pl.BlockSpec((None, lp, width), lambda bi, qi: (bi, 0, 0))

    return pl.pallas_call(
        kern,
        grid=(b, rows // bq),
        in_specs=[pl.BlockSpec((None, bq, 1024), lambda bi, qi: (bi, qi + blk0, 0)),
                  zspec(1024, E_IQ), zspec(128, E_IDX), zspec(1024, E_AG),
                  kspec(256), kspec(256),
                  pl.BlockSpec((None, lp, LANE), lambda bi, qi: (bi, 0, kidx_col)),
                  _const_spec(bias_tiles.shape),
                  pl.BlockSpec(memory_space=pl.ANY)],
        out_specs=pl.BlockSpec((None, bq, 1024), lambda bi, qi: (bi, qi + blk0, 0)),
        out_shape=jax.ShapeDtypeStruct(acc.shape, acc.dtype),
        input_output_aliases={8: 0},
        scratch_shapes=[pltpu.VMEM((bq, lp), jnp.int32)] * 3,
        compiler_params=_params("parallel", "arbitrary"),
        name="dsa",
    )(qa, z, z, z, ka_all, va_all, kidx_all, bias_tiles, acc)


def _causal_classes(s, lp, q0):
    if q0 == 0 and s % CLASS_ROWS == 0 and lp == s:
        return [(c * CLASS_ROWS, CLASS_ROWS, (c + 1) * CLASS_ROWS) for c in range(s // CLASS_ROWS)]
    return [(0, s, lp)]


def _mla_kernel(qb_ref, bg_ref, kb_ref, vb_ref, acc_ref, o_ref, *, bq, lp, l_true, q0, blk0, n_full):
    q_start = q0 + (pl.program_id(1) + blk0) * bq
    qpos = q_start + lax.broadcasted_iota(jnp.int32, (bq, 1), 0)
    kpos = n_full + lax.broadcasted_iota(jnp.int32, (1, lp - n_full), 1)
    valid = ((kpos >> CHUNK_SHIFT) <= (qpos >> CHUNK_SHIFT)) & (kpos < l_true)
    for h in range(H_B):
        q = qb_ref[:, h * 256:(h + 1) * 256]
        parts = []
        if n_full:
            parts.append((lax.dot_general(q, kb_ref[:n_full, h * 256:(h + 1) * 256], NT_DIMS,
                                          preferred_element_type=F32), vb_ref[:n_full, h * V_DIM:(h + 1) * V_DIM]))
        lg = lax.dot_general(q, kb_ref[n_full:, h * 256:(h + 1) * 256], NT_DIMS, preferred_element_type=F32)
        parts.append((jnp.where(valid, lg, -jnp.inf), vb_ref[n_full:, h * V_DIM:(h + 1) * V_DIM]))
        o = _softmax_pv(parts, QK_B ** -0.5)
        gate = bg_ref[:, h * V_DIM:(h + 1) * V_DIM]
        o_ref[:, h * V_DIM:(h + 1) * V_DIM] = (o * _silu(gate)).astype(BF16)


def _mla(qb, z, kb, vb, acc, *, bq, l_true, q0, row0, rows, lp):
    b = qb.shape[0]
    blk0 = row0 // bq
    n_full = min(min(q0 + row0 + CHUNK, l_true) // LANE * LANE, lp - LANE)
    kern = functools.partial(_mla_kernel, bq=bq, lp=lp, l_true=l_true, q0=q0, blk0=blk0, n_full=n_full)
    return pl.pallas_call(
        kern,
        grid=(b, rows // bq),
        in_specs=[pl.BlockSpec((None, bq, 2048), lambda bi, qi: (bi, qi + blk0, 0)),
                  pl.BlockSpec((None, bq, 1024), lambda bi, qi: (bi, qi + blk0, E_BG // 1024)),
                  pl.BlockSpec((None, lp, 2048), lambda bi, qi: (bi, 0, 0)),
                  pl.BlockSpec((None, lp, 1024), lambda bi, qi: (bi, 0, 0)),
                  pl.BlockSpec(memory_space=pl.ANY)],
        out_specs=pl.BlockSpec((None, bq, 1024), lambda bi, qi: (bi, qi + blk0, 0)),
        out_shape=jax.ShapeDtypeStruct(acc.shape, acc.dtype),
        input_output_aliases={4: 0},
        compiler_params=_params("parallel", "arbitrary"),
        name="mla",
    )(qb, z, kb, vb, acc)


def _out_ple_kernel(h_ref, oa_ref, ob_ref, p_ref, wo_ref, wg_ref, wp_ref, o_ref):
    half = oa_ref.shape[-1]
    h1 = (h_ref[...]
          + jnp.dot(oa_ref[...], wo_ref[:half, :], preferred_element_type=F32)
          + jnp.dot(ob_ref[...], wo_ref[half:, :], preferred_element_type=F32))
    r = (h1 * _rs(h1)).astype(BF16)
    gate = 1.0 / (1.0 + jnp.exp(-jnp.dot(r, wg_ref[...], preferred_element_type=F32)))
    o_ref[...] = h1 + gate * jnp.dot(p_ref[...].astype(BF16), wp_ref[...], preferred_element_type=F32)


def _out_ple(h, oa, ob, p_all, layer, wo, wg, wp, bm):
    m, d = h.shape
    half = oa.shape[1]

    def rows(width):
        return pl.BlockSpec((bm, width), lambda i: (i, 0))

    return pl.pallas_call(
        _out_ple_kernel,
        grid=(m // bm,),
        in_specs=[rows(d), rows(half), rows(half), pl.BlockSpec((None, bm, PLE_DIM), lambda i: (layer, i, 0)),
                  _const_spec((2 * half, d)), _const_spec((d, d)), _const_spec((PLE_DIM, d))],
        out_specs=rows(d),
        out_shape=jax.ShapeDtypeStruct((m, d), F32),
        compiler_params=_params("parallel"),
        name="out_ple",
    )(h, oa, ob, p_all, wo, wg, wp)


def _odd_prep_kernel(cq_ref, ck_ref, cv_ref, dv_ref, du_ref, dgate_ref, cqn_ref, ckn_ref, dg_ref, db_ref,
                     ws_ref, bs_ref, qc_o, kcb_o, vcb_o, od_o, kct_o, cvt_o, *maybe_dvn_o, n):
    for h in range(H_C):
        sl = slice(h * DH_C, (h + 1) * DH_C)
        x = cq_ref[:, sl]
        qc_o[:, sl] = (x * _rs(x) * cqn_ref[...]).astype(BF16)
        x = ck_ref[:, sl]
        kn = x * _rs(x) * ckn_ref[...]
        kct_o[:, sl] = kn
        kcb_o[:, sl] = kn.astype(BF16)
    cv = cv_ref[...]
    cvt_o[...] = cv
    vcb_o[...] = cv.astype(BF16)
    dv = dv_ref[...]
    xc = dv - jnp.mean(dv, -1, keepdims=True)
    var = jnp.mean(xc * xc, -1, keepdims=True)
    dvn = xc * lax.rsqrt(var + 1e-5) * dg_ref[...] + db_ref[...]
    for dvn_o in maybe_dvn_o:
        dvn_o[...] = dvn
    dvn = dvn.astype(BF16)
    row = lax.broadcasted_iota(jnp.int32, (n, n), 0)
    col = lax.broadcasted_iota(jnp.int32, (n, n), 1)
    for g in range(G_D):
        sl = slice(g * DG_D, (g + 1) * DG_D)
        w = jnp.where(col <= row, ws_ref[g], 0.0).astype(BF16)
        for c in range(dv.shape[0] // n):
            rs = slice(c * n, (c + 1) * n)
            sg = jnp.dot(w, dvn[rs, sl], preferred_element_type=F32) + bs_ref[:, g:g + 1]
            od_o[rs, sl] = (du_ref[rs, sl] * sg * _silu(dgate_ref[rs, sl])).astype(BF16)


def _odd_prep(z, cqn, ckn, dg, db, ws, bs_t, bs, n, keep, want_dvn):
    b, s, _ = z.shape
    first_tail = (s - keep) // bs
    assert keep % bs == 0

    def zspec(off):
        return pl.BlockSpec((None, bs, 1024), lambda bi, si: (bi, si, off // 1024))

    ospec = pl.BlockSpec((None, bs, 1024), lambda bi, si: (bi, si, 0))
    tspec = pl.BlockSpec((None, bs, 1024), lambda bi, si: (bi, jnp.maximum(si - first_tail, 0), 0))
    act = lambda dt: jax.ShapeDtypeStruct((b, s, 1024), dt)
    tail = jax.ShapeDtypeStruct((b, keep, 1024), F32)
    return pl.pallas_call(
        functools.partial(_odd_prep_kernel, n=n),
        grid=(b, s // bs),
        in_specs=[zspec(O_CQ), zspec(O_CK), zspec(O_CV), zspec(O_DV), zspec(O_DU), zspec(O_DG),
                  _const_spec((1, DH_C)), _const_spec((1, DH_C)), _const_spec((1, W_D)), _const_spec((1, W_D)),
                  _const_spec((G_D, n, n)), _const_spec((n, G_D))],
        out_specs=[ospec] * 4 + [tspec] * 2 + [ospec] * want_dvn,
        out_shape=[act(BF16)] * 4 + [tail] * 2 + [act(F32)] * want_dvn,
        compiler_params=_params("parallel", "arbitrary"),
        name="odd_prep",
    )(z, z, z, z, z, z, cqn, ckn, dg, db, ws, bs_t)


BACK_TILES = C_BACK * CHUNK // LANE


def _band_width(bq):
    return (BACK_TILES + -(-bq // LANE)) * LANE


def _band_kernel(q_ref, cg_ref, k_ref, v_ref, cbias_ref, bias_ref, o_ref, *, bq, sliding, q0, k0, k_end, c0):
    i = pl.program_id(1)
    q_start = q0 + i * bq
    bw = _band_width(bq)
    if sliding:
        first = i * (bq // LANE) - BACK_TILES
        win_start = first * LANE
        rows = [pl.ds(pl.multiple_of(jnp.maximum(first + t, 0) * LANE, LANE), LANE) for t in range(bw // LANE)]
    else:
        win_start = k0
        rows = [pl.ds(t * LANE, LANE) for t in range(bw // LANE)]
    kw = jnp.concatenate([k_ref[r, :] for r in rows], axis=0)
    vw = jnp.concatenate([v_ref[r, :] for r in rows], axis=0)
    qpos = q_start + lax.broadcasted_iota(jnp.int32, (bq, 1), 0)

    def valid(lo, hi):
        kpos = win_start + lo + lax.broadcasted_iota(jnp.int32, (1, hi - lo), 1)
        dc = (qpos >> CHUNK_SHIFT) - (kpos >> CHUNK_SHIFT)
        return (dc >= 0) & (dc <= C_BACK) & (kpos >= 0) & (kpos < k_end)

    segs = [(lo, hi, valid(lo, hi)) for lo, hi in ((0, c0), (c0, bw)) if hi > lo]
    for h in range(H_C):
        sl = slice(h * DH_C, (h + 1) * DH_C)
        q = q_ref[:, sl]
        parts = []
        for lo, hi, ok in segs:
            bias = cbias_ref[h][:, :1] if hi <= c0 else bias_ref[h]
            lg = lax.dot_general(q, kw[lo:hi, sl], NT_DIMS, preferred_element_type=F32) * DH_C ** -0.5
            parts.append((jnp.where(ok, lg + bias, -jnp.inf), vw[lo:hi, sl]))
        o = _softmax_pv(parts)
        o_ref[:, sl] = (o * _silu(cg_ref[:, sl])).astype(BF16)


def _band(qc, z, kcb, vcb, bias, *, bq, sliding, q0, k0, k_end):
    c0, cbias, bias = bias
    b, s, _ = qc.shape
    lk = kcb.shape[1]
    if sliding:
        assert bq % LANE == 0 and q0 == 0 and k0 == 0
    else:
        assert lk == _band_width(bq) and s == bq
    kern = functools.partial(_band_kernel, bq=bq, sliding=sliding, q0=q0, k0=k0, k_end=k_end, c0=c0)
    return pl.pallas_call(
        kern,
        grid=(b, s // bq),
        in_specs=[pl.BlockSpec((None, bq, 1024), lambda bi, qi: (bi, qi, 0)),
                  pl.BlockSpec((None, bq, 1024), lambda bi, qi: (bi, qi, O_CG // 1024)),
                  pl.BlockSpec((None, lk, 1024), lambda bi, qi: (bi, 0, 0)),
                  pl.BlockSpec((None, lk, 1024), lambda bi, qi: (bi, 0, 0)),
                  _const_spec(cbias.shape), _const_spec(bias.shape)],
        out_specs=pl.BlockSpec((None, bq, 1024), lambda bi, qi: (bi, qi, 0)),
        out_shape=jax.ShapeDtypeStruct((b, s, 1024), BF16),
        compiler_params=_params("parallel", "arbitrary"),
        name="band",
    )(qc, z, kcb, vcb, cbias, bias)


def _rope_tables(pos):
    half = ROPE_DIM // 2
    freq = ROPE_BASE ** (-jnp.arange(half, dtype=F32) / half)
    ang = pos.astype(F32)[:, None] * freq[None, :]
    cos, sin = jnp.cos(ang), jnp.sin(ang)
    z = jnp.zeros_like(cos)
    return jnp.concatenate([cos, z, cos, z], 1), jnp.concatenate([-sin, z, sin, z], 1)


def _rope_lanes(x):
    half = ROPE_DIM // 2
    z = jnp.zeros(x.shape[:-1] + (half,), x.dtype)
    return jnp.concatenate([x[..., :half], z, x[..., half:], z], -1)


def _t5_bucket_np(rel):
    nb = T5_BUCKETS // 2
    max_exact = nb // 2
    n = np.abs(rel)
    nf = np.maximum(n, 1).astype(np.float64)
    large = max_exact + (np.log(nf / max_exact) / math.log(T5_MAX_DIST / max_exact) * (nb - max_exact)).astype(np.int64)
    large = np.minimum(large, nb - 1)
    return np.where(rel > 0, nb, 0) + np.where(n < max_exact, n, large)


def _toeplitz(w, rows, width, cols):
    flat = jnp.tile(w, (1,) * (w.ndim - 1) + (rows,))[..., :rows * width]
    return flat.reshape(w.shape[:-1] + (rows, width))[..., :cols]


def _t5_tiles(t5_bias, bq, nd):
    width = 2 * LANE
    k = np.arange(width + 1)
    delta = np.where(k < LANE, k, k - (width + 1))
    rel = (np.arange(nd) - (nd - 1))[:, None] * LANE + delta[None, :]
    bucket = _t5_bucket_np(rel)
    far = 0
    while far + 1 < nd and np.array_equal(bucket[far + 1], bucket[0]):
        far += 1
    w = jnp.transpose(t5_bias[bucket[far:]], (2, 0, 1))
    return _toeplitz(w, bq, width, LANE), far


def _band_bias(rel_tab, bq, qk_off):
    bw = _band_width(bq)
    rel_index = lambda rel: np.clip(rel, -(CHUNK - 1), REL_CLIP) + (CHUNK - 1)
    full = rel_index(qk_off + np.arange(bq)[:, None] - np.arange(bw)[None, :])
    c0 = 0
    while c0 + 2 * LANE <= bw and np.all(full[:, :c0 + LANE] == full[0, 0]):
        c0 += LANE
    const = jnp.broadcast_to(rel_tab[full[0, 0]][:, None, None], (H_C, 1, LANE))
    wv = bw - c0
    width = wv + bq
    k = np.arange(width + 1)
    delta = np.where(k < wv, k, k - (width + 1))
    idx = rel_index(qk_off - c0 - delta)
    return c0, const, _toeplitz(jnp.transpose(rel_tab[idx], (1, 0)), bq, width, wv)


def _even_weights(w_in, b_wuq, b_wukv, b_qn, b_kn):
    d = w_in.shape[0]
    offs = np.cumsum((0,) + EVEN_SPLITS)
    w16 = w_in.astype(BF16)
    aq, ak, av, ag, iq, ik, iw, bcq, bckv, bkpe, bg = [w16[:, offs[t]:offs[t + 1]] for t in range(11)]
    slab_idx = jnp.concatenate([ik, iw, jnp.zeros((d, LANE - D_IDX - H_IDX), BF16)], 1)
    w = jnp.concatenate([aq, ag, iq, bg, bcq, ak, av, bckv, slab_idx, _rope_lanes(bkpe)], 1)
    uq = b_wuq.reshape(Q_LORA, H_B, QK_B)
    uq = jnp.concatenate([uq[..., :NOPE], _rope_lanes(uq[..., NOPE:])], -1).reshape(Q_LORA, H_B * 256).astype(BF16)
    ukv = b_wukv.reshape(KV_LORA, H_B, NOPE + V_DIM)
    ukv = jnp.concatenate([ukv[..., :NOPE].reshape(KV_LORA, H_B * NOPE),
                           ukv[..., NOPE:].reshape(KV_LORA, H_B * V_DIM)], 1).astype(BF16)
    pad_gain = lambda g: jnp.concatenate([g[:NOPE], _rope_lanes(g[NOPE:])])[None, :]
    return w, uq, ukv, pad_gain(b_qn), pad_gain(b_kn)


def _pad_rows(x, lp):
    return jnp.pad(x, ((0, 0), (0, lp - x.shape[1]), (0, 0)))


def _even_layer(h, p_all, layer, past, q0, ln_g, w_in, uq, ukv, a_qn, a_kn, t5_bias, b_qln, b_kvln, bqn, bkn, wo, wg, wp):
    b, s, d = h.shape
    m = b * s
    z = _norm_mm(h.reshape(m, d), ln_g[None, :], w_in, min(m, 1024), E_END // 4).reshape(b, s, E_END)
    cos, sin = _rope_tables(q0 + jnp.arange(s, dtype=jnp.int32))
    bs = min(s, 256)
    qa, ka, qb, ckv, kpe_l, kpe64, idx64 = _even_prep(z, cos, sin, a_qn[None, :], a_kn[None, :], b_qln[None, :],
                                                      b_kvln[None, :], uq, bqn, bs)
    av = z[..., E_AV:E_AV + 256]
    new = (ka.reshape(b, s, HKV_A, DH_A), av.reshape(b, s, HKV_A, DH_A), idx64, ckv, kpe64)
    if past is None:
        l_true = s
        ka_all, va_all, ckv_all, kpe_all = ka, av, ckv, kpe_l
        kidx_all, kidx_col = z, E_IDX // LANE
    else:
        kidx = z[..., E_IDX:E_IDX + LANE]
        kidx_col = 0
        c_k, c_v, c_ik, c_ckv, c_kpe = past
        pl_ = c_k.shape[1]
        l_true = pl_ + s
        lp = -(-l_true // LANE) * LANE
        cat = lambda c, n_: _pad_rows(jnp.concatenate([c, n_], 1), lp)
        ka_all = cat(c_k.reshape(b, pl_, 256), ka)
        va_all = cat(c_v.reshape(b, pl_, 256), av)
        kidx_all = cat(jnp.pad(c_ik, ((0, 0), (0, 0), (0, LANE - D_IDX))), kidx)
        ckv_all = cat(c_ckv, ckv)
        kpe_all = cat(_rope_lanes(c_kpe), kpe_l)
    lp = ka_all.shape[1]
    ksel = min(TOPK_MAX, l_true // 4)
    bq = min(s, LANE)
    nd = (q0 + s - bq) // LANE + 1
    tiles = _t5_tiles(t5_bias, bq, nd)
    classes = _causal_classes(s, lp, q0)
    o_a = jnp.zeros((b, s, H_A * DH_A), BF16)
    for r0, nr, lc in classes:
        o_a = _dsa(qa, z, ka_all, va_all, kidx_all, kidx_col, tiles, o_a, bq=bq, l_true=l_true, q0=q0, ksel=ksel,
                   row0=r0, rows=nr, lp=lc)
    kb, vb = _mla_kv(ckv_all, kpe_all, ukv, bkn, _key_chunk(lp))
    o_b = jnp.zeros((b, s, H_B * V_DIM), BF16)
    for r0, nr, lc in classes:
        o_b = _mla(qb, z, kb, vb, o_b, bq=min(s, 256), l_true=l_true, q0=q0, row0=r0, rows=nr, lp=lc)
    y = _out_ple(h.reshape(m, d), o_a.reshape(m, -1), o_b.reshape(m, -1), p_all, layer, wo, wg, wp, min(m, 256))
    return y.reshape(b, s, d), new


def _odd_layer(h, p_all, layer, past, q0, ln_g, w_in, c_qn, c_kn, c_rel, d_g, d_b, d_ws, d_bs, wo, wg, wp):
    b, s, d = h.shape
    m = b * s
    z = _norm_mm(h.reshape(m, d), ln_g[None, :], w_in, min(m, 1024), 1024).reshape(b, s, O_END)
    n = min(s, D_CHUNK)
    keep = min(C_BACK * CHUNK, s) if past is None else s
    qc, kcb, vcb, o_d, kct, cvt, *dvn = _odd_prep(z, c_qn[None, :], c_kn[None, :], d_g[None, :], d_b[None, :],
                                                  d_ws[:, :n, :n], d_bs[:, :n].T, min(s, 256), n, keep,
                                                  want_dvn=past is not None)
    c_new = (kct.reshape(b, keep, H_C, DH_C), cvt.reshape(b, keep, H_C, DH_C))
    if past is None:
        bqc = 2 * LANE
        bias = _band_bias(c_rel, bqc, C_BACK * CHUNK)
        o_c = _band(qc, z, kcb, vcb, bias, bq=bqc, sliding=True, q0=0, k0=0, k_end=s)
    else:
        nc = past[0].shape[1]
        cat = lambda c, n_: _pad_rows(jnp.concatenate([c.reshape(b, nc, 1024).astype(BF16), n_], 1), _band_width(s))
        bias = _band_bias(c_rel, s, nc)
        o_c = _band(qc, z, cat(past[0], kcb), cat(past[1], vcb), bias,
                    bq=s, sliding=False, q0=q0, k0=q0 - nc, k_end=q0 + s)
    y = _out_ple(h.reshape(m, d), o_c.reshape(m, -1), o_d.reshape(m, -1), p_all, layer, wo, wg, wp, min(m, 256))
    return y.reshape(b, s, d), c_new, (dvn[0] if dvn else None)


def kernel(x_prompt, x_sample, cache_a_k, cache_a_v, cache_a_idx_k, cache_b_ckv, cache_b_kpe, cache_c_k, cache_c_v, p_prompt, p_sample, ln_g, w_in_even, a_q_norm, a_k_norm, t5_bias, b_q_lora_norm, b_kv_lora_norm, b_w_uq, b_w_ukv, b_q_norm, b_k_norm, w_out_even, w_in_odd, c_q_norm, c_k_norm, c_rel_bias, d_ln_g, d_ln_b, d_w_s, d_b_s, w_out_odd, ple_proj, ple_gate):
    depth = ln_g.shape[0]
    past_len = cache_a_k.shape[2]
    hp, hs = x_prompt, x_sample
    pp = p_prompt.reshape(depth, -1, PLE_DIM)
    ps = p_sample.reshape(depth, -1, PLE_DIM)
    ev_p, ev_s, od_p, od_s, dv_s = [], [], [], [], []
    for i in range(depth):
        j = i // 2
        wg = ple_gate[i].astype(BF16)
        wp = ple_proj[i].astype(BF16)
        if i % 2 == 0:
            w_in, uq, ukv, bqn, bkn = _even_weights(w_in_even[j], b_w_uq[j], b_w_ukv[j], b_q_norm[j], b_k_norm[j])
            w = (ln_g[i], w_in, uq, ukv, a_q_norm[j], a_k_norm[j], t5_bias, b_q_lora_norm[j], b_kv_lora_norm[j],
                 bqn, bkn, w_out_even[j].astype(BF16), wg, wp)
            hp, sp = _even_layer(hp, pp, i, None, 0, *w)
            past = (cache_a_k[j], cache_a_v[j], cache_a_idx_k[j], cache_b_ckv[j], cache_b_kpe[j])
            hs, ss = _even_layer(hs, ps, i, past, past_len, *w)
            ev_p.append(sp)
            ev_s.append(ss)
        else:
            w = (ln_g[i], w_in_odd[j].astype(BF16), c_q_norm[j], c_k_norm[j], c_rel_bias[j], d_ln_g[j], d_ln_b[j],
                 d_w_s[j], d_b_s[j], w_out_odd[j].astype(BF16), wg, wp)
            hp, sp, _ = _odd_layer(hp, pp, i, None, 0, *w)
            hs, ss, dvs = _odd_layer(hs, ps, i, (cache_c_k[j], cache_c_v[j]), past_len, *w)
            od_p.append(sp)
            od_s.append(ss)
            dv_s.append(dvs)
    st = lambda lst, n_: jnp.stack([e[n_] for e in lst], 0)
    return (hp, hs, st(ev_p, 0), st(ev_p, 1), st(ev_p, 2), st(ev_p, 3), st(ev_p, 4), st(od_p, 0), st(od_p, 1),
            st(ev_s, 0), st(ev_s, 1), st(ev_s, 2), st(ev_s, 3), st(ev_s, 4), st(od_s, 0), st(od_s, 1),
            jnp.stack(dv_s, 0))
```

```python
import functools
import math

import numpy as np
import jax
import jax.numpy as jnp
from jax import lax
from jax.experimental import pallas as pl
from jax.experimental.pallas import tpu as pltpu

F32 = jnp.float32
BF16 = jnp.bfloat16
INT_MIN = -2 ** 31
LOG2E = math.log2(math.e)
CLASS_ROWS = 256
SEARCH_GROUPS = 4

D_MODEL = 2048
CHUNK = 64
CHUNK_SHIFT = 6
LANE = 128
H_A, HKV_A, GROUP_A, DH_A = 8, 2, 4, 128
H_IDX, D_IDX = 16, 64
TOPK_MAX = 256
T5_BUCKETS, T5_MAX_DIST = 32, 128
H_B, Q_LORA, KV_LORA, NOPE, ROPE_DIM, V_DIM = 8, 512, 256, 128, 64, 128
ROPE_BASE = 10000.0
QK_B = NOPE + ROPE_DIM
H_C, DH_C, C_BACK, REL_CLIP = 8, 128, 8, 128
W_D, G_D, DG_D, D_CHUNK = 1024, 8, 128, 128
PLE_DIM = 256

EVEN_SPLITS = (H_A * DH_A, HKV_A * DH_A, HKV_A * DH_A, H_A * DH_A, H_IDX * D_IDX, D_IDX, H_IDX,
               Q_LORA, KV_LORA, ROPE_DIM, H_B * V_DIM)
E_AQ, E_AG, E_IQ, E_BG, E_BCQ, E_AK, E_AV, E_CKV, E_IDX, E_KPE, E_END = (
    0, 1024, 2048, 3072, 4096, 4608, 4864, 5120, 5376, 5504, 5632)
O_CQ, O_CK, O_CV, O_CG, O_DU, O_DV, O_DG, O_END = 0, 1024, 2048, 3072, 4096, 5120, 6144, 7168

VMEM_LIMIT_BYTES = 56 * 1024 * 1024
NT_DIMS = (((1,), (1,)), ((), ()))


def _params(*sem):
    return pltpu.CompilerParams(dimension_semantics=sem, vmem_limit_bytes=VMEM_LIMIT_BYTES)


def _const_spec(shape):
    zeros = (0,) * len(shape)
    return pl.BlockSpec(shape, lambda *_: zeros, pipeline_mode=pl.Buffered(1))


def _rs(x, n=None, eps=1e-6):
    n = x.shape[-1] if n is None else n
    return lax.rsqrt(jnp.sum(x * x, axis=-1, keepdims=True) / n + eps)


def _silu(x):
    return x * (1.0 / (1.0 + jnp.exp(-x)))


def _softmax_pv(parts, scale=1.0):
    m = functools.reduce(jnp.maximum, [jnp.max(lg, axis=-1, keepdims=True) for lg, _ in parts])
    o = s = None
    for lg, v in parts:
        p = jnp.exp2((lg - m) * (scale * LOG2E))
        ps = jnp.sum(p, axis=-1, keepdims=True)
        po = jnp.dot(p.astype(BF16), v, preferred_element_type=F32)
        o, s = (po, ps) if o is None else (o + po, s + ps)
    return o / s


def _norm_mm_kernel(x_ref, g_ref, w_ref, o_ref, xn_ref):
    @pl.when(pl.program_id(1) == 0)
    def _():
        x = x_ref[...]
        xn_ref[...] = (x * _rs(x) * g_ref[...]).astype(BF16)

    o_ref[...] = jnp.dot(xn_ref[...], w_ref[...], preferred_element_type=F32)


def _norm_mm(x, g, w, bm, bn):
    m, d = x.shape
    n = w.shape[1]
    return pl.pallas_call(
        _norm_mm_kernel,
        grid=(m // bm, n // bn),
        in_specs=[pl.BlockSpec((bm, d), lambda i, j: (i, 0)),
                  pl.BlockSpec((1, d), lambda i, j: (0, 0)),
                  pl.BlockSpec((d, bn), lambda i, j: (0, j))],
        out_specs=pl.BlockSpec((bm, bn), lambda i, j: (i, j)),
        out_shape=jax.ShapeDtypeStruct((m, n), F32),
        scratch_shapes=[pltpu.VMEM((bm, d), BF16)],
        compiler_params=_params("parallel", "arbitrary"),
        name="norm_mm",
    )(x, g, w)


def _rope(x, cos, sin):
    return x * cos + pltpu.roll(x, 64, 1) * sin


def _even_prep_kernel(aq_ref, bcq_ref, ak_ref, ckv_ref, kpe_ref, idx_ref, cos_ref, sin_ref,
                      aqn_ref, akn_ref, qln_ref, kvln_ref, wuq_ref, bqn_ref,
                      qa_o, ka_o, qb_o, ckv_o, kpe_o, kpe64_o, idx64_o):
    for h in range(H_A):
        x = aq_ref[:, h * DH_A:(h + 1) * DH_A]
        qa_o[:, h * DH_A:(h + 1) * DH_A] = (x * _rs(x) * aqn_ref[...]).astype(BF16)
    for n in range(HKV_A):
        x = ak_ref[:, n * DH_A:(n + 1) * DH_A]
        ka_o[:, n * DH_A:(n + 1) * DH_A] = x * _rs(x) * akn_ref[...]
    c = ckv_ref[...]
    ckv_o[...] = c * _rs(c) * kvln_ref[...]
    cos = cos_ref[...]
    sin = sin_ref[...]
    kpe = _rope(kpe_ref[...], cos, sin)
    kpe_o[...] = kpe
    half = ROPE_DIM // 2
    kpe64_o[...] = jnp.concatenate([kpe[:, :half], kpe[:, 2 * half:3 * half]], axis=-1)
    idx64_o[...] = idx_ref[:, :D_IDX]
    cq = bcq_ref[...]
    cqn = (cq * _rs(cq) * qln_ref[...]).astype(BF16)
    qb = jnp.dot(cqn, wuq_ref[...], preferred_element_type=F32)
    g = bqn_ref[...]
    for h in range(H_B):
        nope = qb[:, h * 256:h * 256 + 128]
        rot = _rope(qb[:, h * 256 + 128:(h + 1) * 256], cos, sin)
        ss = jnp.sum(nope * nope, -1, keepdims=True) + jnp.sum(rot * rot, -1, keepdims=True)
        r = lax.rsqrt(ss / QK_B + 1e-6)
        qb_o[:, h * 256:h * 256 + 128] = (nope * r * g[:, :128]).astype(BF16)
        qb_o[:, h * 256 + 128:(h + 1) * 256] = (rot * r * g[:, 128:]).astype(BF16)


def _even_prep(z, cos, sin, aqn, akn, qln, kvln, wuq, bqn, bs):
    b, s, _ = z.shape

    def zspec(width, off):
        return pl.BlockSpec((None, bs, width), lambda bi, si: (bi, si, off // width))

    def ospec(width):
        return pl.BlockSpec((None, bs, width), lambda bi, si: (bi, si, 0))

    pos_spec = pl.BlockSpec((bs, LANE), lambda bi, si: (si, 0))
    return pl.pallas_call(
        _even_prep_kernel,
        grid=(b, s // bs),
        in_specs=[zspec(1024, E_AQ), zspec(512, E_BCQ), zspec(256, E_AK), zspec(256, E_CKV),
                  zspec(128, E_KPE), zspec(128, E_IDX), pos_spec, pos_spec,
                  _const_spec((1, DH_A)), _const_spec((1, DH_A)), _const_spec((1, Q_LORA)),
                  _const_spec((1, KV_LORA)), _const_spec((Q_LORA, H_B * 256)), _const_spec((1, 256))],
        out_specs=[ospec(1024), ospec(256), ospec(2048), ospec(256), ospec(128), ospec(ROPE_DIM), ospec(D_IDX)],
        out_shape=[jax.ShapeDtypeStruct((b, s, 1024), BF16), jax.ShapeDtypeStruct((b, s, 256), F32),
                   jax.ShapeDtypeStruct((b, s, 2048), BF16), jax.ShapeDtypeStruct((b, s, 256), F32),
                   jax.ShapeDtypeStruct((b, s, 128), F32), jax.ShapeDtypeStruct((b, s, ROPE_DIM), F32),
                   jax.ShapeDtypeStruct((b, s, D_IDX), F32)],
        compiler_params=_params("parallel", "arbitrary"),
        name="even_prep",
    )(z, z, z, z, z, z, cos, sin, aqn, akn, qln, kvln, wuq, bqn)


def _mla_kv_kernel(ckv_ref, kpe_ref, w_ref, g_ref, kb_o, vb_o):
    kv = jnp.dot(ckv_ref[...].astype(BF16), w_ref[...], preferred_element_type=F32)
    kp = kpe_ref[...]
    skp = jnp.sum(kp * kp, -1, keepdims=True)
    g = g_ref[...]
    for h in range(H_B):
        nope = kv[:, h * NOPE:(h + 1) * NOPE]
        r = lax.rsqrt((jnp.sum(nope * nope, -1, keepdims=True) + skp) / QK_B + 1e-6)
        kb_o[:, h * 256:h * 256 + 128] = (nope * r * g[:, :128]).astype(BF16)
        kb_o[:, h * 256 + 128:(h + 1) * 256] = (kp * r * g[:, 128:]).astype(BF16)
    vb_o[...] = kv[:, H_B * NOPE:].astype(BF16)


def _mla_kv(ckv_all, kpe_all, wukv, bkn, bl):
    b, lp, _ = ckv_all.shape
    return pl.pallas_call(
        _mla_kv_kernel,
        grid=(b, lp // bl),
        in_specs=[pl.BlockSpec((None, bl, KV_LORA), lambda bi, li: (bi, li, 0)),
                  pl.BlockSpec((None, bl, LANE), lambda bi, li: (bi, li, 0)),
                  _const_spec((KV_LORA, 2048)), _const_spec((1, 256))],
        out_specs=[pl.BlockSpec((None, bl, 2048), lambda bi, li: (bi, li, 0)),
                   pl.BlockSpec((None, bl, 1024), lambda bi, li: (bi, li, 0))],
        out_shape=[jax.ShapeDtypeStruct((b, lp, 2048), BF16), jax.ShapeDtypeStruct((b, lp, 1024), BF16)],
        compiler_params=_params("parallel", "arbitrary"),
        name="mla_kv",
    )(ckv_all, kpe_all, wukv, bkn)


def _key_chunk(lp):
    for c in (512, 384, 256, 128):
        if lp % c == 0:
            return c
    raise ValueError(lp)


def _dsa_kernel(qa_ref, iq_ref, sa_ref, ag_ref, ka_ref, va_ref, kidx_ref, bias_ref, acc_ref, o_ref,
                key_ref, hi_ref, lo_ref, *, bq, lp, l_true, q0, ksel, nd, far, blk0):
    i = pl.program_id(1) + blk0
    q_start = q0 + i * bq
    qpos = q_start + lax.broadcasted_iota(jnp.int32, (bq, 1), 0)
    ck = _key_chunk(lp)

    iq = iq_ref[...].astype(BF16)
    a = jnp.concatenate([iq[:, p * LANE:(p + 1) * LANE] for p in range(H_IDX // 2)], axis=0)
    wi = sa_ref[...] * (H_IDX ** -0.5 * D_IDX ** -0.5)
    for c0 in range(0, lp, ck):
        kk = kidx_ref[c0:c0 + ck, :]
        lane = lax.broadcasted_iota(jnp.int32, kk.shape, 1)
        k_lo = jnp.where(lane < D_IDX, kk, 0.0).astype(BF16)
        k_hi = jnp.where(lane >= D_IDX, pltpu.roll(kk, D_IDX, 1), 0.0).astype(BF16)
        s_lo = lax.dot_general(a, k_lo, NT_DIMS, preferred_element_type=F32)
        s_hi = lax.dot_general(a, k_hi, NT_DIMS, preferred_element_type=F32)
        sc = jnp.zeros((bq, ck), F32)
        for p in range(H_IDX // 2):
            w0 = wi[:, D_IDX + 2 * p:D_IDX + 2 * p + 1]
            w1 = wi[:, D_IDX + 2 * p + 1:D_IDX + 2 * p + 2]
            sc = sc + w0 * jnp.maximum(s_lo[p * bq:(p + 1) * bq], 0.0)
            sc = sc + w1 * jnp.maximum(s_hi[p * bq:(p + 1) * bq], 0.0)
        kpos = c0 + lax.broadcasted_iota(jnp.int32, (1, ck), 1)
        valid = ((kpos >> CHUNK_SHIFT) <= (qpos >> CHUNK_SHIFT)) & (kpos < l_true)
        bits = lax.bitcast_convert_type(sc, jnp.int32)
        key = jnp.where(bits < 0, bits ^ 0x7FFFFFFF, bits)
        key = jnp.where(bits == INT_MIN, 0, key)
        key = jnp.where(valid, key, INT_MIN)
        key_ref[:, c0:c0 + ck] = key
        hi_ref[:, c0:c0 + ck] = (key >> 16) + 2 ** 15
        lo_ref[:, c0:c0 + ck] = key & 0xFFFF

    ng = SEARCH_GROUPS
    gr = bq // ng
    groups = [slice(r * gr, (r + 1) * gr) for r in range(ng)]

    def count_ge(ref, rows, cand):
        acc = None
        for j in range(lp // LANE):
            d = (ref[rows, j * LANE:(j + 1) * LANE] - cand) >> 31
            acc = d if acc is None else acc + d
        return lp + jnp.sum(acc.astype(F32), axis=-1, keepdims=True)

    def search16(ref, need):
        ts = [jnp.zeros((gr, 1), jnp.int32)] * ng
        for bit in reversed(range(16)):
            for r in range(ng):
                cand = ts[r] | (1 << bit)
                ts[r] = jnp.where(count_ge(ref, groups[r], cand) >= need[r], cand, ts[r])
        return ts

    if lp > ksel:
        t_hi = search16(hi_ref, (float(ksel),) * ng)
        need_lo = []
        for rows, t in zip(groups, t_hi):
            need_lo.append(ksel - count_ge(hi_ref, rows, t + 1))
            lo_ref[rows, :] = jnp.where(hi_ref[rows, :] == t, lo_ref[rows, :], -1)
        t_lo = search16(lo_ref, need_lo)
        t_s = jnp.concatenate([lax.shift_left(h - 2 ** 15, 16) | l for h, l in zip(t_hi, t_lo)], axis=0)
        key = key_ref[...]
        n_picked = jnp.sum(((key >= t_s) & (key > INT_MIN)).astype(F32), axis=-1, keepdims=True)

        @pl.when(jnp.max(n_picked) > ksel)
        def _():
            key = key_ref[...]
            rpos = (lp - 1) - lax.broadcasted_iota(jnp.int32, (bq, lp), 1)
            lo_ref[...] = jnp.where((key == t_s) & (key > INT_MIN), rpos, -1)
            need = [ksel - jnp.sum((key[rows] > t_s[rows]).astype(F32), axis=-1, keepdims=True) for rows in groups]
            t_pos = jnp.concatenate(search16(lo_ref, need), axis=0)
            tie = lo_ref[...]
            key_ref[...] = jnp.where((tie >= 0) & (tie < t_pos), INT_MIN, key)
    else:
        t_s = jnp.full((bq, 1), INT_MIN, jnp.int32)
    key = key_ref[...]
    sel = (key >= t_s) & (key > INT_MIN)

    d_base = (nd - 1) - (q0 // LANE + (i if bq == LANE else 0))
    nk = lp // LANE
    for n in range(HKV_A):
        q4 = jnp.concatenate([qa_ref[:, (n * GROUP_A + g) * DH_A:(n * GROUP_A + g + 1) * DH_A]
                              for g in range(GROUP_A)], axis=0)
        k_n = ka_ref[:, n * DH_A:(n + 1) * DH_A].astype(BF16)
        v_n = va_ref[:, n * DH_A:(n + 1) * DH_A].astype(BF16)
        lg4 = lax.dot_general(q4, k_n, NT_DIMS, preferred_element_type=F32) * DH_A ** -0.5
        ps, ss = [], []
        for g in range(GROUP_A):
            bias = jnp.concatenate([bias_ref[n * GROUP_A + g, jnp.maximum(jnp.minimum(d_base + j, nd - 1) - far, 0)]
                                    for j in range(nk)], axis=1)
            lg = jnp.where(sel, lg4[g * bq:(g + 1) * bq] + bias, -jnp.inf)
            p = jnp.exp(lg - jnp.max(lg, axis=-1, keepdims=True))
            ss.append(jnp.sum(p, axis=-1, keepdims=True))
            ps.append(p.astype(BF16))
        o4 = jnp.dot(jnp.concatenate(ps, axis=0), v_n, preferred_element_type=F32)
        for g in range(GROUP_A):
            h = n * GROUP_A + g
            gate = ag_ref[:, h * DH_A:(h + 1) * DH_A]
            o_ref[:, h * DH_A:(h + 1) * DH_A] = (o4[g * bq:(g + 1) * bq] / ss[g] * _silu(gate)).astype(BF16)


def _dsa(qa, z, ka_all, va_all, kidx_all, kidx_col, bias_tiles, acc, *, bq, l_true, q0, ksel, row0, rows, lp):
    b, s, _ = qa.shape
    bias_tiles, far = bias_tiles
    nd = bias_tiles.shape[1] + far
    blk0 = row0 // bq
    assert q0 % LANE == 0 and (bq == LANE or s == bq)
    assert nd == (q0 + s - bq) // LANE + 1
    kern = functools.partial(_dsa_kernel, bq=bq, lp=lp, l_true=l_true, q0=q0, ksel=ksel, nd=nd, far=far, blk0=blk0)

    def zspec(width, off):
        return pl.BlockSpec((None, bq, width), lambda bi, qi: (bi, qi + blk0, off // width))

    def kspec(width):
        return pl.BlockSpec((None, lp, width), lambda bi, qi: (bi, 0, 0))

    return pl.pallas_call(
        kern,
        grid=(b, rows // bq),
        in_specs=[pl.BlockSpec((None, bq, 1024), lambda bi, qi: (bi, qi + blk0, 0)),
                  zspec(1024, E_IQ), zspec(128, E_IDX), zspec(1024, E_AG),
                  kspec(256), kspec(256),
                  pl.BlockSpec((None, lp, LANE), lambda bi, qi: (bi, 0, kidx_col)),
                  _const_spec(bias_tiles.shape),
                  pl.BlockSpec(memory_space=pl.ANY)],
        out_specs=pl.BlockSpec((None, bq, 1024), lambda bi, qi: (bi, qi + blk0, 0)),
        out_shape=jax.ShapeDtypeStruct(acc.shape, acc.dtype),
        input_output_aliases={8: 0},
        scratch_shapes=[pltpu.VMEM((bq, lp), jnp.int32)] * 3,
        compiler_params=_params("parallel", "arbitrary"),
        name="dsa",
    )(qa, z, z, z, ka_all, va_all, kidx_all, bias_tiles, acc)


def _causal_classes(s, lp, q0):
    if q0 == 0 and s % CLASS_ROWS == 0 and lp == s:
        return [(c * CLASS_ROWS, CLASS_ROWS, (c + 1) * CLASS_ROWS) for c in range(s // CLASS_ROWS)]
    return [(0, s, lp)]


def _mla_kernel(qb_ref, bg_ref, kb_ref, vb_ref, acc_ref, o_ref, *, bq, lp, l_true, q0, blk0, n_full):
    q_start = q0 + (pl.program_id(1) + blk0) * bq
    qpos = q_start + lax.broadcasted_iota(jnp.int32, (bq, 1), 0)
    kpos = n_full + lax.broadcasted_iota(jnp.int32, (1, lp - n_full), 1)
    valid = ((kpos >> CHUNK_SHIFT) <= (qpos >> CHUNK_SHIFT)) & (kpos < l_true)
    for h in range(H_B):
        q = qb_ref[:, h * 256:(h + 1) * 256]
        parts = []
        if n_full:
            parts.append((lax.dot_general(q, kb_ref[:n_full, h * 256:(h + 1) * 256], NT_DIMS,
                                          preferred_element_type=F32), vb_ref[:n_full, h * V_DIM:(h + 1) * V_DIM]))
        lg = lax.dot_general(q, kb_ref[n_full:, h * 256:(h + 1) * 256], NT_DIMS, preferred_element_type=F32)
        parts.append((jnp.where(valid, lg, -jnp.inf), vb_ref[n_full:, h * V_DIM:(h + 1) * V_DIM]))
        o = _softmax_pv(parts, QK_B ** -0.5)
        gate = bg_ref[:, h * V_DIM:(h + 1) * V_DIM]
        o_ref[:, h * V_DIM:(h + 1) * V_DIM] = (o * _silu(gate)).astype(BF16)


def _mla(qb, z, kb, vb, acc, *, bq, l_true, q0, row0, rows, lp):
    b = qb.shape[0]
    blk0 = row0 // bq
    n_full = min(min(q0 + row0 + CHUNK, l_true) // LANE * LANE, lp - LANE)
    kern = functools.partial(_mla_kernel, bq=bq, lp=lp, l_true=l_true, q0=q0, blk0=blk0, n_full=n_full)
    return pl.pallas_call(
        kern,
        grid=(b, rows // bq),
        in_specs=[pl.BlockSpec((None, bq, 2048), lambda bi, qi: (bi, qi + blk0, 0)),
                  pl.BlockSpec((None, bq, 1024), lambda bi, qi: (bi, qi + blk0, E_BG // 1024)),
                  pl.BlockSpec((None, lp, 2048), lambda bi, qi: (bi, 0, 0)),
                  pl.BlockSpec((None, lp, 1024), lambda bi, qi: (bi, 0, 0)),
                  pl.BlockSpec(memory_space=pl.ANY)],
        out_specs=pl.BlockSpec((None, bq, 1024), lambda bi, qi: (bi, qi + blk0, 0)),
        out_shape=jax.ShapeDtypeStruct(acc.shape, acc.dtype),
        input_output_aliases={4: 0},
        compiler_params=_params("parallel", "arbitrary"),
        name="mla",
    )(qb, z, kb, vb, acc)


def _out_ple_kernel(h_ref, oa_ref, ob_ref, p_ref, wo_ref, wg_ref, wp_ref, o_ref):
    half = oa_ref.shape[-1]
    h1 = (h_ref[...]
          + jnp.dot(oa_ref[...], wo_ref[:half, :], preferred_element_type=F32)
          + jnp.dot(ob_ref[...], wo_ref[half:, :], preferred_element_type=F32))
    r = (h1 * _rs(h1)).astype(BF16)
    gate = 1.0 / (1.0 + jnp.exp(-jnp.dot(r, wg_ref[...], preferred_element_type=F32)))
    o_ref[...] = h1 + gate * jnp.dot(p_ref[...].astype(BF16), wp_ref[...], preferred_element_type=F32)


def _out_ple(h, oa, ob, p_all, layer, wo, wg, wp, bm):
    m, d = h.shape
    half = oa.shape[1]

    def rows(width):
        return pl.BlockSpec((bm, width), lambda i: (i, 0))

    return pl.pallas_call(
        _out_ple_kernel,
        grid=(m // bm,),
        in_specs=[rows(d), rows(half), rows(half), pl.BlockSpec((None, bm, PLE_DIM), lambda i: (layer, i, 0)),
                  _const_spec((2 * half, d)), _const_spec((d, d)), _const_spec((PLE_DIM, d))],
        out_specs=rows(d),
        out_shape=jax.ShapeDtypeStruct((m, d), F32),
        compiler_params=_params("parallel"),
        name="out_ple",
    )(h, oa, ob, p_all, wo, wg, wp)


def _odd_prep_kernel(cq_ref, ck_ref, cv_ref, dv_ref, du_ref, dgate_ref, cqn_ref, ckn_ref, dg_ref, db_ref,
                     ws_ref, bs_ref, qc_o, kcb_o, vcb_o, od_o, kct_o, cvt_o, *maybe_dvn_o, n):
    for h in range(H_C):
        sl = slice(h * DH_C, (h + 1) * DH_C)
        x = cq_ref[:, sl]
        qc_o[:, sl] = (x * _rs(x) * cqn_ref[...]).astype(BF16)
        x = ck_ref[:, sl]
        kn = x * _rs(x) * ckn_ref[...]
        kct_o[:, sl] = kn
        kcb_o[:, sl] = kn.astype(BF16)
    cv = cv_ref[...]
    cvt_o[...] = cv
    vcb_o[...] = cv.astype(BF16)
    dv = dv_ref[...]
    xc = dv - jnp.mean(dv, -1, keepdims=True)
    var = jnp.mean(xc * xc, -1, keepdims=True)
    dvn = xc * lax.rsqrt(var + 1e-5) * dg_ref[...] + db_ref[...]
    for dvn_o in maybe_dvn_o:
        dvn_o[...] = dvn
    dvn = dvn.astype(BF16)
    row = lax.broadcasted_iota(jnp.int32, (n, n), 0)
    col = lax.broadcasted_iota(jnp.int32, (n, n), 1)
    for g in range(G_D):
        sl = slice(g * DG_D, (g + 1) * DG_D)
        w = jnp.where(col <= row, ws_ref[g], 0.0).astype(BF16)
        for c in range(dv.shape[0] // n):
            rs = slice(c * n, (c + 1) * n)
            sg = jnp.dot(w, dvn[rs, sl], preferred_element_type=F32) + bs_ref[:, g:g + 1]
            od_o[rs, sl] = (du_ref[rs, sl] * sg * _silu(dgate_ref[rs, sl])).astype(BF16)


def _odd_prep(z, cqn, ckn, dg, db, ws, bs_t, bs, n, keep, want_dvn):
    b, s, _ = z.shape
    first_tail = (s - keep) // bs
    assert keep % bs == 0

    def zspec(off):
        return pl.BlockSpec((None, bs, 1024), lambda bi, si: (bi, si, off // 1024))

    ospec = pl.BlockSpec((None, bs, 1024), lambda bi, si: (bi, si, 0))
    tspec = pl.BlockSpec((None, bs, 1024), lambda bi, si: (bi, jnp.maximum(si - first_tail, 0), 0))
    act = lambda dt: jax.ShapeDtypeStruct((b, s, 1024), dt)
    tail = jax.ShapeDtypeStruct((b, keep, 1024), F32)
    return pl.pallas_call(
        functools.partial(_odd_prep_kernel, n=n),
        grid=(b, s // bs),
        in_specs=[zspec(O_CQ), zspec(O_CK), zspec(O_CV), zspec(O_DV), zspec(O_DU), zspec(O_DG),
                  _const_spec((1, DH_C)), _const_spec((1, DH_C)), _const_spec((1, W_D)), _const_spec((1, W_D)),
                  _const_spec((G_D, n, n)), _const_spec((n, G_D))],
        out_specs=[ospec] * 4 + [tspec] * 2 + [ospec] * want_dvn,
        out_shape=[act(BF16)] * 4 + [tail] * 2 + [act(F32)] * want_dvn,
        compiler_params=_params("parallel", "arbitrary"),
        name="odd_prep",
    )(z, z, z, z, z, z, cqn, ckn, dg, db, ws, bs_t)


BACK_TILES = C_BACK * CHUNK // LANE


def _band_width(bq):
    return (BACK_TILES + -(-bq // LANE)) * LANE


def _band_kernel(q_ref, cg_ref, k_ref, v_ref, cbias_ref, bias_ref, o_ref, *, bq, sliding, q0, k0, k_end, c0):
    i = pl.program_id(1)
    q_start = q0 + i * bq
    bw = _band_width(bq)
    if sliding:
        first = i * (bq // LANE) - BACK_TILES
        win_start = first * LANE
        rows = [pl.ds(pl.multiple_of(jnp.maximum(first + t, 0) * LANE, LANE), LANE) for t in range(bw // LANE)]
    else:
        win_start = k0
        rows = [pl.ds(t * LANE, LANE) for t in range(bw // LANE)]
    kw = jnp.concatenate([k_ref[r, :] for r in rows], axis=0)
    vw = jnp.concatenate([v_ref[r, :] for r in rows], axis=0)
    qpos = q_start + lax.broadcasted_iota(jnp.int32, (bq, 1), 0)

    def valid(lo, hi):
        kpos = win_start + lo + lax.broadcasted_iota(jnp.int32, (1, hi - lo), 1)
        dc = (qpos >> CHUNK_SHIFT) - (kpos >> CHUNK_SHIFT)
        return (dc >= 0) & (dc <= C_BACK) & (kpos >= 0) & (kpos < k_end)

    segs = [(lo, hi, valid(lo, hi)) for lo, hi in ((0, c0), (c0, bw)) if hi > lo]
    for h in range(H_C):
        sl = slice(h * DH_C, (h + 1) * DH_C)
        q = q_ref[:, sl]
        parts = []
        for lo, hi, ok in segs:
            bias = cbias_ref[h][:, :1] if hi <= c0 else bias_ref[h]
            lg = lax.dot_general(q, kw[lo:hi, sl], NT_DIMS, preferred_element_type=F32) * DH_C ** -0.5
            parts.append((jnp.where(ok, lg + bias, -jnp.inf), vw[lo:hi, sl]))
        o = _softmax_pv(parts)
        o_ref[:, sl] = (o * _silu(cg_ref[:, sl])).astype(BF16)


def _band(qc, z, kcb, vcb, bias, *, bq, sliding, q0, k0, k_end):
    c0, cbias, bias = bias
    b, s, _ = qc.shape
    lk = kcb.shape[1]
    if sliding:
        assert bq % LANE == 0 and q0 == 0 and k0 == 0
    else:
        assert lk == _band_width(bq) and s == bq
    kern = functools.partial(_band_kernel, bq=bq, sliding=sliding, q0=q0, k0=k0, k_end=k_end, c0=c0)
    return pl.pallas_call(
        kern,
        grid=(b, s // bq),
        in_specs=[pl.BlockSpec((None, bq, 1024), lambda bi, qi: (bi, qi, 0)),
                  pl.BlockSpec((None, bq, 1024), lambda bi, qi: (bi, qi, O_CG // 1024)),
                  pl.BlockSpec((None, lk, 1024), lambda bi, qi: (bi, 0, 0)),
                  pl.BlockSpec((None, lk, 1024), lambda bi, qi: (bi, 0, 0)),
                  _const_spec(cbias.shape), _const_spec(bias.shape)],
        out_specs=pl.BlockSpec((None, bq, 1024), lambda bi, qi: (bi, qi, 0)),
        out_shape=jax.ShapeDtypeStruct((b, s, 1024), BF16),
        compiler_params=_params("parallel", "arbitrary"),
        name="band",
    )(qc, z, kcb, vcb, cbias, bias)


def _rope_tables(pos):
    half = ROPE_DIM // 2
    freq = ROPE_BASE ** (-jnp.arange(half, dtype=F32) / half)
    ang = pos.astype(F32)[:, None] * freq[None, :]
    cos, sin = jnp.cos(ang), jnp.sin(ang)
    z = jnp.zeros_like(cos)
    return jnp.concatenate([cos, z, cos, z], 1), jnp.concatenate([-sin, z, sin, z], 1)


def _rope_lanes(x):
    half = ROPE_DIM // 2
    z = jnp.zeros(x.shape[:-1] + (half,), x.dtype)
    return jnp.concatenate([x[..., :half], z, x[..., half:], z], -1)


def _t5_bucket_np(rel):
    nb = T5_BUCKETS // 2
    max_exact = nb // 2
    n = np.abs(rel)
    nf = np.maximum(n, 1).astype(np.float64)
    large = max_exact + (np.log(nf / max_exact) / math.log(T5_MAX_DIST / max_exact) * (nb - max_exact)).astype(np.int64)
    large = np.minimum(large, nb - 1)
    return np.where(rel > 0, nb, 0) + np.where(n < max_exact, n, large)


def _toeplitz(w, rows, width, cols):
    flat = jnp.tile(w, (1,) * (w.ndim - 1) + (rows,))[..., :rows * width]
    return flat.reshape(w.shape[:-1] + (rows, width))[..., :cols]


def _t5_tiles(t5_bias, bq, nd):
    width = 2 * LANE
    k = np.arange(width + 1)
    delta = np.where(k < LANE, k, k - (width + 1))
    rel = (np.arange(nd) - (nd - 1))[:, None] * LANE + delta[None, :]
    bucket = _t5_bucket_np(rel)
    far = 0
    while far + 1 < nd and np.array_equal(bucket[far + 1], bucket[0]):
        far += 1
    w = jnp.transpose(t5_bias[bucket[far:]], (2, 0, 1))
    return _toeplitz(w, bq, width, LANE), far


def _band_bias(rel_tab, bq, qk_off):
    bw = _band_width(bq)
    rel_index = lambda rel: np.clip(rel, -(CHUNK - 1), REL_CLIP) + (CHUNK - 1)
    full = rel_index(qk_off + np.arange(bq)[:, None] - np.arange(bw)[None, :])
    c0 = 0
    while c0 + 2 * LANE <= bw and np.all(full[:, :c0 + LANE] == full[0, 0]):
        c0 += LANE
    const = jnp.broadcast_to(rel_tab[full[0, 0]][:, None, None], (H_C, 1, LANE))
    wv = bw - c0
    width = wv + bq
    k = np.arange(width + 1)
    delta = np.where(k < wv, k, k - (width + 1))
    idx = rel_index(qk_off - c0 - delta)
    return c0, const, _toeplitz(jnp.transpose(rel_tab[idx], (1, 0)), bq, width, wv)


def _even_weights(w_in, b_wuq, b_wukv, b_qn, b_kn):
    d = w_in.shape[0]
    offs = np.cumsum((0,) + EVEN_SPLITS)
    w16 = w_in.astype(BF16)
    aq, ak, av, ag, iq, ik, iw, bcq, bckv, bkpe, bg = [w16[:, offs[t]:offs[t + 1]] for t in range(11)]
    slab_idx = jnp.concatenate([ik, iw, jnp.zeros((d, LANE - D_IDX - H_IDX), BF16)], 1)
    w = jnp.concatenate([aq, ag, iq, bg, bcq, ak, av, bckv, slab_idx, _rope_lanes(bkpe)], 1)
    uq = b_wuq.reshape(Q_LORA, H_B, QK_B)
    uq = jnp.concatenate([uq[..., :NOPE], _rope_lanes(uq[..., NOPE:])], -1).reshape(Q_LORA, H_B * 256).astype(BF16)
    ukv = b_wukv.reshape(KV_LORA, H_B, NOPE + V_DIM)
    ukv = jnp.concatenate([ukv[..., :NOPE].reshape(KV_LORA, H_B * NOPE),
                           ukv[..., NOPE:].reshape(KV_LORA, H_B * V_DIM)], 1).astype(BF16)
    pad_gain = lambda g: jnp.concatenate([g[:NOPE], _rope_lanes(g[NOPE:])])[None, :]
    return w, uq, ukv, pad_gain(b_qn), pad_gain(b_kn)


def _pad_rows(x, lp):
    return jnp.pad(x, ((0, 0), (0, lp - x.shape[1]), (0, 0)))


def _even_layer(h, p_all, layer, past, q0, ln_g, w_in, uq, ukv, a_qn, a_kn, t5_bias, b_qln, b_kvln, bqn, bkn, wo, wg, wp):
    b, s, d = h.shape
    m = b * s
    z = _norm_mm(h.reshape(m, d), ln_g[None, :], w_in, min(m, 1024), E_END // 4).reshape(b, s, E_END)
    cos, sin = _rope_tables(q0 + jnp.arange(s, dtype=jnp.int32))
    bs = min(s, 256)
    qa, ka, qb, ckv, kpe_l, kpe64, idx64 = _even_prep(z, cos, sin, a_qn[None, :], a_kn[None, :], b_qln[None, :],
                                                      b_kvln[None, :], uq, bqn, bs)
    av = z[..., E_AV:E_AV + 256]
    new = (ka.reshape(b, s, HKV_A, DH_A), av.reshape(b, s, HKV_A, DH_A), idx64, ckv, kpe64)
    if past is None:
        l_true = s
        ka_all, va_all, ckv_all, kpe_all = ka, av, ckv, kpe_l
        kidx_all, kidx_col = z, E_IDX // LANE
    else:
        kidx = z[..., E_IDX:E_IDX + LANE]
        kidx_col = 0
        c_k, c_v, c_ik, c_ckv, c_kpe = past
        pl_ = c_k.shape[1]
        l_true = pl_ + s
        lp = -(-l_true // LANE) * LANE
        cat = lambda c, n_: _pad_rows(jnp.concatenate([c, n_], 1), lp)
        ka_all = cat(c_k.reshape(b, pl_, 256), ka)
        va_all = cat(c_v.reshape(b, pl_, 256), av)
        kidx_all = cat(jnp.pad(c_ik, ((0, 0), (0, 0), (0, LANE - D_IDX))), kidx)
        ckv_all = cat(c_ckv, ckv)
        kpe_all = cat(_rope_lanes(c_kpe), kpe_l)
    lp = ka_all.shape[1]
    ksel = min(TOPK_MAX, l_true // 4)
    bq = min(s, LANE)
    nd = (q0 + s - bq) // LANE + 1
    tiles = _t5_tiles(t5_bias, bq, nd)
    classes = _causal_classes(s, lp, q0)
    o_a = jnp.zeros((b, s, H_A * DH_A), BF16)
    for r0, nr, lc in classes:
        o_a = _dsa(qa, z, ka_all, va_all, kidx_all, kidx_col, tiles, o_a, bq=bq, l_true=l_true, q0=q0, ksel=ksel,
                   row0=r0, rows=nr, lp=lc)
    kb, vb = _mla_kv(ckv_all, kpe_all, ukv, bkn, _key_chunk(lp))
    o_b = jnp.zeros((b, s, H_B * V_DIM), BF16)
    for r0, nr, lc in classes:
        o_b = _mla(qb, z, kb, vb, o_b, bq=min(s, 256), l_true=l_true, q0=q0, row0=r0, rows=nr, lp=lc)
    y = _out_ple(h.reshape(m, d), o_a.reshape(m, -1), o_b.reshape(m, -1), p_all, layer, wo, wg, wp, min(m, 256))
    return y.reshape(b, s, d), new


def _odd_layer(h, p_all, layer, past, q0, ln_g, w_in, c_qn, c_kn, c_rel, d_g, d_b, d_ws, d_bs, wo, wg, wp):
    b, s, d = h.shape
    m = b * s
    z = _norm_mm(h.reshape(m, d), ln_g[None, :], w_in, min(m, 1024), 1024).reshape(b, s, O_END)
    n = min(s, D_CHUNK)
    keep = min(C_BACK * CHUNK, s) if past is None else s
    qc, kcb, vcb, o_d, kct, cvt, *dvn = _odd_prep(z, c_qn[None, :], c_kn[None, :], d_g[None, :], d_b[None, :],
                                                  d_ws[:, :n, :n], d_bs[:, :n].T, min(s, 256), n, keep,
                                                  want_dvn=past is not None)
    c_new = (kct.reshape(b, keep, H_C, DH_C), cvt.reshape(b, keep, H_C, DH_C))
    if past is None:
        bqc = 2 * LANE
        bias = _band_bias(c_rel, bqc, C_BACK * CHUNK)
        o_c = _band(qc, z, kcb, vcb, bias, bq=bqc, sliding=True, q0=0, k0=0, k_end=s)
    else:
        nc = past[0].shape[1]
        cat = lambda c, n_: _pad_rows(jnp.concatenate([c.reshape(b, nc, 1024).astype(BF16), n_], 1), _band_width(s))
        bias = _band_bias(c_rel, s, nc)
        o_c = _band(qc, z, cat(past[0], kcb), cat(past[1], vcb), bias,
                    bq=s, sliding=False, q0=q0, k0=q0 - nc, k_end=q0 + s)
    y = _out_ple(h.reshape(m, d), o_c.reshape(m, -1), o_d.reshape(m, -1), p_all, layer, wo, wg, wp, min(m, 256))
    return y.reshape(b, s, d), c_new, (dvn[0] if dvn else None)


def kernel(x_prompt, x_sample, cache_a_k, cache_a_v, cache_a_idx_k, cache_b_ckv, cache_b_kpe, cache_c_k, cache_c_v, p_prompt, p_sample, ln_g, w_in_even, a_q_norm, a_k_norm, t5_bias, b_q_lora_norm, b_kv_lora_norm, b_w_uq, b_w_ukv, b_q_norm, b_k_norm, w_out_even, w_in_odd, c_q_norm, c_k_norm, c_rel_bias, d_ln_g, d_ln_b, d_w_s, d_b_s, w_out_odd, ple_proj, ple_gate):
    depth = ln_g.shape[0]
    past_len = cache_a_k.shape[2]
    hp, hs = x_prompt, x_sample
    pp = p_prompt.reshape(depth, -1, PLE_DIM)
    ps = p_sample.reshape(depth, -1, PLE_DIM)
    ev_p, ev_s, od_p, od_s, dv_s = [], [], [], [], []
    for i in range(depth):
        j = i // 2
        wg = ple_gate[i].astype(BF16)
        wp = ple_proj[i].astype(BF16)
        if i % 2 == 0:
            w_in, uq, ukv, bqn, bkn = _even_weights(w_in_even[j], b_w_uq[j], b_w_ukv[j], b_q_norm[j], b_k_norm[j])
            w = (ln_g[i], w_in, uq, ukv, a_q_norm[j], a_k_norm[j], t5_bias, b_q_lora_norm[j], b_kv_lora_norm[j],
                 bqn, bkn, w_out_even[j].astype(BF16), wg, wp)
            hp, sp = _even_layer(hp, pp, i, None, 0, *w)
            past = (cache_a_k[j], cache_a_v[j], cache_a_idx_k[j], cache_b_ckv[j], cache_b_kpe[j])
            hs, ss = _even_layer(hs, ps, i, past, past_len, *w)
            ev_p.append(sp)
            ev_s.append(ss)
        else:
            w = (ln_g[i], w_in_odd[j].astype(BF16), c_q_norm[j], c_k_norm[j], c_rel_bias[j], d_ln_g[j], d_ln_b[j],
                 d_w_s[j], d_b_s[j], w_out_odd[j].astype(BF16), wg, wp)
            hp, sp, _ = _odd_layer(hp, pp, i, None, 0, *w)
            hs, ss, dvs = _odd_layer(hs, ps, i, (cache_c_k[j], cache_c_v[j]), past_len, *w)
            od_p.append(sp)
            od_s.append(ss)
            dv_s.append(dvs)
    st = lambda lst, n_: jnp.stack([e[n_] for e in lst], 0)
    return (hp, hs, st(ev_p, 0), st(ev_p, 1), st(ev_p, 2), st(ev_p, 3), st(ev_p, 4), st(od_p, 0), st(od_p, 1),
            st(ev_s, 0), st(ev_s, 1), st(ev_s, 2), st(ev_s, 3), st(ev_s, 4), st(od_s, 0), st(od_s, 1),
            jnp.stack(dv_s, 0))
```

```python
import functools
import math

import numpy as np
import jax
import jax.numpy as jnp
from jax import lax
from jax.experimental import pallas as pl
from jax.experimental.pallas import tpu as pltpu

F32 = jnp.float32
BF16 = jnp.bfloat16
INT_MIN = -2 ** 31
LOG2E = math.log2(math.e)
CLASS_ROWS = 256
SEARCH_GROUPS = 4

D_MODEL = 2048
CHUNK = 64
CHUNK_SHIFT = 6
LANE = 128
MXU_COLS = 256
H_A, HKV_A, GROUP_A, DH_A = 8, 2, 4, 128
H_IDX, D_IDX = 16, 64
TOPK_MAX = 256
T5_BUCKETS, T5_MAX_DIST = 32, 128
H_B, Q_LORA, KV_LORA, NOPE, ROPE_DIM, V_DIM = 8, 512, 256, 128, 64, 128
ROPE_BASE = 10000.0
QK_B = NOPE + ROPE_DIM
H_C, DH_C, C_BACK, REL_CLIP = 8, 128, 8, 128
W_D, G_D, DG_D, D_CHUNK = 1024, 8, 128, 128
PLE_DIM = 256

EVEN_SPLITS = (H_A * DH_A, HKV_A * DH_A, HKV_A * DH_A, H_A * DH_A, H_IDX * D_IDX, D_IDX, H_IDX,
               Q_LORA, KV_LORA, ROPE_DIM, H_B * V_DIM)
E_AQ, E_AG, E_IQ, E_BG, E_BCQ, E_AK, E_AV, E_CKV, E_IDX, E_KPE, E_END = (
    0, 1024, 2048, 3072, 4096, 4608, 4864, 5120, 5376, 5504, 5632)
O_CQ, O_CK, O_CV, O_CG, O_DU, O_DV, O_DG, O_END = 0, 1024, 2048, 3072, 4096, 5120, 6144, 7168

VMEM_LIMIT_BYTES = 56 * 1024 * 1024
NT_DIMS = (((1,), (1,)), ((), ()))


def _params(*sem):
    return pltpu.CompilerParams(dimension_semantics=sem, vmem_limit_bytes=VMEM_LIMIT_BYTES)


def _const_spec(shape):
    zeros = (0,) * len(shape)
    return pl.BlockSpec(shape, lambda *_: zeros, pipeline_mode=pl.Buffered(1))


def _rs(x, n=None, eps=1e-6):
    n = x.shape[-1] if n is None else n
    return lax.rsqrt(jnp.sum(x * x, axis=-1, keepdims=True) / n + eps)


def _silu(x):
    return x * (1.0 / (1.0 + jnp.exp(-x)))


def _softmax_pv(parts, scale=1.0):
    m = functools.reduce(jnp.maximum, [jnp.max(lg, axis=-1, keepdims=True) for lg, _ in parts])
    o = s = None
    for lg, v in parts:
        p = jnp.exp2((lg - m) * (scale * LOG2E))
        ps = jnp.sum(p, axis=-1, keepdims=True)
        po = jnp.dot(p.astype(BF16), v, preferred_element_type=F32)
        o, s = (po, ps) if o is None else (o + po, s + ps)
    return o / s


def _norm_mm_kernel(x_ref, g_ref, w_ref, o_ref, xn_ref):
    @pl.when(pl.program_id(1) == 0)
    def _():
        x = x_ref[...]
        xn_ref[...] = (x * _rs(x) * g_ref[...]).astype(BF16)

    o_ref[...] = jnp.dot(xn_ref[...], w_ref[...], preferred_element_type=F32)


def _norm_mm(x, g, w, bm, bn):
    m, d = x.shape
    n = w.shape[1]
    return pl.pallas_call(
        _norm_mm_kernel,
        grid=(m // bm, n // bn),
        in_specs=[pl.BlockSpec((bm, d), lambda i, j: (i, 0)),
                  pl.BlockSpec((1, d), lambda i, j: (0, 0)),
                  pl.BlockSpec((d, bn), lambda i, j: (0, j))],
        out_specs=pl.BlockSpec((bm, bn), lambda i, j: (i, j)),
        out_shape=jax.ShapeDtypeStruct((m, n), F32),
        scratch_shapes=[pltpu.VMEM((bm, d), BF16)],
        compiler_params=_params("parallel", "arbitrary"),
        name="norm_mm",
    )(x, g, w)


def _rope(x, cos, sin):
    return x * cos + pltpu.roll(x, 64, 1) * sin


def _even_prep_kernel(aq_ref, bcq_ref, ak_ref, ckv_ref, kpe_ref, idx_ref, cos_ref, sin_ref,
                      aqn_ref, akn_ref, qln_ref, kvln_ref, wuq_ref, bqn_ref,
                      qa_o, ka_o, qb_o, ckv_o, kpe_o, kpe64_o, idx64_o):
    for h in range(H_A):
        x = aq_ref[:, h * DH_A:(h + 1) * DH_A]
        qa_o[:, h * DH_A:(h + 1) * DH_A] = (x * _rs(x) * aqn_ref[...]).astype(BF16)
    for n in range(HKV_A):
        x = ak_ref[:, n * DH_A:(n + 1) * DH_A]
        ka_o[:, n * DH_A:(n + 1) * DH_A] = x * _rs(x) * akn_ref[...]
    c = ckv_ref[...]
    ckv_o[...] = c * _rs(c) * kvln_ref[...]
    cos = cos_ref[...]
    sin = sin_ref[...]
    kpe = _rope(kpe_ref[...], cos, sin)
    kpe_o[...] = kpe
    half = ROPE_DIM // 2
    kpe64_o[...] = jnp.concatenate([kpe[:, :half], kpe[:, 2 * half:3 * half]], axis=-1)
    idx64_o[...] = idx_ref[:, :D_IDX]
    cq = bcq_ref[...]
    cqn = (cq * _rs(cq) * qln_ref[...]).astype(BF16)
    qb = jnp.dot(cqn, wuq_ref[...], preferred_element_type=F32)
    g = bqn_ref[...]
    for h in range(H_B):
        nope = qb[:, h * 256:h * 256 + 128]
        rot = _rope(qb[:, h * 256 + 128:(h + 1) * 256], cos, sin)
        ss = jnp.sum(nope * nope + rot * rot, -1, keepdims=True)
        r = lax.rsqrt(ss / QK_B + 1e-6)
        qb_o[:, h * 256:h * 256 + 128] = (nope * r * g[:, :128]).astype(BF16)
        qb_o[:, h * 256 + 128:(h + 1) * 256] = (rot * r * g[:, 128:]).astype(BF16)


def _even_prep(z, cos, sin, aqn, akn, qln, kvln, wuq, bqn, bs):
    b, s, _ = z.shape

    def zspec(width, off):
        return pl.BlockSpec((None, bs, width), lambda bi, si: (bi, si, off // width))

    def ospec(width):
        return pl.BlockSpec((None, bs, width), lambda bi, si: (bi, si, 0))

    pos_spec = pl.BlockSpec((bs, LANE), lambda bi, si: (si, 0))
    return pl.pallas_call(
        _even_prep_kernel,
        grid=(b, s // bs),
        in_specs=[zspec(1024, E_AQ), zspec(512, E_BCQ), zspec(256, E_AK), zspec(256, E_CKV),
                  zspec(128, E_KPE), zspec(128, E_IDX), pos_spec, pos_spec,
                  _const_spec((1, DH_A)), _const_spec((1, DH_A)), _const_spec((1, Q_LORA)),
                  _const_spec((1, KV_LORA)), _const_spec((Q_LORA, H_B * 256)), _const_spec((1, 256))],
        out_specs=[ospec(1024), ospec(256), ospec(2048), ospec(256), ospec(128), ospec(ROPE_DIM), ospec(D_IDX)],
        out_shape=[jax.ShapeDtypeStruct((b, s, 1024), BF16), jax.ShapeDtypeStruct((b, s, 256), F32),
                   jax.ShapeDtypeStruct((b, s, 2048), BF16), jax.ShapeDtypeStruct((b, s, 256), F32),
                   jax.ShapeDtypeStruct((b, s, 128), F32), jax.ShapeDtypeStruct((b, s, ROPE_DIM), F32),
                   jax.ShapeDtypeStruct((b, s, D_IDX), F32)],
        compiler_params=_params("parallel", "arbitrary"),
        name="even_prep",
    )(z, z, z, z, z, z, cos, sin, aqn, akn, qln, kvln, wuq, bqn)


def _mla_kv_kernel(ckv_ref, kpe_ref, w_ref, g_ref, kb_o, vb_o):
    kv = jnp.dot(ckv_ref[...].astype(BF16), w_ref[...], preferred_element_type=F32)
    kp = kpe_ref[...]
    skp = jnp.sum(kp * kp, -1, keepdims=True)
    g = g_ref[...]
    for h in range(H_B):
        nope = kv[:, h * NOPE:(h + 1) * NOPE]
        r = lax.rsqrt((jnp.sum(nope * nope, -1, keepdims=True) + skp) / QK_B + 1e-6)
        kb_o[:, h * 256:h * 256 + 128] = (nope * r * g[:, :128]).astype(BF16)
        kb_o[:, h * 256 + 128:(h + 1) * 256] = (kp * r * g[:, 128:]).astype(BF16)
    vb_o[...] = kv[:, H_B * NOPE:].astype(BF16)


def _mla_kv(ckv_all, kpe_all, wukv, bkn, bl):
    b, lp, _ = ckv_all.shape
    return pl.pallas_call(
        _mla_kv_kernel,
        grid=(b, lp // bl),
        in_specs=[pl.BlockSpec((None, bl, KV_LORA), lambda bi, li: (bi, li, 0)),
                  pl.BlockSpec((None, bl, LANE), lambda bi, li: (bi, li, 0)),
                  _const_spec((KV_LORA, 2048)), _const_spec((1, 256))],
        out_specs=[pl.BlockSpec((None, bl, 2048), lambda bi, li: (bi, li, 0)),
                   pl.BlockSpec((None, bl, 1024), lambda bi, li: (bi, li, 0))],
        out_shape=[jax.ShapeDtypeStruct((b, lp, 2048), BF16), jax.ShapeDtypeStruct((b, lp, 1024), BF16)],
        compiler_params=_params("parallel", "arbitrary"),
        name="mla_kv",
    )(ckv_all, kpe_all, wukv, bkn)


def _key_chunk(lp):
    for c in (512, 384, 256, 128):
        if lp % c == 0:
            return c
    raise ValueError(lp)


def _dsa_kernel(qa_ref, iq_ref, sa_ref, ag_ref, ka_ref, va_ref, kidx_ref, bias_ref, acc_ref, o_ref,
                key_ref, hi_ref, lo_ref, *, bq, lp, l_true, q0, ksel, nd, far, blk0):
    i = pl.program_id(1) + blk0
    q_start = q0 + i * bq
    qpos = q_start + lax.broadcasted_iota(jnp.int32, (bq, 1), 0)
    ck = _key_chunk(lp)

    iq = iq_ref[...].astype(BF16)
    a = jnp.concatenate([iq[:, p * LANE:(p + 1) * LANE] for p in range(H_IDX // 2)], axis=0)
    wi = sa_ref[...] * (H_IDX ** -0.5 * D_IDX ** -0.5)
    for c0 in range(0, lp, ck):
        kk = kidx_ref[c0:c0 + ck, :]
        lane = lax.broadcasted_iota(jnp.int32, kk.shape, 1)
        k_lo = jnp.where(lane < D_IDX, kk, 0.0).astype(BF16)
        k_hi = jnp.where(lane >= D_IDX, pltpu.roll(kk, D_IDX, 1), 0.0).astype(BF16)
        s_lo = lax.dot_general(a, k_lo, NT_DIMS, preferred_element_type=F32)
        s_hi = lax.dot_general(a, k_hi, NT_DIMS, preferred_element_type=F32)
        sc = jnp.zeros((bq, ck), F32)
        for p in range(H_IDX // 2):
            w0 = wi[:, D_IDX + 2 * p:D_IDX + 2 * p + 1]
            w1 = wi[:, D_IDX + 2 * p + 1:D_IDX + 2 * p + 2]
            sc = sc + w0 * jnp.maximum(s_lo[p * bq:(p + 1) * bq], 0.0)
            sc = sc + w1 * jnp.maximum(s_hi[p * bq:(p + 1) * bq], 0.0)
        kpos = c0 + lax.broadcasted_iota(jnp.int32, (1, ck), 1)
        valid = ((kpos >> CHUNK_SHIFT) <= (qpos >> CHUNK_SHIFT)) & (kpos < l_true)
        bits = lax.bitcast_convert_type(sc, jnp.int32)
        key = jnp.where(bits < 0, bits ^ 0x7FFFFFFF, bits)
        key = jnp.where(bits == INT_MIN, 0, key)
        key = jnp.where(valid, key, INT_MIN)
        key_ref[:, c0:c0 + ck] = key
        hi_ref[:, c0:c0 + ck] = (key >> 16) + 2 ** 15
        lo_ref[:, c0:c0 + ck] = key & 0xFFFF

    ng = SEARCH_GROUPS
    gr = bq // ng
    groups = [slice(r * gr, (r + 1) * gr) for r in range(ng)]

    def count_ge(ref, rows, cand):
        acc = None
        for j in range(lp // LANE):
            d = (ref[rows, j * LANE:(j + 1) * LANE] - cand) >> 31
            acc = d if acc is None else acc + d
        return lp + jnp.sum(acc.astype(F32), axis=-1, keepdims=True)

    def search16(ref, need):
        ts = [jnp.zeros((gr, 1), jnp.int32)] * ng
        for bit in reversed(range(16)):
            for r in range(ng):
                cand = ts[r] | (1 << bit)
                ts[r] = jnp.where(count_ge(ref, groups[r], cand) >= need[r], cand, ts[r])
        return ts

    if lp > ksel:
        t_hi = search16(hi_ref, (float(ksel),) * ng)
        need_lo = []
        for rows, t in zip(groups, t_hi):
            need_lo.append(ksel - count_ge(hi_ref, rows, t + 1))
            lo_ref[rows, :] = jnp.where(hi_ref[rows, :] == t, lo_ref[rows, :], -1)
        t_lo = search16(lo_ref, need_lo)
        t_s = jnp.concatenate([lax.shift_left(h - 2 ** 15, 16) | l for h, l in zip(t_hi, t_lo)], axis=0)
        key = key_ref[...]
        n_picked = jnp.sum(((key >= t_s) & (key > INT_MIN)).astype(F32), axis=-1, keepdims=True)

        @pl.when(jnp.max(n_picked) > ksel)
        def _():
            key = key_ref[...]
            rpos = (lp - 1) - lax.broadcasted_iota(jnp.int32, (bq, lp), 1)
            lo_ref[...] = jnp.where((key == t_s) & (key > INT_MIN), rpos, -1)
            need = [ksel - jnp.sum((key[rows] > t_s[rows]).astype(F32), axis=-1, keepdims=True) for rows in groups]
            t_pos = jnp.concatenate(search16(lo_ref, need), axis=0)
            tie = lo_ref[...]
            key_ref[...] = jnp.where((tie >= 0) & (tie < t_pos), INT_MIN, key)
    else:
        t_s = jnp.full((bq, 1), INT_MIN, jnp.int32)
    key = key_ref[...]
    sel = (key >= t_s) & (key > INT_MIN)

    d_base = (nd - 1) - (q0 // LANE + (i if bq == LANE else 0))
    nk = lp // LANE
    for n in range(HKV_A):
        q4 = jnp.concatenate([qa_ref[:, (n * GROUP_A + g) * DH_A:(n * GROUP_A + g + 1) * DH_A]
                              for g in range(GROUP_A)], axis=0)
        k_n = ka_ref[:, n * DH_A:(n + 1) * DH_A].astype(BF16)
        v_n = va_ref[:, n * DH_A:(n + 1) * DH_A].astype(BF16)
        lg4 = lax.dot_general(q4, k_n, NT_DIMS, preferred_element_type=F32) * DH_A ** -0.5
        ps, ss = [], []
        for g in range(GROUP_A):
            bias = jnp.concatenate([bias_ref[n * GROUP_A + g, jnp.maximum(jnp.minimum(d_base + j, nd - 1) - far, 0)]
                                    for j in range(nk)], axis=1)
            lg = jnp.where(sel, lg4[g * bq:(g + 1) * bq] + bias, -jnp.inf)
            p = jnp.exp(lg - jnp.max(lg, axis=-1, keepdims=True))
            ss.append(jnp.sum(p, axis=-1, keepdims=True))
            ps.append(p.astype(BF16))
        o4 = jnp.dot(jnp.concatenate(ps, axis=0), v_n, preferred_element_type=F32)
        for g in range(GROUP_A):
            h = n * GROUP_A + g
            gate = ag_ref[:, h * DH_A:(h + 1) * DH_A]
            o_ref[:, h * DH_A:(h + 1) * DH_A] = (o4[g * bq:(g + 1) * bq] / ss[g] * _silu(gate)).astype(BF16)


def _dsa(qa, z, ka_all, va_all, kidx_all, kidx_col, bias_tiles, acc, *, bq, l_true, q0, ksel, row0, rows, lp):
    b, s, _ = qa.shape
    bias_tiles, far = bias_tiles
    nd = bias_tiles.shape[1] + far
    blk0 = row0 // bq
    assert q0 % LANE == 0 and (bq == LANE or s == bq)
    assert nd == (q0 + s - bq) // LANE + 1
    kern = functools.partial(_dsa_kernel, bq=bq, lp=lp, l_true=l_true, q0=q0, ksel=ksel, nd=nd, far=far, blk0=blk0)

    def zspec(width, off):
        return pl.BlockSpec((None, bq, width), lambda bi, qi: (bi, qi + blk0, off // width))

    def kspec(width):
        return pl.BlockSpec((None, lp, width), lambda bi, qi: (bi, 0, 0))

    return pl.pallas_call(
        kern,
        grid=(b, rows // bq),
        in_specs=[pl.BlockSpec((None, bq, 1024), lambda bi, qi: (bi, qi + blk0, 0)),
                  zspec(1024, E_IQ), zspec(128, E_IDX), zspec(1024, E_AG),
                  kspec(256), kspec(256),
                  pl.BlockSpec((None, lp, LANE), lambda bi, qi: (bi, 0, kidx_col)),
                  _const_spec(bias_tiles.shape),
                  pl.BlockSpec(memory_space=pl.ANY)],
        out_specs=pl.BlockSpec((None, bq, 1024), lambda bi, qi: (bi, qi + blk0, 0)),
        out_shape=jax.ShapeDtypeStruct(acc.shape, acc.dtype),
        input_output_aliases={8: 0},
        scratch_shapes=[pltpu.VMEM((bq, lp), jnp.int32)] * 3,
        compiler_params=_params("parallel", "arbitrary"),
        name="dsa",
    )(qa, z, z, z, ka_all, va_all, kidx_all, bias_tiles, acc)


def _causal_classes(s, lp, q0):
    if q0 == 0 and s % CLASS_ROWS == 0 and lp == s:
        return [(c * CLASS_ROWS, CLASS_ROWS, (c + 1) * CLASS_ROWS) for c in range(s // CLASS_ROWS)]
    return [(0, s, lp)]


def _mla_kernel(qb_ref, bg_ref, kb_ref, vb_ref, acc_ref, o_ref, *, bq, lp, l_true, q0, blk0, n_full):
    q_start = q0 + (pl.program_id(1) + blk0) * bq
    qpos = q_start + lax.broadcasted_iota(jnp.int32, (bq, 1), 0)
    kpos = n_full + lax.broadcasted_iota(jnp.int32, (1, lp - n_full), 1)
    valid = ((kpos >> CHUNK_SHIFT) <= (qpos >> CHUNK_SHIFT)) & (kpos < l_true)
    for h in range(H_B):
        q = qb_ref[:, h * 256:(h + 1) * 256]
        parts = []
        if n_full:
            parts.append((lax.dot_general(q, kb_ref[:n_full, h * 256:(h + 1) * 256], NT_DIMS,
                                          preferred_element_type=F32), vb_ref[:n_full, h * V_DIM:(h + 1) * V_DIM]))
        lg = lax.dot_general(q, kb_ref[n_full:, h * 256:(h + 1) * 256], NT_DIMS, preferred_element_type=F32)
        parts.append((jnp.where(valid, lg, -jnp.inf), vb_ref[n_full:, h * V_DIM:(h + 1) * V_DIM]))
        o = _softmax_pv(parts, QK_B ** -0.5)
        gate = bg_ref[:, h * V_DIM:(h + 1) * V_DIM]
        o_ref[:, h * V_DIM:(h + 1) * V_DIM] = (o * _silu(gate)).astype(BF16)


def _mla(qb, z, kb, vb, acc, *, bq, l_true, q0, row0, rows, lp):
    b = qb.shape[0]
    blk0 = row0 // bq
    n_full = min(min(q0 + row0 + CHUNK, l_true) // LANE * LANE, lp - LANE)
    kern = functools.partial(_mla_kernel, bq=bq, lp=lp, l_true=l_true, q0=q0, blk0=blk0, n_full=n_full)
    return pl.pallas_call(
        kern,
        grid=(b, rows // bq),
        in_specs=[pl.BlockSpec((None, bq, 2048), lambda bi, qi: (bi, qi + blk0, 0)),
                  pl.BlockSpec((None, bq, 1024), lambda bi, qi: (bi, qi + blk0, E_BG // 1024)),
                  pl.BlockSpec((None, lp, 2048), lambda bi, qi: (bi, 0, 0)),
                  pl.BlockSpec((None, lp, 1024), lambda bi, qi: (bi, 0, 0)),
                  pl.BlockSpec(memory_space=pl.ANY)],
        out_specs=pl.BlockSpec((None, bq, 1024), lambda bi, qi: (bi, qi + blk0, 0)),
        out_shape=jax.ShapeDtypeStruct(acc.shape, acc.dtype),
        input_output_aliases={4: 0},
        compiler_params=_params("parallel", "arbitrary"),
        name="mla",
    )(qb, z, kb, vb, acc)


def _out_ple_kernel(h_ref, oa_ref, ob_ref, p_ref, wo_ref, wg_ref, wp_ref, o_ref):
    half = oa_ref.shape[-1]
    h1 = (h_ref[...]
          + jnp.dot(oa_ref[...], wo_ref[:half, :], preferred_element_type=F32)
          + jnp.dot(ob_ref[...], wo_ref[half:, :], preferred_element_type=F32))
    r = (h1 * _rs(h1)).astype(BF16)
    gate = 1.0 / (1.0 + jnp.exp(-jnp.dot(r, wg_ref[...], preferred_element_type=F32)))
    o_ref[...] = h1 + gate * jnp.dot(p_ref[...].astype(BF16), wp_ref[...], preferred_element_type=F32)


def _out_ple(h, oa, ob, p_all, layer, wo, wg, wp, bm):
    m, d = h.shape
    half = oa.shape[1]

    def rows(width):
        return pl.BlockSpec((bm, width), lambda i: (i, 0))

    return pl.pallas_call(
        _out_ple_kernel,
        grid=(m // bm,),
        in_specs=[rows(d), rows(half), rows(half), pl.BlockSpec((None, bm, PLE_DIM), lambda i: (layer, i, 0)),
                  _const_spec((2 * half, d)), _const_spec((d, d)), _const_spec((PLE_DIM, d))],
        out_specs=rows(d),
        out_shape=jax.ShapeDtypeStruct((m, d), F32),
        compiler_params=_params("parallel"),
        name="out_ple",
    )(h, oa, ob, p_all, wo, wg, wp)


def _odd_prep_kernel(cq_ref, ck_ref, cv_ref, dv_ref, du_ref, dgate_ref, cqn_ref, ckn_ref, dg_ref, db_ref,
                     ws_ref, bs_ref, qc_o, kcb_o, vcb_o, od_o, kct_o, cvt_o, *maybe_dvn_o, n):
    for h in range(H_C):
        sl = slice(h * DH_C, (h + 1) * DH_C)
        x = cq_ref[:, sl]
        qc_o[:, sl] = (x * _rs(x) * cqn_ref[...]).astype(BF16)
        x = ck_ref[:, sl]
        kn = x * _rs(x) * ckn_ref[...]
        kct_o[:, sl] = kn
        kcb_o[:, sl] = kn.astype(BF16)
    cv = cv_ref[...]
    cvt_o[...] = cv
    vcb_o[...] = cv.astype(BF16)
    dv = dv_ref[...]
    xc = dv - jnp.mean(dv, -1, keepdims=True)
    var = jnp.mean(xc * xc, -1, keepdims=True)
    dvn = xc * lax.rsqrt(var + 1e-5) * dg_ref[...] + db_ref[...]
    for dvn_o in maybe_dvn_o:
        dvn_o[...] = dvn
    dvn = dvn.astype(BF16)
    row = lax.broadcasted_iota(jnp.int32, (n, n), 0)
    col = lax.broadcasted_iota(jnp.int32, (n, n), 1)
    for g in range(G_D):
        sl = slice(g * DG_D, (g + 1) * DG_D)
        w = jnp.where(col <= row, ws_ref[g], 0.0).astype(BF16)
        for c in range(dv.shape[0] // n):
            rs = slice(c * n, (c + 1) * n)
            sg = jnp.dot(w, dvn[rs, sl], preferred_element_type=F32) + bs_ref[:, g:g + 1]
            od_o[rs, sl] = (du_ref[rs, sl] * sg * _silu(dgate_ref[rs, sl])).astype(BF16)


def _odd_prep(z, cqn, ckn, dg, db, ws, bs_t, bs, n, keep, want_dvn):
    b, s, _ = z.shape
    first_tail = (s - keep) // bs
    assert keep % bs == 0

    def zspec(off):
        return pl.BlockSpec((None, bs, 1024), lambda bi, si: (bi, si, off // 1024))

    ospec = pl.BlockSpec((None, bs, 1024), lambda bi, si: (bi, si, 0))
    tspec = pl.BlockSpec((None, bs, 1024), lambda bi, si: (bi, jnp.maximum(si - first_tail, 0), 0))
    act = lambda dt: jax.ShapeDtypeStruct((b, s, 1024), dt)
    tail = jax.ShapeDtypeStruct((b, keep, 1024), F32)
    return pl.pallas_call(
        functools.partial(_odd_prep_kernel, n=n),
        grid=(b, s // bs),
        in_specs=[zspec(O_CQ), zspec(O_CK), zspec(O_CV), zspec(O_DV), zspec(O_DU), zspec(O_DG),
                  _const_spec((1, DH_C)), _const_spec((1, DH_C)), _const_spec((1, W_D)), _const_spec((1, W_D)),
                  _const_spec((G_D, n, n)), _const_spec((n, G_D))],
        out_specs=[ospec] * 4 + [tspec] * 2 + [ospec] * want_dvn,
        out_shape=[act(BF16)] * 4 + [tail] * 2 + [act(F32)] * want_dvn,
        compiler_params=_params("parallel", "arbitrary"),
        name="odd_prep",
    )(z, z, z, z, z, z, cqn, ckn, dg, db, ws, bs_t)


BACK_TILES = C_BACK * CHUNK // LANE


def _band_width(bq):
    return (BACK_TILES + -(-bq // LANE)) * LANE


def _band_kernel(q_ref, cg_ref, k_ref, v_ref, *rest, bq, sliding, q0, k0, k_end, c0):
    i = pl.program_id(1)
    q_start = q0 + i * bq
    bw = _band_width(bq)
    if sliding:
        cbias_ref, bias_ref, o_ref = rest
        first = i * (bq // LANE) - BACK_TILES
        win_start = first * LANE
        rows = [pl.ds(pl.multiple_of(jnp.maximum(first + t, 0) * LANE, LANE), LANE) for t in range(bw // LANE)]
        kw = jnp.concatenate([k_ref[r, :] for r in rows], axis=0)
        vw = jnp.concatenate([v_ref[r, :] for r in rows], axis=0)
    else:
        kn_ref, vn_ref, cbias_ref, bias_ref, o_ref = rest
        win_start = k0
        pad = jnp.zeros((bw - k_ref.shape[0] - bq, k_ref.shape[1]), BF16)
        kw = jnp.concatenate([k_ref[...].astype(BF16), kn_ref[...], pad], axis=0)
        vw = jnp.concatenate([v_ref[...].astype(BF16), vn_ref[...], pad], axis=0)
    qpos = q_start + lax.broadcasted_iota(jnp.int32, (bq, 1), 0)

    def valid(lo, hi):
        kpos = win_start + lo + lax.broadcasted_iota(jnp.int32, (1, hi - lo), 1)
        dc = (qpos >> CHUNK_SHIFT) - (kpos >> CHUNK_SHIFT)
        return (dc >= 0) & (dc <= C_BACK) & (kpos >= 0) & (kpos < k_end)

    segs = [(lo, hi, valid(lo, hi)) for lo, hi in ((0, c0), (c0, bw)) if hi > lo]
    for h in range(H_C):
        sl = slice(h * DH_C, (h + 1) * DH_C)
        q = q_ref[:, sl]
        parts = []
        for lo, hi, ok in segs:
            bias = cbias_ref[h][:, :1] if hi <= c0 else bias_ref[h]
            lg = lax.dot_general(q, kw[lo:hi, sl], NT_DIMS, preferred_element_type=F32) * DH_C ** -0.5
            parts.append((jnp.where(ok, lg + bias, -jnp.inf), vw[lo:hi, sl]))
        o = _softmax_pv(parts)
        o_ref[:, sl] = (o * _silu(cg_ref[:, sl])).astype(BF16)


def _band(qc, z, keys, vals, bias, *, bq, sliding, q0, k0, k_end):
    c0, cbias, bias = bias
    b, s, _ = qc.shape
    if sliding:
        assert bq % LANE == 0 and q0 == 0 and k0 == 0
        kv = [keys, vals]
    else:
        assert keys[0].shape[1] + bq <= _band_width(bq) and s == bq
        kv = [keys[0], vals[0], keys[1], vals[1]]
    kern = functools.partial(_band_kernel, bq=bq, sliding=sliding, q0=q0, k0=k0, k_end=k_end, c0=c0)
    return pl.pallas_call(
        kern,
        grid=(b, s // bq),
        in_specs=[pl.BlockSpec((None, bq, 1024), lambda bi, qi: (bi, qi, 0)),
                  pl.BlockSpec((None, bq, 1024), lambda bi, qi: (bi, qi, O_CG // 1024))]
                 + [pl.BlockSpec((None,) + a.shape[1:], lambda bi, qi: (bi, 0, 0)) for a in kv]
                 + [_const_spec(cbias.shape), _const_spec(bias.shape)],
        out_specs=pl.BlockSpec((None, bq, 1024), lambda bi, qi: (bi, qi, 0)),
        out_shape=jax.ShapeDtypeStruct((b, s, 1024), BF16),
        compiler_params=_params("parallel", "arbitrary"),
        name="band",
    )(qc, z, *kv, cbias, bias)


def _rope_tables(pos):
    half = ROPE_DIM // 2
    freq = ROPE_BASE ** (-jnp.arange(half, dtype=F32) / half)
    ang = pos.astype(F32)[:, None] * freq[None, :]
    cos, sin = jnp.cos(ang), jnp.sin(ang)
    z = jnp.zeros_like(cos)
    return jnp.concatenate([cos, z, cos, z], 1), jnp.concatenate([-sin, z, sin, z], 1)


def _rope_lanes(x):
    half = ROPE_DIM // 2
    z = jnp.zeros(x.shape[:-1] + (half,), x.dtype)
    return jnp.concatenate([x[..., :half], z, x[..., half:], z], -1)


def _t5_bucket_np(rel):
    nb = T5_BUCKETS // 2
    max_exact = nb // 2
    n = np.abs(rel)
    nf = np.maximum(n, 1).astype(np.float64)
    large = max_exact + (np.log(nf / max_exact) / math.log(T5_MAX_DIST / max_exact) * (nb - max_exact)).astype(np.int64)
    large = np.minimum(large, nb - 1)
    return np.where(rel > 0, nb, 0) + np.where(n < max_exact, n, large)


def _toeplitz(w, rows, width, cols):
    flat = jnp.tile(w, (1,) * (w.ndim - 1) + (rows,))[..., :rows * width]
    return flat.reshape(w.shape[:-1] + (rows, width))[..., :cols]


def _t5_tiles(t5_bias, bq, nd):
    width = 2 * LANE
    k = np.arange(width + 1)
    delta = np.where(k < LANE, k, k - (width + 1))
    rel = (np.arange(nd) - (nd - 1))[:, None] * LANE + delta[None, :]
    bucket = _t5_bucket_np(rel)
    far = 0
    while far + 1 < nd and np.array_equal(bucket[far + 1], bucket[0]):
        far += 1
    w = jnp.transpose(t5_bias[bucket[far:]], (2, 0, 1))
    return _toeplitz(w, bq, width, LANE), far


def _band_bias(rel_tab, bq, qk_off):
    bw = _band_width(bq)
    rel_index = lambda rel: np.clip(rel, -(CHUNK - 1), REL_CLIP) + (CHUNK - 1)
    full = rel_index(qk_off + np.arange(bq)[:, None] - np.arange(bw)[None, :])
    c0 = 0
    while c0 + 2 * MXU_COLS <= bw and np.all(full[:, :c0 + MXU_COLS] == full[0, 0]):
        c0 += MXU_COLS
    const = jnp.broadcast_to(rel_tab[full[0, 0]][:, None, None], (H_C, 1, LANE))
    wv = bw - c0
    width = wv + bq
    k = np.arange(width + 1)
    delta = np.where(k < wv, k, k - (width + 1))
    idx = rel_index(qk_off - c0 - delta)
    return c0, const, _toeplitz(jnp.transpose(rel_tab[idx], (1, 0)), bq, width, wv)


def _even_weights(w_in, b_wuq, b_wukv, b_qn, b_kn):
    d = w_in.shape[0]
    offs = np.cumsum((0,) + EVEN_SPLITS)
    w16 = w_in.astype(BF16)
    aq, ak, av, ag, iq, ik, iw, bcq, bckv, bkpe, bg = [w16[:, offs[t]:offs[t + 1]] for t in range(11)]
    slab_idx = jnp.concatenate([ik, iw, jnp.zeros((d, LANE - D_IDX - H_IDX), BF16)], 1)
    w = jnp.concatenate([aq, ag, iq, bg, bcq, ak, av, bckv, slab_idx, _rope_lanes(bkpe)], 1)
    uq = b_wuq.reshape(Q_LORA, H_B, QK_B)
    uq = jnp.concatenate([uq[..., :NOPE], _rope_lanes(uq[..., NOPE:])], -1).reshape(Q_LORA, H_B * 256).astype(BF16)
    ukv = b_wukv.reshape(KV_LORA, H_B, NOPE + V_DIM)
    ukv = jnp.concatenate([ukv[..., :NOPE].reshape(KV_LORA, H_B * NOPE),
                           ukv[..., NOPE:].reshape(KV_LORA, H_B * V_DIM)], 1).astype(BF16)
    pad_gain = lambda g: jnp.concatenate([g[:NOPE], _rope_lanes(g[NOPE:])])[None, :]
    return w, uq, ukv, pad_gain(b_qn), pad_gain(b_kn)


def _pad_rows(x, lp):
    return jnp.pad(x, ((0, 0), (0, lp - x.shape[1]), (0, 0)))


def _even_layer(h, p_all, layer, past, q0, ln_g, w_in, uq, ukv, a_qn, a_kn, t5_bias, b_qln, b_kvln, bqn, bkn, wo, wg, wp):
    b, s, d = h.shape
    m = b * s
    z = _norm_mm(h.reshape(m, d), ln_g[None, :], w_in, min(m, 1024), E_END // 4).reshape(b, s, E_END)
    cos, sin = _rope_tables(q0 + jnp.arange(s, dtype=jnp.int32))
    bs = min(s, 256)
    qa, ka, qb, ckv, kpe_l, kpe64, idx64 = _even_prep(z, cos, sin, a_qn[None, :], a_kn[None, :], b_qln[None, :],
                                                      b_kvln[None, :], uq, bqn, bs)
    av = z[..., E_AV:E_AV + 256]
    new = (ka.reshape(b, s, HKV_A, DH_A), av.reshape(b, s, HKV_A, DH_A), idx64, ckv, kpe64)
    if past is None:
        l_true = s
        ka_all, va_all, ckv_all, kpe_all = ka, av, ckv, kpe_l
        kidx_all, kidx_col = z, E_IDX // LANE
    else:
        kidx = z[..., E_IDX:E_IDX + LANE]
        kidx_col = 0
        c_k, c_v, c_ik, c_ckv, c_kpe = past
        pl_ = c_k.shape[1]
        l_true = pl_ + s
        lp = -(-l_true // LANE) * LANE
        cat = lambda c, n_: _pad_rows(jnp.concatenate([c, n_], 1), lp)
        ka_all = cat(c_k.reshape(b, pl_, 256), ka)
        va_all = cat(c_v.reshape(b, pl_, 256), av)
        kidx_all = cat(jnp.pad(c_ik, ((0, 0), (0, 0), (0, LANE - D_IDX))), kidx)
        ckv_all = cat(c_ckv, ckv)
        kpe_all = cat(_rope_lanes(c_kpe), kpe_l)
    lp = ka_all.shape[1]
    ksel = min(TOPK_MAX, l_true // 4)
    bq = min(s, LANE)
    nd = (q0 + s - bq) // LANE + 1
    tiles = _t5_tiles(t5_bias, bq, nd)
    classes = _causal_classes(s, lp, q0)
    o_a = jnp.zeros((b, s, H_A * DH_A), BF16)
    for r0, nr, lc in classes:
        o_a = _dsa(qa, z, ka_all, va_all, kidx_all, kidx_col, tiles, o_a, bq=bq, l_true=l_true, q0=q0, ksel=ksel,
                   row0=r0, rows=nr, lp=lc)
    kb, vb = _mla_kv(ckv_all, kpe_all, ukv, bkn, _key_chunk(lp))
    o_b = jnp.zeros((b, s, H_B * V_DIM), BF16)
    for r0, nr, lc in classes:
        o_b = _mla(qb, z, kb, vb, o_b, bq=min(s, 256), l_true=l_true, q0=q0, row0=r0, rows=nr, lp=lc)
    y = _out_ple(h.reshape(m, d), o_a.reshape(m, -1), o_b.reshape(m, -1), p_all, layer, wo, wg, wp, min(m, 256))
    return y.reshape(b, s, d), new


def _odd_layer(h, p_all, layer, past, q0, ln_g, w_in, c_qn, c_kn, c_rel, d_g, d_b, d_ws, d_bs, wo, wg, wp):
    b, s, d = h.shape
    m = b * s
    z = _norm_mm(h.reshape(m, d), ln_g[None, :], w_in, min(m, 1024), 1024).reshape(b, s, O_END)
    n = min(s, D_CHUNK)
    keep = min(C_BACK * CHUNK, s) if past is None else s
    qc, kcb, vcb, o_d, kct, cvt, *dvn = _odd_prep(z, c_qn[None, :], c_kn[None, :], d_g[None, :], d_b[None, :],
                                                  d_ws[:, :n, :n], d_bs[:, :n].T, min(s, 256), n, keep,
                                                  want_dvn=past is not None)
    c_new = (kct.reshape(b, keep, H_C, DH_C), cvt.reshape(b, keep, H_C, DH_C))
    if past is None:
        bqc = 2 * LANE
        bias = _band_bias(c_rel, bqc, C_BACK * CHUNK)
        o_c = _band(qc, z, kcb, vcb, bias, bq=bqc, sliding=True, q0=0, k0=0, k_end=s)
    else:
        nc = past[0].shape[1]
        bias = _band_bias(c_rel, s, nc)
        o_c = _band(qc, z, (past[0].reshape(b, nc, 1024), kcb), (past[1].reshape(b, nc, 1024), vcb), bias,
                    bq=s, sliding=False, q0=q0, k0=q0 - nc, k_end=q0 + s)
    y = _out_ple(h.reshape(m, d), o_c.reshape(m, -1), o_d.reshape(m, -1), p_all, layer, wo, wg, wp, min(m, 256))
    return y.reshape(b, s, d), c_new, (dvn[0] if dvn else None)


def kernel(x_prompt, x_sample, cache_a_k, cache_a_v, cache_a_idx_k, cache_b_ckv, cache_b_kpe, cache_c_k, cache_c_v, p_prompt, p_sample, ln_g, w_in_even, a_q_norm, a_k_norm, t5_bias, b_q_lora_norm, b_kv_lora_norm, b_w_uq, b_w_ukv, b_q_norm, b_k_norm, w_out_even, w_in_odd, c_q_norm, c_k_norm, c_rel_bias, d_ln_g, d_ln_b, d_w_s, d_b_s, w_out_odd, ple_proj, ple_gate):
    depth = ln_g.shape[0]
    past_len = cache_a_k.shape[2]
    hp, hs = x_prompt, x_sample
    pp = p_prompt.reshape(depth, -1, PLE_DIM)
    ps = p_sample.reshape(depth, -1, PLE_DIM)
    ev_p, ev_s, od_p, od_s, dv_s = [], [], [], [], []
    for i in range(depth):
        j = i // 2
        wg = ple_gate[i].astype(BF16)
        wp = ple_proj[i].astype(BF16)
        if i % 2 == 0:
            w_in, uq, ukv, bqn, bkn = _even_weights(w_in_even[j], b_w_uq[j], b_w_ukv[j], b_q_norm[j], b_k_norm[j])
            w = (ln_g[i], w_in, uq, ukv, a_q_norm[j], a_k_norm[j], t5_bias, b_q_lora_norm[j], b_kv_lora_norm[j],
                 bqn, bkn, w_out_even[j].astype(BF16), wg, wp)
            hp, sp = _even_layer(hp, pp, i, None, 0, *w)
            past = (cache_a_k[j], cache_a_v[j], cache_a_idx_k[j], cache_b_ckv[j], cache_b_kpe[j])
            hs, ss = _even_layer(hs, ps, i, past, past_len, *w)
            ev_p.append(sp)
            ev_s.append(ss)
        else:
            w = (ln_g[i], w_in_odd[j].astype(BF16), c_q_norm[j], c_k_norm[j], c_rel_bias[j], d_ln_g[j], d_ln_b[j],
                 d_w_s[j], d_b_s[j], w_out_odd[j].astype(BF16), wg, wp)
            hp, sp, _ = _odd_layer(hp, pp, i, None, 0, *w)
            hs, ss, dvs = _odd_layer(hs, ps, i, (cache_c_k[j], cache_c_v[j]), past_len, *w)
            od_p.append(sp)
            od_s.append(ss)
            dv_s.append(dvs)
    st = lambda lst, n_: jnp.stack([e[n_] for e in lst], 0)
    return (hp, hs, st(ev_p, 0), st(ev_p, 1), st(ev_p, 2), st(ev_p, 3), st(ev_p, 4), st(od_p, 0), st(od_p, 1),
            st(ev_s, 0), st(ev_s, 1), st(ev_s, 2), st(ev_s, 3), st(ev_s, 4), st(od_s, 0), st(od_s, 1),
            jnp.stack(dv_s, 0))
```

```python
import functools
import math

import numpy as np
import jax
import jax.numpy as jnp
from jax import lax
from jax.experimental import pallas as pl
from jax.experimental.pallas import tpu as pltpu

F32 = jnp.float32
BF16 = jnp.bfloat16
INT_MIN = -2 ** 31
LOG2E = math.log2(math.e)
CLASS_ROWS = 256
SEARCH_GROUPS = 4
DSA_WIDE_BLOCK_KEYS = 1024

D_MODEL = 2048
CHUNK = 64
CHUNK_SHIFT = 6
LANE = 128
MXU_COLS = 256
H_A, HKV_A, GROUP_A, DH_A = 8, 2, 4, 128
H_IDX, D_IDX = 16, 64
TOPK_MAX = 256
T5_BUCKETS, T5_MAX_DIST = 32, 128
H_B, Q_LORA, KV_LORA, NOPE, ROPE_DIM, V_DIM = 8, 512, 256, 128, 64, 128
ROPE_BASE = 10000.0
QK_B = NOPE + ROPE_DIM
H_C, DH_C, C_BACK, REL_CLIP = 8, 128, 8, 128
W_D, G_D, DG_D, D_CHUNK = 1024, 8, 128, 128
PLE_DIM = 256

EVEN_SPLITS = (H_A * DH_A, HKV_A * DH_A, HKV_A * DH_A, H_A * DH_A, H_IDX * D_IDX, D_IDX, H_IDX,
               Q_LORA, KV_LORA, ROPE_DIM, H_B * V_DIM)
E_AQ, E_AG, E_IQ, E_BG, E_BCQ, E_AK, E_AV, E_CKV, E_IDX, E_KPE, E_END = (
    0, 1024, 2048, 3072, 4096, 4608, 4864, 5120, 5376, 5504, 5632)
O_CQ, O_CK, O_CV, O_CG, O_DU, O_DV, O_DG, O_END = 0, 1024, 2048, 3072, 4096, 5120, 6144, 7168

VMEM_LIMIT_BYTES = 56 * 1024 * 1024
NT_DIMS = (((1,), (1,)), ((), ()))


def _params(*sem):
    return pltpu.CompilerParams(dimension_semantics=sem, vmem_limit_bytes=VMEM_LIMIT_BYTES)


def _const_spec(shape):
    zeros = (0,) * len(shape)
    return pl.BlockSpec(shape, lambda *_: zeros, pipeline_mode=pl.Buffered(1))


def _rs(x, n=None, eps=1e-6):
    n = x.shape[-1] if n is None else n
    return lax.rsqrt(jnp.sum(x * x, axis=-1, keepdims=True) / n + eps)


def _silu(x):
    return x * (1.0 / (1.0 + jnp.exp(-x)))


def _softmax_pv(parts, scale=1.0):
    m = functools.reduce(jnp.maximum, [jnp.max(lg, axis=-1, keepdims=True) for lg, _ in parts])
    o = s = None
    for lg, v in parts:
        p = jnp.exp2((lg - m) * (scale * LOG2E))
        ps = jnp.sum(p, axis=-1, keepdims=True)
        po = jnp.dot(p.astype(BF16), v, preferred_element_type=F32)
        o, s = (po, ps) if o is None else (o + po, s + ps)
    return o / s


def _norm_mm_kernel(x_ref, g_ref, w_ref, o_ref, xn_ref):
    @pl.when(pl.program_id(1) == 0)
    def _():
        x = x_ref[...]
        xn_ref[...] = (x * _rs(x) * g_ref[...]).astype(BF16)

    o_ref[...] = jnp.dot(xn_ref[...], w_ref[...], preferred_element_type=F32)


def _norm_mm(x, g, w, bm, bn):
    m, d = x.shape
    n = w.shape[1]
    return pl.pallas_call(
        _norm_mm_kernel,
        grid=(m // bm, n // bn),
        in_specs=[pl.BlockSpec((bm, d), lambda i, j: (i, 0)),
                  pl.BlockSpec((1, d), lambda i, j: (0, 0)),
                  pl.BlockSpec((d, bn), lambda i, j: (0, j))],
        out_specs=pl.BlockSpec((bm, bn), lambda i, j: (i, j)),
        out_shape=jax.ShapeDtypeStruct((m, n), F32),
        scratch_shapes=[pltpu.VMEM((bm, d), BF16)],
        compiler_params=_params("parallel", "arbitrary"),
        name="norm_mm",
    )(x, g, w)


def _rope(x, cos, sin):
    return x * cos + pltpu.roll(x, 64, 1) * sin


def _even_prep_kernel(aq_ref, bcq_ref, ak_ref, ckv_ref, kpe_ref, idx_ref, cos_ref, sin_ref,
                      aqn_ref, akn_ref, qln_ref, kvln_ref, wuq_ref, bqn_ref,
                      qa_o, ka_o, qb_o, ckv_o, kpe_o, kpe64_o, idx64_o):
    for h in range(H_A):
        x = aq_ref[:, h * DH_A:(h + 1) * DH_A]
        qa_o[:, h * DH_A:(h + 1) * DH_A] = (x * _rs(x) * aqn_ref[...]).astype(BF16)
    for n in range(HKV_A):
        x = ak_ref[:, n * DH_A:(n + 1) * DH_A]
        ka_o[:, n * DH_A:(n + 1) * DH_A] = x * _rs(x) * akn_ref[...]
    c = ckv_ref[...]
    ckv_o[...] = c * _rs(c) * kvln_ref[...]
    cos = cos_ref[...]
    sin = sin_ref[...]
    kpe = _rope(kpe_ref[...], cos, sin)
    kpe_o[...] = kpe
    half = ROPE_DIM // 2
    kpe64_o[...] = jnp.concatenate([kpe[:, :half], kpe[:, 2 * half:3 * half]], axis=-1)
    idx64_o[...] = idx_ref[:, :D_IDX]
    cq = bcq_ref[...]
    cqn = (cq * _rs(cq) * qln_ref[...]).astype(BF16)
    qb = jnp.dot(cqn, wuq_ref[...], preferred_element_type=F32)
    g = bqn_ref[...]
    for h in range(H_B):
        nope = qb[:, h * 256:h * 256 + 128]
        rot = _rope(qb[:, h * 256 + 128:(h + 1) * 256], cos, sin)
        ss = jnp.sum(nope * nope + rot * rot, -1, keepdims=True)
        r = lax.rsqrt(ss / QK_B + 1e-6)
        qb_o[:, h * 256:h * 256 + 128] = (nope * r * g[:, :128]).astype(BF16)
        qb_o[:, h * 256 + 128:(h + 1) * 256] = (rot * r * g[:, 128:]).astype(BF16)


def _even_prep(z, cos, sin, aqn, akn, qln, kvln, wuq, bqn, bs):
    b, s, _ = z.shape

    def zspec(width, off):
        return pl.BlockSpec((None, bs, width), lambda bi, si: (bi, si, off // width))

    def ospec(width):
        return pl.BlockSpec((None, bs, width), lambda bi, si: (bi, si, 0))

    pos_spec = pl.BlockSpec((bs, LANE), lambda bi, si: (si, 0))
    return pl.pallas_call(
        _even_prep_kernel,
        grid=(b, s // bs),
        in_specs=[zspec(1024, E_AQ), zspec(512, E_BCQ), zspec(256, E_AK), zspec(256, E_CKV),
                  zspec(128, E_KPE), zspec(128, E_IDX), pos_spec, pos_spec,
                  _const_spec((1, DH_A)), _const_spec((1, DH_A)), _const_spec((1, Q_LORA)),
                  _const_spec((1, KV_LORA)), _const_spec((Q_LORA, H_B * 256)), _const_spec((1, 256))],
        out_specs=[ospec(1024), ospec(256), ospec(2048), ospec(256), ospec(128), ospec(ROPE_DIM), ospec(D_IDX)],
        out_shape=[jax.ShapeDtypeStruct((b, s, 1024), BF16), jax.ShapeDtypeStruct((b, s, 256), F32),
                   jax.ShapeDtypeStruct((b, s, 2048), BF16), jax.ShapeDtypeStruct((b, s, 256), F32),
                   jax.ShapeDtypeStruct((b, s, 128), F32), jax.ShapeDtypeStruct((b, s, ROPE_DIM), F32),
                   jax.ShapeDtypeStruct((b, s, D_IDX), F32)],
        compiler_params=_params("parallel", "arbitrary"),
        name="even_prep",
    )(z, z, z, z, z, z, cos, sin, aqn, akn, qln, kvln, wuq, bqn)


def _mla_kv_kernel(ckv_ref, kpe_ref, w_ref, g_ref, kb_o, vb_o):
    kv = jnp.dot(ckv_ref[...].astype(BF16), w_ref[...], preferred_element_type=F32)
    kp = kpe_ref[...]
    skp = jnp.sum(kp * kp, -1, keepdims=True)
    g = g_ref[...]
    for h in range(H_B):
        nope = kv[:, h * NOPE:(h + 1) * NOPE]
        r = lax.rsqrt((jnp.sum(nope * nope, -1, keepdims=True) + skp) / QK_B + 1e-6)
        kb_o[:, h * 256:h * 256 + 128] = (nope * r * g[:, :128]).astype(BF16)
        kb_o[:, h * 256 + 128:(h + 1) * 256] = (kp * r * g[:, 128:]).astype(BF16)
    vb_o[...] = kv[:, H_B * NOPE:].astype(BF16)


def _mla_kv(ckv_all, kpe_all, wukv, bkn, bl):
    b, lp, _ = ckv_all.shape
    return pl.pallas_call(
        _mla_kv_kernel,
        grid=(b, lp // bl),
        in_specs=[pl.BlockSpec((None, bl, KV_LORA), lambda bi, li: (bi, li, 0)),
                  pl.BlockSpec((None, bl, LANE), lambda bi, li: (bi, li, 0)),
                  _const_spec((KV_LORA, 2048)), _const_spec((1, 256))],
        out_specs=[pl.BlockSpec((None, bl, 2048), lambda bi, li: (bi, li, 0)),
                   pl.BlockSpec((None, bl, 1024), lambda bi, li: (bi, li, 0))],
        out_shape=[jax.ShapeDtypeStruct((b, lp, 2048), BF16), jax.ShapeDtypeStruct((b, lp, 1024), BF16)],
        compiler_params=_params("parallel", "arbitrary"),
        name="mla_kv",
    )(ckv_all, kpe_all, wukv, bkn)


def _key_chunk(lp):
    for c in (512, 384, 256, 128):
        if lp % c == 0:
            return c
    raise ValueError(lp)


def _dsa_kernel(qa_ref, iq_ref, sa_ref, ag_ref, ka_ref, va_ref, kidx_ref, bias_ref, acc_ref, o_ref,
                key_ref, hi_ref, lo_ref, *, bq, lp, l_true, q0, ksel, d_min, d_max, far, blk0):
    i = pl.program_id(1) + blk0
    q_start = q0 + i * bq
    qpos = q_start + lax.broadcasted_iota(jnp.int32, (bq, 1), 0)
    ck = _key_chunk(lp)

    iq = iq_ref[...].astype(BF16)
    a = jnp.concatenate([iq[:, p * LANE:(p + 1) * LANE] for p in range(H_IDX // 2)], axis=0)
    wi = sa_ref[...] * (H_IDX ** -0.5 * D_IDX ** -0.5)
    for c0 in range(0, lp, ck):
        kk = kidx_ref[c0:c0 + ck, :]
        lane = lax.broadcasted_iota(jnp.int32, kk.shape, 1)
        k_lo = jnp.where(lane < D_IDX, kk, 0.0).astype(BF16)
        k_hi = jnp.where(lane >= D_IDX, pltpu.roll(kk, D_IDX, 1), 0.0).astype(BF16)
        s_lo = lax.dot_general(a, k_lo, NT_DIMS, preferred_element_type=F32)
        s_hi = lax.dot_general(a, k_hi, NT_DIMS, preferred_element_type=F32)
        sc = jnp.zeros((bq, ck), F32)
        for p in range(H_IDX // 2):
            w0 = wi[:, D_IDX + 2 * p:D_IDX + 2 * p + 1]
            w1 = wi[:, D_IDX + 2 * p + 1:D_IDX + 2 * p + 2]
            sc = sc + w0 * jnp.maximum(s_lo[p * bq:(p + 1) * bq], 0.0)
            sc = sc + w1 * jnp.maximum(s_hi[p * bq:(p + 1) * bq], 0.0)
        kpos = c0 + lax.broadcasted_iota(jnp.int32, (1, ck), 1)
        valid = ((kpos >> CHUNK_SHIFT) <= (qpos >> CHUNK_SHIFT)) & (kpos < l_true)
        bits = lax.bitcast_convert_type(sc, jnp.int32)
        key = jnp.where(bits < 0, bits ^ 0x7FFFFFFF, bits)
        key = jnp.where(bits == INT_MIN, 0, key)
        key = jnp.where(valid, key, INT_MIN)
        key_ref[:, c0:c0 + ck] = key
        hi_ref[:, c0:c0 + ck] = (key >> 16) + 2 ** 15
        lo_ref[:, c0:c0 + ck] = key & 0xFFFF

    ng = SEARCH_GROUPS
    gr = bq // ng
    groups = [slice(r * gr, (r + 1) * gr) for r in range(ng)]

    def count_ge(ref, rows, cand):
        acc = None
        for j in range(lp // LANE):
            d = (ref[rows, j * LANE:(j + 1) * LANE] - cand) >> 31
            acc = d if acc is None else acc + d
        return lp + jnp.sum(acc.astype(F32), axis=-1, keepdims=True)

    def search16(ref, need):
        ts = [jnp.zeros((gr, 1), jnp.int32)] * ng
        for bit in reversed(range(16)):
            for r in range(ng):
                cand = ts[r] | (1 << bit)
                ts[r] = jnp.where(count_ge(ref, groups[r], cand) >= need[r], cand, ts[r])
        return ts

    if lp > ksel:
        t_hi = search16(hi_ref, (float(ksel),) * ng)
        need_lo = []
        for rows, t in zip(groups, t_hi):
            need_lo.append(ksel - count_ge(hi_ref, rows, t + 1))
            lo_ref[rows, :] = jnp.where(hi_ref[rows, :] == t, lo_ref[rows, :], -1)
        t_lo = search16(lo_ref, need_lo)
        t_s = jnp.concatenate([lax.shift_left(h - 2 ** 15, 16) | l for h, l in zip(t_hi, t_lo)], axis=0)
        key = key_ref[...]
        n_picked = jnp.sum(((key >= t_s) & (key > INT_MIN)).astype(F32), axis=-1, keepdims=True)

        @pl.when(jnp.max(n_picked) > ksel)
        def _():
            key = key_ref[...]
            rpos = (lp - 1) - lax.broadcasted_iota(jnp.int32, (bq, lp), 1)
            lo_ref[...] = jnp.where((key == t_s) & (key > INT_MIN), rpos, -1)
            need = [ksel - jnp.sum((key[rows] > t_s[rows]).astype(F32), axis=-1, keepdims=True) for rows in groups]
            t_pos = jnp.concatenate(search16(lo_ref, need), axis=0)
            tie = lo_ref[...]
            key_ref[...] = jnp.where((tie >= 0) & (tie < t_pos), INT_MIN, key)
    else:
        t_s = jnp.full((bq, 1), INT_MIN, jnp.int32)
    key = key_ref[...]
    sel = (key >= t_s) & (key > INT_MIN)

    d_base = -(q0 // LANE + i * (bq // LANE))
    nk = lp // LANE
    for n in range(HKV_A):
        q4 = jnp.concatenate([qa_ref[:, (n * GROUP_A + g) * DH_A:(n * GROUP_A + g + 1) * DH_A]
                              for g in range(GROUP_A)], axis=0)
        k_n = ka_ref[:, n * DH_A:(n + 1) * DH_A].astype(BF16)
        v_n = va_ref[:, n * DH_A:(n + 1) * DH_A].astype(BF16)
        lg4 = lax.dot_general(q4, k_n, NT_DIMS, preferred_element_type=F32) * DH_A ** -0.5
        ps, ss = [], []
        for g in range(GROUP_A):
            bias = jnp.concatenate([bias_ref[n * GROUP_A + g,
                                             jnp.maximum(jnp.minimum(d_base + j, d_max) - (d_min + far), 0)]
                                    for j in range(nk)], axis=1)
            lg = jnp.where(sel, lg4[g * bq:(g + 1) * bq] + bias, -jnp.inf)
            p = jnp.exp(lg - jnp.max(lg, axis=-1, keepdims=True))
            ss.append(jnp.sum(p, axis=-1, keepdims=True))
            ps.append(p.astype(BF16))
        o4 = jnp.dot(jnp.concatenate(ps, axis=0), v_n, preferred_element_type=F32)
        for g in range(GROUP_A):
            h = n * GROUP_A + g
            gate = ag_ref[:, h * DH_A:(h + 1) * DH_A]
            o_ref[:, h * DH_A:(h + 1) * DH_A] = (o4[g * bq:(g + 1) * bq] / ss[g] * _silu(gate)).astype(BF16)


def _dsa(qa, z, ka_all, va_all, kidx_all, kidx_col, bias_tiles, acc, *, bq, l_true, q0, ksel, row0, rows, lp):
    b, s, _ = qa.shape
    bias_tiles, d_min, d_max, far = bias_tiles
    blk0 = row0 // bq
    assert q0 % LANE == 0 and (bq % LANE == 0 or s == bq)
    assert d_min <= -((q0 + s - bq) // LANE) and bias_tiles.shape[2] == bq
    kern = functools.partial(_dsa_kernel, bq=bq, lp=lp, l_true=l_true, q0=q0, ksel=ksel,
                             d_min=d_min, d_max=d_max, far=far, blk0=blk0)

    def zspec(width, off):
        return pl.BlockSpec((None, bq, width), lambda bi, qi: (bi, qi + blk0, off // width))

    def kspec(width):
        return pl.BlockSpec((None, lp, width), lambda bi, qi: (bi, 0, 0))

    return pl.pallas_call(
        kern,
        grid=(b, rows // bq),
        in_specs=[pl.BlockSpec((None, bq, 1024), lambda bi, qi: (bi, qi + blk0, 0)),
                  zspec(1024, E_IQ), zspec(128, E_IDX), zspec(1024, E_AG),
                  kspec(256), kspec(256),
                  pl.BlockSpec((None, lp, LANE), lambda bi, qi: (bi, 0, kidx_col)),
                  _const_spec(bias_tiles.shape),
                  pl.BlockSpec(memory_space=pl.ANY)],
        out_specs=pl.BlockSpec((None, bq, 1024), lambda bi, qi: (bi, qi + blk0, 0)),
        out_shape=jax.ShapeDtypeStruct(acc.shape, acc.dtype),
        input_output_aliases={8: 0},
        scratch_shapes=[pltpu.VMEM((bq, lp), jnp.int32)] * 3,
        compiler_params=_params("parallel", "arbitrary"),
        name="dsa",
    )(qa, z, z, z, ka_all, va_all, kidx_all, bias_tiles, acc)


def _causal_classes(s, lp, q0):
    if q0 == 0 and s % CLASS_ROWS == 0 and lp == s:
        return [(c * CLASS_ROWS, CLASS_ROWS, (c + 1) * CLASS_ROWS) for c in range(s // CLASS_ROWS)]
    return [(0, s, lp)]


def _mla_kernel(qb_ref, bg_ref, kb_ref, vb_ref, acc_ref, o_ref, *, bq, lp, l_true, q0, blk0, n_full):
    q_start = q0 + (pl.program_id(1) + blk0) * bq
    qpos = q_start + lax.broadcasted_iota(jnp.int32, (bq, 1), 0)
    kpos = n_full + lax.broadcasted_iota(jnp.int32, (1, lp - n_full), 1)
    valid = ((kpos >> CHUNK_SHIFT) <= (qpos >> CHUNK_SHIFT)) & (kpos < l_true)
    for h in range(H_B):
        q = qb_ref[:, h * 256:(h + 1) * 256]
        parts = []
        if n_full:
            parts.append((lax.dot_general(q, kb_ref[:n_full, h * 256:(h + 1) * 256], NT_DIMS,
                                          preferred_element_type=F32), vb_ref[:n_full, h * V_DIM:(h + 1) * V_DIM]))
        lg = lax.dot_general(q, kb_ref[n_full:, h * 256:(h + 1) * 256], NT_DIMS, preferred_element_type=F32)
        parts.append((jnp.where(valid, lg, -jnp.inf), vb_ref[n_full:, h * V_DIM:(h + 1) * V_DIM]))
        o = _softmax_pv(parts, QK_B ** -0.5)
        gate = bg_ref[:, h * V_DIM:(h + 1) * V_DIM]
        o_ref[:, h * V_DIM:(h + 1) * V_DIM] = (o * _silu(gate)).astype(BF16)


def _mla(qb, z, kb, vb, acc, *, bq, l_true, q0, row0, rows, lp):
    b = qb.shape[0]
    blk0 = row0 // bq
    n_full = min(min(q0 + row0 + CHUNK, l_true) // LANE * LANE, lp - LANE)
    kern = functools.partial(_mla_kernel, bq=bq, lp=lp, l_true=l_true, q0=q0, blk0=blk0, n_full=n_full)
    return pl.pallas_call(
        kern,
        grid=(b, rows // bq),
        in_specs=[pl.BlockSpec((None, bq, 2048), lambda bi, qi: (bi, qi + blk0, 0)),
                  pl.BlockSpec((None, bq, 1024), lambda bi, qi: (bi, qi + blk0, E_BG // 1024)),
                  pl.BlockSpec((None, lp, 2048), lambda bi, qi: (bi, 0, 0)),
                  pl.BlockSpec((None, lp, 1024), lambda bi, qi: (bi, 0, 0)),
                  pl.BlockSpec(memory_space=pl.ANY)],
        out_specs=pl.BlockSpec((None, bq, 1024), lambda bi, qi: (bi, qi + blk0, 0)),
        out_shape=jax.ShapeDtypeStruct(acc.shape, acc.dtype),
        input_output_aliases={4: 0},
        compiler_params=_params("parallel", "arbitrary"),
        name="mla",
    )(qb, z, kb, vb, acc)


def _out_ple_kernel(h_ref, oa_ref, ob_ref, p_ref, wo_ref, wg_ref, wp_ref, o_ref):
    half = oa_ref.shape[-1]
    h1 = (h_ref[...]
          + jnp.dot(oa_ref[...], wo_ref[:half, :], preferred_element_type=F32)
          + jnp.dot(ob_ref[...], wo_ref[half:, :], preferred_element_type=F32))
    r = (h1 * _rs(h1)).astype(BF16)
    gate = 1.0 / (1.0 + jnp.exp(-jnp.dot(r, wg_ref[...], preferred_element_type=F32)))
    o_ref[...] = h1 + gate * jnp.dot(p_ref[...].astype(BF16), wp_ref[...], preferred_element_type=F32)


def _out_ple(h, oa, ob, p_all, layer, wo, wg, wp, bm):
    m, d = h.shape
    half = oa.shape[1]

    def rows(width):
        return pl.BlockSpec((bm, width), lambda i: (i, 0))

    return pl.pallas_call(
        _out_ple_kernel,
        grid=(m // bm,),
        in_specs=[rows(d), rows(half), rows(half), pl.BlockSpec((None, bm, PLE_DIM), lambda i: (layer, i, 0)),
                  _const_spec((2 * half, d)), _const_spec((d, d)), _const_spec((PLE_DIM, d))],
        out_specs=rows(d),
        out_shape=jax.ShapeDtypeStruct((m, d), F32),
        compiler_params=_params("parallel"),
        name="out_ple",
    )(h, oa, ob, p_all, wo, wg, wp)


def _odd_prep_kernel(cq_ref, ck_ref, cv_ref, dv_ref, du_ref, dgate_ref, cqn_ref, ckn_ref, dg_ref, db_ref,
                     ws_ref, bs_ref, qc_o, kcb_o, vcb_o, od_o, kct_o, cvt_o, *maybe_dvn_o, n):
    for h in range(H_C):
        sl = slice(h * DH_C, (h + 1) * DH_C)
        x = cq_ref[:, sl]
        qc_o[:, sl] = (x * _rs(x) * cqn_ref[...]).astype(BF16)
        x = ck_ref[:, sl]
        kn = x * _rs(x) * ckn_ref[...]
        kct_o[:, sl] = kn
        kcb_o[:, sl] = kn.astype(BF16)
    cv = cv_ref[...]
    cvt_o[...] = cv
    vcb_o[...] = cv.astype(BF16)
    dv = dv_ref[...]
    xc = dv - jnp.mean(dv, -1, keepdims=True)
    var = jnp.mean(xc * xc, -1, keepdims=True)
    dvn = xc * lax.rsqrt(var + 1e-5) * dg_ref[...] + db_ref[...]
    for dvn_o in maybe_dvn_o:
        dvn_o[...] = dvn
    dvn = dvn.astype(BF16)
    row = lax.broadcasted_iota(jnp.int32, (n, n), 0)
    col = lax.broadcasted_iota(jnp.int32, (n, n), 1)
    for g in range(G_D):
        sl = slice(g * DG_D, (g + 1) * DG_D)
        w = jnp.where(col <= row, ws_ref[g], 0.0).astype(BF16)
        for c in range(dv.shape[0] // n):
            rs = slice(c * n, (c + 1) * n)
            sg = jnp.dot(w, dvn[rs, sl], preferred_element_type=F32) + bs_ref[:, g:g + 1]
            od_o[rs, sl] = (du_ref[rs, sl] * sg * _silu(dgate_ref[rs, sl])).astype(BF16)


def _odd_prep(z, cqn, ckn, dg, db, ws, bs_t, bs, n, keep, want_dvn):
    b, s, _ = z.shape
    first_tail = (s - keep) // bs
    assert keep % bs == 0

    def zspec(off):
        return pl.BlockSpec((None, bs, 1024), lambda bi, si: (bi, si, off // 1024))

    ospec = pl.BlockSpec((None, bs, 1024), lambda bi, si: (bi, si, 0))
    tspec = pl.BlockSpec((None, bs, 1024), lambda bi, si: (bi, jnp.maximum(si - first_tail, 0), 0))
    act = lambda dt: jax.ShapeDtypeStruct((b, s, 1024), dt)
    tail = jax.ShapeDtypeStruct((b, keep, 1024), F32)
    return pl.pallas_call(
        functools.partial(_odd_prep_kernel, n=n),
        grid=(b, s // bs),
        in_specs=[zspec(O_CQ), zspec(O_CK), zspec(O_CV), zspec(O_DV), zspec(O_DU), zspec(O_DG),
                  _const_spec((1, DH_C)), _const_spec((1, DH_C)), _const_spec((1, W_D)), _const_spec((1, W_D)),
                  _const_spec((G_D, n, n)), _const_spec((n, G_D))],
        out_specs=[ospec] * 4 + [tspec] * 2 + [ospec] * want_dvn,
        out_shape=[act(BF16)] * 4 + [tail] * 2 + [act(F32)] * want_dvn,
        compiler_params=_params("parallel", "arbitrary"),
        name="odd_prep",
    )(z, z, z, z, z, z, cqn, ckn, dg, db, ws, bs_t)


BACK_TILES = C_BACK * CHUNK // LANE


def _band_width(bq):
    return (BACK_TILES + -(-bq // LANE)) * LANE


def _band_kernel(q_ref, cg_ref, k_ref, v_ref, *rest, bq, sliding, q0, k0, k_end, c0):
    i = pl.program_id(1)
    q_start = q0 + i * bq
    bw = _band_width(bq)
    if sliding:
        cbias_ref, bias_ref, o_ref = rest
        first = i * (bq // LANE) - BACK_TILES
        win_start = first * LANE
        rows = [pl.ds(pl.multiple_of(jnp.maximum(first + t, 0) * LANE, LANE), LANE) for t in range(bw // LANE)]
        kw = jnp.concatenate([k_ref[r, :] for r in rows], axis=0)
        vw = jnp.concatenate([v_ref[r, :] for r in rows], axis=0)
    else:
        kn_ref, vn_ref, cbias_ref, bias_ref, o_ref = rest
        win_start = k0
        pad = jnp.zeros((bw - k_ref.shape[0] - bq, k_ref.shape[1]), BF16)
        kw = jnp.concatenate([k_ref[...].astype(BF16), kn_ref[...], pad], axis=0)
        vw = jnp.concatenate([v_ref[...].astype(BF16), vn_ref[...], pad], axis=0)
    qpos = q_start + lax.broadcasted_iota(jnp.int32, (bq, 1), 0)

    def valid(lo, hi):
        kpos = win_start + lo + lax.broadcasted_iota(jnp.int32, (1, hi - lo), 1)
        dc = (qpos >> CHUNK_SHIFT) - (kpos >> CHUNK_SHIFT)
        return (dc >= 0) & (dc <= C_BACK) & (kpos >= 0) & (kpos < k_end)

    segs = [(lo, hi, valid(lo, hi)) for lo, hi in ((0, c0), (c0, bw)) if hi > lo]
    for h in range(H_C):
        sl = slice(h * DH_C, (h + 1) * DH_C)
        q = q_ref[:, sl]
        parts = []
        for lo, hi, ok in segs:
            bias = cbias_ref[h][:, :1] if hi <= c0 else bias_ref[h]
            lg = lax.dot_general(q, kw[lo:hi, sl], NT_DIMS, preferred_element_type=F32) * DH_C ** -0.5
            parts.append((jnp.where(ok, lg + bias, -jnp.inf), vw[lo:hi, sl]))
        o = _softmax_pv(parts)
        o_ref[:, sl] = (o * _silu(cg_ref[:, sl])).astype(BF16)


def _band(qc, z, keys, vals, bias, *, bq, sliding, q0, k0, k_end):
    c0, cbias, bias = bias
    b, s, _ = qc.shape
    if sliding:
        assert bq % LANE == 0 and q0 == 0 and k0 == 0
        kv = [keys, vals]
    else:
        assert keys[0].shape[1] + bq <= _band_width(bq) and s == bq
        kv = [keys[0], vals[0], keys[1], vals[1]]
    kern = functools.partial(_band_kernel, bq=bq, sliding=sliding, q0=q0, k0=k0, k_end=k_end, c0=c0)
    return pl.pallas_call(
        kern,
        grid=(b, s // bq),
        in_specs=[pl.BlockSpec((None, bq, 1024), lambda bi, qi: (bi, qi, 0)),
                  pl.BlockSpec((None, bq, 1024), lambda bi, qi: (bi, qi, O_CG // 1024))]
                 + [pl.BlockSpec((None,) + a.shape[1:], lambda bi, qi: (bi, 0, 0)) for a in kv]
                 + [_const_spec(cbias.shape), _const_spec(bias.shape)],
        out_specs=pl.BlockSpec((None, bq, 1024), lambda bi, qi: (bi, qi, 0)),
        out_shape=jax.ShapeDtypeStruct((b, s, 1024), BF16),
        compiler_params=_params("parallel", "arbitrary"),
        name="band",
    )(qc, z, *kv, cbias, bias)


def _rope_tables(pos):
    half = ROPE_DIM // 2
    freq = ROPE_BASE ** (-jnp.arange(half, dtype=F32) / half)
    ang = pos.astype(F32)[:, None] * freq[None, :]
    cos, sin = jnp.cos(ang), jnp.sin(ang)
    z = jnp.zeros_like(cos)
    return jnp.concatenate([cos, z, cos, z], 1), jnp.concatenate([-sin, z, sin, z], 1)


def _rope_lanes(x):
    half = ROPE_DIM // 2
    z = jnp.zeros(x.shape[:-1] + (half,), x.dtype)
    return jnp.concatenate([x[..., :half], z, x[..., half:], z], -1)


def _t5_bucket_np(rel):
    nb = T5_BUCKETS // 2
    max_exact = nb // 2
    n = np.abs(rel)
    nf = np.maximum(n, 1).astype(np.float64)
    large = max_exact + (np.log(nf / max_exact) / math.log(T5_MAX_DIST / max_exact) * (nb - max_exact)).astype(np.int64)
    large = np.minimum(large, nb - 1)
    return np.where(rel > 0, nb, 0) + np.where(n < max_exact, n, large)


def _toeplitz(w, rows, width, cols):
    flat = jnp.tile(w, (1,) * (w.ndim - 1) + (rows,))[..., :rows * width]
    return flat.reshape(w.shape[:-1] + (rows, width))[..., :cols]


def _t5_tiles(t5_bias, bq, d_min):
    d_max = max(bq // LANE - 1, 0)
    width = LANE + max(bq, LANE)
    k = np.arange(width + 1)
    delta = np.where(k < LANE, k, k - (width + 1))
    rel = np.arange(d_min, d_max + 1)[:, None] * LANE + delta[None, :]
    bucket = _t5_bucket_np(rel)
    far = 0
    while far + 1 < len(bucket) and np.array_equal(bucket[far + 1], bucket[0]):
        far += 1
    w = jnp.transpose(t5_bias[bucket[far:]], (2, 0, 1))
    return _toeplitz(w, bq, width, LANE), d_min, d_max, far


def _band_bias(rel_tab, bq, qk_off):
    bw = _band_width(bq)
    rel_index = lambda rel: np.clip(rel, -(CHUNK - 1), REL_CLIP) + (CHUNK - 1)
    full = rel_index(qk_off + np.arange(bq)[:, None] - np.arange(bw)[None, :])
    c0 = 0
    while c0 + 2 * MXU_COLS <= bw and np.all(full[:, :c0 + MXU_COLS] == full[0, 0]):
        c0 += MXU_COLS
    const = jnp.broadcast_to(rel_tab[full[0, 0]][:, None, None], (H_C, 1, LANE))
    wv = bw - c0
    width = wv + bq
    k = np.arange(width + 1)
    delta = np.where(k < wv, k, k - (width + 1))
    idx = rel_index(qk_off - c0 - delta)
    return c0, const, _toeplitz(jnp.transpose(rel_tab[idx], (1, 0)), bq, width, wv)


def _even_weights(w_in, b_wuq, b_wukv, b_qn, b_kn):
    d = w_in.shape[0]
    offs = np.cumsum((0,) + EVEN_SPLITS)
    w16 = w_in.astype(BF16)
    aq, ak, av, ag, iq, ik, iw, bcq, bckv, bkpe, bg = [w16[:, offs[t]:offs[t + 1]] for t in range(11)]
    slab_idx = jnp.concatenate([ik, iw, jnp.zeros((d, LANE - D_IDX - H_IDX), BF16)], 1)
    w = jnp.concatenate([aq, ag, iq, bg, bcq, ak, av, bckv, slab_idx, _rope_lanes(bkpe)], 1)
    uq = b_wuq.reshape(Q_LORA, H_B, QK_B)
    uq = jnp.concatenate([uq[..., :NOPE], _rope_lanes(uq[..., NOPE:])], -1).reshape(Q_LORA, H_B * 256).astype(BF16)
    ukv = b_wukv.reshape(KV_LORA, H_B, NOPE + V_DIM)
    ukv = jnp.concatenate([ukv[..., :NOPE].reshape(KV_LORA, H_B * NOPE),
                           ukv[..., NOPE:].reshape(KV_LORA, H_B * V_DIM)], 1).astype(BF16)
    pad_gain = lambda g: jnp.concatenate([g[:NOPE], _rope_lanes(g[NOPE:])])[None, :]
    return w, uq, ukv, pad_gain(b_qn), pad_gain(b_kn)


def _pad_rows(x, lp):
    return jnp.pad(x, ((0, 0), (0, lp - x.shape[1]), (0, 0)))


def _even_layer(h, p_all, layer, past, q0, ln_g, w_in, uq, ukv, a_qn, a_kn, t5_bias, b_qln, b_kvln, bqn, bkn, wo, wg, wp):
    b, s, d = h.shape
    m = b * s
    z = _norm_mm(h.reshape(m, d), ln_g[None, :], w_in, min(m, 1024), E_END // 4).reshape(b, s, E_END)
    cos, sin = _rope_tables(q0 + jnp.arange(s, dtype=jnp.int32))
    bs = min(s, 256)
    qa, ka, qb, ckv, kpe_l, kpe64, idx64 = _even_prep(z, cos, sin, a_qn[None, :], a_kn[None, :], b_qln[None, :],
                                                      b_kvln[None, :], uq, bqn, bs)
    av = z[..., E_AV:E_AV + 256]
    new = (ka.reshape(b, s, HKV_A, DH_A), av.reshape(b, s, HKV_A, DH_A), idx64, ckv, kpe64)
    if past is None:
        l_true = s
        ka_all, va_all, ckv_all, kpe_all = ka, av, ckv, kpe_l
        kidx_all, kidx_col = z, E_IDX // LANE
    else:
        kidx = z[..., E_IDX:E_IDX + LANE]
        kidx_col = 0
        c_k, c_v, c_ik, c_ckv, c_kpe = past
        pl_ = c_k.shape[1]
        l_true = pl_ + s
        lp = -(-l_true // LANE) * LANE
        cat = lambda c, n_: _pad_rows(jnp.concatenate([c, n_], 1), lp)
        ka_all = cat(c_k.reshape(b, pl_, 256), ka)
        va_all = cat(c_v.reshape(b, pl_, 256), av)
        kidx_all = cat(jnp.pad(c_ik, ((0, 0), (0, 0), (0, LANE - D_IDX))), kidx)
        ckv_all = cat(c_ckv, ckv)
        kpe_all = cat(_rope_lanes(c_kpe), kpe_l)
    lp = ka_all.shape[1]
    ksel = min(TOPK_MAX, l_true // 4)
    classes = _causal_classes(s, lp, q0)
    bqs = [nr if (nr % LANE == 0 and lc <= DSA_WIDE_BLOCK_KEYS) else min(s, LANE) for _, nr, lc in classes]
    tiles = {bq: _t5_tiles(t5_bias, bq, -((q0 + s - bq) // LANE)) for bq in set(bqs)}
    o_a = jnp.zeros((b, s, H_A * DH_A), BF16)
    for (r0, nr, lc), bq in zip(classes, bqs):
        o_a = _dsa(qa, z, ka_all, va_all, kidx_all, kidx_col, tiles[bq], o_a, bq=bq, l_true=l_true, q0=q0,
                   ksel=ksel, row0=r0, rows=nr, lp=lc)
    kb, vb = _mla_kv(ckv_all, kpe_all, ukv, bkn, _key_chunk(lp))
    o_b = jnp.zeros((b, s, H_B * V_DIM), BF16)
    for r0, nr, lc in classes:
        o_b = _mla(qb, z, kb, vb, o_b, bq=min(s, 256), l_true=l_true, q0=q0, row0=r0, rows=nr, lp=lc)
    y = _out_ple(h.reshape(m, d), o_a.reshape(m, -1), o_b.reshape(m, -1), p_all, layer, wo, wg, wp, min(m, 256))
    return y.reshape(b, s, d), new


def _odd_layer(h, p_all, layer, past, q0, ln_g, w_in, c_qn, c_kn, c_rel, d_g, d_b, d_ws, d_bs, wo, wg, wp):
    b, s, d = h.shape
    m = b * s
    z = _norm_mm(h.reshape(m, d), ln_g[None, :], w_in, min(m, 1024), 1024).reshape(b, s, O_END)
    n = min(s, D_CHUNK)
    keep = min(C_BACK * CHUNK, s) if past is None else s
    qc, kcb, vcb, o_d, kct, cvt, *dvn = _odd_prep(z, c_qn[None, :], c_kn[None, :], d_g[None, :], d_b[None, :],
                                                  d_ws[:, :n, :n], d_bs[:, :n].T, min(s, 256), n, keep,
                                                  want_dvn=past is not None)
    c_new = (kct.reshape(b, keep, H_C, DH_C), cvt.reshape(b, keep, H_C, DH_C))
    if past is None:
        bqc = 2 * LANE
        bias = _band_bias(c_rel, bqc, C_BACK * CHUNK)
        o_c = _band(qc, z, kcb, vcb, bias, bq=bqc, sliding=True, q0=0, k0=0, k_end=s)
    else:
        nc = past[0].shape[1]
        bias = _band_bias(c_rel, s, nc)
        o_c = _band(qc, z, (past[0].reshape(b, nc, 1024), kcb), (past[1].reshape(b, nc, 1024), vcb), bias,
                    bq=s, sliding=False, q0=q0, k0=q0 - nc, k_end=q0 + s)
    y = _out_ple(h.reshape(m, d), o_c.reshape(m, -1), o_d.reshape(m, -1), p_all, layer, wo, wg, wp, min(m, 256))
    return y.reshape(b, s, d), c_new, (dvn[0] if dvn else None)


def kernel(x_prompt, x_sample, cache_a_k, cache_a_v, cache_a_idx_k, cache_b_ckv, cache_b_kpe, cache_c_k, cache_c_v, p_prompt, p_sample, ln_g, w_in_even, a_q_norm, a_k_norm, t5_bias, b_q_lora_norm, b_kv_lora_norm, b_w_uq, b_w_ukv, b_q_norm, b_k_norm, w_out_even, w_in_odd, c_q_norm, c_k_norm, c_rel_bias, d_ln_g, d_ln_b, d_w_s, d_b_s, w_out_odd, ple_proj, ple_gate):
    depth = ln_g.shape[0]
    past_len = cache_a_k.shape[2]
    hp, hs = x_prompt, x_sample
    pp = p_prompt.reshape(depth, -1, PLE_DIM)
    ps = p_sample.reshape(depth, -1, PLE_DIM)
    ev_p, ev_s, od_p, od_s, dv_s = [], [], [], [], []
    for i in range(depth):
        j = i // 2
        wg = ple_gate[i].astype(BF16)
        wp = ple_proj[i].astype(BF16)
        if i % 2 == 0:
            w_in, uq, ukv, bqn, bkn = _even_weights(w_in_even[j], b_w_uq[j], b_w_ukv[j], b_q_norm[j], b_k_norm[j])
            w = (ln_g[i], w_in, uq, ukv, a_q_norm[j], a_k_norm[j], t5_bias, b_q_lora_norm[j], b_kv_lora_norm[j],
                 bqn, bkn, w_out_even[j].astype(BF16), wg, wp)
            hp, sp = _even_layer(hp, pp, i, None, 0, *w)
            past = (cache_a_k[j], cache_a_v[j], cache_a_idx_k[j], cache_b_ckv[j], cache_b_kpe[j])
            hs, ss = _even_layer(hs, ps, i, past, past_len, *w)
            ev_p.append(sp)
            ev_s.append(ss)
        else:
            w = (ln_g[i], w_in_odd[j].astype(BF16), c_q_norm[j], c_k_norm[j], c_rel_bias[j], d_ln_g[j], d_ln_b[j],
                 d_w_s[j], d_b_s[j], w_out_odd[j].astype(BF16), wg, wp)
            hp, sp, _ = _odd_layer(hp, pp, i, None, 0, *w)
            hs, ss, dvs = _odd_layer(hs, ps, i, (cache_c_k[j], cache_c_v[j]), past_len, *w)
            od_p.append(sp)
            od_s.append(ss)
            dv_s.append(dvs)
    st = lambda lst, n_: jnp.stack([e[n_] for e in lst], 0)
    return (hp, hs, st(ev_p, 0), st(ev_p, 1), st(ev_p, 2), st(ev_p, 3), st(ev_p, 4), st(od_p, 0), st(od_p, 1),
            st(ev_s, 0), st(ev_s, 1), st(ev_s, 2), st(ev_s, 3), st(ev_s, 4), st(od_s, 0), st(od_s, 1),
            jnp.stack(dv_s, 0))
```

```python
import functools
import math

import numpy as np
import jax
import jax.numpy as jnp
from jax import lax
from jax.experimental import pallas as pl
from jax.experimental.pallas import tpu as pltpu

F32 = jnp.float32
BF16 = jnp.bfloat16
INT_MIN = -2 ** 31
LOG2E = math.log2(math.e)
CLASS_ROWS = 256
SEARCH_GROUPS = 4
DSA_WIDE_BLOCK_KEYS = 1536

D_MODEL = 2048
CHUNK = 64
CHUNK_SHIFT = 6
LANE = 128
MXU_COLS = 256
H_A, HKV_A, GROUP_A, DH_A = 8, 2, 4, 128
H_IDX, D_IDX = 16, 64
TOPK_MAX = 256
T5_BUCKETS, T5_MAX_DIST = 32, 128
H_B, Q_LORA, KV_LORA, NOPE, ROPE_DIM, V_DIM = 8, 512, 256, 128, 64, 128
ROPE_BASE = 10000.0
QK_B = NOPE + ROPE_DIM
H_C, DH_C, C_BACK, REL_CLIP = 8, 128, 8, 128
W_D, G_D, DG_D, D_CHUNK = 1024, 8, 128, 128
PLE_DIM = 256

EVEN_SPLITS = (H_A * DH_A, HKV_A * DH_A, HKV_A * DH_A, H_A * DH_A, H_IDX * D_IDX, D_IDX, H_IDX,
               Q_LORA, KV_LORA, ROPE_DIM, H_B * V_DIM)
E_AQ, E_AG, E_IQ, E_BG, E_BCQ, E_AK, E_AV, E_CKV, E_IDX, E_KPE, E_END = (
    0, 1024, 2048, 3072, 4096, 4608, 4864, 5120, 5376, 5504, 5632)
O_CQ, O_CK, O_CV, O_CG, O_DU, O_DV, O_DG, O_END = 0, 1024, 2048, 3072, 4096, 5120, 6144, 7168

VMEM_LIMIT_BYTES = 56 * 1024 * 1024
NT_DIMS = (((1,), (1,)), ((), ()))


def _params(*sem):
    return pltpu.CompilerParams(dimension_semantics=sem, vmem_limit_bytes=VMEM_LIMIT_BYTES)


def _const_spec(shape):
    zeros = (0,) * len(shape)
    return pl.BlockSpec(shape, lambda *_: zeros, pipeline_mode=pl.Buffered(1))


def _rs(x, n=None, eps=1e-6):
    n = x.shape[-1] if n is None else n
    return lax.rsqrt(jnp.sum(x * x, axis=-1, keepdims=True) / n + eps)


def _silu(x):
    return x * (1.0 / (1.0 + jnp.exp(-x)))


def _softmax_pv(parts, scale=1.0):
    m = functools.reduce(jnp.maximum, [jnp.max(lg, axis=-1, keepdims=True) for lg, _ in parts])
    o = s = None
    for lg, v in parts:
        p = jnp.exp2((lg - m) * (scale * LOG2E))
        ps = jnp.sum(p, axis=-1, keepdims=True)
        po = jnp.dot(p.astype(BF16), v, preferred_element_type=F32)
        o, s = (po, ps) if o is None else (o + po, s + ps)
    return o / s


def _norm_mm_kernel(x_ref, g_ref, w_ref, o_ref, xn_ref):
    @pl.when(pl.program_id(1) == 0)
    def _():
        x = x_ref[...]
        xn_ref[...] = (x * _rs(x) * g_ref[...]).astype(BF16)

    o_ref[...] = jnp.dot(xn_ref[...], w_ref[...], preferred_element_type=F32)


def _norm_mm(x, g, w, bm, bn):
    m, d = x.shape
    n = w.shape[1]
    return pl.pallas_call(
        _norm_mm_kernel,
        grid=(m // bm, n // bn),
        in_specs=[pl.BlockSpec((bm, d), lambda i, j: (i, 0)),
                  pl.BlockSpec((1, d), lambda i, j: (0, 0)),
                  pl.BlockSpec((d, bn), lambda i, j: (0, j))],
        out_specs=pl.BlockSpec((bm, bn), lambda i, j: (i, j)),
        out_shape=jax.ShapeDtypeStruct((m, n), F32),
        scratch_shapes=[pltpu.VMEM((bm, d), BF16)],
        compiler_params=_params("parallel", "arbitrary"),
        name="norm_mm",
    )(x, g, w)


def _rope(x, cos, sin):
    return x * cos + pltpu.roll(x, 64, 1) * sin


def _even_prep_kernel(aq_ref, bcq_ref, ak_ref, ckv_ref, kpe_ref, idx_ref, cos_ref, sin_ref,
                      aqn_ref, akn_ref, qln_ref, kvln_ref, wuq_ref, bqn_ref,
                      qa_o, ka_o, qb_o, ckv_o, kpe_o, kpe64_o, idx64_o):
    for h in range(H_A):
        x = aq_ref[:, h * DH_A:(h + 1) * DH_A]
        qa_o[:, h * DH_A:(h + 1) * DH_A] = (x * _rs(x) * aqn_ref[...]).astype(BF16)
    for n in range(HKV_A):
        x = ak_ref[:, n * DH_A:(n + 1) * DH_A]
        ka_o[:, n * DH_A:(n + 1) * DH_A] = x * _rs(x) * akn_ref[...]
    c = ckv_ref[...]
    ckv_o[...] = c * _rs(c) * kvln_ref[...]
    cos = cos_ref[...]
    sin = sin_ref[...]
    kpe = _rope(kpe_ref[...], cos, sin)
    kpe_o[...] = kpe
    half = ROPE_DIM // 2
    kpe64_o[...] = jnp.concatenate([kpe[:, :half], kpe[:, 2 * half:3 * half]], axis=-1)
    idx64_o[...] = idx_ref[:, :D_IDX]
    cq = bcq_ref[...]
    cqn = (cq * _rs(cq) * qln_ref[...]).astype(BF16)
    qb = jnp.dot(cqn, wuq_ref[...], preferred_element_type=F32)
    g = bqn_ref[...]
    for h in range(H_B):
        nope = qb[:, h * 256:h * 256 + 128]
        rot = _rope(qb[:, h * 256 + 128:(h + 1) * 256], cos, sin)
        ss = jnp.sum(nope * nope + rot * rot, -1, keepdims=True)
        r = lax.rsqrt(ss / QK_B + 1e-6)
        qb_o[:, h * 256:h * 256 + 128] = (nope * r * g[:, :128]).astype(BF16)
        qb_o[:, h * 256 + 128:(h + 1) * 256] = (rot * r * g[:, 128:]).astype(BF16)


def _even_prep(z, cos, sin, aqn, akn, qln, kvln, wuq, bqn, bs):
    b, s, _ = z.shape

    def zspec(width, off):
        return pl.BlockSpec((None, bs, width), lambda bi, si: (bi, si, off // width))

    def ospec(width):
        return pl.BlockSpec((None, bs, width), lambda bi, si: (bi, si, 0))

    pos_spec = pl.BlockSpec((bs, LANE), lambda bi, si: (si, 0))
    return pl.pallas_call(
        _even_prep_kernel,
        grid=(b, s // bs),
        in_specs=[zspec(1024, E_AQ), zspec(512, E_BCQ), zspec(256, E_AK), zspec(256, E_CKV),
                  zspec(128, E_KPE), zspec(128, E_IDX), pos_spec, pos_spec,
                  _const_spec((1, DH_A)), _const_spec((1, DH_A)), _const_spec((1, Q_LORA)),
                  _const_spec((1, KV_LORA)), _const_spec((Q_LORA, H_B * 256)), _const_spec((1, 256))],
        out_specs=[ospec(1024), ospec(256), ospec(2048), ospec(256), ospec(128), ospec(ROPE_DIM), ospec(D_IDX)],
        out_shape=[jax.ShapeDtypeStruct((b, s, 1024), BF16), jax.ShapeDtypeStruct((b, s, 256), F32),
                   jax.ShapeDtypeStruct((b, s, 2048), BF16), jax.ShapeDtypeStruct((b, s, 256), F32),
                   jax.ShapeDtypeStruct((b, s, 128), F32), jax.ShapeDtypeStruct((b, s, ROPE_DIM), F32),
                   jax.ShapeDtypeStruct((b, s, D_IDX), F32)],
        compiler_params=_params("parallel", "arbitrary"),
        name="even_prep",
    )(z, z, z, z, z, z, cos, sin, aqn, akn, qln, kvln, wuq, bqn)


def _mla_kv_kernel(ckv_ref, kpe_ref, w_ref, g_ref, kb_o, vb_o):
    kv = jnp.dot(ckv_ref[...].astype(BF16), w_ref[...], preferred_element_type=F32)
    kp = kpe_ref[...]
    skp = jnp.sum(kp * kp, -1, keepdims=True)
    g = g_ref[...]
    for h in range(H_B):
        nope = kv[:, h * NOPE:(h + 1) * NOPE]
        r = lax.rsqrt((jnp.sum(nope * nope, -1, keepdims=True) + skp) / QK_B + 1e-6)
        kb_o[:, h * 256:h * 256 + 128] = (nope * r * g[:, :128]).astype(BF16)
        kb_o[:, h * 256 + 128:(h + 1) * 256] = (kp * r * g[:, 128:]).astype(BF16)
    vb_o[...] = kv[:, H_B * NOPE:].astype(BF16)


def _mla_kv(ckv_all, kpe_all, wukv, bkn, bl):
    b, lp, _ = ckv_all.shape
    return pl.pallas_call(
        _mla_kv_kernel,
        grid=(b, lp // bl),
        in_specs=[pl.BlockSpec((None, bl, KV_LORA), lambda bi, li: (bi, li, 0)),
                  pl.BlockSpec((None, bl, LANE), lambda bi, li: (bi, li, 0)),
                  _const_spec((KV_LORA, 2048)), _const_spec((1, 256))],
        out_specs=[pl.BlockSpec((None, bl, 2048), lambda bi, li: (bi, li, 0)),
                   pl.BlockSpec((None, bl, 1024), lambda bi, li: (bi, li, 0))],
        out_shape=[jax.ShapeDtypeStruct((b, lp, 2048), BF16), jax.ShapeDtypeStruct((b, lp, 1024), BF16)],
        compiler_params=_params("parallel", "arbitrary"),
        name="mla_kv",
    )(ckv_all, kpe_all, wukv, bkn)


def _key_chunk(lp):
    for c in (512, 384, 256, 128):
        if lp % c == 0:
            return c
    raise ValueError(lp)


def _dsa_kernel(qa_ref, iq_ref, sa_ref, ag_ref, ka_ref, va_ref, kidx_ref, bias_ref, acc_ref, o_ref,
                key_ref, hi_ref, lo_ref, *, bq, lp, l_true, q0, ksel, d_min, d_max, far, blk0):
    i = pl.program_id(1) + blk0
    q_start = q0 + i * bq
    qpos = q_start + lax.broadcasted_iota(jnp.int32, (bq, 1), 0)
    ck = _key_chunk(lp)

    iq = iq_ref[...].astype(BF16)
    a = jnp.concatenate([iq[:, p * LANE:(p + 1) * LANE] for p in range(H_IDX // 2)], axis=0)
    wi = sa_ref[...] * (H_IDX ** -0.5 * D_IDX ** -0.5)
    for c0 in range(0, lp, ck):
        kk = kidx_ref[c0:c0 + ck, :]
        lane = lax.broadcasted_iota(jnp.int32, kk.shape, 1)
        k_lo = jnp.where(lane < D_IDX, kk, 0.0).astype(BF16)
        k_hi = jnp.where(lane >= D_IDX, pltpu.roll(kk, D_IDX, 1), 0.0).astype(BF16)
        s_lo = lax.dot_general(a, k_lo, NT_DIMS, preferred_element_type=F32)
        s_hi = lax.dot_general(a, k_hi, NT_DIMS, preferred_element_type=F32)
        sc = jnp.zeros((bq, ck), F32)
        for p in range(H_IDX // 2):
            w0 = wi[:, D_IDX + 2 * p:D_IDX + 2 * p + 1]
            w1 = wi[:, D_IDX + 2 * p + 1:D_IDX + 2 * p + 2]
            sc = sc + w0 * jnp.maximum(s_lo[p * bq:(p + 1) * bq], 0.0)
            sc = sc + w1 * jnp.maximum(s_hi[p * bq:(p + 1) * bq], 0.0)
        kpos = c0 + lax.broadcasted_iota(jnp.int32, (1, ck), 1)
        valid = ((kpos >> CHUNK_SHIFT) <= (qpos >> CHUNK_SHIFT)) & (kpos < l_true)
        bits = lax.bitcast_convert_type(sc, jnp.int32)
        key = jnp.where(bits < 0, bits ^ 0x7FFFFFFF, bits)
        key = jnp.where(bits == INT_MIN, 0, key)
        key = jnp.where(valid, key, INT_MIN)
        key_ref[:, c0:c0 + ck] = key
        hi_ref[:, c0:c0 + ck] = (key >> 16) + 2 ** 15
        lo_ref[:, c0:c0 + ck] = key & 0xFFFF

    ng = SEARCH_GROUPS
    gr = bq // ng
    groups = [slice(r * gr, (r + 1) * gr) for r in range(ng)]

    def count_ge(ref, rows, cand):
        acc = None
        for j in range(lp // LANE):
            d = (ref[rows, j * LANE:(j + 1) * LANE] - cand) >> 31
            acc = d if acc is None else acc + d
        return lp + jnp.sum(acc.astype(F32), axis=-1, keepdims=True)

    def search16(ref, need):
        ts = [jnp.zeros((gr, 1), jnp.int32)] * ng
        for bit in reversed(range(16)):
            for r in range(ng):
                cand = ts[r] | (1 << bit)
                ts[r] = jnp.where(count_ge(ref, groups[r], cand) >= need[r], cand, ts[r])
        return ts

    if lp > ksel:
        t_hi = search16(hi_ref, (float(ksel),) * ng)
        need_lo = []
        for rows, t in zip(groups, t_hi):
            need_lo.append(ksel - count_ge(hi_ref, rows, t + 1))
            lo_ref[rows, :] = jnp.where(hi_ref[rows, :] == t, lo_ref[rows, :], -1)
        t_lo = search16(lo_ref, need_lo)
        t_s = jnp.concatenate([lax.shift_left(h - 2 ** 15, 16) | l for h, l in zip(t_hi, t_lo)], axis=0)
        key = key_ref[...]
        n_picked = jnp.sum(((key >= t_s) & (key > INT_MIN)).astype(F32), axis=-1, keepdims=True)

        @pl.when(jnp.max(n_picked) > ksel)
        def _():
            key = key_ref[...]
            rpos = (lp - 1) - lax.broadcasted_iota(jnp.int32, (bq, lp), 1)
            lo_ref[...] = jnp.where((key == t_s) & (key > INT_MIN), rpos, -1)
            need = [ksel - jnp.sum((key[rows] > t_s[rows]).astype(F32), axis=-1, keepdims=True) for rows in groups]
            t_pos = jnp.concatenate(search16(lo_ref, need), axis=0)
            tie = lo_ref[...]
            key_ref[...] = jnp.where((tie >= 0) & (tie < t_pos), INT_MIN, key)
    else:
        t_s = jnp.full((bq, 1), INT_MIN, jnp.int32)
    key = key_ref[...]
    sel = (key >= t_s) & (key > INT_MIN)

    d_base = -(q0 // LANE + i * (bq // LANE))
    nk = lp // LANE
    for n in range(HKV_A):
        q4 = jnp.concatenate([qa_ref[:, (n * GROUP_A + g) * DH_A:(n * GROUP_A + g + 1) * DH_A]
                              for g in range(GROUP_A)], axis=0)
        k_n = ka_ref[:, n * DH_A:(n + 1) * DH_A].astype(BF16)
        v_n = va_ref[:, n * DH_A:(n + 1) * DH_A].astype(BF16)
        lg4 = lax.dot_general(q4, k_n, NT_DIMS, preferred_element_type=F32) * DH_A ** -0.5
        ps, ss = [], []
        for g in range(GROUP_A):
            bias = jnp.concatenate(
                [jnp.concatenate([bias_ref[n * GROUP_A + g,
                                           jnp.maximum(jnp.minimum(d_base + j - u, d_max) - (d_min + far), 0)]
                                  for j in range(nk)], axis=1)
                 for u in range(bq // bias_ref.shape[2])], axis=0)
            lg = jnp.where(sel, lg4[g * bq:(g + 1) * bq] + bias, -jnp.inf)
            p = jnp.exp(lg - jnp.max(lg, axis=-1, keepdims=True))
            ss.append(jnp.sum(p, axis=-1, keepdims=True))
            ps.append(p.astype(BF16))
        o4 = jnp.dot(jnp.concatenate(ps, axis=0), v_n, preferred_element_type=F32)
        for g in range(GROUP_A):
            h = n * GROUP_A + g
            gate = ag_ref[:, h * DH_A:(h + 1) * DH_A]
            o_ref[:, h * DH_A:(h + 1) * DH_A] = (o4[g * bq:(g + 1) * bq] / ss[g] * _silu(gate)).astype(BF16)


def _dsa(qa, z, ka_all, va_all, kidx_all, kidx_col, bias_tiles, acc, *, bq, l_true, q0, ksel, row0, rows, lp):
    b, s, _ = qa.shape
    bias_tiles, d_min, d_max, far = bias_tiles
    blk0 = row0 // bq
    assert q0 % LANE == 0 and (bq % LANE == 0 or s == bq)
    assert d_min <= -((q0 + s - bias_tiles.shape[2]) // LANE) and bq % bias_tiles.shape[2] == 0
    kern = functools.partial(_dsa_kernel, bq=bq, lp=lp, l_true=l_true, q0=q0, ksel=ksel,
                             d_min=d_min, d_max=d_max, far=far, blk0=blk0)

    def zspec(width, off):
        return pl.BlockSpec((None, bq, width), lambda bi, qi: (bi, qi + blk0, off // width))

    def kspec(width):
        return pl.BlockSpec((None, lp, width), lambda bi, qi: (bi, 0, 0))

    return pl.pallas_call(
        kern,
        grid=(b, rows // bq),
        in_specs=[pl.BlockSpec((None, bq, 1024), lambda bi, qi: (bi, qi + blk0, 0)),
                  zspec(1024, E_IQ), zspec(128, E_IDX), zspec(1024, E_AG),
                  kspec(256), kspec(256),
                  pl.BlockSpec((None, lp, LANE), lambda bi, qi: (bi, 0, kidx_col)),
                  _const_spec(bias_tiles.shape),
                  pl.BlockSpec(memory_space=pl.ANY)],
        out_specs=pl.BlockSpec((None, bq, 1024), lambda bi, qi: (bi, qi + blk0, 0)),
        out_shape=jax.ShapeDtypeStruct(acc.shape, acc.dtype),
        input_output_aliases={8: 0},
        scratch_shapes=[pltpu.VMEM((bq, lp), jnp.int32)] * 3,
        compiler_params=_params("parallel", "arbitrary"),
        name="dsa",
    )(qa, z, z, z, ka_all, va_all, kidx_all, bias_tiles, acc)


def _causal_classes(s, lp, q0):
    if q0 == 0 and s % CLASS_ROWS == 0 and lp == s:
        return [(c * CLASS_ROWS, CLASS_ROWS, (c + 1) * CLASS_ROWS) for c in range(s // CLASS_ROWS)]
    return [(0, s, lp)]


def _mla_kernel(qb_ref, bg_ref, kb_ref, vb_ref, acc_ref, o_ref, *, bq, lp, l_true, q0, blk0, n_full):
    q_start = q0 + (pl.program_id(1) + blk0) * bq
    qpos = q_start + lax.broadcasted_iota(jnp.int32, (bq, 1), 0)
    kpos = n_full + lax.broadcasted_iota(jnp.int32, (1, lp - n_full), 1)
    valid = ((kpos >> CHUNK_SHIFT) <= (qpos >> CHUNK_SHIFT)) & (kpos < l_true)
    for h in range(H_B):
        q = qb_ref[:, h * 256:(h + 1) * 256]
        parts = []
        if n_full:
            parts.append((lax.dot_general(q, kb_ref[:n_full, h * 256:(h + 1) * 256], NT_DIMS,
                                          preferred_element_type=F32), vb_ref[:n_full, h * V_DIM:(h + 1) * V_DIM]))
        lg = lax.dot_general(q, kb_ref[n_full:, h * 256:(h + 1) * 256], NT_DIMS, preferred_element_type=F32)
        parts.append((jnp.where(valid, lg, -jnp.inf), vb_ref[n_full:, h * V_DIM:(h + 1) * V_DIM]))
        o = _softmax_pv(parts, QK_B ** -0.5)
        gate = bg_ref[:, h * V_DIM:(h + 1) * V_DIM]
        o_ref[:, h * V_DIM:(h + 1) * V_DIM] = (o * _silu(gate)).astype(BF16)


def _mla(qb, z, kb, vb, acc, *, bq, l_true, q0, row0, rows, lp):
    b = qb.shape[0]
    blk0 = row0 // bq
    n_full = min(min(q0 + row0 + CHUNK, l_true) // LANE * LANE, lp - LANE)
    kern = functools.partial(_mla_kernel, bq=bq, lp=lp, l_true=l_true, q0=q0, blk0=blk0, n_full=n_full)
    return pl.pallas_call(
        kern,
        grid=(b, rows // bq),
        in_specs=[pl.BlockSpec((None, bq, 2048), lambda bi, qi: (bi, qi + blk0, 0)),
                  pl.BlockSpec((None, bq, 1024), lambda bi, qi: (bi, qi + blk0, E_BG // 1024)),
                  pl.BlockSpec((None, lp, 2048), lambda bi, qi: (bi, 0, 0)),
                  pl.BlockSpec((None, lp, 1024), lambda bi, qi: (bi, 0, 0)),
                  pl.BlockSpec(memory_space=pl.ANY)],
        out_specs=pl.BlockSpec((None, bq, 1024), lambda bi, qi: (bi, qi + blk0, 0)),
        out_shape=jax.ShapeDtypeStruct(acc.shape, acc.dtype),
        input_output_aliases={4: 0},
        compiler_params=_params("parallel", "arbitrary"),
        name="mla",
    )(qb, z, kb, vb, acc)


def _out_ple_kernel(h_ref, oa_ref, ob_ref, p_ref, wo_ref, wg_ref, wp_ref, o_ref):
    half = oa_ref.shape[-1]
    h1 = (h_ref[...]
          + jnp.dot(oa_ref[...], wo_ref[:half, :], preferred_element_type=F32)
          + jnp.dot(ob_ref[...], wo_ref[half:, :], preferred_element_type=F32))
    r = (h1 * _rs(h1)).astype(BF16)
    gate = 1.0 / (1.0 + jnp.exp(-jnp.dot(r, wg_ref[...], preferred_element_type=F32)))
    o_ref[...] = h1 + gate * jnp.dot(p_ref[...].astype(BF16), wp_ref[...], preferred_element_type=F32)


def _out_ple(h, oa, ob, p_all, layer, wo, wg, wp, bm):
    m, d = h.shape
    half = oa.shape[1]

    def rows(width):
        return pl.BlockSpec((bm, width), lambda i: (i, 0))

    return pl.pallas_call(
        _out_ple_kernel,
        grid=(m // bm,),
        in_specs=[rows(d), rows(half), rows(half), pl.BlockSpec((None, bm, PLE_DIM), lambda i: (layer, i, 0)),
                  _const_spec((2 * half, d)), _const_spec((d, d)), _const_spec((PLE_DIM, d))],
        out_specs=rows(d),
        out_shape=jax.ShapeDtypeStruct((m, d), F32),
        compiler_params=_params("parallel"),
        name="out_ple",
    )(h, oa, ob, p_all, wo, wg, wp)


def _odd_prep_kernel(cq_ref, ck_ref, cv_ref, dv_ref, du_ref, dgate_ref, cqn_ref, ckn_ref, dg_ref, db_ref,
                     ws_ref, bs_ref, qc_o, kcb_o, vcb_o, od_o, kct_o, cvt_o, *maybe_dvn_o, n):
    for h in range(H_C):
        sl = slice(h * DH_C, (h + 1) * DH_C)
        x = cq_ref[:, sl]
        qc_o[:, sl] = (x * _rs(x) * cqn_ref[...]).astype(BF16)
        x = ck_ref[:, sl]
        kn = x * _rs(x) * ckn_ref[...]
        kct_o[:, sl] = kn
        kcb_o[:, sl] = kn.astype(BF16)
    cv = cv_ref[...]
    cvt_o[...] = cv
    vcb_o[...] = cv.astype(BF16)
    dv = dv_ref[...]
    xc = dv - jnp.mean(dv, -1, keepdims=True)
    var = jnp.mean(xc * xc, -1, keepdims=True)
    dvn = xc * lax.rsqrt(var + 1e-5) * dg_ref[...] + db_ref[...]
    for dvn_o in maybe_dvn_o:
        dvn_o[...] = dvn
    dvn = dvn.astype(BF16)
    row = lax.broadcasted_iota(jnp.int32, (n, n), 0)
    col = lax.broadcasted_iota(jnp.int32, (n, n), 1)
    for g in range(G_D):
        sl = slice(g * DG_D, (g + 1) * DG_D)
        w = jnp.where(col <= row, ws_ref[g], 0.0).astype(BF16)
        for c in range(dv.shape[0] // n):
            rs = slice(c * n, (c + 1) * n)
            sg = jnp.dot(w, dvn[rs, sl], preferred_element_type=F32) + bs_ref[:, g:g + 1]
            od_o[rs, sl] = (du_ref[rs, sl] * sg * _silu(dgate_ref[rs, sl])).astype(BF16)


def _odd_prep(z, cqn, ckn, dg, db, ws, bs_t, bs, n, keep, want_dvn):
    b, s, _ = z.shape
    first_tail = (s - keep) // bs
    assert keep % bs == 0

    def zspec(off):
        return pl.BlockSpec((None, bs, 1024), lambda bi, si: (bi, si, off // 1024))

    ospec = pl.BlockSpec((None, bs, 1024), lambda bi, si: (bi, si, 0))
    tspec = pl.BlockSpec((None, bs, 1024), lambda bi, si: (bi, jnp.maximum(si - first_tail, 0), 0))
    act = lambda dt: jax.ShapeDtypeStruct((b, s, 1024), dt)
    tail = jax.ShapeDtypeStruct((b, keep, 1024), F32)
    return pl.pallas_call(
        functools.partial(_odd_prep_kernel, n=n),
        grid=(b, s // bs),
        in_specs=[zspec(O_CQ), zspec(O_CK), zspec(O_CV), zspec(O_DV), zspec(O_DU), zspec(O_DG),
                  _const_spec((1, DH_C)), _const_spec((1, DH_C)), _const_spec((1, W_D)), _const_spec((1, W_D)),
                  _const_spec((G_D, n, n)), _const_spec((n, G_D))],
        out_specs=[ospec] * 4 + [tspec] * 2 + [ospec] * want_dvn,
        out_shape=[act(BF16)] * 4 + [tail] * 2 + [act(F32)] * want_dvn,
        compiler_params=_params("parallel", "arbitrary"),
        name="odd_prep",
    )(z, z, z, z, z, z, cqn, ckn, dg, db, ws, bs_t)


BACK_TILES = C_BACK * CHUNK // LANE


def _band_width(bq):
    return (BACK_TILES + -(-bq // LANE)) * LANE


def _band_kernel(q_ref, cg_ref, k_ref, v_ref, *rest, bq, sliding, q0, k0, k_end, c0):
    i = pl.program_id(1)
    q_start = q0 + i * bq
    bw = _band_width(bq)
    if sliding:
        cbias_ref, bias_ref, o_ref = rest
        first = i * (bq // LANE) - BACK_TILES
        win_start = first * LANE
        rows = [pl.ds(pl.multiple_of(jnp.maximum(first + t, 0) * LANE, LANE), LANE) for t in range(bw // LANE)]
        kw = jnp.concatenate([k_ref[r, :] for r in rows], axis=0)
        vw = jnp.concatenate([v_ref[r, :] for r in rows], axis=0)
    else:
        kn_ref, vn_ref, cbias_ref, bias_ref, o_ref = rest
        win_start = k0
        pad = jnp.zeros((bw - k_ref.shape[0] - bq, k_ref.shape[1]), BF16)
        kw = jnp.concatenate([k_ref[...].astype(BF16), kn_ref[...], pad], axis=0)
        vw = jnp.concatenate([v_ref[...].astype(BF16), vn_ref[...], pad], axis=0)
    qpos = q_start + lax.broadcasted_iota(jnp.int32, (bq, 1), 0)

    def valid(lo, hi):
        kpos = win_start + lo + lax.broadcasted_iota(jnp.int32, (1, hi - lo), 1)
        dc = (qpos >> CHUNK_SHIFT) - (kpos >> CHUNK_SHIFT)
        return (dc >= 0) & (dc <= C_BACK) & (kpos >= 0) & (kpos < k_end)

    segs = [(lo, hi, valid(lo, hi)) for lo, hi in ((0, c0), (c0, bw)) if hi > lo]
    for h in range(H_C):
        sl = slice(h * DH_C, (h + 1) * DH_C)
        q = q_ref[:, sl]
        parts = []
        for lo, hi, ok in segs:
            bias = cbias_ref[h][:, :1] if hi <= c0 else bias_ref[h]
            lg = lax.dot_general(q, kw[lo:hi, sl], NT_DIMS, preferred_element_type=F32) * DH_C ** -0.5
            parts.append((jnp.where(ok, lg + bias, -jnp.inf), vw[lo:hi, sl]))
        o = _softmax_pv(parts)
        o_ref[:, sl] = (o * _silu(cg_ref[:, sl])).astype(BF16)


def _band(qc, z, keys, vals, bias, *, bq, sliding, q0, k0, k_end):
    c0, cbias, bias = bias
    b, s, _ = qc.shape
    if sliding:
        assert bq % LANE == 0 and q0 == 0 and k0 == 0
        kv = [keys, vals]
    else:
        assert keys[0].shape[1] + bq <= _band_width(bq) and s == bq
        kv = [keys[0], vals[0], keys[1], vals[1]]
    kern = functools.partial(_band_kernel, bq=bq, sliding=sliding, q0=q0, k0=k0, k_end=k_end, c0=c0)
    return pl.pallas_call(
        kern,
        grid=(b, s // bq),
        in_specs=[pl.BlockSpec((None, bq, 1024), lambda bi, qi: (bi, qi, 0)),
                  pl.BlockSpec((None, bq, 1024), lambda bi, qi: (bi, qi, O_CG // 1024))]
                 + [pl.BlockSpec((None,) + a.shape[1:], lambda bi, qi: (bi, 0, 0)) for a in kv]
                 + [_const_spec(cbias.shape), _const_spec(bias.shape)],
        out_specs=pl.BlockSpec((None, bq, 1024), lambda bi, qi: (bi, qi, 0)),
        out_shape=jax.ShapeDtypeStruct((b, s, 1024), BF16),
        compiler_params=_params("parallel", "arbitrary"),
        name="band",
    )(qc, z, *kv, cbias, bias)


def _rope_tables(pos):
    half = ROPE_DIM // 2
    freq = ROPE_BASE ** (-jnp.arange(half, dtype=F32) / half)
    ang = pos.astype(F32)[:, None] * freq[None, :]
    cos, sin = jnp.cos(ang), jnp.sin(ang)
    z = jnp.zeros_like(cos)
    return jnp.concatenate([cos, z, cos, z], 1), jnp.concatenate([-sin, z, sin, z], 1)


def _rope_lanes(x):
    half = ROPE_DIM // 2
    z = jnp.zeros(x.shape[:-1] + (half,), x.dtype)
    return jnp.concatenate([x[..., :half], z, x[..., half:], z], -1)


def _t5_bucket_np(rel):
    nb = T5_BUCKETS // 2
    max_exact = nb // 2
    n = np.abs(rel)
    nf = np.maximum(n, 1).astype(np.float64)
    large = max_exact + (np.log(nf / max_exact) / math.log(T5_MAX_DIST / max_exact) * (nb - max_exact)).astype(np.int64)
    large = np.minimum(large, nb - 1)
    return np.where(rel > 0, nb, 0) + np.where(n < max_exact, n, large)


def _toeplitz(w, rows, width, cols):
    flat = jnp.tile(w, (1,) * (w.ndim - 1) + (rows,))[..., :rows * width]
    return flat.reshape(w.shape[:-1] + (rows, width))[..., :cols]


def _t5_tiles(t5_bias, bq, d_min):
    d_max = max(bq // LANE - 1, 0)
    width = LANE + max(bq, LANE)
    k = np.arange(width + 1)
    delta = np.where(k < LANE, k, k - (width + 1))
    rel = np.arange(d_min, d_max + 1)[:, None] * LANE + delta[None, :]
    bucket = _t5_bucket_np(rel)
    far = 0
    while far + 1 < len(bucket) and np.array_equal(bucket[far + 1], bucket[0]):
        far += 1
    w = jnp.transpose(t5_bias[bucket[far:]], (2, 0, 1))
    return _toeplitz(w, bq, width, LANE), d_min, d_max, far


def _band_bias(rel_tab, bq, qk_off):
    bw = _band_width(bq)
    rel_index = lambda rel: np.clip(rel, -(CHUNK - 1), REL_CLIP) + (CHUNK - 1)
    full = rel_index(qk_off + np.arange(bq)[:, None] - np.arange(bw)[None, :])
    c0 = 0
    while c0 + 2 * MXU_COLS <= bw and np.all(full[:, :c0 + MXU_COLS] == full[0, 0]):
        c0 += MXU_COLS
    const = jnp.broadcast_to(rel_tab[full[0, 0]][:, None, None], (H_C, 1, LANE))
    wv = bw - c0
    width = wv + bq
    k = np.arange(width + 1)
    delta = np.where(k < wv, k, k - (width + 1))
    idx = rel_index(qk_off - c0 - delta)
    return c0, const, _toeplitz(jnp.transpose(rel_tab[idx], (1, 0)), bq, width, wv)


def _even_weights(w_in, b_wuq, b_wukv, b_qn, b_kn):
    d = w_in.shape[0]
    offs = np.cumsum((0,) + EVEN_SPLITS)
    w16 = w_in.astype(BF16)
    aq, ak, av, ag, iq, ik, iw, bcq, bckv, bkpe, bg = [w16[:, offs[t]:offs[t + 1]] for t in range(11)]
    slab_idx = jnp.concatenate([ik, iw, jnp.zeros((d, LANE - D_IDX - H_IDX), BF16)], 1)
    w = jnp.concatenate([aq, ag, iq, bg, bcq, ak, av, bckv, slab_idx, _rope_lanes(bkpe)], 1)
    uq = b_wuq.reshape(Q_LORA, H_B, QK_B)
    uq = jnp.concatenate([uq[..., :NOPE], _rope_lanes(uq[..., NOPE:])], -1).reshape(Q_LORA, H_B * 256).astype(BF16)
    ukv = b_wukv.reshape(KV_LORA, H_B, NOPE + V_DIM)
    ukv = jnp.concatenate([ukv[..., :NOPE].reshape(KV_LORA, H_B * NOPE),
                           ukv[..., NOPE:].reshape(KV_LORA, H_B * V_DIM)], 1).astype(BF16)
    pad_gain = lambda g: jnp.concatenate([g[:NOPE], _rope_lanes(g[NOPE:])])[None, :]
    return w, uq, ukv, pad_gain(b_qn), pad_gain(b_kn)


def _pad_rows(x, lp):
    return jnp.pad(x, ((0, 0), (0, lp - x.shape[1]), (0, 0)))


def _even_layer(h, p_all, layer, past, q0, ln_g, w_in, uq, ukv, a_qn, a_kn, t5_bias, b_qln, b_kvln, bqn, bkn, wo, wg, wp):
    b, s, d = h.shape
    m = b * s
    z = _norm_mm(h.reshape(m, d), ln_g[None, :], w_in, min(m, 1024), E_END // 4).reshape(b, s, E_END)
    cos, sin = _rope_tables(q0 + jnp.arange(s, dtype=jnp.int32))
    bs = min(s, 256)
    qa, ka, qb, ckv, kpe_l, kpe64, idx64 = _even_prep(z, cos, sin, a_qn[None, :], a_kn[None, :], b_qln[None, :],
                                                      b_kvln[None, :], uq, bqn, bs)
    av = z[..., E_AV:E_AV + 256]
    new = (ka.reshape(b, s, HKV_A, DH_A), av.reshape(b, s, HKV_A, DH_A), idx64, ckv, kpe64)
    if past is None:
        l_true = s
        ka_all, va_all, ckv_all, kpe_all = ka, av, ckv, kpe_l
        kidx_all, kidx_col = z, E_IDX // LANE
    else:
        kidx = z[..., E_IDX:E_IDX + LANE]
        kidx_col = 0
        c_k, c_v, c_ik, c_ckv, c_kpe = past
        pl_ = c_k.shape[1]
        l_true = pl_ + s
        lp = -(-l_true // LANE) * LANE
        cat = lambda c, n_: _pad_rows(jnp.concatenate([c, n_], 1), lp)
        ka_all = cat(c_k.reshape(b, pl_, 256), ka)
        va_all = cat(c_v.reshape(b, pl_, 256), av)
        kidx_all = cat(jnp.pad(c_ik, ((0, 0), (0, 0), (0, LANE - D_IDX))), kidx)
        ckv_all = cat(c_ckv, ckv)
        kpe_all = cat(_rope_lanes(c_kpe), kpe_l)
    lp = ka_all.shape[1]
    ksel = min(TOPK_MAX, l_true // 4)
    classes = _causal_classes(s, lp, q0)
    bqs = [nr if (nr % LANE == 0 and lc <= DSA_WIDE_BLOCK_KEYS) else min(s, LANE) for _, nr, lc in classes]
    tb = min(s, LANE)
    tiles = _t5_tiles(t5_bias, tb, -((q0 + s - tb) // LANE))
    o_a = jnp.zeros((b, s, H_A * DH_A), BF16)
    for (r0, nr, lc), bq in zip(classes, bqs):
        o_a = _dsa(qa, z, ka_all, va_all, kidx_all, kidx_col, tiles, o_a, bq=bq, l_true=l_true, q0=q0,
                   ksel=ksel, row0=r0, rows=nr, lp=lc)
    kb, vb = _mla_kv(ckv_all, kpe_all, ukv, bkn, _key_chunk(lp))
    o_b = jnp.zeros((b, s, H_B * V_DIM), BF16)
    for r0, nr, lc in classes:
        o_b = _mla(qb, z, kb, vb, o_b, bq=min(s, 256), l_true=l_true, q0=q0, row0=r0, rows=nr, lp=lc)
    y = _out_ple(h.reshape(m, d), o_a.reshape(m, -1), o_b.reshape(m, -1), p_all, layer, wo, wg, wp, min(m, 256))
    return y.reshape(b, s, d), new


def _odd_layer(h, p_all, layer, past, q0, ln_g, w_in, c_qn, c_kn, c_rel, d_g, d_b, d_ws, d_bs, wo, wg, wp):
    b, s, d = h.shape
    m = b * s
    z = _norm_mm(h.reshape(m, d), ln_g[None, :], w_in, min(m, 1024), 1024).reshape(b, s, O_END)
    n = min(s, D_CHUNK)
    keep = min(C_BACK * CHUNK, s) if past is None else s
    qc, kcb, vcb, o_d, kct, cvt, *dvn = _odd_prep(z, c_qn[None, :], c_kn[None, :], d_g[None, :], d_b[None, :],
                                                  d_ws[:, :n, :n], d_bs[:, :n].T, min(s, 256), n, keep,
                                                  want_dvn=past is not None)
    c_new = (kct.reshape(b, keep, H_C, DH_C), cvt.reshape(b, keep, H_C, DH_C))
    if past is None:
        bqc = 2 * LANE
        bias = _band_bias(c_rel, bqc, C_BACK * CHUNK)
        o_c = _band(qc, z, kcb, vcb, bias, bq=bqc, sliding=True, q0=0, k0=0, k_end=s)
    else:
        nc = past[0].shape[1]
        bias = _band_bias(c_rel, s, nc)
        o_c = _band(qc, z, (past[0].reshape(b, nc, 1024), kcb), (past[1].reshape(b, nc, 1024), vcb), bias,
                    bq=s, sliding=False, q0=q0, k0=q0 - nc, k_end=q0 + s)
    y = _out_ple(h.reshape(m, d), o_c.reshape(m, -1), o_d.reshape(m, -1), p_all, layer, wo, wg, wp, min(m, 256))
    return y.reshape(b, s, d), c_new, (dvn[0] if dvn else None)


def kernel(x_prompt, x_sample, cache_a_k, cache_a_v, cache_a_idx_k, cache_b_ckv, cache_b_kpe, cache_c_k, cache_c_v, p_prompt, p_sample, ln_g, w_in_even, a_q_norm, a_k_norm, t5_bias, b_q_lora_norm, b_kv_lora_norm, b_w_uq, b_w_ukv, b_q_norm, b_k_norm, w_out_even, w_in_odd, c_q_norm, c_k_norm, c_rel_bias, d_ln_g, d_ln_b, d_w_s, d_b_s, w_out_odd, ple_proj, ple_gate):
    depth = ln_g.shape[0]
    past_len = cache_a_k.shape[2]
    hp, hs = x_prompt, x_sample
    pp = p_prompt.reshape(depth, -1, PLE_DIM)
    ps = p_sample.reshape(depth, -1, PLE_DIM)
    ev_p, ev_s, od_p, od_s, dv_s = [], [], [], [], []
    for i in range(depth):
        j = i // 2
        wg = ple_gate[i].astype(BF16)
        wp = ple_proj[i].astype(BF16)
        if i % 2 == 0:
            w_in, uq, ukv, bqn, bkn = _even_weights(w_in_even[j], b_w_uq[j], b_w_ukv[j], b_q_norm[j], b_k_norm[j])
            w = (ln_g[i], w_in, uq, ukv, a_q_norm[j], a_k_norm[j], t5_bias, b_q_lora_norm[j], b_kv_lora_norm[j],
                 bqn, bkn, w_out_even[j].astype(BF16), wg, wp)
            hp, sp = _even_layer(hp, pp, i, None, 0, *w)
            past = (cache_a_k[j], cache_a_v[j], cache_a_idx_k[j], cache_b_ckv[j], cache_b_kpe[j])
            hs, ss = _even_layer(hs, ps, i, past, past_len, *w)
            ev_p.append(sp)
            ev_s.append(ss)
        else:
            w = (ln_g[i], w_in_odd[j].astype(BF16), c_q_norm[j], c_k_norm[j], c_rel_bias[j], d_ln_g[j], d_ln_b[j],
                 d_w_s[j], d_b_s[j], w_out_odd[j].astype(BF16), wg, wp)
            hp, sp, _ = _odd_layer(hp, pp, i, None, 0, *w)
            hs, ss, dvs = _odd_layer(hs, ps, i, (cache_c_k[j], cache_c_v[j]), past_len, *w)
            od_p.append(sp)
            od_s.append(ss)
            dv_s.append(dvs)
    st = lambda lst, n_: jnp.stack([e[n_] for e in lst], 0)
    return (hp, hs, st(ev_p, 0), st(ev_p, 1), st(ev_p, 2), st(ev_p, 3), st(ev_p, 4), st(od_p, 0), st(od_p, 1),
            st(ev_s, 0), st(ev_s, 1), st(ev_s, 2), st(ev_s, 3), st(ev_s, 4), st(od_s, 0), st(od_s, 1),
            jnp.stack(dv_s, 0))
```

```python
import functools
import math

import numpy as np
import jax
import jax.numpy as jnp
from jax import lax
from jax.experimental import pallas as pl
from jax.experimental.pallas import tpu as pltpu

F32 = jnp.float32
BF16 = jnp.bfloat16
INT_MIN = -2 ** 31
LOG2E = math.log2(math.e)
CLASS_ROWS = 256
SEARCH_GROUPS = 4
DSA_WIDE_BLOCK_KEYS = 1280

D_MODEL = 2048
CHUNK = 64
CHUNK_SHIFT = 6
LANE = 128
MXU_COLS = 256
H_A, HKV_A, GROUP_A, DH_A = 8, 2, 4, 128
H_IDX, D_IDX = 16, 64
TOPK_MAX = 256
T5_BUCKETS, T5_MAX_DIST = 32, 128
H_B, Q_LORA, KV_LORA, NOPE, ROPE_DIM, V_DIM = 8, 512, 256, 128, 64, 128
ROPE_BASE = 10000.0
QK_B = NOPE + ROPE_DIM
H_C, DH_C, C_BACK, REL_CLIP = 8, 128, 8, 128
W_D, G_D, DG_D, D_CHUNK = 1024, 8, 128, 128
PLE_DIM = 256

EVEN_SPLITS = (H_A * DH_A, HKV_A * DH_A, HKV_A * DH_A, H_A * DH_A, H_IDX * D_IDX, D_IDX, H_IDX,
               Q_LORA, KV_LORA, ROPE_DIM, H_B * V_DIM)
E_AQ, E_AG, E_IQ, E_BG, E_BCQ, E_AK, E_AV, E_CKV, E_IDX, E_KPE, E_END = (
    0, 1024, 2048, 3072, 4096, 4608, 4864, 5120, 5376, 5504, 5632)
O_CQ, O_CK, O_CV, O_CG, O_DU, O_DV, O_DG, O_END = 0, 1024, 2048, 3072, 4096, 5120, 6144, 7168

VMEM_LIMIT_BYTES = 56 * 1024 * 1024
NT_DIMS = (((1,), (1,)), ((), ()))


def _params(*sem):
    return pltpu.CompilerParams(dimension_semantics=sem, vmem_limit_bytes=VMEM_LIMIT_BYTES)


def _const_spec(shape):
    zeros = (0,) * len(shape)
    return pl.BlockSpec(shape, lambda *_: zeros, pipeline_mode=pl.Buffered(1))


def _rs(x, n=None, eps=1e-6):
    n = x.shape[-1] if n is None else n
    return lax.rsqrt(jnp.sum(x * x, axis=-1, keepdims=True) / n + eps)


def _silu(x):
    return x * (1.0 / (1.0 + jnp.exp(-x)))


def _softmax_pv(parts, scale=1.0):
    m = functools.reduce(jnp.maximum, [jnp.max(lg, axis=-1, keepdims=True) for lg, _ in parts])
    o = s = None
    for lg, v in parts:
        p = jnp.exp2((lg - m) * (scale * LOG2E))
        ps = jnp.sum(p, axis=-1, keepdims=True)
        po = jnp.dot(p.astype(BF16), v, preferred_element_type=F32)
        o, s = (po, ps) if o is None else (o + po, s + ps)
    return o / s


def _norm_mm_kernel(x_ref, g_ref, w_ref, o_ref, xn_ref):
    @pl.when(pl.program_id(1) == 0)
    def _():
        x = x_ref[...]
        xn_ref[...] = (x * _rs(x) * g_ref[...]).astype(BF16)

    o_ref[...] = jnp.dot(xn_ref[...], w_ref[...], preferred_element_type=F32)


def _norm_mm(x, g, w, bm, bn):
    m, d = x.shape
    n = w.shape[1]
    return pl.pallas_call(
        _norm_mm_kernel,
        grid=(m // bm, n // bn),
        in_specs=[pl.BlockSpec((bm, d), lambda i, j: (i, 0)),
                  pl.BlockSpec((1, d), lambda i, j: (0, 0)),
                  pl.BlockSpec((d, bn), lambda i, j: (0, j))],
        out_specs=pl.BlockSpec((bm, bn), lambda i, j: (i, j)),
        out_shape=jax.ShapeDtypeStruct((m, n), F32),
        scratch_shapes=[pltpu.VMEM((bm, d), BF16)],
        compiler_params=_params("parallel", "arbitrary"),
        name="norm_mm",
    )(x, g, w)


def _rope(x, cos, sin):
    return x * cos + pltpu.roll(x, 64, 1) * sin


def _even_prep_kernel(aq_ref, bcq_ref, ak_ref, ckv_ref, kpe_ref, idx_ref, cos_ref, sin_ref,
                      aqn_ref, akn_ref, qln_ref, kvln_ref, wuq_ref, bqn_ref,
                      qa_o, ka_o, qb_o, ckv_o, kpe_o, kpe64_o, idx64_o):
    for h in range(H_A):
        x = aq_ref[:, h * DH_A:(h + 1) * DH_A]
        qa_o[:, h * DH_A:(h + 1) * DH_A] = (x * _rs(x) * aqn_ref[...]).astype(BF16)
    for n in range(HKV_A):
        x = ak_ref[:, n * DH_A:(n + 1) * DH_A]
        ka_o[:, n * DH_A:(n + 1) * DH_A] = x * _rs(x) * akn_ref[...]
    c = ckv_ref[...]
    ckv_o[...] = c * _rs(c) * kvln_ref[...]
    cos = cos_ref[...]
    sin = sin_ref[...]
    kpe = _rope(kpe_ref[...], cos, sin)
    kpe_o[...] = kpe
    half = ROPE_DIM // 2
    kpe64_o[...] = jnp.concatenate([kpe[:, :half], kpe[:, 2 * half:3 * half]], axis=-1)
    idx64_o[...] = idx_ref[:, :D_IDX]
    cq = bcq_ref[...]
    cqn = (cq * _rs(cq) * qln_ref[...]).astype(BF16)
    qb = jnp.dot(cqn, wuq_ref[...], preferred_element_type=F32)
    g = bqn_ref[...]
    for h in range(H_B):
        nope = qb[:, h * 256:h * 256 + 128]
        rot = _rope(qb[:, h * 256 + 128:(h + 1) * 256], cos, sin)
        ss = jnp.sum(nope * nope + rot * rot, -1, keepdims=True)
        r = lax.rsqrt(ss / QK_B + 1e-6)
        qb_o[:, h * 256:h * 256 + 128] = (nope * r * g[:, :128]).astype(BF16)
        qb_o[:, h * 256 + 128:(h + 1) * 256] = (rot * r * g[:, 128:]).astype(BF16)


def _even_prep(z, cos, sin, aqn, akn, qln, kvln, wuq, bqn, bs):
    b, s, _ = z.shape

    def zspec(width, off):
        return pl.BlockSpec((None, bs, width), lambda bi, si: (bi, si, off // width))

    def ospec(width):
        return pl.BlockSpec((None, bs, width), lambda bi, si: (bi, si, 0))

    pos_spec = pl.BlockSpec((bs, LANE), lambda bi, si: (si, 0))
    return pl.pallas_call(
        _even_prep_kernel,
        grid=(b, s // bs),
        in_specs=[zspec(1024, E_AQ), zspec(512, E_BCQ), zspec(256, E_AK), zspec(256, E_CKV),
                  zspec(128, E_KPE), zspec(128, E_IDX), pos_spec, pos_spec,
                  _const_spec((1, DH_A)), _const_spec((1, DH_A)), _const_spec((1, Q_LORA)),
                  _const_spec((1, KV_LORA)), _const_spec((Q_LORA, H_B * 256)), _const_spec((1, 256))],
        out_specs=[ospec(1024), ospec(256), ospec(2048), ospec(256), ospec(128), ospec(ROPE_DIM), ospec(D_IDX)],
        out_shape=[jax.ShapeDtypeStruct((b, s, 1024), BF16), jax.ShapeDtypeStruct((b, s, 256), F32),
                   jax.ShapeDtypeStruct((b, s, 2048), BF16), jax.ShapeDtypeStruct((b, s, 256), F32),
                   jax.ShapeDtypeStruct((b, s, 128), F32), jax.ShapeDtypeStruct((b, s, ROPE_DIM), F32),
                   jax.ShapeDtypeStruct((b, s, D_IDX), F32)],
        compiler_params=_params("parallel", "arbitrary"),
        name="even_prep",
    )(z, z, z, z, z, z, cos, sin, aqn, akn, qln, kvln, wuq, bqn)


def _mla_kv_kernel(ckv_ref, kpe_ref, w_ref, g_ref, kb_o, vb_o):
    kv = jnp.dot(ckv_ref[...].astype(BF16), w_ref[...], preferred_element_type=F32)
    kp = kpe_ref[...]
    skp = jnp.sum(kp * kp, -1, keepdims=True)
    g = g_ref[...]
    for h in range(H_B):
        nope = kv[:, h * NOPE:(h + 1) * NOPE]
        r = lax.rsqrt((jnp.sum(nope * nope, -1, keepdims=True) + skp) / QK_B + 1e-6)
        kb_o[:, h * 256:h * 256 + 128] = (nope * r * g[:, :128]).astype(BF16)
        kb_o[:, h * 256 + 128:(h + 1) * 256] = (kp * r * g[:, 128:]).astype(BF16)
    vb_o[...] = kv[:, H_B * NOPE:].astype(BF16)


def _mla_kv(ckv_all, kpe_all, wukv, bkn, bl):
    b, lp, _ = ckv_all.shape
    return pl.pallas_call(
        _mla_kv_kernel,
        grid=(b, lp // bl),
        in_specs=[pl.BlockSpec((None, bl, KV_LORA), lambda bi, li: (bi, li, 0)),
                  pl.BlockSpec((None, bl, LANE), lambda bi, li: (bi, li, 0)),
                  _const_spec((KV_LORA, 2048)), _const_spec((1, 256))],
        out_specs=[pl.BlockSpec((None, bl, 2048), lambda bi, li: (bi, li, 0)),
                   pl.BlockSpec((None, bl, 1024), lambda bi, li: (bi, li, 0))],
        out_shape=[jax.ShapeDtypeStruct((b, lp, 2048), BF16), jax.ShapeDtypeStruct((b, lp, 1024), BF16)],
        compiler_params=_params("parallel", "arbitrary"),
        name="mla_kv",
    )(ckv_all, kpe_all, wukv, bkn)


def _key_chunk(lp):
    for c in (512, 384, 256, 128):
        if lp % c == 0:
            return c
    raise ValueError(lp)


def _dsa_kernel(qa_ref, iq_ref, sa_ref, ag_ref, ka_ref, va_ref, kidx_ref, bias_ref, acc_ref, o_ref,
                key_ref, hi_ref, lo_ref, *, bq, lp, l_true, q0, ksel, d_min, d_max, far, blk0):
    i = pl.program_id(1) + blk0
    q_start = q0 + i * bq
    qpos = q_start + lax.broadcasted_iota(jnp.int32, (bq, 1), 0)
    ck = _key_chunk(lp)

    iq = iq_ref[...].astype(BF16)
    a = jnp.concatenate([iq[:, p * LANE:(p + 1) * LANE] for p in range(H_IDX // 2)], axis=0)
    wi = sa_ref[...] * (H_IDX ** -0.5 * D_IDX ** -0.5)
    for c0 in range(0, lp, ck):
        kk = kidx_ref[c0:c0 + ck, :]
        lane = lax.broadcasted_iota(jnp.int32, kk.shape, 1)
        k_lo = jnp.where(lane < D_IDX, kk, 0.0).astype(BF16)
        k_hi = jnp.where(lane >= D_IDX, pltpu.roll(kk, D_IDX, 1), 0.0).astype(BF16)
        s_lo = lax.dot_general(a, k_lo, NT_DIMS, preferred_element_type=F32)
        s_hi = lax.dot_general(a, k_hi, NT_DIMS, preferred_element_type=F32)
        sc = jnp.zeros((bq, ck), F32)
        for p in range(H_IDX // 2):
            w0 = wi[:, D_IDX + 2 * p:D_IDX + 2 * p + 1]
            w1 = wi[:, D_IDX + 2 * p + 1:D_IDX + 2 * p + 2]
            sc = sc + w0 * jnp.maximum(s_lo[p * bq:(p + 1) * bq], 0.0)
            sc = sc + w1 * jnp.maximum(s_hi[p * bq:(p + 1) * bq], 0.0)
        kpos = c0 + lax.broadcasted_iota(jnp.int32, (1, ck), 1)
        valid = ((kpos >> CHUNK_SHIFT) <= (qpos >> CHUNK_SHIFT)) & (kpos < l_true)
        bits = lax.bitcast_convert_type(sc, jnp.int32)
        key = jnp.where(bits < 0, bits ^ 0x7FFFFFFF, bits)
        key = jnp.where(bits == INT_MIN, 0, key)
        key = jnp.where(valid, key, INT_MIN)
        key_ref[:, c0:c0 + ck] = key
        hi_ref[:, c0:c0 + ck] = (key >> 16) + 2 ** 15
        lo_ref[:, c0:c0 + ck] = key & 0xFFFF

    ng = SEARCH_GROUPS
    gr = bq // ng
    groups = [slice(r * gr, (r + 1) * gr) for r in range(ng)]

    def count_ge(ref, rows, cand):
        acc = None
        for j in range(lp // LANE):
            d = (ref[rows, j * LANE:(j + 1) * LANE] - cand) >> 31
            acc = d if acc is None else acc + d
        return lp + jnp.sum(acc.astype(F32), axis=-1, keepdims=True)

    def search16(ref, need):
        ts = [jnp.zeros((gr, 1), jnp.int32)] * ng
        for bit in reversed(range(16)):
            for r in range(ng):
                cand = ts[r] | (1 << bit)
                ts[r] = jnp.where(count_ge(ref, groups[r], cand) >= need[r], cand, ts[r])
        return ts

    if lp > ksel:
        t_hi = search16(hi_ref, (float(ksel),) * ng)
        need_lo = []
        for rows, t in zip(groups, t_hi):
            need_lo.append(ksel - count_ge(hi_ref, rows, t + 1))
            lo_ref[rows, :] = jnp.where(hi_ref[rows, :] == t, lo_ref[rows, :], -1)
        t_lo = search16(lo_ref, need_lo)
        t_s = jnp.concatenate([lax.shift_left(h - 2 ** 15, 16) | l for h, l in zip(t_hi, t_lo)], axis=0)
        key = key_ref[...]
        n_picked = jnp.sum(((key >= t_s) & (key > INT_MIN)).astype(F32), axis=-1, keepdims=True)

        @pl.when(jnp.max(n_picked) > ksel)
        def _():
            key = key_ref[...]
            rpos = (lp - 1) - lax.broadcasted_iota(jnp.int32, (bq, lp), 1)
            lo_ref[...] = jnp.where((key == t_s) & (key > INT_MIN), rpos, -1)
            need = [ksel - jnp.sum((key[rows] > t_s[rows]).astype(F32), axis=-1, keepdims=True) for rows in groups]
            t_pos = jnp.concatenate(search16(lo_ref, need), axis=0)
            tie = lo_ref[...]
            key_ref[...] = jnp.where((tie >= 0) & (tie < t_pos), INT_MIN, key)
    else:
        t_s = jnp.full((bq, 1), INT_MIN, jnp.int32)
    key = key_ref[...]
    sel = (key >= t_s) & (key > INT_MIN)

    d_base = -(q0 // LANE + i * (bq // LANE))
    nk = lp // LANE
    for n in range(HKV_A):
        q4 = jnp.concatenate([qa_ref[:, (n * GROUP_A + g) * DH_A:(n * GROUP_A + g + 1) * DH_A]
                              for g in range(GROUP_A)], axis=0)
        k_n = ka_ref[:, n * DH_A:(n + 1) * DH_A].astype(BF16)
        v_n = va_ref[:, n * DH_A:(n + 1) * DH_A].astype(BF16)
        lg4 = lax.dot_general(q4, k_n, NT_DIMS, preferred_element_type=F32) * DH_A ** -0.5
        ps, ss = [], []
        for g in range(GROUP_A):
            bias = jnp.concatenate(
                [jnp.concatenate([bias_ref[n * GROUP_A + g,
                                           jnp.maximum(jnp.minimum(d_base + j - u, d_max) - (d_min + far), 0)]
                                  for j in range(nk)], axis=1)
                 for u in range(bq // bias_ref.shape[2])], axis=0)
            lg = jnp.where(sel, lg4[g * bq:(g + 1) * bq] + bias, -jnp.inf)
            p = jnp.exp(lg - jnp.max(lg, axis=-1, keepdims=True))
            ss.append(jnp.sum(p, axis=-1, keepdims=True))
            ps.append(p.astype(BF16))
        o4 = jnp.dot(jnp.concatenate(ps, axis=0), v_n, preferred_element_type=F32)
        for g in range(GROUP_A):
            h = n * GROUP_A + g
            gate = ag_ref[:, h * DH_A:(h + 1) * DH_A]
            o_ref[:, h * DH_A:(h + 1) * DH_A] = (o4[g * bq:(g + 1) * bq] / ss[g] * _silu(gate)).astype(BF16)


def _dsa(qa, z, ka_all, va_all, kidx_all, kidx_col, bias_tiles, acc, *, bq, l_true, q0, ksel, row0, rows, lp):
    b, s, _ = qa.shape
    bias_tiles, d_min, d_max, far = bias_tiles
    blk0 = row0 // bq
    assert q0 % LANE == 0 and (bq % LANE == 0 or s == bq)
    assert d_min <= -((q0 + s - bias_tiles.shape[2]) // LANE) and bq % bias_tiles.shape[2] == 0
    kern = functools.partial(_dsa_kernel, bq=bq, lp=lp, l_true=l_true, q0=q0, ksel=ksel,
                             d_min=d_min, d_max=d_max, far=far, blk0=blk0)

    def zspec(width, off):
        return pl.BlockSpec((None, bq, width), lambda bi, qi: (bi, qi + blk0, off // width))

    def kspec(width):
        return pl.BlockSpec((None, lp, width), lambda bi, qi: (bi, 0, 0))

    return pl.pallas_call(
        kern,
        grid=(b, rows // bq),
        in_specs=[pl.BlockSpec((None, bq, 1024), lambda bi, qi: (bi, qi + blk0, 0)),
                  zspec(1024, E_IQ), zspec(128, E_IDX), zspec(1024, E_AG),
                  kspec(256), kspec(256),
                  pl.BlockSpec((None, lp, LANE), lambda bi, qi: (bi, 0, kidx_col)),
                  _const_spec(bias_tiles.shape),
                  pl.BlockSpec(memory_space=pl.ANY)],
        out_specs=pl.BlockSpec((None, bq, 1024), lambda bi, qi: (bi, qi + blk0, 0)),
        out_shape=jax.ShapeDtypeStruct(acc.shape, acc.dtype),
        input_output_aliases={8: 0},
        scratch_shapes=[pltpu.VMEM((bq, lp), jnp.int32)] * 3,
        compiler_params=_params("parallel", "arbitrary"),
        name="dsa",
    )(qa, z, z, z, ka_all, va_all, kidx_all, bias_tiles, acc)


def _causal_classes(s, lp, q0):
    if q0 == 0 and s % CLASS_ROWS == 0 and lp == s:
        return [(c * CLASS_ROWS, CLASS_ROWS, (c + 1) * CLASS_ROWS) for c in range(s // CLASS_ROWS)]
    return [(0, s, lp)]


def _mla_kernel(qb_ref, bg_ref, kb_ref, vb_ref, acc_ref, o_ref, *, bq, lp, l_true, q0, blk0, n_full):
    q_start = q0 + (pl.program_id(1) + blk0) * bq
    qpos = q_start + lax.broadcasted_iota(jnp.int32, (bq, 1), 0)
    kpos = n_full + lax.broadcasted_iota(jnp.int32, (1, lp - n_full), 1)
    valid = ((kpos >> CHUNK_SHIFT) <= (qpos >> CHUNK_SHIFT)) & (kpos < l_true)
    for h in range(H_B):
        q = qb_ref[:, h * 256:(h + 1) * 256]
        parts = []
        if n_full:
            parts.append((lax.dot_general(q, kb_ref[:n_full, h * 256:(h + 1) * 256], NT_DIMS,
                                          preferred_element_type=F32), vb_ref[:n_full, h * V_DIM:(h + 1) * V_DIM]))
        lg = lax.dot_general(q, kb_ref[n_full:, h * 256:(h + 1) * 256], NT_DIMS, preferred_element_type=F32)
        parts.append((jnp.where(valid, lg, -jnp.inf), vb_ref[n_full:, h * V_DIM:(h + 1) * V_DIM]))
        o = _softmax_pv(parts, QK_B ** -0.5)
        gate = bg_ref[:, h * V_DIM:(h + 1) * V_DIM]
        o_ref[:, h * V_DIM:(h + 1) * V_DIM] = (o * _silu(gate)).astype(BF16)


def _mla(qb, z, kb, vb, acc, *, bq, l_true, q0, row0, rows, lp):
    b = qb.shape[0]
    blk0 = row0 // bq
    n_full = min(min(q0 + row0 + CHUNK, l_true) // LANE * LANE, lp - LANE)
    kern = functools.partial(_mla_kernel, bq=bq, lp=lp, l_true=l_true, q0=q0, blk0=blk0, n_full=n_full)
    return pl.pallas_call(
        kern,
        grid=(b, rows // bq),
        in_specs=[pl.BlockSpec((None, bq, 2048), lambda bi, qi: (bi, qi + blk0, 0)),
                  pl.BlockSpec((None, bq, 1024), lambda bi, qi: (bi, qi + blk0, E_BG // 1024)),
                  pl.BlockSpec((None, lp, 2048), lambda bi, qi: (bi, 0, 0)),
                  pl.BlockSpec((None, lp, 1024), lambda bi, qi: (bi, 0, 0)),
                  pl.BlockSpec(memory_space=pl.ANY)],
        out_specs=pl.BlockSpec((None, bq, 1024), lambda bi, qi: (bi, qi + blk0, 0)),
        out_shape=jax.ShapeDtypeStruct(acc.shape, acc.dtype),
        input_output_aliases={4: 0},
        compiler_params=_params("parallel", "arbitrary"),
        name="mla",
    )(qb, z, kb, vb, acc)


def _out_ple_kernel(h_ref, oa_ref, ob_ref, p_ref, wo_ref, wg_ref, wp_ref, o_ref):
    half = oa_ref.shape[-1]
    h1 = (h_ref[...]
          + jnp.dot(oa_ref[...], wo_ref[:half, :], preferred_element_type=F32)
          + jnp.dot(ob_ref[...], wo_ref[half:, :], preferred_element_type=F32))
    r = (h1 * _rs(h1)).astype(BF16)
    gate = 1.0 / (1.0 + jnp.exp(-jnp.dot(r, wg_ref[...], preferred_element_type=F32)))
    o_ref[...] = h1 + gate * jnp.dot(p_ref[...].astype(BF16), wp_ref[...], preferred_element_type=F32)


def _out_ple(h, oa, ob, p_all, layer, wo, wg, wp, bm):
    m, d = h.shape
    half = oa.shape[1]

    def rows(width):
        return pl.BlockSpec((bm, width), lambda i: (i, 0))

    return pl.pallas_call(
        _out_ple_kernel,
        grid=(m // bm,),
        in_specs=[rows(d), rows(half), rows(half), pl.BlockSpec((None, bm, PLE_DIM), lambda i: (layer, i, 0)),
                  _const_spec((2 * half, d)), _const_spec((d, d)), _const_spec((PLE_DIM, d))],
        out_specs=rows(d),
        out_shape=jax.ShapeDtypeStruct((m, d), F32),
        compiler_params=_params("parallel"),
        name="out_ple",
    )(h, oa, ob, p_all, wo, wg, wp)


def _odd_prep_kernel(cq_ref, ck_ref, cv_ref, dv_ref, du_ref, dgate_ref, cqn_ref, ckn_ref, dg_ref, db_ref,
                     ws_ref, bs_ref, qc_o, kcb_o, vcb_o, od_o, kct_o, cvt_o, *maybe_dvn_o, n):
    for h in range(H_C):
        sl = slice(h * DH_C, (h + 1) * DH_C)
        x = cq_ref[:, sl]
        qc_o[:, sl] = (x * _rs(x) * cqn_ref[...]).astype(BF16)
        x = ck_ref[:, sl]
        kn = x * _rs(x) * ckn_ref[...]
        kct_o[:, sl] = kn
        kcb_o[:, sl] = kn.astype(BF16)
    cv = cv_ref[...]
    cvt_o[...] = cv
    vcb_o[...] = cv.astype(BF16)
    dv = dv_ref[...]
    xc = dv - jnp.mean(dv, -1, keepdims=True)
    var = jnp.mean(xc * xc, -1, keepdims=True)
    dvn = xc * lax.rsqrt(var + 1e-5) * dg_ref[...] + db_ref[...]
    for dvn_o in maybe_dvn_o:
        dvn_o[...] = dvn
    dvn = dvn.astype(BF16)
    row = lax.broadcasted_iota(jnp.int32, (n, n), 0)
    col = lax.broadcasted_iota(jnp.int32, (n, n), 1)
    for g in range(G_D):
        sl = slice(g * DG_D, (g + 1) * DG_D)
        w = jnp.where(col <= row, ws_ref[g], 0.0).astype(BF16)
        for c in range(dv.shape[0] // n):
            rs = slice(c * n, (c + 1) * n)
            sg = jnp.dot(w, dvn[rs, sl], preferred_element_type=F32) + bs_ref[:, g:g + 1]
            od_o[rs, sl] = (du_ref[rs, sl] * sg * _silu(dgate_ref[rs, sl])).astype(BF16)


def _odd_prep(z, cqn, ckn, dg, db, ws, bs_t, bs, n, keep, want_dvn):
    b, s, _ = z.shape
    first_tail = (s - keep) // bs
    assert keep % bs == 0

    def zspec(off):
        return pl.BlockSpec((None, bs, 1024), lambda bi, si: (bi, si, off // 1024))

    ospec = pl.BlockSpec((None, bs, 1024), lambda bi, si: (bi, si, 0))
    tspec = pl.BlockSpec((None, bs, 1024), lambda bi, si: (bi, jnp.maximum(si - first_tail, 0), 0))
    act = lambda dt: jax.ShapeDtypeStruct((b, s, 1024), dt)
    tail = jax.ShapeDtypeStruct((b, keep, 1024), F32)
    return pl.pallas_call(
        functools.partial(_odd_prep_kernel, n=n),
        grid=(b, s // bs),
        in_specs=[zspec(O_CQ), zspec(O_CK), zspec(O_CV), zspec(O_DV), zspec(O_DU), zspec(O_DG),
                  _const_spec((1, DH_C)), _const_spec((1, DH_C)), _const_spec((1, W_D)), _const_spec((1, W_D)),
                  _const_spec((G_D, n, n)), _const_spec((n, G_D))],
        out_specs=[ospec] * 4 + [tspec] * 2 + [ospec] * want_dvn,
        out_shape=[act(BF16)] * 4 + [tail] * 2 + [act(F32)] * want_dvn,
        compiler_params=_params("parallel", "arbitrary"),
        name="odd_prep",
    )(z, z, z, z, z, z, cqn, ckn, dg, db, ws, bs_t)


BACK_TILES = C_BACK * CHUNK // LANE


def _band_width(bq):
    return (BACK_TILES + -(-bq // LANE)) * LANE


def _band_kernel(q_ref, cg_ref, k_ref, v_ref, *rest, bq, sliding, q0, k0, k_end, c0):
    i = pl.program_id(1)
    q_start = q0 + i * bq
    bw = _band_width(bq)
    if sliding:
        cbias_ref, bias_ref, o_ref = rest
        first = i * (bq // LANE) - BACK_TILES
        win_start = first * LANE
        rows = [pl.ds(pl.multiple_of(jnp.maximum(first + t, 0) * LANE, LANE), LANE) for t in range(bw // LANE)]
        kw = jnp.concatenate([k_ref[r, :] for r in rows], axis=0)
        vw = jnp.concatenate([v_ref[r, :] for r in rows], axis=0)
    else:
        kn_ref, vn_ref, cbias_ref, bias_ref, o_ref = rest
        win_start = k0
        pad = jnp.zeros((bw - k_ref.shape[0] - bq, k_ref.shape[1]), BF16)
        kw = jnp.concatenate([k_ref[...].astype(BF16), kn_ref[...], pad], axis=0)
        vw = jnp.concatenate([v_ref[...].astype(BF16), vn_ref[...], pad], axis=0)
    qpos = q_start + lax.broadcasted_iota(jnp.int32, (bq, 1), 0)

    def valid(lo, hi):
        kpos = win_start + lo + lax.broadcasted_iota(jnp.int32, (1, hi - lo), 1)
        dc = (qpos >> CHUNK_SHIFT) - (kpos >> CHUNK_SHIFT)
        return (dc >= 0) & (dc <= C_BACK) & (kpos >= 0) & (kpos < k_end)

    segs = [(lo, hi, valid(lo, hi)) for lo, hi in ((0, c0), (c0, bw)) if hi > lo]
    for h in range(H_C):
        sl = slice(h * DH_C, (h + 1) * DH_C)
        q = q_ref[:, sl]
        parts = []
        for lo, hi, ok in segs:
            bias = cbias_ref[h][:, :1] if hi <= c0 else bias_ref[h]
            lg = lax.dot_general(q, kw[lo:hi, sl], NT_DIMS, preferred_element_type=F32) * DH_C ** -0.5
            parts.append((jnp.where(ok, lg + bias, -jnp.inf), vw[lo:hi, sl]))
        o = _softmax_pv(parts)
        o_ref[:, sl] = (o * _silu(cg_ref[:, sl])).astype(BF16)


def _band(qc, z, keys, vals, bias, *, bq, sliding, q0, k0, k_end):
    c0, cbias, bias = bias
    b, s, _ = qc.shape
    if sliding:
        assert bq % LANE == 0 and q0 == 0 and k0 == 0
        kv = [keys, vals]
    else:
        assert keys[0].shape[1] + bq <= _band_width(bq) and s == bq
        kv = [keys[0], vals[0], keys[1], vals[1]]
    kern = functools.partial(_band_kernel, bq=bq, sliding=sliding, q0=q0, k0=k0, k_end=k_end, c0=c0)
    return pl.pallas_call(
        kern,
        grid=(b, s // bq),
        in_specs=[pl.BlockSpec((None, bq, 1024), lambda bi, qi: (bi, qi, 0)),
                  pl.BlockSpec((None, bq, 1024), lambda bi, qi: (bi, qi, O_CG // 1024))]
                 + [pl.BlockSpec((None,) + a.shape[1:], lambda bi, qi: (bi, 0, 0)) for a in kv]
                 + [_const_spec(cbias.shape), _const_spec(bias.shape)],
        out_specs=pl.BlockSpec((None, bq, 1024), lambda bi, qi: (bi, qi, 0)),
        out_shape=jax.ShapeDtypeStruct((b, s, 1024), BF16),
        compiler_params=_params("parallel", "arbitrary"),
        name="band",
    )(qc, z, *kv, cbias, bias)


def _rope_tables(pos):
    half = ROPE_DIM // 2
    freq = ROPE_BASE ** (-jnp.arange(half, dtype=F32) / half)
    ang = pos.astype(F32)[:, None] * freq[None, :]
    cos, sin = jnp.cos(ang), jnp.sin(ang)
    z = jnp.zeros_like(cos)
    return jnp.concatenate([cos, z, cos, z], 1), jnp.concatenate([-sin, z, sin, z], 1)


def _rope_lanes(x):
    half = ROPE_DIM // 2
    z = jnp.zeros(x.shape[:-1] + (half,), x.dtype)
    return jnp.concatenate([x[..., :half], z, x[..., half:], z], -1)


def _t5_bucket_np(rel):
    nb = T5_BUCKETS // 2
    max_exact = nb // 2
    n = np.abs(rel)
    nf = np.maximum(n, 1).astype(np.float64)
    large = max_exact + (np.log(nf / max_exact) / math.log(T5_MAX_DIST / max_exact) * (nb - max_exact)).astype(np.int64)
    large = np.minimum(large, nb - 1)
    return np.where(rel > 0, nb, 0) + np.where(n < max_exact, n, large)


def _toeplitz(w, rows, width, cols):
    flat = jnp.tile(w, (1,) * (w.ndim - 1) + (rows,))[..., :rows * width]
    return flat.reshape(w.shape[:-1] + (rows, width))[..., :cols]


def _t5_tiles(t5_bias, bq, d_min):
    d_max = max(bq // LANE - 1, 0)
    width = LANE + max(bq, LANE)
    k = np.arange(width + 1)
    delta = np.where(k < LANE, k, k - (width + 1))
    rel = np.arange(d_min, d_max + 1)[:, None] * LANE + delta[None, :]
    bucket = _t5_bucket_np(rel)
    far = 0
    while far + 1 < len(bucket) and np.array_equal(bucket[far + 1], bucket[0]):
        far += 1
    w = jnp.transpose(t5_bias[bucket[far:]], (2, 0, 1))
    return _toeplitz(w, bq, width, LANE), d_min, d_max, far


def _band_bias(rel_tab, bq, qk_off):
    bw = _band_width(bq)
    rel_index = lambda rel: np.clip(rel, -(CHUNK - 1), REL_CLIP) + (CHUNK - 1)
    full = rel_index(qk_off + np.arange(bq)[:, None] - np.arange(bw)[None, :])
    c0 = 0
    while c0 + 2 * MXU_COLS <= bw and np.all(full[:, :c0 + MXU_COLS] == full[0, 0]):
        c0 += MXU_COLS
    const = jnp.broadcast_to(rel_tab[full[0, 0]][:, None, None], (H_C, 1, LANE))
    wv = bw - c0
    width = wv + bq
    k = np.arange(width + 1)
    delta = np.where(k < wv, k, k - (width + 1))
    idx = rel_index(qk_off - c0 - delta)
    return c0, const, _toeplitz(jnp.transpose(rel_tab[idx], (1, 0)), bq, width, wv)


def _even_weights(w_in, b_wuq, b_wukv, b_qn, b_kn):
    d = w_in.shape[0]
    offs = np.cumsum((0,) + EVEN_SPLITS)
    w16 = w_in.astype(BF16)
    aq, ak, av, ag, iq, ik, iw, bcq, bckv, bkpe, bg = [w16[:, offs[t]:offs[t + 1]] for t in range(11)]
    slab_idx = jnp.concatenate([ik, iw, jnp.zeros((d, LANE - D_IDX - H_IDX), BF16)], 1)
    w = jnp.concatenate([aq, ag, iq, bg, bcq, ak, av, bckv, slab_idx, _rope_lanes(bkpe)], 1)
    uq = b_wuq.reshape(Q_LORA, H_B, QK_B)
    uq = jnp.concatenate([uq[..., :NOPE], _rope_lanes(uq[..., NOPE:])], -1).reshape(Q_LORA, H_B * 256).astype(BF16)
    ukv = b_wukv.reshape(KV_LORA, H_B, NOPE + V_DIM)
    ukv = jnp.concatenate([ukv[..., :NOPE].reshape(KV_LORA, H_B * NOPE),
                           ukv[..., NOPE:].reshape(KV_LORA, H_B * V_DIM)], 1).astype(BF16)
    pad_gain = lambda g: jnp.concatenate([g[:NOPE], _rope_lanes(g[NOPE:])])[None, :]
    return w, uq, ukv, pad_gain(b_qn), pad_gain(b_kn)


def _pad_rows(x, lp):
    return jnp.pad(x, ((0, 0), (0, lp - x.shape[1]), (0, 0)))


def _even_layer(h, p_all, layer, past, q0, ln_g, w_in, uq, ukv, a_qn, a_kn, t5_bias, b_qln, b_kvln, bqn, bkn, wo, wg, wp):
    b, s, d = h.shape
    m = b * s
    z = _norm_mm(h.reshape(m, d), ln_g[None, :], w_in, min(m, 1024), E_END // 4).reshape(b, s, E_END)
    cos, sin = _rope_tables(q0 + jnp.arange(s, dtype=jnp.int32))
    bs = min(s, 256)
    qa, ka, qb, ckv, kpe_l, kpe64, idx64 = _even_prep(z, cos, sin, a_qn[None, :], a_kn[None, :], b_qln[None, :],
                                                      b_kvln[None, :], uq, bqn, bs)
    av = z[..., E_AV:E_AV + 256]
    new = (ka.reshape(b, s, HKV_A, DH_A), av.reshape(b, s, HKV_A, DH_A), idx64, ckv, kpe64)
    if past is None:
        l_true = s
        ka_all, va_all, ckv_all, kpe_all = ka, av, ckv, kpe_l
        kidx_all, kidx_col = z, E_IDX // LANE
    else:
        kidx = z[..., E_IDX:E_IDX + LANE]
        kidx_col = 0
        c_k, c_v, c_ik, c_ckv, c_kpe = past
        pl_ = c_k.shape[1]
        l_true = pl_ + s
        lp = -(-l_true // LANE) * LANE
        cat = lambda c, n_: _pad_rows(jnp.concatenate([c, n_], 1), lp)
        ka_all = cat(c_k.reshape(b, pl_, 256), ka)
        va_all = cat(c_v.reshape(b, pl_, 256), av)
        kidx_all = cat(jnp.pad(c_ik, ((0, 0), (0, 0), (0, LANE - D_IDX))), kidx)
        ckv_all = cat(c_ckv, ckv)
        kpe_all = cat(_rope_lanes(c_kpe), kpe_l)
    lp = ka_all.shape[1]
    ksel = min(TOPK_MAX, l_true // 4)
    classes = _causal_classes(s, lp, q0)
    bqs = [nr if (nr % LANE == 0 and lc <= DSA_WIDE_BLOCK_KEYS) else min(s, LANE) for _, nr, lc in classes]
    tb = min(s, LANE)
    tiles = _t5_tiles(t5_bias, tb, -((q0 + s - tb) // LANE))
    o_a = jnp.zeros((b, s, H_A * DH_A), BF16)
    for (r0, nr, lc), bq in zip(classes, bqs):
        o_a = _dsa(qa, z, ka_all, va_all, kidx_all, kidx_col, tiles, o_a, bq=bq, l_true=l_true, q0=q0,
                   ksel=ksel, row0=r0, rows=nr, lp=lc)
    kb, vb = _mla_kv(ckv_all, kpe_all, ukv, bkn, _key_chunk(lp))
    o_b = jnp.zeros((b, s, H_B * V_DIM), BF16)
    for r0, nr, lc in classes:
        o_b = _mla(qb, z, kb, vb, o_b, bq=min(s, 256), l_true=l_true, q0=q0, row0=r0, rows=nr, lp=lc)
    y = _out_ple(h.reshape(m, d), o_a.reshape(m, -1), o_b.reshape(m, -1), p_all, layer, wo, wg, wp, min(m, 512))
    return y.reshape(b, s, d), new


def _odd_layer(h, p_all, layer, past, q0, ln_g, w_in, c_qn, c_kn, c_rel, d_g, d_b, d_ws, d_bs, wo, wg, wp):
    b, s, d = h.shape
    m = b * s
    z = _norm_mm(h.reshape(m, d), ln_g[None, :], w_in, min(m, 1024), 1024).reshape(b, s, O_END)
    n = min(s, D_CHUNK)
    keep = min(C_BACK * CHUNK, s) if past is None else s
    qc, kcb, vcb, o_d, kct, cvt, *dvn = _odd_prep(z, c_qn[None, :], c_kn[None, :], d_g[None, :], d_b[None, :],
                                                  d_ws[:, :n, :n], d_bs[:, :n].T, min(s, 256), n, keep,
                                                  want_dvn=past is not None)
    c_new = (kct.reshape(b, keep, H_C, DH_C), cvt.reshape(b, keep, H_C, DH_C))
    if past is None:
        bqc = 2 * LANE
        bias = _band_bias(c_rel, bqc, C_BACK * CHUNK)
        o_c = _band(qc, z, kcb, vcb, bias, bq=bqc, sliding=True, q0=0, k0=0, k_end=s)
    else:
        nc = past[0].shape[1]
        bias = _band_bias(c_rel, s, nc)
        o_c = _band(qc, z, (past[0].reshape(b, nc, 1024), kcb), (past[1].reshape(b, nc, 1024), vcb), bias,
                    bq=s, sliding=False, q0=q0, k0=q0 - nc, k_end=q0 + s)
    y = _out_ple(h.reshape(m, d), o_c.reshape(m, -1), o_d.reshape(m, -1), p_all, layer, wo, wg, wp, min(m, 512))
    return y.reshape(b, s, d), c_new, (dvn[0] if dvn else None)


def kernel(x_prompt, x_sample, cache_a_k, cache_a_v, cache_a_idx_k, cache_b_ckv, cache_b_kpe, cache_c_k, cache_c_v, p_prompt, p_sample, ln_g, w_in_even, a_q_norm, a_k_norm, t5_bias, b_q_lora_norm, b_kv_lora_norm, b_w_uq, b_w_ukv, b_q_norm, b_k_norm, w_out_even, w_in_odd, c_q_norm, c_k_norm, c_rel_bias, d_ln_g, d_ln_b, d_w_s, d_b_s, w_out_odd, ple_proj, ple_gate):
    depth = ln_g.shape[0]
    past_len = cache_a_k.shape[2]
    hp, hs = x_prompt, x_sample
    pp = p_prompt.reshape(depth, -1, PLE_DIM)
    ps = p_sample.reshape(depth, -1, PLE_DIM)
    ev_p, ev_s, od_p, od_s, dv_s = [], [], [], [], []
    for i in range(depth):
        j = i // 2
        wg = ple_gate[i].astype(BF16)
        wp = ple_proj[i].astype(BF16)
        if i % 2 == 0:
            w_in, uq, ukv, bqn, bkn = _even_weights(w_in_even[j], b_w_uq[j], b_w_ukv[j], b_q_norm[j], b_k_norm[j])
            w = (ln_g[i], w_in, uq, ukv, a_q_norm[j], a_k_norm[j], t5_bias, b_q_lora_norm[j], b_kv_lora_norm[j],
                 bqn, bkn, w_out_even[j].astype(BF16), wg, wp)
            hp, sp = _even_layer(hp, pp, i, None, 0, *w)
            past = (cache_a_k[j], cache_a_v[j], cache_a_idx_k[j], cache_b_ckv[j], cache_b_kpe[j])
            hs, ss = _even_layer(hs, ps, i, past, past_len, *w)
            ev_p.append(sp)
            ev_s.append(ss)
        else:
            w = (ln_g[i], w_in_odd[j].astype(BF16), c_q_norm[j], c_k_norm[j], c_rel_bias[j], d_ln_g[j], d_ln_b[j],
                 d_w_s[j], d_b_s[j], w_out_odd[j].astype(BF16), wg, wp)
            hp, sp, _ = _odd_layer(hp, pp, i, None, 0, *w)
            hs, ss, dvs = _odd_layer(hs, ps, i, (cache_c_k[j], cache_c_v[j]), past_len, *w)
            od_p.append(sp)
            od_s.append(ss)
            dv_s.append(dvs)
    st = lambda lst, n_: jnp.stack([e[n_] for e in lst], 0)
    return (hp, hs, st(ev_p, 0), st(ev_p, 1), st(ev_p, 2), st(ev_p, 3), st(ev_p, 4), st(od_p, 0), st(od_p, 1),
            st(ev_s, 0), st(ev_s, 1), st(ev_s, 2), st(ev_s, 3), st(ev_s, 4), st(od_s, 0), st(od_s, 1),
            jnp.stack(dv_s, 0))
```

```python
import functools
import math

import numpy as np
import jax
import jax.numpy as jnp
from jax import lax
from jax.experimental import pallas as pl
from jax.experimental.pallas import tpu as pltpu

F32 = jnp.float32
BF16 = jnp.bfloat16
INT_MIN = -2 ** 31
LOG2E = math.log2(math.e)
CLASS_ROWS = 256
SEARCH_GROUPS = 4
DSA_WIDE_BLOCK_KEYS = 1280

D_MODEL = 2048
CHUNK = 64
CHUNK_SHIFT = 6
LANE = 128
MXU_COLS = 256
H_A, HKV_A, GROUP_A, DH_A = 8, 2, 4, 128
H_IDX, D_IDX = 16, 64
TOPK_MAX = 256
T5_BUCKETS, T5_MAX_DIST = 32, 128
H_B, Q_LORA, KV_LORA, NOPE, ROPE_DIM, V_DIM = 8, 512, 256, 128, 64, 128
ROPE_BASE = 10000.0
QK_B = NOPE + ROPE_DIM
H_C, DH_C, C_BACK, REL_CLIP = 8, 128, 8, 128
W_D, G_D, DG_D, D_CHUNK = 1024, 8, 128, 128
PLE_DIM = 256

EVEN_SPLITS = (H_A * DH_A, HKV_A * DH_A, HKV_A * DH_A, H_A * DH_A, H_IDX * D_IDX, D_IDX, H_IDX,
               Q_LORA, KV_LORA, ROPE_DIM, H_B * V_DIM)
E_AQ, E_AG, E_IQ, E_BG, E_BCQ, E_AK, E_AV, E_CKV, E_IDX, E_KPE, E_END = (
    0, 1024, 2048, 3072, 4096, 4608, 4864, 5120, 5376, 5504, 5632)
O_CQ, O_CK, O_CV, O_CG, O_DU, O_DV, O_DG, O_END = 0, 1024, 2048, 3072, 4096, 5120, 6144, 7168

VMEM_LIMIT_BYTES = 56 * 1024 * 1024
NT_DIMS = (((1,), (1,)), ((), ()))


def _params(*sem):
    return pltpu.CompilerParams(dimension_semantics=sem, vmem_limit_bytes=VMEM_LIMIT_BYTES)


def _const_spec(shape):
    zeros = (0,) * len(shape)
    return pl.BlockSpec(shape, lambda *_: zeros, pipeline_mode=pl.Buffered(1))


def _rs(x, n=None, eps=1e-6):
    n = x.shape[-1] if n is None else n
    return lax.rsqrt(jnp.sum(x * x, axis=-1, keepdims=True) / n + eps)


def _silu(x):
    return x * (1.0 / (1.0 + jnp.exp(-x)))


def _softmax_pv(parts, scale=1.0):
    m = functools.reduce(jnp.maximum, [jnp.max(lg, axis=-1, keepdims=True) for lg, _ in parts])
    o = s = None
    for lg, v in parts:
        p = jnp.exp2((lg - m) * (scale * LOG2E))
        ps = jnp.sum(p, axis=-1, keepdims=True)
        po = jnp.dot(p.astype(BF16), v, preferred_element_type=F32)
        o, s = (po, ps) if o is None else (o + po, s + ps)
    return o / s


def _norm_mm_kernel(x_ref, g_ref, w_ref, o_ref, xn_ref):
    @pl.when(pl.program_id(1) == 0)
    def _():
        x = x_ref[...]
        xn_ref[...] = (x * _rs(x) * g_ref[...]).astype(BF16)

    o_ref[...] = jnp.dot(xn_ref[...], w_ref[...], preferred_element_type=F32)


def _norm_mm(x, g, w, bm, bn):
    m, d = x.shape
    n = w.shape[1]
    return pl.pallas_call(
        _norm_mm_kernel,
        grid=(m // bm, n // bn),
        in_specs=[pl.BlockSpec((bm, d), lambda i, j: (i, 0)),
                  pl.BlockSpec((1, d), lambda i, j: (0, 0)),
                  pl.BlockSpec((d, bn), lambda i, j: (0, j))],
        out_specs=pl.BlockSpec((bm, bn), lambda i, j: (i, j)),
        out_shape=jax.ShapeDtypeStruct((m, n), F32),
        scratch_shapes=[pltpu.VMEM((bm, d), BF16)],
        compiler_params=_params("parallel", "arbitrary"),
        name="norm_mm",
    )(x, g, w)


def _rope(x, cos, sin):
    return x * cos + pltpu.roll(x, 64, 1) * sin


def _even_prep_kernel(aq_ref, bcq_ref, ak_ref, ckv_ref, kpe_ref, idx_ref, cos_ref, sin_ref,
                      aqn_ref, akn_ref, qln_ref, kvln_ref, wuq_ref, bqn_ref,
                      qa_o, ka_o, qb_o, ckv_o, kpe_o, kpe64_o, idx64_o):
    for h in range(H_A):
        x = aq_ref[:, h * DH_A:(h + 1) * DH_A]
        qa_o[:, h * DH_A:(h + 1) * DH_A] = (x * _rs(x) * aqn_ref[...]).astype(BF16)
    for n in range(HKV_A):
        x = ak_ref[:, n * DH_A:(n + 1) * DH_A]
        ka_o[:, n * DH_A:(n + 1) * DH_A] = x * _rs(x) * akn_ref[...]
    c = ckv_ref[...]
    ckv_o[...] = c * _rs(c) * kvln_ref[...]
    cos = cos_ref[...]
    sin = sin_ref[...]
    kpe = _rope(kpe_ref[...], cos, sin)
    kpe_o[...] = kpe
    half = ROPE_DIM // 2
    kpe64_o[...] = jnp.concatenate([kpe[:, :half], kpe[:, 2 * half:3 * half]], axis=-1)
    idx64_o[...] = idx_ref[:, :D_IDX]
    cq = bcq_ref[...]
    cqn = (cq * _rs(cq) * qln_ref[...]).astype(BF16)
    qb = jnp.dot(cqn, wuq_ref[...], preferred_element_type=F32)
    g = bqn_ref[...]
    for h in range(H_B):
        nope = qb[:, h * 256:h * 256 + 128]
        rot = _rope(qb[:, h * 256 + 128:(h + 1) * 256], cos, sin)
        ss = jnp.sum(nope * nope + rot * rot, -1, keepdims=True)
        r = lax.rsqrt(ss / QK_B + 1e-6)
        qb_o[:, h * 256:h * 256 + 128] = (nope * r * g[:, :128]).astype(BF16)
        qb_o[:, h * 256 + 128:(h + 1) * 256] = (rot * r * g[:, 128:]).astype(BF16)


def _even_prep(z, cos, sin, aqn, akn, qln, kvln, wuq, bqn, bs):
    b, s, _ = z.shape

    def zspec(width, off):
        return pl.BlockSpec((None, bs, width), lambda bi, si: (bi, si, off // width))

    def ospec(width):
        return pl.BlockSpec((None, bs, width), lambda bi, si: (bi, si, 0))

    pos_spec = pl.BlockSpec((bs, LANE), lambda bi, si: (si, 0))
    return pl.pallas_call(
        _even_prep_kernel,
        grid=(b, s // bs),
        in_specs=[zspec(1024, E_AQ), zspec(512, E_BCQ), zspec(256, E_AK), zspec(256, E_CKV),
                  zspec(128, E_KPE), zspec(128, E_IDX), pos_spec, pos_spec,
                  _const_spec((1, DH_A)), _const_spec((1, DH_A)), _const_spec((1, Q_LORA)),
                  _const_spec((1, KV_LORA)), _const_spec((Q_LORA, H_B * 256)), _const_spec((1, 256))],
        out_specs=[ospec(1024), ospec(256), ospec(2048), ospec(256), ospec(128), ospec(ROPE_DIM), ospec(D_IDX)],
        out_shape=[jax.ShapeDtypeStruct((b, s, 1024), BF16), jax.ShapeDtypeStruct((b, s, 256), F32),
                   jax.ShapeDtypeStruct((b, s, 2048), BF16), jax.ShapeDtypeStruct((b, s, 256), F32),
                   jax.ShapeDtypeStruct((b, s, 128), F32), jax.ShapeDtypeStruct((b, s, ROPE_DIM), F32),
                   jax.ShapeDtypeStruct((b, s, D_IDX), F32)],
        compiler_params=_params("parallel", "arbitrary"),
        name="even_prep",
    )(z, z, z, z, z, z, cos, sin, aqn, akn, qln, kvln, wuq, bqn)


def _mla_kv_kernel(ckv_ref, kpe_ref, w_ref, g_ref, kb_o, vb_o):
    kv = jnp.dot(ckv_ref[...].astype(BF16), w_ref[...], preferred_element_type=F32)
    kp = kpe_ref[...]
    skp = jnp.sum(kp * kp, -1, keepdims=True)
    g = g_ref[...]
    for h in range(H_B):
        nope = kv[:, h * NOPE:(h + 1) * NOPE]
        r = lax.rsqrt((jnp.sum(nope * nope, -1, keepdims=True) + skp) / QK_B + 1e-6)
        kb_o[:, h * 256:h * 256 + 128] = (nope * r * g[:, :128]).astype(BF16)
        kb_o[:, h * 256 + 128:(h + 1) * 256] = (kp * r * g[:, 128:]).astype(BF16)
    vb_o[...] = kv[:, H_B * NOPE:].astype(BF16)


def _mla_kv(ckv_all, kpe_all, wukv, bkn, bl):
    b, lp, _ = ckv_all.shape
    return pl.pallas_call(
        _mla_kv_kernel,
        grid=(b, lp // bl),
        in_specs=[pl.BlockSpec((None, bl, KV_LORA), lambda bi, li: (bi, li, 0)),
                  pl.BlockSpec((None, bl, LANE), lambda bi, li: (bi, li, 0)),
                  _const_spec((KV_LORA, 2048)), _const_spec((1, 256))],
        out_specs=[pl.BlockSpec((None, bl, 2048), lambda bi, li: (bi, li, 0)),
                   pl.BlockSpec((None, bl, 1024), lambda bi, li: (bi, li, 0))],
        out_shape=[jax.ShapeDtypeStruct((b, lp, 2048), BF16), jax.ShapeDtypeStruct((b, lp, 1024), BF16)],
        compiler_params=_params("parallel", "arbitrary"),
        name="mla_kv",
    )(ckv_all, kpe_all, wukv, bkn)


def _key_chunk(lp):
    for c in (512, 384, 256, 128):
        if lp % c == 0:
            return c
    raise ValueError(lp)


def _dsa_kernel(qa_ref, iq_ref, sa_ref, ag_ref, ka_ref, va_ref, kidx_ref, bias_ref, acc_ref, o_ref,
                key_ref, hi_ref, lo_ref, *, bq, lp, l_true, q0, ksel, d_min, d_max, far, blk0):
    i = pl.program_id(1) + blk0
    q_start = q0 + i * bq
    qpos = q_start + lax.broadcasted_iota(jnp.int32, (bq, 1), 0)
    ck = _key_chunk(lp)

    iq = iq_ref[...].astype(BF16)
    a = jnp.concatenate([iq[:, p * LANE:(p + 1) * LANE] for p in range(H_IDX // 2)], axis=0)
    wi = sa_ref[...] * (H_IDX ** -0.5 * D_IDX ** -0.5)
    for c0 in range(0, lp, ck):
        kk = kidx_ref[c0:c0 + ck, :]
        lane = lax.broadcasted_iota(jnp.int32, kk.shape, 1)
        k_lo = jnp.where(lane < D_IDX, kk, 0.0).astype(BF16)
        k_hi = jnp.where(lane >= D_IDX, pltpu.roll(kk, D_IDX, 1), 0.0).astype(BF16)
        s_lo = lax.dot_general(a, k_lo, NT_DIMS, preferred_element_type=F32)
        s_hi = lax.dot_general(a, k_hi, NT_DIMS, preferred_element_type=F32)
        sc = jnp.zeros((bq, ck), F32)
        for p in range(H_IDX // 2):
            w0 = wi[:, D_IDX + 2 * p:D_IDX + 2 * p + 1]
            w1 = wi[:, D_IDX + 2 * p + 1:D_IDX + 2 * p + 2]
            sc = sc + w0 * jnp.maximum(s_lo[p * bq:(p + 1) * bq], 0.0)
            sc = sc + w1 * jnp.maximum(s_hi[p * bq:(p + 1) * bq], 0.0)
        kpos = c0 + lax.broadcasted_iota(jnp.int32, (1, ck), 1)
        valid = ((kpos >> CHUNK_SHIFT) <= (qpos >> CHUNK_SHIFT)) & (kpos < l_true)
        bits = lax.bitcast_convert_type(sc, jnp.int32)
        key = jnp.where(bits < 0, bits ^ 0x7FFFFFFF, bits)
        key = jnp.where(bits == INT_MIN, 0, key)
        key = jnp.where(valid, key, INT_MIN)
        key_ref[:, c0:c0 + ck] = key
        hi_ref[:, c0:c0 + ck] = (key >> 16) + 2 ** 15
        lo_ref[:, c0:c0 + ck] = key & 0xFFFF

    ng = SEARCH_GROUPS
    gr = bq // ng
    groups = [slice(r * gr, (r + 1) * gr) for r in range(ng)]

    def count_ge(ref, rows, cand):
        acc = None
        for j in range(lp // LANE):
            d = (ref[rows, j * LANE:(j + 1) * LANE] - cand) >> 31
            acc = d if acc is None else acc + d
        return lp + jnp.sum(acc.astype(F32), axis=-1, keepdims=True)

    def search16(ref, need):
        ts = [jnp.zeros((gr, 1), jnp.int32)] * ng
        for bit in reversed(range(16)):
            for r in range(ng):
                cand = ts[r] | (1 << bit)
                ts[r] = jnp.where(count_ge(ref, groups[r], cand) >= need[r], cand, ts[r])
        return ts

    if lp > ksel:
        t_hi = search16(hi_ref, (float(ksel),) * ng)
        need_lo = []
        for rows, t in zip(groups, t_hi):
            need_lo.append(ksel - count_ge(hi_ref, rows, t + 1))
            lo_ref[rows, :] = jnp.where(hi_ref[rows, :] == t, lo_ref[rows, :], -1)
        t_lo = search16(lo_ref, need_lo)
        t_s = jnp.concatenate([lax.shift_left(h - 2 ** 15, 16) | l for h, l in zip(t_hi, t_lo)], axis=0)
        key = key_ref[...]
        n_picked = jnp.sum(((key >= t_s) & (key > INT_MIN)).astype(F32), axis=-1, keepdims=True)

        @pl.when(jnp.max(n_picked) > ksel)
        def _():
            key = key_ref[...]
            rpos = (lp - 1) - lax.broadcasted_iota(jnp.int32, (bq, lp), 1)
            lo_ref[...] = jnp.where((key == t_s) & (key > INT_MIN), rpos, -1)
            need = [ksel - jnp.sum((key[rows] > t_s[rows]).astype(F32), axis=-1, keepdims=True) for rows in groups]
            t_pos = jnp.concatenate(search16(lo_ref, need), axis=0)
            tie = lo_ref[...]
            key_ref[...] = jnp.where((tie >= 0) & (tie < t_pos), INT_MIN, key)
    else:
        t_s = jnp.full((bq, 1), INT_MIN, jnp.int32)
    key = key_ref[...]
    sel = (key >= t_s) & (key > INT_MIN)

    d_base = -(q0 // LANE + i * (bq // LANE))
    nk = lp // LANE
    for n in range(HKV_A):
        q4 = jnp.concatenate([qa_ref[:, (n * GROUP_A + g) * DH_A:(n * GROUP_A + g + 1) * DH_A]
                              for g in range(GROUP_A)], axis=0)
        k_n = ka_ref[:, n * DH_A:(n + 1) * DH_A].astype(BF16)
        v_n = va_ref[:, n * DH_A:(n + 1) * DH_A].astype(BF16)
        lg4 = lax.dot_general(q4, k_n, NT_DIMS, preferred_element_type=F32) * DH_A ** -0.5
        ps, ss = [], []
        for g in range(GROUP_A):
            bias = jnp.concatenate(
                [jnp.concatenate([bias_ref[n * GROUP_A + g,
                                           jnp.maximum(jnp.minimum(d_base + j - u, d_max) - (d_min + far), 0)]
                                  for j in range(nk)], axis=1)
                 for u in range(bq // bias_ref.shape[2])], axis=0)
            lg = jnp.where(sel, lg4[g * bq:(g + 1) * bq] + bias, -jnp.inf)
            p = jnp.exp(lg - jnp.max(lg, axis=-1, keepdims=True))
            ss.append(jnp.sum(p, axis=-1, keepdims=True))
            ps.append(p.astype(BF16))
        o4 = jnp.dot(jnp.concatenate(ps, axis=0), v_n, preferred_element_type=F32)
        for g in range(GROUP_A):
            h = n * GROUP_A + g
            gate = ag_ref[:, h * DH_A:(h + 1) * DH_A]
            o_ref[:, h * DH_A:(h + 1) * DH_A] = (o4[g * bq:(g + 1) * bq] / ss[g] * _silu(gate)).astype(BF16)


def _dsa(qa, z, ka_all, va_all, kidx_all, kidx_col, bias_tiles, acc, *, bq, l_true, q0, ksel, row0, rows, lp):
    b, s, _ = qa.shape
    bias_tiles, d_min, d_max, far = bias_tiles
    blk0 = row0 // bq
    assert q0 % LANE == 0 and (bq % LANE == 0 or s == bq)
    assert d_min <= -((q0 + s - bias_tiles.shape[2]) // LANE) and bq % bias_tiles.shape[2] == 0
    kern = functools.partial(_dsa_kernel, bq=bq, lp=lp, l_true=l_true, q0=q0, ksel=ksel,
                             d_min=d_min, d_max=d_max, far=far, blk0=blk0)

    def zspec(width, off):
        return pl.BlockSpec((None, bq, width), lambda bi, qi: (bi, qi + blk0, off // width))

    def kspec(width):
        return pl.BlockSpec((None, lp, width), lambda bi, qi: (bi, 0, 0))

    return pl.pallas_call(
        kern,
        grid=(b, rows // bq),
        in_specs=[pl.BlockSpec((None, bq, 1024), lambda bi, qi: (bi, qi + blk0, 0)),
                  zspec(1024, E_IQ), zspec(128, E_IDX), zspec(1024, E_AG),
                  kspec(256), kspec(256),
                  pl.BlockSpec((None, lp, LANE), lambda bi, qi: (bi, 0, kidx_col)),
                  _const_spec(bias_tiles.shape),
                  pl.BlockSpec(memory_space=pl.ANY)],
        out_specs=pl.BlockSpec((None, bq, 1024), lambda bi, qi: (bi, qi + blk0, 0)),
        out_shape=jax.ShapeDtypeStruct(acc.shape, acc.dtype),
        input_output_aliases={8: 0},
        scratch_shapes=[pltpu.VMEM((bq, lp), jnp.int32)] * 3,
        compiler_params=_params("parallel", "arbitrary"),
        name="dsa",
    )(qa, z, z, z, ka_all, va_all, kidx_all, bias_tiles, acc)


def _causal_classes(s, lp, q0):
    if q0 == 0 and s % CLASS_ROWS == 0 and lp == s:
        return [(c * CLASS_ROWS, CLASS_ROWS, (c + 1) * CLASS_ROWS) for c in range(s // CLASS_ROWS)]
    return [(0, s, lp)]


def _mla_kernel(qb_ref, bg_ref, kb_ref, vb_ref, acc_ref, o_ref, *, bq, lp, l_true, q0, blk0, n_full):
    q_start = q0 + (pl.program_id(1) + blk0) * bq
    qpos = q_start + lax.broadcasted_iota(jnp.int32, (bq, 1), 0)
    kpos = n_full + lax.broadcasted_iota(jnp.int32, (1, lp - n_full), 1)
    valid = ((kpos >> CHUNK_SHIFT) <= (qpos >> CHUNK_SHIFT)) & (kpos < l_true)
    for h in range(H_B):
        q = qb_ref[:, h * 256:(h + 1) * 256]
        parts = []
        if n_full:
            parts.append((lax.dot_general(q, kb_ref[:n_full, h * 256:(h + 1) * 256], NT_DIMS,
                                          preferred_element_type=F32), vb_ref[:n_full, h * V_DIM:(h + 1) * V_DIM]))
        lg = lax.dot_general(q, kb_ref[n_full:, h * 256:(h + 1) * 256], NT_DIMS, preferred_element_type=F32)
        parts.append((jnp.where(valid, lg, -jnp.inf), vb_ref[n_full:, h * V_DIM:(h + 1) * V_DIM]))
        o = _softmax_pv(parts, QK_B ** -0.5)
        gate = bg_ref[:, h * V_DIM:(h + 1) * V_DIM]
        o_ref[:, h * V_DIM:(h + 1) * V_DIM] = (o * _silu(gate)).astype(BF16)


def _mla(qb, z, kb, vb, acc, *, bq, l_true, q0, row0, rows, lp):
    b = qb.shape[0]
    blk0 = row0 // bq
    n_full = min(min(q0 + row0 + CHUNK, l_true) // LANE * LANE, lp - LANE)
    kern = functools.partial(_mla_kernel, bq=bq, lp=lp, l_true=l_true, q0=q0, blk0=blk0, n_full=n_full)
    return pl.pallas_call(
        kern,
        grid=(b, rows // bq),
        in_specs=[pl.BlockSpec((None, bq, 2048), lambda bi, qi: (bi, qi + blk0, 0)),
                  pl.BlockSpec((None, bq, 1024), lambda bi, qi: (bi, qi + blk0, E_BG // 1024)),
                  pl.BlockSpec((None, lp, 2048), lambda bi, qi: (bi, 0, 0)),
                  pl.BlockSpec((None, lp, 1024), lambda bi, qi: (bi, 0, 0)),
                  pl.BlockSpec(memory_space=pl.ANY)],
        out_specs=pl.BlockSpec((None, bq, 1024), lambda bi, qi: (bi, qi + blk0, 0)),
        out_shape=jax.ShapeDtypeStruct(acc.shape, acc.dtype),
        input_output_aliases={4: 0},
        compiler_params=_params("parallel", "arbitrary"),
        name="mla",
    )(qb, z, kb, vb, acc)


def _out_ple_kernel(h_ref, oa_ref, ob_ref, p_ref, wo_ref, wg_ref, wp_ref, o_ref):
    half = oa_ref.shape[-1]
    h1 = (h_ref[...]
          + jnp.dot(oa_ref[...], wo_ref[:half, :], preferred_element_type=F32)
          + jnp.dot(ob_ref[...], wo_ref[half:, :], preferred_element_type=F32))
    r = (h1 * _rs(h1)).astype(BF16)
    gate = 1.0 / (1.0 + jnp.exp(-jnp.dot(r, wg_ref[...], preferred_element_type=F32)))
    o_ref[...] = h1 + gate * jnp.dot(p_ref[...].astype(BF16), wp_ref[...], preferred_element_type=F32)


def _out_ple(h, oa, ob, p_all, layer, wo, wg, wp, bm):
    m, d = h.shape
    half = oa.shape[1]

    def rows(width):
        return pl.BlockSpec((bm, width), lambda i: (i, 0))

    return pl.pallas_call(
        _out_ple_kernel,
        grid=(m // bm,),
        in_specs=[rows(d), rows(half), rows(half), pl.BlockSpec((None, bm, PLE_DIM), lambda i: (layer, i, 0)),
                  _const_spec((2 * half, d)), _const_spec((d, d)), _const_spec((PLE_DIM, d))],
        out_specs=rows(d),
        out_shape=jax.ShapeDtypeStruct((m, d), F32),
        compiler_params=_params("parallel"),
        name="out_ple",
    )(h, oa, ob, p_all, wo, wg, wp)


def _odd_prep_kernel(cq_ref, ck_ref, cv_ref, dv_ref, du_ref, dgate_ref, cqn_ref, ckn_ref, dg_ref, db_ref,
                     ws_ref, bs_ref, qc_o, kcb_o, vcb_o, od_o, kct_o, cvt_o, *maybe_dvn_o, n):
    for h in range(H_C):
        sl = slice(h * DH_C, (h + 1) * DH_C)
        x = cq_ref[:, sl]
        qc_o[:, sl] = (x * _rs(x) * cqn_ref[...]).astype(BF16)
        x = ck_ref[:, sl]
        kn = x * _rs(x) * ckn_ref[...]
        kct_o[:, sl] = kn
        kcb_o[:, sl] = kn.astype(BF16)
    cv = cv_ref[...]
    cvt_o[...] = cv
    vcb_o[...] = cv.astype(BF16)
    dv = dv_ref[...]
    xc = dv - jnp.mean(dv, -1, keepdims=True)
    var = jnp.mean(xc * xc, -1, keepdims=True)
    dvn = xc * lax.rsqrt(var + 1e-5) * dg_ref[...] + db_ref[...]
    for dvn_o in maybe_dvn_o:
        dvn_o[...] = dvn
    dvn = dvn.astype(BF16)
    row = lax.broadcasted_iota(jnp.int32, (n, n), 0)
    col = lax.broadcasted_iota(jnp.int32, (n, n), 1)
    for g in range(G_D):
        sl = slice(g * DG_D, (g + 1) * DG_D)
        w = jnp.where(col <= row, ws_ref[g], 0.0).astype(BF16)
        for c in range(dv.shape[0] // n):
            rs = slice(c * n, (c + 1) * n)
            sg = jnp.dot(w, dvn[rs, sl], preferred_element_type=F32) + bs_ref[:, g:g + 1]
            od_o[rs, sl] = (du_ref[rs, sl] * sg * _silu(dgate_ref[rs, sl])).astype(BF16)


def _odd_prep(z, cqn, ckn, dg, db, ws, bs_t, bs, n, keep, want_dvn):
    b, s, _ = z.shape
    first_tail = (s - keep) // bs
    assert keep % bs == 0

    def zspec(off):
        return pl.BlockSpec((None, bs, 1024), lambda bi, si: (bi, si, off // 1024))

    ospec = pl.BlockSpec((None, bs, 1024), lambda bi, si: (bi, si, 0))
    tspec = pl.BlockSpec((None, bs, 1024), lambda bi, si: (bi, jnp.maximum(si - first_tail, 0), 0))
    act = lambda dt: jax.ShapeDtypeStruct((b, s, 1024), dt)
    tail = jax.ShapeDtypeStruct((b, keep, 1024), F32)
    return pl.pallas_call(
        functools.partial(_odd_prep_kernel, n=n),
        grid=(b, s // bs),
        in_specs=[zspec(O_CQ), zspec(O_CK), zspec(O_CV), zspec(O_DV), zspec(O_DU), zspec(O_DG),
                  _const_spec((1, DH_C)), _const_spec((1, DH_C)), _const_spec((1, W_D)), _const_spec((1, W_D)),
                  _const_spec((G_D, n, n)), _const_spec((n, G_D))],
        out_specs=[ospec] * 4 + [tspec] * 2 + [ospec] * want_dvn,
        out_shape=[act(BF16)] * 4 + [tail] * 2 + [act(F32)] * want_dvn,
        compiler_params=_params("parallel", "arbitrary"),
        name="odd_prep",
    )(z, z, z, z, z, z, cqn, ckn, dg, db, ws, bs_t)


BACK_TILES = C_BACK * CHUNK // LANE


def _band_width(bq):
    return (BACK_TILES + -(-bq // LANE)) * LANE


def _band_kernel(q_ref, cg_ref, k_ref, v_ref, *rest, bq, sliding, q0, k0, k_end, c0):
    i = pl.program_id(1)
    q_start = q0 + i * bq
    bw = _band_width(bq)
    if sliding:
        cbias_ref, bias_ref, o_ref = rest
        first = i * (bq // LANE) - BACK_TILES
        win_start = first * LANE
        rows = [pl.ds(pl.multiple_of(jnp.maximum(first + t, 0) * LANE, LANE), LANE) for t in range(bw // LANE)]
        kw = jnp.concatenate([k_ref[r, :] for r in rows], axis=0)
        vw = jnp.concatenate([v_ref[r, :] for r in rows], axis=0)
    else:
        kn_ref, vn_ref, cbias_ref, bias_ref, o_ref = rest
        win_start = k0
        pad = jnp.zeros((bw - k_ref.shape[0] - bq, k_ref.shape[1]), BF16)
        kw = jnp.concatenate([k_ref[...].astype(BF16), kn_ref[...], pad], axis=0)
        vw = jnp.concatenate([v_ref[...].astype(BF16), vn_ref[...], pad], axis=0)
    qpos = q_start + lax.broadcasted_iota(jnp.int32, (bq, 1), 0)

    def valid(lo, hi):
        kpos = win_start + lo + lax.broadcasted_iota(jnp.int32, (1, hi - lo), 1)
        dc = (qpos >> CHUNK_SHIFT) - (kpos >> CHUNK_SHIFT)
        return (dc >= 0) & (dc <= C_BACK) & (kpos >= 0) & (kpos < k_end)

    segs = [(lo, hi, valid(lo, hi)) for lo, hi in ((0, c0), (c0, bw)) if hi > lo]
    for h in range(H_C):
        sl = slice(h * DH_C, (h + 1) * DH_C)
        q = q_ref[:, sl]
        parts = []
        for lo, hi, ok in segs:
            bias = cbias_ref[h][:, :1] if hi <= c0 else bias_ref[h]
            lg = lax.dot_general(q, kw[lo:hi, sl], NT_DIMS, preferred_element_type=F32) * DH_C ** -0.5
            parts.append((jnp.where(ok, lg + bias, -jnp.inf), vw[lo:hi, sl]))
        o = _softmax_pv(parts)
        o_ref[:, sl] = (o * _silu(cg_ref[:, sl])).astype(BF16)


def _band(qc, z, keys, vals, bias, *, bq, sliding, q0, k0, k_end):
    c0, cbias, bias = bias
    b, s, _ = qc.shape
    if sliding:
        assert bq % LANE == 0 and q0 == 0 and k0 == 0
        kv = [keys, vals]
    else:
        assert keys[0].shape[1] + bq <= _band_width(bq) and s == bq
        kv = [keys[0], vals[0], keys[1], vals[1]]
    kern = functools.partial(_band_kernel, bq=bq, sliding=sliding, q0=q0, k0=k0, k_end=k_end, c0=c0)
    return pl.pallas_call(
        kern,
        grid=(b, s // bq),
        in_specs=[pl.BlockSpec((None, bq, 1024), lambda bi, qi: (bi, qi, 0)),
                  pl.BlockSpec((None, bq, 1024), lambda bi, qi: (bi, qi, O_CG // 1024))]
                 + [pl.BlockSpec((None,) + a.shape[1:], lambda bi, qi: (bi, 0, 0)) for a in kv]
                 + [_const_spec(cbias.shape), _const_spec(bias.shape)],
        out_specs=pl.BlockSpec((None, bq, 1024), lambda bi, qi: (bi, qi, 0)),
        out_shape=jax.ShapeDtypeStruct((b, s, 1024), BF16),
        compiler_params=_params("parallel", "arbitrary"),
        name="band",
    )(qc, z, *kv, cbias, bias)


def _rope_tables(pos):
    half = ROPE_DIM // 2
    freq = ROPE_BASE ** (-jnp.arange(half, dtype=F32) / half)
    ang = pos.astype(F32)[:, None] * freq[None, :]
    cos, sin = jnp.cos(ang), jnp.sin(ang)
    z = jnp.zeros_like(cos)
    return jnp.concatenate([cos, z, cos, z], 1), jnp.concatenate([-sin, z, sin, z], 1)


def _rope_lanes(x):
    half = ROPE_DIM // 2
    z = jnp.zeros(x.shape[:-1] + (half,), x.dtype)
    return jnp.concatenate([x[..., :half], z, x[..., half:], z], -1)


def _t5_bucket_np(rel):
    nb = T5_BUCKETS // 2
    max_exact = nb // 2
    n = np.abs(rel)
    nf = np.maximum(n, 1).astype(np.float64)
    large = max_exact + (np.log(nf / max_exact) / math.log(T5_MAX_DIST / max_exact) * (nb - max_exact)).astype(np.int64)
    large = np.minimum(large, nb - 1)
    return np.where(rel > 0, nb, 0) + np.where(n < max_exact, n, large)


def _toeplitz(w, rows, width, cols):
    flat = jnp.tile(w, (1,) * (w.ndim - 1) + (rows,))[..., :rows * width]
    return flat.reshape(w.shape[:-1] + (rows, width))[..., :cols]


def _t5_tiles(t5_bias, bq, d_min):
    d_max = max(bq // LANE - 1, 0)
    width = LANE + max(bq, LANE)
    k = np.arange(width + 1)
    delta = np.where(k < LANE, k, k - (width + 1))
    rel = np.arange(d_min, d_max + 1)[:, None] * LANE + delta[None, :]
    bucket = _t5_bucket_np(rel)
    far = 0
    while far + 1 < len(bucket) and np.array_equal(bucket[far + 1], bucket[0]):
        far += 1
    w = jnp.transpose(t5_bias[bucket[far:]], (2, 0, 1))
    return _toeplitz(w, bq, width, LANE), d_min, d_max, far


def _band_bias(rel_tab, bq, qk_off):
    bw = _band_width(bq)
    rel_index = lambda rel: np.clip(rel, -(CHUNK - 1), REL_CLIP) + (CHUNK - 1)
    full = rel_index(qk_off + np.arange(bq)[:, None] - np.arange(bw)[None, :])
    c0 = 0
    while c0 + 2 * MXU_COLS <= bw and np.all(full[:, :c0 + MXU_COLS] == full[0, 0]):
        c0 += MXU_COLS
    const = jnp.broadcast_to(rel_tab[full[0, 0]][:, None, None], (H_C, 1, LANE))
    wv = bw - c0
    width = wv + bq
    k = np.arange(width + 1)
    delta = np.where(k < wv, k, k - (width + 1))
    idx = rel_index(qk_off - c0 - delta)
    return c0, const, _toeplitz(jnp.transpose(rel_tab[idx], (1, 0)), bq, width, wv)


def _even_weights(w_in, b_wuq, b_wukv, b_qn, b_kn):
    d = w_in.shape[0]
    offs = np.cumsum((0,) + EVEN_SPLITS)
    w16 = w_in.astype(BF16)
    aq, ak, av, ag, iq, ik, iw, bcq, bckv, bkpe, bg = [w16[:, offs[t]:offs[t + 1]] for t in range(11)]
    slab_idx = jnp.concatenate([ik, iw, jnp.zeros((d, LANE - D_IDX - H_IDX), BF16)], 1)
    w = jnp.concatenate([aq, ag, iq, bg, bcq, ak, av, bckv, slab_idx, _rope_lanes(bkpe)], 1)
    uq = b_wuq.reshape(Q_LORA, H_B, QK_B)
    uq = jnp.concatenate([uq[..., :NOPE], _rope_lanes(uq[..., NOPE:])], -1).reshape(Q_LORA, H_B * 256).astype(BF16)
    ukv = b_wukv.reshape(KV_LORA, H_B, NOPE + V_DIM)
    ukv = jnp.concatenate([ukv[..., :NOPE].reshape(KV_LORA, H_B * NOPE),
                           ukv[..., NOPE:].reshape(KV_LORA, H_B * V_DIM)], 1).astype(BF16)
    pad_gain = lambda g: jnp.concatenate([g[:NOPE], _rope_lanes(g[NOPE:])])[None, :]
    return w, uq, ukv, pad_gain(b_qn), pad_gain(b_kn)


def _pad_rows(x, lp):
    return jnp.pad(x, ((0, 0), (0, lp - x.shape[1]), (0, 0)))


def _even_layer(h, p_all, layer, past, q0, ln_g, w_in, uq, ukv, a_qn, a_kn, t5_bias, b_qln, b_kvln, bqn, bkn, wo, wg, wp):
    b, s, d = h.shape
    m = b * s
    z = _norm_mm(h.reshape(m, d), ln_g[None, :], w_in, min(m, 1024), E_END // 4).reshape(b, s, E_END)
    cos, sin = _rope_tables(q0 + jnp.arange(s, dtype=jnp.int32))
    bs = min(s, 512)
    qa, ka, qb, ckv, kpe_l, kpe64, idx64 = _even_prep(z, cos, sin, a_qn[None, :], a_kn[None, :], b_qln[None, :],
                                                      b_kvln[None, :], uq, bqn, bs)
    av = z[..., E_AV:E_AV + 256]
    new = (ka.reshape(b, s, HKV_A, DH_A), av.reshape(b, s, HKV_A, DH_A), idx64, ckv, kpe64)
    if past is None:
        l_true = s
        ka_all, va_all, ckv_all, kpe_all = ka, av, ckv, kpe_l
        kidx_all, kidx_col = z, E_IDX // LANE
    else:
        kidx = z[..., E_IDX:E_IDX + LANE]
        kidx_col = 0
        c_k, c_v, c_ik, c_ckv, c_kpe = past
        pl_ = c_k.shape[1]
        l_true = pl_ + s
        lp = -(-l_true // LANE) * LANE
        cat = lambda c, n_: _pad_rows(jnp.concatenate([c, n_], 1), lp)
        ka_all = cat(c_k.reshape(b, pl_, 256), ka)
        va_all = cat(c_v.reshape(b, pl_, 256), av)
        kidx_all = cat(jnp.pad(c_ik, ((0, 0), (0, 0), (0, LANE - D_IDX))), kidx)
        ckv_all = cat(c_ckv, ckv)
        kpe_all = cat(_rope_lanes(c_kpe), kpe_l)
    lp = ka_all.shape[1]
    ksel = min(TOPK_MAX, l_true // 4)
    classes = _causal_classes(s, lp, q0)
    bqs = [nr if (nr % LANE == 0 and lc <= DSA_WIDE_BLOCK_KEYS) else min(s, LANE) for _, nr, lc in classes]
    tb = min(s, LANE)
    tiles = _t5_tiles(t5_bias, tb, -((q0 + s - tb) // LANE))
    o_a = jnp.zeros((b, s, H_A * DH_A), BF16)
    for (r0, nr, lc), bq in zip(classes, bqs):
        o_a = _dsa(qa, z, ka_all, va_all, kidx_all, kidx_col, tiles, o_a, bq=bq, l_true=l_true, q0=q0,
                   ksel=ksel, row0=r0, rows=nr, lp=lc)
    kb, vb = _mla_kv(ckv_all, kpe_all, ukv, bkn, 1024 if lp % 1024 == 0 else _key_chunk(lp))
    o_b = jnp.zeros((b, s, H_B * V_DIM), BF16)
    for r0, nr, lc in classes:
        o_b = _mla(qb, z, kb, vb, o_b, bq=min(s, 256), l_true=l_true, q0=q0, row0=r0, rows=nr, lp=lc)
    y = _out_ple(h.reshape(m, d), o_a.reshape(m, -1), o_b.reshape(m, -1), p_all, layer, wo, wg, wp, min(m, 512))
    return y.reshape(b, s, d), new


def _odd_layer(h, p_all, layer, past, q0, ln_g, w_in, c_qn, c_kn, c_rel, d_g, d_b, d_ws, d_bs, wo, wg, wp):
    b, s, d = h.shape
    m = b * s
    z = _norm_mm(h.reshape(m, d), ln_g[None, :], w_in, min(m, 1024), 1024).reshape(b, s, O_END)
    n = min(s, D_CHUNK)
    keep = min(C_BACK * CHUNK, s) if past is None else s
    qc, kcb, vcb, o_d, kct, cvt, *dvn = _odd_prep(z, c_qn[None, :], c_kn[None, :], d_g[None, :], d_b[None, :],
                                                  d_ws[:, :n, :n], d_bs[:, :n].T, min(s, 512), n, keep,
                                                  want_dvn=past is not None)
    c_new = (kct.reshape(b, keep, H_C, DH_C), cvt.reshape(b, keep, H_C, DH_C))
    if past is None:
        bqc = 2 * LANE
        bias = _band_bias(c_rel, bqc, C_BACK * CHUNK)
        o_c = _band(qc, z, kcb, vcb, bias, bq=bqc, sliding=True, q0=0, k0=0, k_end=s)
    else:
        nc = past[0].shape[1]
        bias = _band_bias(c_rel, s, nc)
        o_c = _band(qc, z, (past[0].reshape(b, nc, 1024), kcb), (past[1].reshape(b, nc, 1024), vcb), bias,
                    bq=s, sliding=False, q0=q0, k0=q0 - nc, k_end=q0 + s)
    y = _out_ple(h.reshape(m, d), o_c.reshape(m, -1), o_d.reshape(m, -1), p_all, layer, wo, wg, wp, min(m, 512))
    return y.reshape(b, s, d), c_new, (dvn[0] if dvn else None)


def kernel(x_prompt, x_sample, cache_a_k, cache_a_v, cache_a_idx_k, cache_b_ckv, cache_b_kpe, cache_c_k, cache_c_v, p_prompt, p_sample, ln_g, w_in_even, a_q_norm, a_k_norm, t5_bias, b_q_lora_norm, b_kv_lora_norm, b_w_uq, b_w_ukv, b_q_norm, b_k_norm, w_out_even, w_in_odd, c_q_norm, c_k_norm, c_rel_bias, d_ln_g, d_ln_b, d_w_s, d_b_s, w_out_odd, ple_proj, ple_gate):
    depth = ln_g.shape[0]
    past_len = cache_a_k.shape[2]
    hp, hs = x_prompt, x_sample
    pp = p_prompt.reshape(depth, -1, PLE_DIM)
    ps = p_sample.reshape(depth, -1, PLE_DIM)
    ev_p, ev_s, od_p, od_s, dv_s = [], [], [], [], []
    for i in range(depth):
        j = i // 2
        wg = ple_gate[i].astype(BF16)
        wp = ple_proj[i].astype(BF16)
        if i % 2 == 0:
            w_in, uq, ukv, bqn, bkn = _even_weights(w_in_even[j], b_w_uq[j], b_w_ukv[j], b_q_norm[j], b_k_norm[j])
            w = (ln_g[i], w_in, uq, ukv, a_q_norm[j], a_k_norm[j], t5_bias, b_q_lora_norm[j], b_kv_lora_norm[j],
                 bqn, bkn, w_out_even[j].astype(BF16), wg, wp)
            hp, sp = _even_layer(hp, pp, i, None, 0, *w)
            past = (cache_a_k[j], cache_a_v[j], cache_a_idx_k[j], cache_b_ckv[j], cache_b_kpe[j])
            hs, ss = _even_layer(hs, ps, i, past, past_len, *w)
            ev_p.append(sp)
            ev_s.append(ss)
        else:
            w = (ln_g[i], w_in_odd[j].astype(BF16), c_q_norm[j], c_k_norm[j], c_rel_bias[j], d_ln_g[j], d_ln_b[j],
                 d_w_s[j], d_b_s[j], w_out_odd[j].astype(BF16), wg, wp)
            hp, sp, _ = _odd_layer(hp, pp, i, None, 0, *w)
            hs, ss, dvs = _odd_layer(hs, ps, i, (cache_c_k[j], cache_c_v[j]), past_len, *w)
            od_p.append(sp)
            od_s.append(ss)
            dv_s.append(dvs)
    st = lambda lst, n_: jnp.stack([e[n_] for e in lst], 0)
    return (hp, hs, st(ev_p, 0), st(ev_p, 1), st(ev_p, 2), st(ev_p, 3), st(ev_p, 4), st(od_p, 0), st(od_p, 1),
            st(ev_s, 0), st(ev_s, 1), st(ev_s, 2), st(ev_s, 3), st(ev_s, 4), st(od_s, 0), st(od_s, 1),
            jnp.stack(dv_s, 0))
```

```python
import functools
import math

import numpy as np
import jax
import jax.numpy as jnp
from jax import lax
from jax.experimental import pallas as pl
from jax.experimental.pallas import tpu as pltpu

F32 = jnp.float32
BF16 = jnp.bfloat16
INT_MIN = -2 ** 31
LOG2E = math.log2(math.e)
CLASS_ROWS = 256
SEARCH_GROUPS = 4
DSA_WIDE_BLOCK_KEYS = 1280

D_MODEL = 2048
CHUNK = 64
CHUNK_SHIFT = 6
LANE = 128
MXU_COLS = 256
H_A, HKV_A, GROUP_A, DH_A = 8, 2, 4, 128
H_IDX, D_IDX = 16, 64
TOPK_MAX = 256
T5_BUCKETS, T5_MAX_DIST = 32, 128
H_B, Q_LORA, KV_LORA, NOPE, ROPE_DIM, V_DIM = 8, 512, 256, 128, 64, 128
ROPE_BASE = 10000.0
QK_B = NOPE + ROPE_DIM
H_C, DH_C, C_BACK, REL_CLIP = 8, 128, 8, 128
W_D, G_D, DG_D, D_CHUNK = 1024, 8, 128, 128
PLE_DIM = 256

EVEN_SPLITS = (H_A * DH_A, HKV_A * DH_A, HKV_A * DH_A, H_A * DH_A, H_IDX * D_IDX, D_IDX, H_IDX,
               Q_LORA, KV_LORA, ROPE_DIM, H_B * V_DIM)
E_AQ, E_AG, E_IQ, E_BG, E_BCQ, E_AK, E_AV, E_CKV, E_IDX, E_KPE, E_END = (
    0, 1024, 2048, 3072, 4096, 4608, 4864, 5120, 5376, 5504, 5632)
O_CQ, O_CK, O_CV, O_CG, O_DU, O_DV, O_DG, O_END = 0, 1024, 2048, 3072, 4096, 5120, 6144, 7168

VMEM_LIMIT_BYTES = 56 * 1024 * 1024
NT_DIMS = (((1,), (1,)), ((), ()))


def _params(*sem):
    return pltpu.CompilerParams(dimension_semantics=sem, vmem_limit_bytes=VMEM_LIMIT_BYTES)


def _const_spec(shape):
    zeros = (0,) * len(shape)
    return pl.BlockSpec(shape, lambda *_: zeros, pipeline_mode=pl.Buffered(1))


def _rs(x, n=None, eps=1e-6):
    n = x.shape[-1] if n is None else n
    return lax.rsqrt(jnp.sum(x * x, axis=-1, keepdims=True) / n + eps)


def _silu(x):
    return x * (1.0 / (1.0 + jnp.exp(-x)))


def _softmax_pv(parts, scale=1.0):
    m = functools.reduce(jnp.maximum, [jnp.max(lg, axis=-1, keepdims=True) for lg, _ in parts])
    o = s = None
    for lg, v in parts:
        p = jnp.exp2((lg - m) * (scale * LOG2E))
        ps = jnp.sum(p, axis=-1, keepdims=True)
        po = jnp.dot(p.astype(BF16), v, preferred_element_type=F32)
        o, s = (po, ps) if o is None else (o + po, s + ps)
    return o / s


def _norm_mm_kernel(x_ref, g_ref, w_ref, o_ref, xn_ref):
    @pl.when(pl.program_id(1) == 0)
    def _():
        x = x_ref[...]
        xn_ref[...] = (x * _rs(x) * g_ref[...]).astype(BF16)

    o_ref[...] = jnp.dot(xn_ref[...], w_ref[...], preferred_element_type=F32)


def _norm_mm(x, g, w, bm, bn):
    m, d = x.shape
    n = w.shape[1]
    return pl.pallas_call(
        _norm_mm_kernel,
        grid=(m // bm, n // bn),
        in_specs=[pl.BlockSpec((bm, d), lambda i, j: (i, 0)),
                  pl.BlockSpec((1, d), lambda i, j: (0, 0)),
                  pl.BlockSpec((d, bn), lambda i, j: (0, j))],
        out_specs=pl.BlockSpec((bm, bn), lambda i, j: (i, j)),
        out_shape=jax.ShapeDtypeStruct((m, n), F32),
        scratch_shapes=[pltpu.VMEM((bm, d), BF16)],
        compiler_params=_params("parallel", "arbitrary"),
        name="norm_mm",
    )(x, g, w)


def _rope(x, cos, sin):
    return x * cos + pltpu.roll(x, 64, 1) * sin


def _even_prep_kernel(aq_ref, bcq_ref, ak_ref, ckv_ref, kpe_ref, idx_ref, cos_ref, sin_ref,
                      aqn_ref, akn_ref, qln_ref, kvln_ref, wuq_ref, bqn_ref,
                      qa_o, ka_o, qb_o, ckv_o, kpe_o, kpe64_o, idx64_o):
    for h in range(H_A):
        x = aq_ref[:, h * DH_A:(h + 1) * DH_A]
        qa_o[:, h * DH_A:(h + 1) * DH_A] = (x * _rs(x) * aqn_ref[...]).astype(BF16)
    for n in range(HKV_A):
        x = ak_ref[:, n * DH_A:(n + 1) * DH_A]
        ka_o[:, n * DH_A:(n + 1) * DH_A] = x * _rs(x) * akn_ref[...]
    c = ckv_ref[...]
    ckv_o[...] = c * _rs(c) * kvln_ref[...]
    cos = cos_ref[...]
    sin = sin_ref[...]
    kpe = _rope(kpe_ref[...], cos, sin)
    kpe_o[...] = kpe
    half = ROPE_DIM // 2
    kpe64_o[...] = jnp.concatenate([kpe[:, :half], kpe[:, 2 * half:3 * half]], axis=-1)
    idx64_o[...] = idx_ref[:, :D_IDX]
    cq = bcq_ref[...]
    cqn = (cq * _rs(cq) * qln_ref[...]).astype(BF16)
    qb = jnp.dot(cqn, wuq_ref[...], preferred_element_type=F32)
    g = bqn_ref[...]
    for h in range(H_B):
        nope = qb[:, h * 256:h * 256 + 128]
        rot = _rope(qb[:, h * 256 + 128:(h + 1) * 256], cos, sin)
        ss = jnp.sum(nope * nope + rot * rot, -1, keepdims=True)
        r = lax.rsqrt(ss / QK_B + 1e-6)
        qb_o[:, h * 256:h * 256 + 128] = (nope * r * g[:, :128]).astype(BF16)
        qb_o[:, h * 256 + 128:(h + 1) * 256] = (rot * r * g[:, 128:]).astype(BF16)


def _even_prep(z, cos, sin, aqn, akn, qln, kvln, wuq, bqn, bs):
    b, s, _ = z.shape

    def zspec(width, off):
        return pl.BlockSpec((None, bs, width), lambda bi, si: (bi, si, off // width))

    def ospec(width):
        return pl.BlockSpec((None, bs, width), lambda bi, si: (bi, si, 0))

    pos_spec = pl.BlockSpec((bs, LANE), lambda bi, si: (si, 0))
    return pl.pallas_call(
        _even_prep_kernel,
        grid=(b, s // bs),
        in_specs=[zspec(1024, E_AQ), zspec(512, E_BCQ), zspec(256, E_AK), zspec(256, E_CKV),
                  zspec(128, E_KPE), zspec(128, E_IDX), pos_spec, pos_spec,
                  _const_spec((1, DH_A)), _const_spec((1, DH_A)), _const_spec((1, Q_LORA)),
                  _const_spec((1, KV_LORA)), _const_spec((Q_LORA, H_B * 256)), _const_spec((1, 256))],
        out_specs=[ospec(1024), ospec(256), ospec(2048), ospec(256), ospec(128), ospec(ROPE_DIM), ospec(D_IDX)],
        out_shape=[jax.ShapeDtypeStruct((b, s, 1024), BF16), jax.ShapeDtypeStruct((b, s, 256), F32),
                   jax.ShapeDtypeStruct((b, s, 2048), BF16), jax.ShapeDtypeStruct((b, s, 256), F32),
                   jax.ShapeDtypeStruct((b, s, 128), F32), jax.ShapeDtypeStruct((b, s, ROPE_DIM), F32),
                   jax.ShapeDtypeStruct((b, s, D_IDX), F32)],
        compiler_params=_params("parallel", "arbitrary"),
        name="even_prep",
    )(z, z, z, z, z, z, cos, sin, aqn, akn, qln, kvln, wuq, bqn)


def _mla_kv_kernel(ckv_ref, kpe_ref, w_ref, g_ref, kb_o, vb_o):
    kv = jnp.dot(ckv_ref[...].astype(BF16), w_ref[...], preferred_element_type=F32)
    kp = kpe_ref[...]
    skp = jnp.sum(kp * kp, -1, keepdims=True)
    g = g_ref[...]
    for h in range(H_B):
        nope = kv[:, h * NOPE:(h + 1) * NOPE]
        r = lax.rsqrt((jnp.sum(nope * nope, -1, keepdims=True) + skp) / QK_B + 1e-6)
        kb_o[:, h * 256:h * 256 + 128] = (nope * r * g[:, :128]).astype(BF16)
        kb_o[:, h * 256 + 128:(h + 1) * 256] = (kp * r * g[:, 128:]).astype(BF16)
    vb_o[...] = kv[:, H_B * NOPE:].astype(BF16)


def _mla_kv(ckv_all, kpe_all, wukv, bkn, bl):
    b, lp, _ = ckv_all.shape
    return pl.pallas_call(
        _mla_kv_kernel,
        grid=(b, lp // bl),
        in_specs=[pl.BlockSpec((None, bl, KV_LORA), lambda bi, li: (bi, li, 0)),
                  pl.BlockSpec((None, bl, LANE), lambda bi, li: (bi, li, 0)),
                  _const_spec((KV_LORA, 2048)), _const_spec((1, 256))],
        out_specs=[pl.BlockSpec((None, bl, 2048), lambda bi, li: (bi, li, 0)),
                   pl.BlockSpec((None, bl, 1024), lambda bi, li: (bi, li, 0))],
        out_shape=[jax.ShapeDtypeStruct((b, lp, 2048), BF16), jax.ShapeDtypeStruct((b, lp, 1024), BF16)],
        compiler_params=_params("parallel", "arbitrary"),
        name="mla_kv",
    )(ckv_all, kpe_all, wukv, bkn)


def _key_chunk(lp):
    for c in (512, 384, 256, 128):
        if lp % c == 0:
            return c
    raise ValueError(lp)


def _dsa_kernel(qa_ref, iq_ref, sa_ref, ag_ref, ka_ref, va_ref, kidx_ref, bias_ref, acc_ref, o_ref,
                key_ref, hi_ref, lo_ref, *, bq, lp, l_true, q0, ksel, d_min, d_max, far, blk0):
    i = pl.program_id(1) + blk0
    q_start = q0 + i * bq
    qpos = q_start + lax.broadcasted_iota(jnp.int32, (bq, 1), 0)
    ck = _key_chunk(lp)

    iq = iq_ref[...].astype(BF16)
    a = jnp.concatenate([iq[:, p * LANE:(p + 1) * LANE] for p in range(H_IDX // 2)], axis=0)
    wi = sa_ref[...] * (H_IDX ** -0.5 * D_IDX ** -0.5)
    for c0 in range(0, lp, ck):
        kk = kidx_ref[c0:c0 + ck, :]
        lane = lax.broadcasted_iota(jnp.int32, kk.shape, 1)
        k_lo = jnp.where(lane < D_IDX, kk, 0.0).astype(BF16)
        k_hi = jnp.where(lane >= D_IDX, pltpu.roll(kk, D_IDX, 1), 0.0).astype(BF16)
        s_lo = lax.dot_general(a, k_lo, NT_DIMS, preferred_element_type=F32)
        s_hi = lax.dot_general(a, k_hi, NT_DIMS, preferred_element_type=F32)
        sc = jnp.zeros((bq, ck), F32)
        for p in range(H_IDX // 2):
            w0 = wi[:, D_IDX + 2 * p:D_IDX + 2 * p + 1]
            w1 = wi[:, D_IDX + 2 * p + 1:D_IDX + 2 * p + 2]
            sc = sc + w0 * jnp.maximum(s_lo[p * bq:(p + 1) * bq], 0.0)
            sc = sc + w1 * jnp.maximum(s_hi[p * bq:(p + 1) * bq], 0.0)
        kpos = c0 + lax.broadcasted_iota(jnp.int32, (1, ck), 1)
        valid = ((kpos >> CHUNK_SHIFT) <= (qpos >> CHUNK_SHIFT)) & (kpos < l_true)
        bits = lax.bitcast_convert_type(sc, jnp.int32)
        key = jnp.where(bits < 0, bits ^ 0x7FFFFFFF, bits)
        key = jnp.where(bits == INT_MIN, 0, key)
        key = jnp.where(valid, key, INT_MIN)
        key_ref[:, c0:c0 + ck] = key
        hi_ref[:, c0:c0 + ck] = (key >> 16) + 2 ** 15
        lo_ref[:, c0:c0 + ck] = key & 0xFFFF

    ng = SEARCH_GROUPS
    gr = bq // ng
    groups = [slice(r * gr, (r + 1) * gr) for r in range(ng)]

    def count_ge(ref, rows, cand):
        acc = None
        for j in range(lp // LANE):
            d = (ref[rows, j * LANE:(j + 1) * LANE] - cand) >> 31
            acc = d if acc is None else acc + d
        return lp + jnp.sum(acc.astype(F32), axis=-1, keepdims=True)

    def search16(ref, need):
        ts = [jnp.zeros((gr, 1), jnp.int32)] * ng
        for bit in reversed(range(16)):
            for r in range(ng):
                cand = ts[r] | (1 << bit)
                ts[r] = jnp.where(count_ge(ref, groups[r], cand) >= need[r], cand, ts[r])
        return ts

    if lp > ksel:
        t_hi = search16(hi_ref, (float(ksel),) * ng)
        need_lo = []
        for rows, t in zip(groups, t_hi):
            need_lo.append(ksel - count_ge(hi_ref, rows, t + 1))
            lo_ref[rows, :] = jnp.where(hi_ref[rows, :] == t, lo_ref[rows, :], -1)
        t_lo = search16(lo_ref, need_lo)
        t_s = jnp.concatenate([lax.shift_left(h - 2 ** 15, 16) | l for h, l in zip(t_hi, t_lo)], axis=0)
        key = key_ref[...]
        n_picked = jnp.sum(((key >= t_s) & (key > INT_MIN)).astype(F32), axis=-1, keepdims=True)

        @pl.when(jnp.max(n_picked) > ksel)
        def _():
            key = key_ref[...]
            rpos = (lp - 1) - lax.broadcasted_iota(jnp.int32, (bq, lp), 1)
            lo_ref[...] = jnp.where((key == t_s) & (key > INT_MIN), rpos, -1)
            need = [ksel - jnp.sum((key[rows] > t_s[rows]).astype(F32), axis=-1, keepdims=True) for rows in groups]
            t_pos = jnp.concatenate(search16(lo_ref, need), axis=0)
            tie = lo_ref[...]
            key_ref[...] = jnp.where((tie >= 0) & (tie < t_pos), INT_MIN, key)
    else:
        t_s = jnp.full((bq, 1), INT_MIN, jnp.int32)
    key = key_ref[...]
    sel = (key >= t_s) & (key > INT_MIN)

    d_base = -(q0 // LANE + i * (bq // LANE))
    nk = lp // LANE
    for n in range(HKV_A):
        q4 = jnp.concatenate([qa_ref[:, (n * GROUP_A + g) * DH_A:(n * GROUP_A + g + 1) * DH_A]
                              for g in range(GROUP_A)], axis=0)
        k_n = ka_ref[:, n * DH_A:(n + 1) * DH_A].astype(BF16)
        v_n = va_ref[:, n * DH_A:(n + 1) * DH_A].astype(BF16)
        lg4 = lax.dot_general(q4, k_n, NT_DIMS, preferred_element_type=F32) * DH_A ** -0.5
        ps, ss = [], []
        for g in range(GROUP_A):
            bias = jnp.concatenate(
                [jnp.concatenate([bias_ref[n * GROUP_A + g,
                                           jnp.maximum(jnp.minimum(d_base + j - u, d_max) - (d_min + far), 0)]
                                  for j in range(nk)], axis=1)
                 for u in range(bq // bias_ref.shape[2])], axis=0)
            lg = jnp.where(sel, lg4[g * bq:(g + 1) * bq] + bias, -jnp.inf)
            p = jnp.exp(lg - jnp.max(lg, axis=-1, keepdims=True))
            ss.append(jnp.sum(p, axis=-1, keepdims=True))
            ps.append(p.astype(BF16))
        o4 = jnp.dot(jnp.concatenate(ps, axis=0), v_n, preferred_element_type=F32)
        for g in range(GROUP_A):
            h = n * GROUP_A + g
            gate = ag_ref[:, h * DH_A:(h + 1) * DH_A]
            o_ref[:, h * DH_A:(h + 1) * DH_A] = (o4[g * bq:(g + 1) * bq] / ss[g] * _silu(gate)).astype(BF16)


def _dsa(qa, z, ka_all, va_all, kidx_all, kidx_col, bias_tiles, acc, *, bq, l_true, q0, ksel, row0, rows, lp):
    b, s, _ = qa.shape
    bias_tiles, d_min, d_max, far = bias_tiles
    blk0 = row0 // bq
    assert q0 % LANE == 0 and (bq % LANE == 0 or s == bq)
    assert d_min <= -((q0 + s - bias_tiles.shape[2]) // LANE) and bq % bias_tiles.shape[2] == 0
    kern = functools.partial(_dsa_kernel, bq=bq, lp=lp, l_true=l_true, q0=q0, ksel=ksel,
                             d_min=d_min, d_max=d_max, far=far, blk0=blk0)

    def zspec(width, off):
        return pl.BlockSpec((None, bq, width), lambda bi, qi: (bi, qi + blk0, off // width))

    def kspec(width):
        return pl.BlockSpec((None, lp, width), lambda bi, qi: (bi, 0, 0))

    return pl.pallas_call(
        kern,
        grid=(b, rows // bq),
        in_specs=[pl.BlockSpec((None, bq, 1024), lambda bi, qi: (bi, qi + blk0, 0)),
                  zspec(1024, E_IQ), zspec(128, E_IDX), zspec(1024, E_AG),
                  kspec(256), kspec(256),
                  pl.BlockSpec((None, lp, LANE), lambda bi, qi: (bi, 0, kidx_col)),
                  _const_spec(bias_tiles.shape),
                  pl.BlockSpec(memory_space=pl.ANY)],
        out_specs=pl.BlockSpec((None, bq, 1024), lambda bi, qi: (bi, qi + blk0, 0)),
        out_shape=jax.ShapeDtypeStruct(acc.shape, acc.dtype),
        input_output_aliases={8: 0},
        scratch_shapes=[pltpu.VMEM((bq, lp), jnp.int32)] * 3,
        compiler_params=_params("parallel", "arbitrary"),
        name="dsa",
    )(qa, z, z, z, ka_all, va_all, kidx_all, bias_tiles, acc)


def _causal_classes(s, lp, q0):
    if q0 == 0 and s % CLASS_ROWS == 0 and lp == s:
        return [(c * CLASS_ROWS, CLASS_ROWS, (c + 1) * CLASS_ROWS) for c in range(s // CLASS_ROWS)]
    return [(0, s, lp)]


def _mla_kernel(qb_ref, bg_ref, kb_ref, vb_ref, acc_ref, o_ref, *, bq, lp, l_true, q0, blk0, n_full):
    q_start = q0 + (pl.program_id(1) + blk0) * bq
    qpos = q_start + lax.broadcasted_iota(jnp.int32, (bq, 1), 0)
    kpos = n_full + lax.broadcasted_iota(jnp.int32, (1, lp - n_full), 1)
    valid = ((kpos >> CHUNK_SHIFT) <= (qpos >> CHUNK_SHIFT)) & (kpos < l_true)
    for h in range(H_B):
        q = qb_ref[:, h * 256:(h + 1) * 256]
        parts = []
        if n_full:
            parts.append((lax.dot_general(q, kb_ref[:n_full, h * 256:(h + 1) * 256], NT_DIMS,
                                          preferred_element_type=F32), vb_ref[:n_full, h * V_DIM:(h + 1) * V_DIM]))
        lg = lax.dot_general(q, kb_ref[n_full:, h * 256:(h + 1) * 256], NT_DIMS, preferred_element_type=F32)
        parts.append((jnp.where(valid, lg, -jnp.inf), vb_ref[n_full:, h * V_DIM:(h + 1) * V_DIM]))
        o = _softmax_pv(parts, QK_B ** -0.5)
        gate = bg_ref[:, h * V_DIM:(h + 1) * V_DIM]
        o_ref[:, h * V_DIM:(h + 1) * V_DIM] = (o * _silu(gate)).astype(BF16)


def _mla(qb, z, kb, vb, acc, *, bq, l_true, q0, row0, rows, lp):
    b = qb.shape[0]
    blk0 = row0 // bq
    n_full = min(min(q0 + row0 + CHUNK, l_true) // LANE * LANE, lp - LANE)
    kern = functools.partial(_mla_kernel, bq=bq, lp=lp, l_true=l_true, q0=q0, blk0=blk0, n_full=n_full)
    return pl.pallas_call(
        kern,
        grid=(b, rows // bq),
        in_specs=[pl.BlockSpec((None, bq, 2048), lambda bi, qi: (bi, qi + blk0, 0)),
                  pl.BlockSpec((None, bq, 1024), lambda bi, qi: (bi, qi + blk0, E_BG // 1024)),
                  pl.BlockSpec((None, lp, 2048), lambda bi, qi: (bi, 0, 0)),
                  pl.BlockSpec((None, lp, 1024), lambda bi, qi: (bi, 0, 0)),
                  pl.BlockSpec(memory_space=pl.ANY)],
        out_specs=pl.BlockSpec((None, bq, 1024), lambda bi, qi: (bi, qi + blk0, 0)),
        out_shape=jax.ShapeDtypeStruct(acc.shape, acc.dtype),
        input_output_aliases={4: 0},
        compiler_params=_params("parallel", "arbitrary"),
        name="mla",
    )(qb, z, kb, vb, acc)


def _out_ple_kernel(h_ref, oa_ref, ob_ref, p_ref, wo_ref, wg_ref, wp_ref, o_ref):
    half = oa_ref.shape[-1]
    h1 = (h_ref[...]
          + jnp.dot(oa_ref[...], wo_ref[:half, :], preferred_element_type=F32)
          + jnp.dot(ob_ref[...], wo_ref[half:, :], preferred_element_type=F32))
    r = (h1 * _rs(h1)).astype(BF16)
    gate = 1.0 / (1.0 + jnp.exp(-jnp.dot(r, wg_ref[...], preferred_element_type=F32)))
    o_ref[...] = h1 + gate * jnp.dot(p_ref[...].astype(BF16), wp_ref[...], preferred_element_type=F32)


def _out_ple(h, oa, ob, p_all, layer, wo, wg, wp, bm):
    m, d = h.shape
    half = oa.shape[1]

    def rows(width):
        return pl.BlockSpec((bm, width), lambda i: (i, 0))

    return pl.pallas_call(
        _out_ple_kernel,
        grid=(m // bm,),
        in_specs=[rows(d), rows(half), rows(half), pl.BlockSpec((None, bm, PLE_DIM), lambda i: (layer, i, 0)),
                  _const_spec((2 * half, d)), _const_spec((d, d)), _const_spec((PLE_DIM, d))],
        out_specs=rows(d),
        out_shape=jax.ShapeDtypeStruct((m, d), F32),
        compiler_params=_params("parallel"),
        name="out_ple",
    )(h, oa, ob, p_all, wo, wg, wp)


def _odd_prep_kernel(cq_ref, ck_ref, cv_ref, dv_ref, du_ref, dgate_ref, cqn_ref, ckn_ref, dg_ref, db_ref,
                     ws_ref, bs_ref, qc_o, kcb_o, vcb_o, od_o, kct_o, cvt_o, *maybe_dvn_o, n):
    for h in range(H_C):
        sl = slice(h * DH_C, (h + 1) * DH_C)
        x = cq_ref[:, sl]
        qc_o[:, sl] = (x * _rs(x) * cqn_ref[...]).astype(BF16)
        x = ck_ref[:, sl]
        kn = x * _rs(x) * ckn_ref[...]
        kct_o[:, sl] = kn
        kcb_o[:, sl] = kn.astype(BF16)
    cv = cv_ref[...]
    cvt_o[...] = cv
    vcb_o[...] = cv.astype(BF16)
    dv = dv_ref[...]
    xc = dv - jnp.mean(dv, -1, keepdims=True)
    var = jnp.mean(xc * xc, -1, keepdims=True)
    dvn = xc * lax.rsqrt(var + 1e-5) * dg_ref[...] + db_ref[...]
    for dvn_o in maybe_dvn_o:
        dvn_o[...] = dvn
    dvn = dvn.astype(BF16)
    row = lax.broadcasted_iota(jnp.int32, (n, n), 0)
    col = lax.broadcasted_iota(jnp.int32, (n, n), 1)
    for g in range(G_D):
        sl = slice(g * DG_D, (g + 1) * DG_D)
        w = jnp.where(col <= row, ws_ref[g], 0.0).astype(BF16)
        for c in range(dv.shape[0] // n):
            rs = slice(c * n, (c + 1) * n)
            sg = jnp.dot(w, dvn[rs, sl], preferred_element_type=F32) + bs_ref[:, g:g + 1]
            od_o[rs, sl] = (du_ref[rs, sl] * sg * _silu(dgate_ref[rs, sl])).astype(BF16)


def _odd_prep(z, cqn, ckn, dg, db, ws, bs_t, bs, n, keep, want_dvn):
    b, s, _ = z.shape
    first_tail = (s - keep) // bs
    assert keep % bs == 0

    def zspec(off):
        return pl.BlockSpec((None, bs, 1024), lambda bi, si: (bi, si, off // 1024))

    ospec = pl.BlockSpec((None, bs, 1024), lambda bi, si: (bi, si, 0))
    tspec = pl.BlockSpec((None, bs, 1024), lambda bi, si: (bi, jnp.maximum(si - first_tail, 0), 0))
    act = lambda dt: jax.ShapeDtypeStruct((b, s, 1024), dt)
    tail = jax.ShapeDtypeStruct((b, keep, 1024), F32)
    return pl.pallas_call(
        functools.partial(_odd_prep_kernel, n=n),
        grid=(b, s // bs),
        in_specs=[zspec(O_CQ), zspec(O_CK), zspec(O_CV), zspec(O_DV), zspec(O_DU), zspec(O_DG),
                  _const_spec((1, DH_C)), _const_spec((1, DH_C)), _const_spec((1, W_D)), _const_spec((1, W_D)),
                  _const_spec((G_D, n, n)), _const_spec((n, G_D))],
        out_specs=[ospec] * 4 + [tspec] * 2 + [ospec] * want_dvn,
        out_shape=[act(BF16)] * 4 + [tail] * 2 + [act(F32)] * want_dvn,
        compiler_params=_params("parallel", "arbitrary"),
        name="odd_prep",
    )(z, z, z, z, z, z, cqn, ckn, dg, db, ws, bs_t)


BACK_TILES = C_BACK * CHUNK // LANE


def _band_width(bq):
    return (BACK_TILES + -(-bq // LANE)) * LANE


def _band_kernel(q_ref, cg_ref, k_ref, v_ref, *rest, bq, sliding, q0, k0, k_end, c0):
    i = pl.program_id(1)
    q_start = q0 + i * bq
    bw = _band_width(bq)
    if sliding:
        cbias_ref, bias_ref, o_ref = rest
        first = i * (bq // LANE) - BACK_TILES
        win_start = first * LANE
        rows = [pl.ds(pl.multiple_of(jnp.maximum(first + t, 0) * LANE, LANE), LANE) for t in range(bw // LANE)]
        kw = jnp.concatenate([k_ref[r, :] for r in rows], axis=0)
        vw = jnp.concatenate([v_ref[r, :] for r in rows], axis=0)
    else:
        kn_ref, vn_ref, cbias_ref, bias_ref, o_ref = rest
        win_start = k0
        pad = jnp.zeros((bw - k_ref.shape[0] - bq, k_ref.shape[1]), BF16)
        kw = jnp.concatenate([k_ref[...].astype(BF16), kn_ref[...], pad], axis=0)
        vw = jnp.concatenate([v_ref[...].astype(BF16), vn_ref[...], pad], axis=0)
    qpos = q_start + lax.broadcasted_iota(jnp.int32, (bq, 1), 0)

    def valid(lo, hi):
        kpos = win_start + lo + lax.broadcasted_iota(jnp.int32, (1, hi - lo), 1)
        dc = (qpos >> CHUNK_SHIFT) - (kpos >> CHUNK_SHIFT)
        return (dc >= 0) & (dc <= C_BACK) & (kpos >= 0) & (kpos < k_end)

    segs = [(lo, hi, valid(lo, hi)) for lo, hi in ((0, c0), (c0, bw)) if hi > lo]
    for h in range(H_C):
        sl = slice(h * DH_C, (h + 1) * DH_C)
        q = q_ref[:, sl]
        parts = []
        for lo, hi, ok in segs:
            bias = cbias_ref[h][:, :1] if hi <= c0 else bias_ref[h]
            lg = lax.dot_general(q, kw[lo:hi, sl], NT_DIMS, preferred_element_type=F32) * DH_C ** -0.5
            parts.append((jnp.where(ok, lg + bias, -jnp.inf), vw[lo:hi, sl]))
        o = _softmax_pv(parts)
        o_ref[:, sl] = (o * _silu(cg_ref[:, sl])).astype(BF16)


def _band(qc, z, keys, vals, bias, *, bq, sliding, q0, k0, k_end):
    c0, cbias, bias = bias
    b, s, _ = qc.shape
    if sliding:
        assert bq % LANE == 0 and q0 == 0 and k0 == 0
        kv = [keys, vals]
    else:
        assert keys[0].shape[1] + bq <= _band_width(bq) and s == bq
        kv = [keys[0], vals[0], keys[1], vals[1]]
    kern = functools.partial(_band_kernel, bq=bq, sliding=sliding, q0=q0, k0=k0, k_end=k_end, c0=c0)
    return pl.pallas_call(
        kern,
        grid=(b, s // bq),
        in_specs=[pl.BlockSpec((None, bq, 1024), lambda bi, qi: (bi, qi, 0)),
                  pl.BlockSpec((None, bq, 1024), lambda bi, qi: (bi, qi, O_CG // 1024))]
                 + [pl.BlockSpec((None,) + a.shape[1:], lambda bi, qi: (bi, 0, 0)) for a in kv]
                 + [_const_spec(cbias.shape), _const_spec(bias.shape)],
        out_specs=pl.BlockSpec((None, bq, 1024), lambda bi, qi: (bi, qi, 0)),
        out_shape=jax.ShapeDtypeStruct((b, s, 1024), BF16),
        compiler_params=_params("parallel", "arbitrary"),
        name="band",
    )(qc, z, *kv, cbias, bias)


def _rope_tables(pos):
    half = ROPE_DIM // 2
    freq = ROPE_BASE ** (-jnp.arange(half, dtype=F32) / half)
    ang = pos.astype(F32)[:, None] * freq[None, :]
    cos, sin = jnp.cos(ang), jnp.sin(ang)
    z = jnp.zeros_like(cos)
    return jnp.concatenate([cos, z, cos, z], 1), jnp.concatenate([-sin, z, sin, z], 1)


def _rope_lanes(x):
    half = ROPE_DIM // 2
    z = jnp.zeros(x.shape[:-1] + (half,), x.dtype)
    return jnp.concatenate([x[..., :half], z, x[..., half:], z], -1)


def _t5_bucket_np(rel):
    nb = T5_BUCKETS // 2
    max_exact = nb // 2
    n = np.abs(rel)
    nf = np.maximum(n, 1).astype(np.float64)
    large = max_exact + (np.log(nf / max_exact) / math.log(T5_MAX_DIST / max_exact) * (nb - max_exact)).astype(np.int64)
    large = np.minimum(large, nb - 1)
    return np.where(rel > 0, nb, 0) + np.where(n < max_exact, n, large)


def _toeplitz(w, rows, width, cols):
    flat = jnp.tile(w, (1,) * (w.ndim - 1) + (rows,))[..., :rows * width]
    return flat.reshape(w.shape[:-1] + (rows, width))[..., :cols]


def _t5_tiles(t5_bias, bq, d_min):
    d_max = max(bq // LANE - 1, 0)
    width = LANE + max(bq, LANE)
    k = np.arange(width + 1)
    delta = np.where(k < LANE, k, k - (width + 1))
    rel = np.arange(d_min, d_max + 1)[:, None] * LANE + delta[None, :]
    bucket = _t5_bucket_np(rel)
    far = 0
    while far + 1 < len(bucket) and np.array_equal(bucket[far + 1], bucket[0]):
        far += 1
    w = jnp.transpose(t5_bias[bucket[far:]], (2, 0, 1))
    return _toeplitz(w, bq, width, LANE), d_min, d_max, far


def _band_bias(rel_tab, bq, qk_off):
    bw = _band_width(bq)
    rel_index = lambda rel: np.clip(rel, -(CHUNK - 1), REL_CLIP) + (CHUNK - 1)
    full = rel_index(qk_off + np.arange(bq)[:, None] - np.arange(bw)[None, :])
    c0 = 0
    while c0 + 2 * MXU_COLS <= bw and np.all(full[:, :c0 + MXU_COLS] == full[0, 0]):
        c0 += MXU_COLS
    const = jnp.broadcast_to(rel_tab[full[0, 0]][:, None, None], (H_C, 1, LANE))
    wv = bw - c0
    width = wv + bq
    k = np.arange(width + 1)
    delta = np.where(k < wv, k, k - (width + 1))
    idx = rel_index(qk_off - c0 - delta)
    return c0, const, _toeplitz(jnp.transpose(rel_tab[idx], (1, 0)), bq, width, wv)


def _even_weights(w_in, b_wuq, b_wukv, b_qn, b_kn):
    d = w_in.shape[0]
    offs = np.cumsum((0,) + EVEN_SPLITS)
    w16 = w_in.astype(BF16)
    aq, ak, av, ag, iq, ik, iw, bcq, bckv, bkpe, bg = [w16[:, offs[t]:offs[t + 1]] for t in range(11)]
    slab_idx = jnp.concatenate([ik, iw, jnp.zeros((d, LANE - D_IDX - H_IDX), BF16)], 1)
    w = jnp.concatenate([aq, ag, iq, bg, bcq, ak, av, bckv, slab_idx, _rope_lanes(bkpe)], 1)
    uq = b_wuq.reshape(Q_LORA, H_B, QK_B)
    uq = jnp.concatenate([uq[..., :NOPE], _rope_lanes(uq[..., NOPE:])], -1).reshape(Q_LORA, H_B * 256).astype(BF16)
    ukv = b_wukv.reshape(KV_LORA, H_B, NOPE + V_DIM)
    ukv = jnp.concatenate([ukv[..., :NOPE].reshape(KV_LORA, H_B * NOPE),
                           ukv[..., NOPE:].reshape(KV_LORA, H_B * V_DIM)], 1).astype(BF16)
    pad_gain = lambda g: jnp.concatenate([g[:NOPE], _rope_lanes(g[NOPE:])])[None, :]
    return w, uq, ukv, pad_gain(b_qn), pad_gain(b_kn)


def _pad_rows(x, lp):
    return jnp.pad(x, ((0, 0), (0, lp - x.shape[1]), (0, 0)))


def _even_layer(h, p_all, layer, past, q0, ln_g, w_in, uq, ukv, a_qn, a_kn, t5_bias, b_qln, b_kvln, bqn, bkn, wo, wg, wp):
    b, s, d = h.shape
    m = b * s
    z = _norm_mm(h.reshape(m, d), ln_g[None, :], w_in, min(m, 1024), E_END // 4).reshape(b, s, E_END)
    cos, sin = _rope_tables(q0 + jnp.arange(s, dtype=jnp.int32))
    bs = min(s, 512)
    qa, ka, qb, ckv, kpe_l, kpe64, idx64 = _even_prep(z, cos, sin, a_qn[None, :], a_kn[None, :], b_qln[None, :],
                                                      b_kvln[None, :], uq, bqn, bs)
    av = z[..., E_AV:E_AV + 256]
    new = (ka.reshape(b, s, HKV_A, DH_A), av.reshape(b, s, HKV_A, DH_A), idx64, ckv, kpe64)
    if past is None:
        l_true = s
        ka_all, va_all, ckv_all, kpe_all = ka, av, ckv, kpe_l
        kidx_all, kidx_col = z, E_IDX // LANE
    else:
        kidx = z[..., E_IDX:E_IDX + LANE]
        kidx_col = 0
        c_k, c_v, c_ik, c_ckv, c_kpe = past
        pl_ = c_k.shape[1]
        l_true = pl_ + s
        lp = -(-l_true // LANE) * LANE
        cat = lambda c, n_: _pad_rows(jnp.concatenate([c, n_], 1), lp)
        ka_all = cat(c_k.reshape(b, pl_, 256), ka)
        va_all = cat(c_v.reshape(b, pl_, 256), av)
        kidx_all = cat(jnp.pad(c_ik, ((0, 0), (0, 0), (0, LANE - D_IDX))), kidx)
        ckv_all = cat(c_ckv, ckv)
        kpe_all = cat(_rope_lanes(c_kpe), kpe_l)
    lp = ka_all.shape[1]
    ksel = min(TOPK_MAX, l_true // 4)
    classes = _causal_classes(s, lp, q0)
    bqs = [nr if (nr % LANE == 0 and lc <= DSA_WIDE_BLOCK_KEYS) else min(s, LANE) for _, nr, lc in classes]
    tb = min(s, LANE)
    tiles = _t5_tiles(t5_bias, tb, -((q0 + s - tb) // LANE))
    o_a = jnp.zeros((b, s, H_A * DH_A), BF16)
    for (r0, nr, lc), bq in zip(classes, bqs):
        o_a = _dsa(qa, z, ka_all, va_all, kidx_all, kidx_col, tiles, o_a, bq=bq, l_true=l_true, q0=q0,
                   ksel=ksel, row0=r0, rows=nr, lp=lc)
    kb, vb = _mla_kv(ckv_all, kpe_all, ukv, bkn, 1024 if lp % 1024 == 0 else _key_chunk(lp))
    o_b = jnp.zeros((b, s, H_B * V_DIM), BF16)
    for r0, nr, lc in classes:
        o_b = _mla(qb, z, kb, vb, o_b, bq=min(s, 256), l_true=l_true, q0=q0, row0=r0, rows=nr, lp=lc)
    y = _out_ple(h.reshape(m, d), o_a.reshape(m, -1), o_b.reshape(m, -1), p_all, layer, wo, wg, wp, min(m, 512))
    return y.reshape(b, s, d), new


def _odd_layer(h, p_all, layer, past, q0, ln_g, w_in, c_qn, c_kn, c_rel, d_g, d_b, d_ws, d_bs, wo, wg, wp):
    b, s, d = h.shape
    m = b * s
    z = _norm_mm(h.reshape(m, d), ln_g[None, :], w_in, min(m, 1024), O_END // 4).reshape(b, s, O_END)
    n = min(s, D_CHUNK)
    keep = min(C_BACK * CHUNK, s) if past is None else s
    qc, kcb, vcb, o_d, kct, cvt, *dvn = _odd_prep(z, c_qn[None, :], c_kn[None, :], d_g[None, :], d_b[None, :],
                                                  d_ws[:, :n, :n], d_bs[:, :n].T, min(s, 512), n, keep,
                                                  want_dvn=past is not None)
    c_new = (kct.reshape(b, keep, H_C, DH_C), cvt.reshape(b, keep, H_C, DH_C))
    if past is None:
        bqc = 2 * LANE
        bias = _band_bias(c_rel, bqc, C_BACK * CHUNK)
        o_c = _band(qc, z, kcb, vcb, bias, bq=bqc, sliding=True, q0=0, k0=0, k_end=s)
    else:
        nc = past[0].shape[1]
        bias = _band_bias(c_rel, s, nc)
        o_c = _band(qc, z, (past[0].reshape(b, nc, 1024), kcb), (past[1].reshape(b, nc, 1024), vcb), bias,
                    bq=s, sliding=False, q0=q0, k0=q0 - nc, k_end=q0 + s)
    y = _out_ple(h.reshape(m, d), o_c.reshape(m, -1), o_d.reshape(m, -1), p_all, layer, wo, wg, wp, min(m, 512))
    return y.reshape(b, s, d), c_new, (dvn[0] if dvn else None)


def kernel(x_prompt, x_sample, cache_a_k, cache_a_v, cache_a_idx_k, cache_b_ckv, cache_b_kpe, cache_c_k, cache_c_v, p_prompt, p_sample, ln_g, w_in_even, a_q_norm, a_k_norm, t5_bias, b_q_lora_norm, b_kv_lora_norm, b_w_uq, b_w_ukv, b_q_norm, b_k_norm, w_out_even, w_in_odd, c_q_norm, c_k_norm, c_rel_bias, d_ln_g, d_ln_b, d_w_s, d_b_s, w_out_odd, ple_proj, ple_gate):
    depth = ln_g.shape[0]
    past_len = cache_a_k.shape[2]
    hp, hs = x_prompt, x_sample
    pp = p_prompt.reshape(depth, -1, PLE_DIM)
    ps = p_sample.reshape(depth, -1, PLE_DIM)
    ev_p, ev_s, od_p, od_s, dv_s = [], [], [], [], []
    for i in range(depth):
        j = i // 2
        wg = ple_gate[i].astype(BF16)
        wp = ple_proj[i].astype(BF16)
        if i % 2 == 0:
            w_in, uq, ukv, bqn, bkn = _even_weights(w_in_even[j], b_w_uq[j], b_w_ukv[j], b_q_norm[j], b_k_norm[j])
            w = (ln_g[i], w_in, uq, ukv, a_q_norm[j], a_k_norm[j], t5_bias, b_q_lora_norm[j], b_kv_lora_norm[j],
                 bqn, bkn, w_out_even[j].astype(BF16), wg, wp)
            hp, sp = _even_layer(hp, pp, i, None, 0, *w)
            past = (cache_a_k[j], cache_a_v[j], cache_a_idx_k[j], cache_b_ckv[j], cache_b_kpe[j])
            hs, ss = _even_layer(hs, ps, i, past, past_len, *w)
            ev_p.append(sp)
            ev_s.append(ss)
        else:
            w = (ln_g[i], w_in_odd[j].astype(BF16), c_q_norm[j], c_k_norm[j], c_rel_bias[j], d_ln_g[j], d_ln_b[j],
                 d_w_s[j], d_b_s[j], w_out_odd[j].astype(BF16), wg, wp)
            hp, sp, _ = _odd_layer(hp, pp, i, None, 0, *w)
            hs, ss, dvs = _odd_layer(hs, ps, i, (cache_c_k[j], cache_c_v[j]), past_len, *w)
            od_p.append(sp)
            od_s.append(ss)
            dv_s.append(dvs)
    st = lambda lst, n_: jnp.stack([e[n_] for e in lst], 0)
    return (hp, hs, st(ev_p, 0), st(ev_p, 1), st(ev_p, 2), st(ev_p, 3), st(ev_p, 4), st(od_p, 0), st(od_p, 1),
            st(ev_s, 0), st(ev_s, 1), st(ev_s, 2), st(ev_s, 3), st(ev_s, 4), st(od_s, 0), st(od_s, 1),
            jnp.stack(dv_s, 0))
```

```python
import functools
import math

import numpy as np
import jax
import jax.numpy as jnp
from jax import lax
from jax.experimental import pallas as pl
from jax.experimental.pallas import tpu as pltpu

F32 = jnp.float32
BF16 = jnp.bfloat16
INT_MIN = -2 ** 31
LOG2E = math.log2(math.e)
CLASS_ROWS = 256
SEARCH_GROUPS = 4
DSA_WIDE_BLOCK_KEYS = 1280

D_MODEL = 2048
CHUNK = 64
CHUNK_SHIFT = 6
LANE = 128
MXU_COLS = 256
H_A, HKV_A, GROUP_A, DH_A = 8, 2, 4, 128
H_IDX, D_IDX = 16, 64
TOPK_MAX = 256
T5_BUCKETS, T5_MAX_DIST = 32, 128
H_B, Q_LORA, KV_LORA, NOPE, ROPE_DIM, V_DIM = 8, 512, 256, 128, 64, 128
ROPE_BASE = 10000.0
QK_B = NOPE + ROPE_DIM
H_C, DH_C, C_BACK, REL_CLIP = 8, 128, 8, 128
W_D, G_D, DG_D, D_CHUNK = 1024, 8, 128, 128
PLE_DIM = 256

EVEN_SPLITS = (H_A * DH_A, HKV_A * DH_A, HKV_A * DH_A, H_A * DH_A, H_IDX * D_IDX, D_IDX, H_IDX,
               Q_LORA, KV_LORA, ROPE_DIM, H_B * V_DIM)
E_AQ, E_AG, E_IQ, E_BG, E_BCQ, E_AK, E_AV, E_CKV, E_IDX, E_KPE, E_END = (
    0, 1024, 2048, 3072, 4096, 4608, 4864, 5120, 5376, 5504, 5632)
O_CQ, O_CK, O_CV, O_CG, O_DU, O_DV, O_DG, O_END = 0, 1024, 2048, 3072, 4096, 5120, 6144, 7168

VMEM_LIMIT_BYTES = 58 * 1024 * 1024
NT_DIMS = (((1,), (1,)), ((), ()))


def _params(*sem):
    return pltpu.CompilerParams(dimension_semantics=sem, vmem_limit_bytes=VMEM_LIMIT_BYTES)


def _const_spec(shape):
    zeros = (0,) * len(shape)
    return pl.BlockSpec(shape, lambda *_: zeros, pipeline_mode=pl.Buffered(1))


def _rs(x, n=None, eps=1e-6):
    n = x.shape[-1] if n is None else n
    return lax.rsqrt(jnp.sum(x * x, axis=-1, keepdims=True) / n + eps)


def _silu(x):
    return x * (1.0 / (1.0 + jnp.exp(-x)))


def _softmax_pv(parts, scale=1.0):
    m = functools.reduce(jnp.maximum, [jnp.max(lg, axis=-1, keepdims=True) for lg, _ in parts])
    o = s = None
    for lg, v in parts:
        p = jnp.exp2((lg - m) * (scale * LOG2E))
        ps = jnp.sum(p, axis=-1, keepdims=True)
        po = jnp.dot(p.astype(BF16), v, preferred_element_type=F32)
        o, s = (po, ps) if o is None else (o + po, s + ps)
    return o / s


def _norm_mm_kernel(x_ref, g_ref, w_ref, o_ref, xn_ref):
    @pl.when(pl.program_id(1) == 0)
    def _():
        x = x_ref[...]
        xn_ref[...] = (x * _rs(x) * g_ref[...]).astype(BF16)

    o_ref[...] = jnp.dot(xn_ref[...], w_ref[...], preferred_element_type=F32)


def _norm_mm(x, g, w, bm, bn):
    m, d = x.shape
    n = w.shape[1]
    return pl.pallas_call(
        _norm_mm_kernel,
        grid=(m // bm, n // bn),
        in_specs=[pl.BlockSpec((bm, d), lambda i, j: (i, 0)),
                  pl.BlockSpec((1, d), lambda i, j: (0, 0)),
                  pl.BlockSpec((d, bn), lambda i, j: (0, j))],
        out_specs=pl.BlockSpec((bm, bn), lambda i, j: (i, j)),
        out_shape=jax.ShapeDtypeStruct((m, n), F32),
        scratch_shapes=[pltpu.VMEM((bm, d), BF16)],
        compiler_params=_params("parallel", "arbitrary"),
        name="norm_mm",
    )(x, g, w)


def _rope(x, cos, sin):
    return x * cos + pltpu.roll(x, 64, 1) * sin


def _even_prep_kernel(aq_ref, bcq_ref, ak_ref, ckv_ref, kpe_ref, idx_ref, cos_ref, sin_ref,
                      aqn_ref, akn_ref, qln_ref, kvln_ref, wuq_ref, bqn_ref,
                      qa_o, ka_o, qb_o, ckv_o, kpe_o, kpe64_o, idx64_o):
    for h in range(H_A):
        x = aq_ref[:, h * DH_A:(h + 1) * DH_A]
        qa_o[:, h * DH_A:(h + 1) * DH_A] = (x * _rs(x) * aqn_ref[...]).astype(BF16)
    for n in range(HKV_A):
        x = ak_ref[:, n * DH_A:(n + 1) * DH_A]
        ka_o[:, n * DH_A:(n + 1) * DH_A] = x * _rs(x) * akn_ref[...]
    c = ckv_ref[...]
    ckv_o[...] = c * _rs(c) * kvln_ref[...]
    cos = cos_ref[...]
    sin = sin_ref[...]
    kpe = _rope(kpe_ref[...], cos, sin)
    kpe_o[...] = kpe
    half = ROPE_DIM // 2
    kpe64_o[...] = jnp.concatenate([kpe[:, :half], kpe[:, 2 * half:3 * half]], axis=-1)
    idx64_o[...] = idx_ref[:, :D_IDX]
    cq = bcq_ref[...]
    cqn = (cq * _rs(cq) * qln_ref[...]).astype(BF16)
    qb = jnp.dot(cqn, wuq_ref[...], preferred_element_type=F32)
    g = bqn_ref[...]
    for h in range(H_B):
        nope = qb[:, h * 256:h * 256 + 128]
        rot = _rope(qb[:, h * 256 + 128:(h + 1) * 256], cos, sin)
        ss = jnp.sum(nope * nope + rot * rot, -1, keepdims=True)
        r = lax.rsqrt(ss / QK_B + 1e-6)
        qb_o[:, h * 256:h * 256 + 128] = (nope * r * g[:, :128]).astype(BF16)
        qb_o[:, h * 256 + 128:(h + 1) * 256] = (rot * r * g[:, 128:]).astype(BF16)


def _even_prep(z, cos, sin, aqn, akn, qln, kvln, wuq, bqn, bs):
    b, s, _ = z.shape

    def zspec(width, off):
        return pl.BlockSpec((None, bs, width), lambda bi, si: (bi, si, off // width))

    def ospec(width):
        return pl.BlockSpec((None, bs, width), lambda bi, si: (bi, si, 0))

    pos_spec = pl.BlockSpec((bs, LANE), lambda bi, si: (si, 0))
    return pl.pallas_call(
        _even_prep_kernel,
        grid=(b, s // bs),
        in_specs=[zspec(1024, E_AQ), zspec(512, E_BCQ), zspec(256, E_AK), zspec(256, E_CKV),
                  zspec(128, E_KPE), zspec(128, E_IDX), pos_spec, pos_spec,
                  _const_spec((1, DH_A)), _const_spec((1, DH_A)), _const_spec((1, Q_LORA)),
                  _const_spec((1, KV_LORA)), _const_spec((Q_LORA, H_B * 256)), _const_spec((1, 256))],
        out_specs=[ospec(1024), ospec(256), ospec(2048), ospec(256), ospec(128), ospec(ROPE_DIM), ospec(D_IDX)],
        out_shape=[jax.ShapeDtypeStruct((b, s, 1024), BF16), jax.ShapeDtypeStruct((b, s, 256), F32),
                   jax.ShapeDtypeStruct((b, s, 2048), BF16), jax.ShapeDtypeStruct((b, s, 256), F32),
                   jax.ShapeDtypeStruct((b, s, 128), F32), jax.ShapeDtypeStruct((b, s, ROPE_DIM), F32),
                   jax.ShapeDtypeStruct((b, s, D_IDX), F32)],
        compiler_params=_params("parallel", "arbitrary"),
        name="even_prep",
    )(z, z, z, z, z, z, cos, sin, aqn, akn, qln, kvln, wuq, bqn)


def _mla_kv_kernel(ckv_ref, kpe_ref, w_ref, g_ref, kb_o, vb_o):
    kv = jnp.dot(ckv_ref[...].astype(BF16), w_ref[...], preferred_element_type=F32)
    kp = kpe_ref[...]
    skp = jnp.sum(kp * kp, -1, keepdims=True)
    g = g_ref[...]
    for h in range(H_B):
        nope = kv[:, h * NOPE:(h + 1) * NOPE]
        r = lax.rsqrt((jnp.sum(nope * nope, -1, keepdims=True) + skp) / QK_B + 1e-6)
        kb_o[:, h * 256:h * 256 + 128] = (nope * r * g[:, :128]).astype(BF16)
        kb_o[:, h * 256 + 128:(h + 1) * 256] = (kp * r * g[:, 128:]).astype(BF16)
    vb_o[...] = kv[:, H_B * NOPE:].astype(BF16)


def _mla_kv(ckv_all, kpe_all, wukv, bkn, bl):
    b, lp, _ = ckv_all.shape
    return pl.pallas_call(
        _mla_kv_kernel,
        grid=(b, lp // bl),
        in_specs=[pl.BlockSpec((None, bl, KV_LORA), lambda bi, li: (bi, li, 0)),
                  pl.BlockSpec((None, bl, LANE), lambda bi, li: (bi, li, 0)),
                  _const_spec((KV_LORA, 2048)), _const_spec((1, 256))],
        out_specs=[pl.BlockSpec((None, bl, 2048), lambda bi, li: (bi, li, 0)),
                   pl.BlockSpec((None, bl, 1024), lambda bi, li: (bi, li, 0))],
        out_shape=[jax.ShapeDtypeStruct((b, lp, 2048), BF16), jax.ShapeDtypeStruct((b, lp, 1024), BF16)],
        compiler_params=_params("parallel", "arbitrary"),
        name="mla_kv",
    )(ckv_all, kpe_all, wukv, bkn)


def _key_chunk(lp):
    for c in (512, 384, 256, 128):
        if lp % c == 0:
            return c
    raise ValueError(lp)


def _dsa_kernel(qa_ref, iq_ref, sa_ref, ag_ref, ka_ref, va_ref, kidx_ref, bias_ref, acc_ref, o_ref,
                key_ref, hi_ref, lo_ref, *, bq, lp, l_true, q0, ksel, d_min, d_max, far, blk0):
    i = pl.program_id(1) + blk0
    q_start = q0 + i * bq
    qpos = q_start + lax.broadcasted_iota(jnp.int32, (bq, 1), 0)
    ck = _key_chunk(lp)

    iq = iq_ref[...].astype(BF16)
    a = jnp.concatenate([iq[:, p * LANE:(p + 1) * LANE] for p in range(H_IDX // 2)], axis=0)
    wi = sa_ref[...] * (H_IDX ** -0.5 * D_IDX ** -0.5)
    for c0 in range(0, lp, ck):
        kk = kidx_ref[c0:c0 + ck, :]
        lane = lax.broadcasted_iota(jnp.int32, kk.shape, 1)
        k_lo = jnp.where(lane < D_IDX, kk, 0.0).astype(BF16)
        k_hi = jnp.where(lane >= D_IDX, pltpu.roll(kk, D_IDX, 1), 0.0).astype(BF16)
        s_lo = lax.dot_general(a, k_lo, NT_DIMS, preferred_element_type=F32)
        s_hi = lax.dot_general(a, k_hi, NT_DIMS, preferred_element_type=F32)
        sc = jnp.zeros((bq, ck), F32)
        for p in range(H_IDX // 2):
            w0 = wi[:, D_IDX + 2 * p:D_IDX + 2 * p + 1]
            w1 = wi[:, D_IDX + 2 * p + 1:D_IDX + 2 * p + 2]
            sc = sc + w0 * jnp.maximum(s_lo[p * bq:(p + 1) * bq], 0.0)
            sc = sc + w1 * jnp.maximum(s_hi[p * bq:(p + 1) * bq], 0.0)
        kpos = c0 + lax.broadcasted_iota(jnp.int32, (1, ck), 1)
        valid = ((kpos >> CHUNK_SHIFT) <= (qpos >> CHUNK_SHIFT)) & (kpos < l_true)
        bits = lax.bitcast_convert_type(sc, jnp.int32)
        key = jnp.where(bits < 0, bits ^ 0x7FFFFFFF, bits)
        key = jnp.where(bits == INT_MIN, 0, key)
        key = jnp.where(valid, key, INT_MIN)
        key_ref[:, c0:c0 + ck] = key
        hi_ref[:, c0:c0 + ck] = (key >> 16) + 2 ** 15
        lo_ref[:, c0:c0 + ck] = key & 0xFFFF

    ng = SEARCH_GROUPS
    gr = bq // ng
    groups = [slice(r * gr, (r + 1) * gr) for r in range(ng)]

    def count_ge(ref, rows, cand):
        acc = None
        for j in range(lp // LANE):
            d = (ref[rows, j * LANE:(j + 1) * LANE] - cand) >> 31
            acc = d if acc is None else acc + d
        return lp + jnp.sum(acc.astype(F32), axis=-1, keepdims=True)

    def search16(ref, need):
        ts = [jnp.zeros((gr, 1), jnp.int32)] * ng
        for bit in reversed(range(16)):
            for r in range(ng):
                cand = ts[r] | (1 << bit)
                ts[r] = jnp.where(count_ge(ref, groups[r], cand) >= need[r], cand, ts[r])
        return ts

    if lp > ksel:
        t_hi = search16(hi_ref, (float(ksel),) * ng)
        need_lo = []
        for rows, t in zip(groups, t_hi):
            need_lo.append(ksel - count_ge(hi_ref, rows, t + 1))
            lo_ref[rows, :] = jnp.where(hi_ref[rows, :] == t, lo_ref[rows, :], -1)
        t_lo = search16(lo_ref, need_lo)
        t_s = jnp.concatenate([lax.shift_left(h - 2 ** 15, 16) | l for h, l in zip(t_hi, t_lo)], axis=0)
        key = key_ref[...]
        n_picked = jnp.sum(((key >= t_s) & (key > INT_MIN)).astype(F32), axis=-1, keepdims=True)

        @pl.when(jnp.max(n_picked) > ksel)
        def _():
            key = key_ref[...]
            rpos = (lp - 1) - lax.broadcasted_iota(jnp.int32, (bq, lp), 1)
            lo_ref[...] = jnp.where((key == t_s) & (key > INT_MIN), rpos, -1)
            need = [ksel - jnp.sum((key[rows] > t_s[rows]).astype(F32), axis=-1, keepdims=True) for rows in groups]
            t_pos = jnp.concatenate(search16(lo_ref, need), axis=0)
            tie = lo_ref[...]
            key_ref[...] = jnp.where((tie >= 0) & (tie < t_pos), INT_MIN, key)
    else:
        t_s = jnp.full((bq, 1), INT_MIN, jnp.int32)
    key = key_ref[...]
    sel = (key >= t_s) & (key > INT_MIN)

    d_base = -(q0 // LANE + i * (bq // LANE))
    nk = lp // LANE
    for n in range(HKV_A):
        q4 = jnp.concatenate([qa_ref[:, (n * GROUP_A + g) * DH_A:(n * GROUP_A + g + 1) * DH_A]
                              for g in range(GROUP_A)], axis=0)
        k_n = ka_ref[:, n * DH_A:(n + 1) * DH_A].astype(BF16)
        v_n = va_ref[:, n * DH_A:(n + 1) * DH_A].astype(BF16)
        lg4 = lax.dot_general(q4, k_n, NT_DIMS, preferred_element_type=F32) * DH_A ** -0.5
        ps, ss = [], []
        for g in range(GROUP_A):
            bias = jnp.concatenate(
                [jnp.concatenate([bias_ref[n * GROUP_A + g,
                                           jnp.maximum(jnp.minimum(d_base + j - u, d_max) - (d_min + far), 0)]
                                  for j in range(nk)], axis=1)
                 for u in range(bq // bias_ref.shape[2])], axis=0)
            lg = jnp.where(sel, lg4[g * bq:(g + 1) * bq] + bias, -jnp.inf)
            p = jnp.exp(lg - jnp.max(lg, axis=-1, keepdims=True))
            ss.append(jnp.sum(p, axis=-1, keepdims=True))
            ps.append(p.astype(BF16))
        o4 = jnp.dot(jnp.concatenate(ps, axis=0), v_n, preferred_element_type=F32)
        for g in range(GROUP_A):
            h = n * GROUP_A + g
            gate = ag_ref[:, h * DH_A:(h + 1) * DH_A]
            o_ref[:, h * DH_A:(h + 1) * DH_A] = (o4[g * bq:(g + 1) * bq] / ss[g] * _silu(gate)).astype(BF16)


def _dsa(qa, z, ka_all, va_all, kidx_all, kidx_col, bias_tiles, acc, *, bq, l_true, q0, ksel, row0, rows, lp):
    b, s, _ = qa.shape
    bias_tiles, d_min, d_max, far = bias_tiles
    blk0 = row0 // bq
    assert q0 % LANE == 0 and (bq % LANE == 0 or s == bq)
    assert d_min <= -((q0 + s - bias_tiles.shape[2]) // LANE) and bq % bias_tiles.shape[2] == 0
    kern = functools.partial(_dsa_kernel, bq=bq, lp=lp, l_true=l_true, q0=q0, ksel=ksel,
                             d_min=d_min, d_max=d_max, far=far, blk0=blk0)

    def zspec(width, off):
        return pl.BlockSpec((None, bq, width), lambda bi, qi: (bi, qi + blk0, off // width))

    def kspec(width):
        return pl.BlockSpec((None, lp, width), lambda bi, qi: (bi, 0, 0))

    return pl.pallas_call(
        kern,
        grid=(b, rows // bq),
        in_specs=[pl.BlockSpec((None, bq, 1024), lambda bi, qi: (bi, qi + blk0, 0)),
                  zspec(1024, E_IQ), zspec(128, E_IDX), zspec(1024, E_AG),
                  kspec(256), kspec(256),
                  pl.BlockSpec((None, lp, LANE), lambda bi, qi: (bi, 0, kidx_col)),
                  _const_spec(bias_tiles.shape),
                  pl.BlockSpec(memory_space=pl.ANY)],
        out_specs=pl.BlockSpec((None, bq, 1024), lambda bi, qi: (bi, qi + blk0, 0)),
        out_shape=jax.ShapeDtypeStruct(acc.shape, acc.dtype),
        input_output_aliases={8: 0},
        scratch_shapes=[pltpu.VMEM((bq, lp), jnp.int32)] * 3,
        compiler_params=_params("parallel", "arbitrary"),
        name="dsa",
    )(qa, z, z, z, ka_all, va_all, kidx_all, bias_tiles, acc)


def _causal_classes(s, lp, q0):
    if q0 == 0 and s % CLASS_ROWS == 0 and lp == s:
        return [(c * CLASS_ROWS, CLASS_ROWS, (c + 1) * CLASS_ROWS) for c in range(s // CLASS_ROWS)]
    return [(0, s, lp)]


def _mla_kernel(qb_ref, bg_ref, kb_ref, vb_ref, acc_ref, o_ref, *, bq, lp, l_true, q0, blk0, n_full):
    q_start = q0 + (pl.program_id(1) + blk0) * bq
    qpos = q_start + lax.broadcasted_iota(jnp.int32, (bq, 1), 0)
    kpos = n_full + lax.broadcasted_iota(jnp.int32, (1, lp - n_full), 1)
    valid = ((kpos >> CHUNK_SHIFT) <= (qpos >> CHUNK_SHIFT)) & (kpos < l_true)
    for h in range(H_B):
        q = qb_ref[:, h * 256:(h + 1) * 256]
        parts = []
        if n_full:
            parts.append((lax.dot_general(q, kb_ref[:n_full, h * 256:(h + 1) * 256], NT_DIMS,
                                          preferred_element_type=F32), vb_ref[:n_full, h * V_DIM:(h + 1) * V_DIM]))
        lg = lax.dot_general(q, kb_ref[n_full:, h * 256:(h + 1) * 256], NT_DIMS, preferred_element_type=F32)
        parts.append((jnp.where(valid, lg, -jnp.inf), vb_ref[n_full:, h * V_DIM:(h + 1) * V_DIM]))
        o = _softmax_pv(parts, QK_B ** -0.5)
        gate = bg_ref[:, h * V_DIM:(h + 1) * V_DIM]
        o_ref[:, h * V_DIM:(h + 1) * V_DIM] = (o * _silu(gate)).astype(BF16)


def _mla(qb, z, kb, vb, acc, *, bq, l_true, q0, row0, rows, lp):
    b = qb.shape[0]
    blk0 = row0 // bq
    n_full = min(min(q0 + row0 + CHUNK, l_true) // LANE * LANE, lp - LANE)
    kern = functools.partial(_mla_kernel, bq=bq, lp=lp, l_true=l_true, q0=q0, blk0=blk0, n_full=n_full)
    return pl.pallas_call(
        kern,
        grid=(b, rows // bq),
        in_specs=[pl.BlockSpec((None, bq, 2048), lambda bi, qi: (bi, qi + blk0, 0)),
                  pl.BlockSpec((None, bq, 1024), lambda bi, qi: (bi, qi + blk0, E_BG // 1024)),
                  pl.BlockSpec((None, lp, 2048), lambda bi, qi: (bi, 0, 0)),
                  pl.BlockSpec((None, lp, 1024), lambda bi, qi: (bi, 0, 0)),
                  pl.BlockSpec(memory_space=pl.ANY)],
        out_specs=pl.BlockSpec((None, bq, 1024), lambda bi, qi: (bi, qi + blk0, 0)),
        out_shape=jax.ShapeDtypeStruct(acc.shape, acc.dtype),
        input_output_aliases={4: 0},
        compiler_params=_params("parallel", "arbitrary"),
        name="mla",
    )(qb, z, kb, vb, acc)


def _out_ple_kernel(h_ref, oa_ref, ob_ref, p_ref, wo_ref, wg_ref, wp_ref, o_ref):
    half = oa_ref.shape[-1]
    h1 = (h_ref[...]
          + jnp.dot(oa_ref[...], wo_ref[:half, :], preferred_element_type=F32)
          + jnp.dot(ob_ref[...], wo_ref[half:, :], preferred_element_type=F32))
    r = (h1 * _rs(h1)).astype(BF16)
    gate = 1.0 / (1.0 + jnp.exp(-jnp.dot(r, wg_ref[...], preferred_element_type=F32)))
    o_ref[...] = h1 + gate * jnp.dot(p_ref[...].astype(BF16), wp_ref[...], preferred_element_type=F32)


def _out_ple(h, oa, ob, p_all, layer, wo, wg, wp, bm):
    m, d = h.shape
    half = oa.shape[1]

    def rows(width):
        return pl.BlockSpec((bm, width), lambda i: (i, 0))

    return pl.pallas_call(
        _out_ple_kernel,
        grid=(m // bm,),
        in_specs=[rows(d), rows(half), rows(half), pl.BlockSpec((None, bm, PLE_DIM), lambda i: (layer, i, 0)),
                  _const_spec((2 * half, d)), _const_spec((d, d)), _const_spec((PLE_DIM, d))],
        out_specs=rows(d),
        out_shape=jax.ShapeDtypeStruct((m, d), F32),
        compiler_params=_params("parallel"),
        name="out_ple",
    )(h, oa, ob, p_all, wo, wg, wp)


def _odd_prep_kernel(cq_ref, ck_ref, cv_ref, dv_ref, du_ref, dgate_ref, cqn_ref, ckn_ref, dg_ref, db_ref,
                     ws_ref, bs_ref, qc_o, kcb_o, vcb_o, od_o, kct_o, cvt_o, *maybe_dvn_o, n):
    for h in range(H_C):
        sl = slice(h * DH_C, (h + 1) * DH_C)
        x = cq_ref[:, sl]
        qc_o[:, sl] = (x * _rs(x) * cqn_ref[...]).astype(BF16)
        x = ck_ref[:, sl]
        kn = x * _rs(x) * ckn_ref[...]
        kct_o[:, sl] = kn
        kcb_o[:, sl] = kn.astype(BF16)
    cv = cv_ref[...]
    cvt_o[...] = cv
    vcb_o[...] = cv.astype(BF16)
    dv = dv_ref[...]
    xc = dv - jnp.mean(dv, -1, keepdims=True)
    var = jnp.mean(xc * xc, -1, keepdims=True)
    dvn = xc * lax.rsqrt(var + 1e-5) * dg_ref[...] + db_ref[...]
    for dvn_o in maybe_dvn_o:
        dvn_o[...] = dvn
    dvn = dvn.astype(BF16)
    row = lax.broadcasted_iota(jnp.int32, (n, n), 0)
    col = lax.broadcasted_iota(jnp.int32, (n, n), 1)
    for g in range(G_D):
        sl = slice(g * DG_D, (g + 1) * DG_D)
        w = jnp.where(col <= row, ws_ref[g], 0.0).astype(BF16)
        for c in range(dv.shape[0] // n):
            rs = slice(c * n, (c + 1) * n)
            sg = jnp.dot(w, dvn[rs, sl], preferred_element_type=F32) + bs_ref[:, g:g + 1]
            od_o[rs, sl] = (du_ref[rs, sl] * sg * _silu(dgate_ref[rs, sl])).astype(BF16)


def _odd_prep(z, cqn, ckn, dg, db, ws, bs_t, bs, n, keep, want_dvn):
    b, s, _ = z.shape
    first_tail = (s - keep) // bs
    assert keep % bs == 0

    def zspec(off):
        return pl.BlockSpec((None, bs, 1024), lambda bi, si: (bi, si, off // 1024))

    ospec = pl.BlockSpec((None, bs, 1024), lambda bi, si: (bi, si, 0))
    tspec = pl.BlockSpec((None, bs, 1024), lambda bi, si: (bi, jnp.maximum(si - first_tail, 0), 0))
    act = lambda dt: jax.ShapeDtypeStruct((b, s, 1024), dt)
    tail = jax.ShapeDtypeStruct((b, keep, 1024), F32)
    return pl.pallas_call(
        functools.partial(_odd_prep_kernel, n=n),
        grid=(b, s // bs),
        in_specs=[zspec(O_CQ), zspec(O_CK), zspec(O_CV), zspec(O_DV), zspec(O_DU), zspec(O_DG),
                  _const_spec((1, DH_C)), _const_spec((1, DH_C)), _const_spec((1, W_D)), _const_spec((1, W_D)),
                  _const_spec((G_D, n, n)), _const_spec((n, G_D))],
        out_specs=[ospec] * 4 + [tspec] * 2 + [ospec] * want_dvn,
        out_shape=[act(BF16)] * 4 + [tail] * 2 + [act(F32)] * want_dvn,
        compiler_params=_params("parallel", "arbitrary"),
        name="odd_prep",
    )(z, z, z, z, z, z, cqn, ckn, dg, db, ws, bs_t)


BACK_TILES = C_BACK * CHUNK // LANE


def _band_width(bq):
    return (BACK_TILES + -(-bq // LANE)) * LANE


def _band_kernel(q_ref, cg_ref, k_ref, v_ref, *rest, bq, sliding, q0, k0, k_end, c0):
    i = pl.program_id(1)
    q_start = q0 + i * bq
    bw = _band_width(bq)
    if sliding:
        cbias_ref, bias_ref, o_ref = rest
        first = i * (bq // LANE) - BACK_TILES
        win_start = first * LANE
        rows = [pl.ds(pl.multiple_of(jnp.maximum(first + t, 0) * LANE, LANE), LANE) for t in range(bw // LANE)]
        kw = jnp.concatenate([k_ref[r, :] for r in rows], axis=0)
        vw = jnp.concatenate([v_ref[r, :] for r in rows], axis=0)
    else:
        kn_ref, vn_ref, cbias_ref, bias_ref, o_ref = rest
        win_start = k0
        pad = jnp.zeros((bw - k_ref.shape[0] - bq, k_ref.shape[1]), BF16)
        kw = jnp.concatenate([k_ref[...].astype(BF16), kn_ref[...], pad], axis=0)
        vw = jnp.concatenate([v_ref[...].astype(BF16), vn_ref[...], pad], axis=0)
    qpos = q_start + lax.broadcasted_iota(jnp.int32, (bq, 1), 0)

    def valid(lo, hi):
        kpos = win_start + lo + lax.broadcasted_iota(jnp.int32, (1, hi - lo), 1)
        dc = (qpos >> CHUNK_SHIFT) - (kpos >> CHUNK_SHIFT)
        return (dc >= 0) & (dc <= C_BACK) & (kpos >= 0) & (kpos < k_end)

    segs = [(lo, hi, valid(lo, hi)) for lo, hi in ((0, c0), (c0, bw)) if hi > lo]
    for h in range(H_C):
        sl = slice(h * DH_C, (h + 1) * DH_C)
        q = q_ref[:, sl]
        parts = []
        for lo, hi, ok in segs:
            bias = cbias_ref[h][:, :1] if hi <= c0 else bias_ref[h]
            lg = lax.dot_general(q, kw[lo:hi, sl], NT_DIMS, preferred_element_type=F32) * DH_C ** -0.5
            parts.append((jnp.where(ok, lg + bias, -jnp.inf), vw[lo:hi, sl]))
        o = _softmax_pv(parts)
        o_ref[:, sl] = (o * _silu(cg_ref[:, sl])).astype(BF16)


def _band(qc, z, keys, vals, bias, *, bq, sliding, q0, k0, k_end):
    c0, cbias, bias = bias
    b, s, _ = qc.shape
    if sliding:
        assert bq % LANE == 0 and q0 == 0 and k0 == 0
        kv = [keys, vals]
    else:
        assert keys[0].shape[1] + bq <= _band_width(bq) and s == bq
        kv = [keys[0], vals[0], keys[1], vals[1]]
    kern = functools.partial(_band_kernel, bq=bq, sliding=sliding, q0=q0, k0=k0, k_end=k_end, c0=c0)
    return pl.pallas_call(
        kern,
        grid=(b, s // bq),
        in_specs=[pl.BlockSpec((None, bq, 1024), lambda bi, qi: (bi, qi, 0)),
                  pl.BlockSpec((None, bq, 1024), lambda bi, qi: (bi, qi, O_CG // 1024))]
                 + [pl.BlockSpec((None,) + a.shape[1:], lambda bi, qi: (bi, 0, 0)) for a in kv]
                 + [_const_spec(cbias.shape), _const_spec(bias.shape)],
        out_specs=pl.BlockSpec((None, bq, 1024), lambda bi, qi: (bi, qi, 0)),
        out_shape=jax.ShapeDtypeStruct((b, s, 1024), BF16),
        compiler_params=_params("parallel", "arbitrary"),
        name="band",
    )(qc, z, *kv, cbias, bias)


def _rope_tables(pos):
    half = ROPE_DIM // 2
    freq = ROPE_BASE ** (-jnp.arange(half, dtype=F32) / half)
    ang = pos.astype(F32)[:, None] * freq[None, :]
    cos, sin = jnp.cos(ang), jnp.sin(ang)
    z = jnp.zeros_like(cos)
    return jnp.concatenate([cos, z, cos, z], 1), jnp.concatenate([-sin, z, sin, z], 1)


def _rope_lanes(x):
    half = ROPE_DIM // 2
    z = jnp.zeros(x.shape[:-1] + (half,), x.dtype)
    return jnp.concatenate([x[..., :half], z, x[..., half:], z], -1)


def _t5_bucket_np(rel):
    nb = T5_BUCKETS // 2
    max_exact = nb // 2
    n = np.abs(rel)
    nf = np.maximum(n, 1).astype(np.float64)
    large = max_exact + (np.log(nf / max_exact) / math.log(T5_MAX_DIST / max_exact) * (nb - max_exact)).astype(np.int64)
    large = np.minimum(large, nb - 1)
    return np.where(rel > 0, nb, 0) + np.where(n < max_exact, n, large)


def _toeplitz(w, rows, width, cols):
    flat = jnp.tile(w, (1,) * (w.ndim - 1) + (rows,))[..., :rows * width]
    return flat.reshape(w.shape[:-1] + (rows, width))[..., :cols]


def _t5_tiles(t5_bias, bq, d_min):
    d_max = max(bq // LANE - 1, 0)
    width = LANE + max(bq, LANE)
    k = np.arange(width + 1)
    delta = np.where(k < LANE, k, k - (width + 1))
    rel = np.arange(d_min, d_max + 1)[:, None] * LANE + delta[None, :]
    bucket = _t5_bucket_np(rel)
    far = 0
    while far + 1 < len(bucket) and np.array_equal(bucket[far + 1], bucket[0]):
        far += 1
    w = jnp.transpose(t5_bias[bucket[far:]], (2, 0, 1))
    return _toeplitz(w, bq, width, LANE), d_min, d_max, far


def _band_bias(rel_tab, bq, qk_off):
    bw = _band_width(bq)
    rel_index = lambda rel: np.clip(rel, -(CHUNK - 1), REL_CLIP) + (CHUNK - 1)
    full = rel_index(qk_off + np.arange(bq)[:, None] - np.arange(bw)[None, :])
    c0 = 0
    while c0 + 2 * MXU_COLS <= bw and np.all(full[:, :c0 + MXU_COLS] == full[0, 0]):
        c0 += MXU_COLS
    const = jnp.broadcast_to(rel_tab[full[0, 0]][:, None, None], (H_C, 1, LANE))
    wv = bw - c0
    width = wv + bq
    k = np.arange(width + 1)
    delta = np.where(k < wv, k, k - (width + 1))
    idx = rel_index(qk_off - c0 - delta)
    return c0, const, _toeplitz(jnp.transpose(rel_tab[idx], (1, 0)), bq, width, wv)


def _even_weights(w_in, b_wuq, b_wukv, b_qn, b_kn):
    d = w_in.shape[0]
    offs = np.cumsum((0,) + EVEN_SPLITS)
    w16 = w_in.astype(BF16)
    aq, ak, av, ag, iq, ik, iw, bcq, bckv, bkpe, bg = [w16[:, offs[t]:offs[t + 1]] for t in range(11)]
    slab_idx = jnp.concatenate([ik, iw, jnp.zeros((d, LANE - D_IDX - H_IDX), BF16)], 1)
    w = jnp.concatenate([aq, ag, iq, bg, bcq, ak, av, bckv, slab_idx, _rope_lanes(bkpe)], 1)
    uq = b_wuq.reshape(Q_LORA, H_B, QK_B)
    uq = jnp.concatenate([uq[..., :NOPE], _rope_lanes(uq[..., NOPE:])], -1).reshape(Q_LORA, H_B * 256).astype(BF16)
    ukv = b_wukv.reshape(KV_LORA, H_B, NOPE + V_DIM)
    ukv = jnp.concatenate([ukv[..., :NOPE].reshape(KV_LORA, H_B * NOPE),
                           ukv[..., NOPE:].reshape(KV_LORA, H_B * V_DIM)], 1).astype(BF16)
    pad_gain = lambda g: jnp.concatenate([g[:NOPE], _rope_lanes(g[NOPE:])])[None, :]
    return w, uq, ukv, pad_gain(b_qn), pad_gain(b_kn)


def _pad_rows(x, lp):
    return jnp.pad(x, ((0, 0), (0, lp - x.shape[1]), (0, 0)))


def _even_layer(h, p_all, layer, past, q0, ln_g, w_in, uq, ukv, a_qn, a_kn, t5_bias, b_qln, b_kvln, bqn, bkn, wo, wg, wp):
    b, s, d = h.shape
    m = b * s
    z = _norm_mm(h.reshape(m, d), ln_g[None, :], w_in, min(m, 2048), 512).reshape(b, s, E_END)
    cos, sin = _rope_tables(q0 + jnp.arange(s, dtype=jnp.int32))
    bs = min(s, 512)
    qa, ka, qb, ckv, kpe_l, kpe64, idx64 = _even_prep(z, cos, sin, a_qn[None, :], a_kn[None, :], b_qln[None, :],
                                                      b_kvln[None, :], uq, bqn, bs)
    av = z[..., E_AV:E_AV + 256]
    new = (ka.reshape(b, s, HKV_A, DH_A), av.reshape(b, s, HKV_A, DH_A), idx64, ckv, kpe64)
    if past is None:
        l_true = s
        ka_all, va_all, ckv_all, kpe_all = ka, av, ckv, kpe_l
        kidx_all, kidx_col = z, E_IDX // LANE
    else:
        kidx = z[..., E_IDX:E_IDX + LANE]
        kidx_col = 0
        c_k, c_v, c_ik, c_ckv, c_kpe = past
        pl_ = c_k.shape[1]
        l_true = pl_ + s
        lp = -(-l_true // LANE) * LANE
        cat = lambda c, n_: _pad_rows(jnp.concatenate([c, n_], 1), lp)
        ka_all = cat(c_k.reshape(b, pl_, 256), ka)
        va_all = cat(c_v.reshape(b, pl_, 256), av)
        kidx_all = cat(jnp.pad(c_ik, ((0, 0), (0, 0), (0, LANE - D_IDX))), kidx)
        ckv_all = cat(c_ckv, ckv)
        kpe_all = cat(_rope_lanes(c_kpe), kpe_l)
    lp = ka_all.shape[1]
    ksel = min(TOPK_MAX, l_true // 4)
    classes = _causal_classes(s, lp, q0)
    bqs = [nr if (nr % LANE == 0 and lc <= DSA_WIDE_BLOCK_KEYS) else min(s, LANE) for _, nr, lc in classes]
    tb = min(s, LANE)
    tiles = _t5_tiles(t5_bias, tb, -((q0 + s - tb) // LANE))
    o_a = jnp.zeros((b, s, H_A * DH_A), BF16)
    for (r0, nr, lc), bq in zip(classes, bqs):
        o_a = _dsa(qa, z, ka_all, va_all, kidx_all, kidx_col, tiles, o_a, bq=bq, l_true=l_true, q0=q0,
                   ksel=ksel, row0=r0, rows=nr, lp=lc)
    kb, vb = _mla_kv(ckv_all, kpe_all, ukv, bkn, 1024 if lp % 1024 == 0 else _key_chunk(lp))
    o_b = jnp.zeros((b, s, H_B * V_DIM), BF16)
    for r0, nr, lc in classes:
        o_b = _mla(qb, z, kb, vb, o_b, bq=min(s, 256), l_true=l_true, q0=q0, row0=r0, rows=nr, lp=lc)
    y = _out_ple(h.reshape(m, d), o_a.reshape(m, -1), o_b.reshape(m, -1), p_all, layer, wo, wg, wp, min(m, 512))
    return y.reshape(b, s, d), new


def _odd_layer(h, p_all, layer, past, q0, ln_g, w_in, c_qn, c_kn, c_rel, d_g, d_b, d_ws, d_bs, wo, wg, wp):
    b, s, d = h.shape
    m = b * s
    z = _norm_mm(h.reshape(m, d), ln_g[None, :], w_in, min(m, 1024), O_END // 4).reshape(b, s, O_END)
    n = min(s, D_CHUNK)
    keep = min(C_BACK * CHUNK, s) if past is None else s
    qc, kcb, vcb, o_d, kct, cvt, *dvn = _odd_prep(z, c_qn[None, :], c_kn[None, :], d_g[None, :], d_b[None, :],
                                                  d_ws[:, :n, :n], d_bs[:, :n].T, min(s, 512), n, keep,
                                                  want_dvn=past is not None)
    c_new = (kct.reshape(b, keep, H_C, DH_C), cvt.reshape(b, keep, H_C, DH_C))
    if past is None:
        bqc = 2 * LANE
        bias = _band_bias(c_rel, bqc, C_BACK * CHUNK)
        o_c = _band(qc, z, kcb, vcb, bias, bq=bqc, sliding=True, q0=0, k0=0, k_end=s)
    else:
        nc = past[0].shape[1]
        bias = _band_bias(c_rel, s, nc)
        o_c = _band(qc, z, (past[0].reshape(b, nc, 1024), kcb), (past[1].reshape(b, nc, 1024), vcb), bias,
                    bq=s, sliding=False, q0=q0, k0=q0 - nc, k_end=q0 + s)
    y = _out_ple(h.reshape(m, d), o_c.reshape(m, -1), o_d.reshape(m, -1), p_all, layer, wo, wg, wp, min(m, 512))
    return y.reshape(b, s, d), c_new, (dvn[0] if dvn else None)


def kernel(x_prompt, x_sample, cache_a_k, cache_a_v, cache_a_idx_k, cache_b_ckv, cache_b_kpe, cache_c_k, cache_c_v, p_prompt, p_sample, ln_g, w_in_even, a_q_norm, a_k_norm, t5_bias, b_q_lora_norm, b_kv_lora_norm, b_w_uq, b_w_ukv, b_q_norm, b_k_norm, w_out_even, w_in_odd, c_q_norm, c_k_norm, c_rel_bias, d_ln_g, d_ln_b, d_w_s, d_b_s, w_out_odd, ple_proj, ple_gate):
    depth = ln_g.shape[0]
    past_len = cache_a_k.shape[2]
    hp, hs = x_prompt, x_sample
    pp = p_prompt.reshape(depth, -1, PLE_DIM)
    ps = p_sample.reshape(depth, -1, PLE_DIM)
    ev_p, ev_s, od_p, od_s, dv_s = [], [], [], [], []
    for i in range(depth):
        j = i // 2
        wg = ple_gate[i].astype(BF16)
        wp = ple_proj[i].astype(BF16)
        if i % 2 == 0:
            w_in, uq, ukv, bqn, bkn = _even_weights(w_in_even[j], b_w_uq[j], b_w_ukv[j], b_q_norm[j], b_k_norm[j])
            w = (ln_g[i], w_in, uq, ukv, a_q_norm[j], a_k_norm[j], t5_bias, b_q_lora_norm[j], b_kv_lora_norm[j],
                 bqn, bkn, w_out_even[j].astype(BF16), wg, wp)
            hp, sp = _even_layer(hp, pp, i, None, 0, *w)
            past = (cache_a_k[j], cache_a_v[j], cache_a_idx_k[j], cache_b_ckv[j], cache_b_kpe[j])
            hs, ss = _even_layer(hs, ps, i, past, past_len, *w)
            ev_p.append(sp)
            ev_s.append(ss)
        else:
            w = (ln_g[i], w_in_odd[j].astype(BF16), c_q_norm[j], c_k_norm[j], c_rel_bias[j], d_ln_g[j], d_ln_b[j],
                 d_w_s[j], d_b_s[j], w_out_odd[j].astype(BF16), wg, wp)
            hp, sp, _ = _odd_layer(hp, pp, i, None, 0, *w)
            hs, ss, dvs = _odd_layer(hs, ps, i, (cache_c_k[j], cache_c_v[j]), past_len, *w)
            od_p.append(sp)
            od_s.append(ss)
            dv_s.append(dvs)
    st = lambda lst, n_: jnp.stack([e[n_] for e in lst], 0)
    return (hp, hs, st(ev_p, 0), st(ev_p, 1), st(ev_p, 2), st(ev_p, 3), st(ev_p, 4), st(od_p, 0), st(od_p, 1),
            st(ev_s, 0), st(ev_s, 1), st(ev_s, 2), st(ev_s, 3), st(ev_s, 4), st(od_s, 0), st(od_s, 1),
            jnp.stack(dv_s, 0))
```
